```python
import jax, jax.numpy as jnp
from jax import lax
import numpy as np

D_MODEL = 1024
BATCH = 8
SEQ = 8192
DEPTH = 1

EXPAND = 2
D_MIX = EXPAND * D_MODEL
C_CONV = D_MIX // 2
C_SGU = D_MIX - C_CONV
CONV_GROUPS = 8
SGU_HEADS = 8
SGU_HEAD_DIM = C_SGU // SGU_HEADS
CHUNK = 128
CONV_WIDTH = 31
CONV_PAD = CONV_WIDTH // 2
D_IN = 3 * C_CONV + 3 * C_SGU
EPS = 1e-6

kernel_name = "hybrid_conformer_conv_chunked_sgu_block"


def rms_norm(x, g):
    xf = x.astype(jnp.float32)
    y = xf * lax.rsqrt(jnp.mean(xf * xf, axis=-1, keepdims=True) + EPS)
    return (y * g.astype(jnp.float32)).astype(x.dtype)


def layer_norm(x, g, b):
    xf = x.astype(jnp.float32)
    mu = jnp.mean(xf, axis=-1, keepdims=True)
    xc = xf - mu
    var = jnp.mean(xc * xc, axis=-1, keepdims=True)
    y = xc * lax.rsqrt(var + EPS)
    return (y * g.astype(jnp.float32) + b.astype(jnp.float32)).astype(x.dtype)


def depthwise_conv(x, w, b):
    y = lax.conv_general_dilated(
        x, w[:, None, :].astype(x.dtype), window_strides=(1,),
        padding=[(CONV_PAD, CONV_PAD)],
        dimension_numbers=("NWC", "WIO", "NWC"),
        feature_group_count=x.shape[-1])
    return y + b.astype(x.dtype)


def conformer_conv_branch(a_val, a_gate, conv_w, conv_b, ln_g, ln_b):
    h = a_val * jax.nn.sigmoid(a_gate)
    h = depthwise_conv(h, conv_w, conv_b)
    h = layer_norm(h, ln_g, ln_b)
    return jax.nn.silu(h)


def chunked_sgu_branch(u, v, ln_g, ln_b, w_s, b_s):
    bsz, seq, _ = v.shape
    n_chunks = seq // CHUNK
    v = layer_norm(v, ln_g, ln_b)
    v = v.reshape(bsz, n_chunks, CHUNK, SGU_HEADS, SGU_HEAD_DIM)
    mixed = jnp.einsum("hpq,bcqhd->bcphd", w_s.astype(v.dtype), v)
    mixed = mixed + jnp.transpose(b_s).astype(v.dtype)[None, None, :, :, None]
    return u * mixed.reshape(bsz, seq, C_SGU)


def _fwd_setup_inputs(seed: int = 0) -> dict:
    key = jax.random.key(seed)
    ks = jax.random.split(key, 16)
    f32 = jnp.float32
    x = jax.random.normal(ks[0], (BATCH, SEQ, D_MODEL), f32)
    norm_g = 1.0 + 0.02 * jax.random.normal(ks[1], (DEPTH, D_MODEL), f32)
    w_in = jax.random.normal(ks[2], (DEPTH, D_MODEL, D_IN), f32) * D_MODEL ** -0.5
    conv_w = jax.random.normal(ks[3], (DEPTH, CONV_WIDTH, C_CONV), f32) * CONV_WIDTH ** -0.5
    conv_b = 0.02 * jax.random.normal(ks[4], (DEPTH, C_CONV), f32)
    conv_ln_g = 1.0 + 0.02 * jax.random.normal(ks[5], (DEPTH, C_CONV), f32)
    conv_ln_b = 0.02 * jax.random.normal(ks[6], (DEPTH, C_CONV), f32)
    sgu_ln_g = 1.0 + 0.02 * jax.random.normal(ks[7], (DEPTH, C_SGU), f32)
    sgu_ln_b = 0.02 * jax.random.normal(ks[8], (DEPTH, C_SGU), f32)
    w_s = jax.random.normal(ks[9], (DEPTH, SGU_HEADS, CHUNK, CHUNK), f32) * CHUNK ** -0.5
    b_s = 1.0 + 0.02 * jax.random.normal(ks[10], (DEPTH, SGU_HEADS, CHUNK), f32)
    w_out = jax.random.normal(ks[11], (DEPTH, D_MIX, D_MODEL), f32) * D_MIX ** -0.5
    final_g = 1.0 + 0.02 * jax.random.normal(ks[12], (D_MODEL,), f32)
    return {"x": x, "norm_g": norm_g, "w_in": w_in, "conv_w": conv_w, "conv_b": conv_b,
            "conv_ln_g": conv_ln_g, "conv_ln_b": conv_ln_b, "sgu_ln_g": sgu_ln_g,
            "sgu_ln_b": sgu_ln_b, "w_s": w_s, "b_s": b_s, "w_out": w_out, "final_g": final_g}


def _fwd_reference(x, norm_g, w_in, conv_w, conv_b, conv_ln_g, conv_ln_b, sgu_ln_g,
              sgu_ln_b, w_s, b_s, w_out, final_g):
    split_points = [C_CONV, 2 * C_CONV, 3 * C_CONV,
                    3 * C_CONV + C_SGU, 3 * C_CONV + 2 * C_SGU]
    for l in range(DEPTH):
        h = rms_norm(x, norm_g[l])
        proj = jnp.einsum("bsd,de->bse", h, w_in[l].astype(h.dtype))
        a_val, a_gate, g_conv, u, v, g_sgu = jnp.split(proj, split_points, axis=-1)
        y_conv = conformer_conv_branch(a_val, a_gate, conv_w[l], conv_b[l],
                                       conv_ln_g[l], conv_ln_b[l]) * jax.nn.silu(g_conv)
        y_sgu = chunked_sgu_branch(u, v, sgu_ln_g[l], sgu_ln_b[l],
                                   w_s[l], b_s[l]) * jax.nn.silu(g_sgu)
        y = jnp.concatenate([y_conv, y_sgu], axis=-1)
        x = x + jnp.einsum("bse,ed->bsd", y, w_out[l].astype(y.dtype))
    return rms_norm(x, final_g)


import jax as _jax
import jax.numpy as _jnp

TWIN_FORMAT = 'train_step'
FWD_PARAMS = ['x', 'norm_g', 'w_in', 'conv_w', 'conv_b', 'conv_ln_g', 'conv_ln_b', 'sgu_ln_g', 'sgu_ln_b', 'w_s', 'b_s', 'w_out', 'final_g']
TWIN_WEIGHTS = ['norm_g', 'w_in', 'conv_w', 'conv_b', 'conv_ln_g', 'conv_ln_b', 'sgu_ln_g', 'sgu_ln_b', 'w_s', 'b_s', 'w_out', 'final_g']
TWIN_DIFF_INPUT = 'x'
TWIN_INPUTS = ['x', 'norm_g', 'w_in', 'conv_w', 'conv_b', 'conv_ln_g', 'conv_ln_b', 'sgu_ln_g', 'sgu_ln_b', 'w_s', 'b_s', 'w_out', 'final_g', 'loss_target', 'm_norm_g', 'm_w_in', 'm_conv_w', 'm_conv_b', 'm_conv_ln_g', 'm_conv_ln_b', 'm_sgu_ln_g', 'm_sgu_ln_b', 'm_w_s', 'm_b_s', 'm_w_out', 'm_final_g', 'v_norm_g', 'v_w_in', 'v_conv_w', 'v_conv_b', 'v_conv_ln_g', 'v_conv_ln_b', 'v_sgu_ln_g', 'v_sgu_ln_b', 'v_w_s', 'v_b_s', 'v_w_out', 'v_final_g']
TWIN_OUTPUTS = ['loss', 'grad_x', 'grad_norm_g', 'grad_w_in', 'grad_conv_w', 'grad_conv_b', 'grad_conv_ln_g', 'grad_conv_ln_b', 'grad_sgu_ln_g', 'grad_sgu_ln_b', 'grad_w_s', 'grad_b_s', 'grad_w_out', 'grad_final_g', 'delta_norm_g', 'delta_w_in', 'delta_conv_w', 'delta_conv_b', 'delta_conv_ln_g', 'delta_conv_ln_b', 'delta_sgu_ln_g', 'delta_sgu_ln_b', 'delta_w_s', 'delta_b_s', 'delta_w_out', 'delta_final_g', 'new_m_norm_g', 'new_m_w_in', 'new_m_conv_w', 'new_m_conv_b', 'new_m_conv_ln_g', 'new_m_conv_ln_b', 'new_m_sgu_ln_g', 'new_m_sgu_ln_b', 'new_m_w_s', 'new_m_b_s', 'new_m_w_out', 'new_m_final_g', 'new_v_norm_g', 'new_v_w_in', 'new_v_conv_w', 'new_v_conv_b', 'new_v_conv_ln_g', 'new_v_conv_ln_b', 'new_v_sgu_ln_g', 'new_v_sgu_ln_b', 'new_v_w_s', 'new_v_b_s', 'new_v_w_out', 'new_v_final_g']
TWIN_LEAF_KINDS = {'loss': 'loss', 'grad_x': 'grad_x', 'grad_norm_g': 'grad_w', 'grad_w_in': 'grad_w', 'grad_conv_w': 'grad_w', 'grad_conv_b': 'grad_w', 'grad_conv_ln_g': 'grad_w', 'grad_conv_ln_b': 'grad_w', 'grad_sgu_ln_g': 'grad_w', 'grad_sgu_ln_b': 'grad_w', 'grad_w_s': 'grad_w', 'grad_b_s': 'grad_w', 'grad_w_out': 'grad_w', 'grad_final_g': 'grad_w', 'delta_norm_g': 'delta_w', 'delta_w_in': 'delta_w', 'delta_conv_w': 'delta_w', 'delta_conv_b': 'delta_w', 'delta_conv_ln_g': 'delta_w', 'delta_conv_ln_b': 'delta_w', 'delta_sgu_ln_g': 'delta_w', 'delta_sgu_ln_b': 'delta_w', 'delta_w_s': 'delta_w', 'delta_b_s': 'delta_w', 'delta_w_out': 'delta_w', 'delta_final_g': 'delta_w', 'new_m_norm_g': 'new_m', 'new_m_w_in': 'new_m', 'new_m_conv_w': 'new_m', 'new_m_conv_b': 'new_m', 'new_m_conv_ln_g': 'new_m', 'new_m_conv_ln_b': 'new_m', 'new_m_sgu_ln_g': 'new_m', 'new_m_sgu_ln_b': 'new_m', 'new_m_w_s': 'new_m', 'new_m_b_s': 'new_m', 'new_m_w_out': 'new_m', 'new_m_final_g': 'new_m', 'new_v_norm_g': 'new_v', 'new_v_w_in': 'new_v', 'new_v_conv_w': 'new_v', 'new_v_conv_b': 'new_v', 'new_v_conv_ln_g': 'new_v', 'new_v_conv_ln_b': 'new_v', 'new_v_sgu_ln_g': 'new_v', 'new_v_sgu_ln_b': 'new_v', 'new_v_w_s': 'new_v', 'new_v_b_s': 'new_v', 'new_v_w_out': 'new_v', 'new_v_final_g': 'new_v'}


def _forward(args):
    return _fwd_reference(*[args[k] for k in FWD_PARAMS])


def _output_shape():
    out = _jax.eval_shape(lambda: _forward(_fwd_setup_inputs(0)))
    return out.shape, out.dtype

N_MICROBATCH = 1
ADAM_LR = 0.001
ADAM_B1 = 0.9
ADAM_B2 = 0.999
ADAM_EPS = 1e-08
ADAM_WD = 0.01
ADAM_STEP = 10
PER_EXAMPLE_BATCH_AXIS = {'x': 0, 'loss_target': 0}
SHARED_INPUTS = []
_WEIGHT_DTYPES = {'norm_g': _jnp.float32, 'w_in': _jnp.float32, 'conv_w': _jnp.float32, 'conv_b': _jnp.float32, 'conv_ln_g': _jnp.float32, 'conv_ln_b': _jnp.float32, 'sgu_ln_g': _jnp.float32, 'sgu_ln_b': _jnp.float32, 'w_s': _jnp.float32, 'b_s': _jnp.float32, 'w_out': _jnp.float32, 'final_g': _jnp.float32}
MOMENT_SCALE = {'norm_g': 2.445677e-01, 'w_in': 8.903972e-02, 'conv_w': 5.529925e-02, 'conv_b': 1.047405e-01, 'conv_ln_g': 6.427219e-02, 'conv_ln_b': 5.513961e-02, 'sgu_ln_g': 9.106763e-02, 'sgu_ln_b': 9.292866e-02, 'w_s': 8.869919e-02, 'b_s': 8.658213e-02, 'w_out': 1.359821e-01, 'final_g': 6.397225e+01}


def _to_microbatches(a, axis):
    t = _jnp.moveaxis(a, axis, 0)
    t = t.reshape((N_MICROBATCH, t.shape[0] // N_MICROBATCH) + t.shape[1:])
    return _jnp.moveaxis(t, 1, axis + 1)


def setup_inputs(seed: int = 0) -> dict:
    inp = _fwd_setup_inputs(seed)
    key = _jax.random.fold_in(_jax.random.key(seed), 7919)
    shape, _ = _output_shape()
    out = dict(inp)
    out["loss_target"] = _jax.random.normal(_jax.random.fold_in(key, 0), shape, _jnp.float32)
    for i, name in enumerate(TWIN_WEIGHTS):
        w = inp[name].astype(_jnp.float32)
        if MOMENT_SCALE is None:
            s = _jnp.sqrt(_jnp.mean(_jnp.square(w)) + 1e-30)
        else:
            s = MOMENT_SCALE[name]
        km, kv = _jax.random.split(_jax.random.fold_in(key, i + 1))
        out[name] = w
        out["m_" + name] = s * _jax.random.normal(km, w.shape, _jnp.float32)
        out["v_" + name] = (s * s) * _jax.random.uniform(kv, w.shape, _jnp.float32, 0.5, 1.5)
    if N_MICROBATCH > 1:
        for name, axis in PER_EXAMPLE_BATCH_AXIS.items():
            out[name] = _to_microbatches(out[name], axis)
    return {'x': out['x'], 'norm_g': out['norm_g'], 'w_in': out['w_in'], 'conv_w': out['conv_w'], 'conv_b': out['conv_b'], 'conv_ln_g': out['conv_ln_g'], 'conv_ln_b': out['conv_ln_b'], 'sgu_ln_g': out['sgu_ln_g'], 'sgu_ln_b': out['sgu_ln_b'], 'w_s': out['w_s'], 'b_s': out['b_s'], 'w_out': out['w_out'], 'final_g': out['final_g'], 'loss_target': out['loss_target'], 'm_norm_g': out['m_norm_g'], 'm_w_in': out['m_w_in'], 'm_conv_w': out['m_conv_w'], 'm_conv_b': out['m_conv_b'], 'm_conv_ln_g': out['m_conv_ln_g'], 'm_conv_ln_b': out['m_conv_ln_b'], 'm_sgu_ln_g': out['m_sgu_ln_g'], 'm_sgu_ln_b': out['m_sgu_ln_b'], 'm_w_s': out['m_w_s'], 'm_b_s': out['m_b_s'], 'm_w_out': out['m_w_out'], 'm_final_g': out['m_final_g'], 'v_norm_g': out['v_norm_g'], 'v_w_in': out['v_w_in'], 'v_conv_w': out['v_conv_w'], 'v_conv_b': out['v_conv_b'], 'v_conv_ln_g': out['v_conv_ln_g'], 'v_conv_ln_b': out['v_conv_ln_b'], 'v_sgu_ln_g': out['v_sgu_ln_g'], 'v_sgu_ln_b': out['v_sgu_ln_b'], 'v_w_s': out['v_w_s'], 'v_b_s': out['v_b_s'], 'v_w_out': out['v_w_out'], 'v_final_g': out['v_final_g']}


def _loss(weights, diff, rest, loss_target):
    with _jax.named_scope("forward"):
        args = {**rest, TWIN_DIFF_INPUT: diff, **{k: w.astype(_WEIGHT_DTYPES[k]) for k, w in weights.items()}}
        y = _forward(args)
    with _jax.named_scope("loss_head"):
        err = _jnp.square(y.astype(_jnp.float32) - loss_target)
        return 0.5 * _jnp.sum(_jnp.mean(err, axis=-1)) if err.ndim else 0.5 * err


def _adamw(w, g, m, v):
    m = ADAM_B1 * m + (1.0 - ADAM_B1) * g
    v = ADAM_B2 * v + (1.0 - ADAM_B2) * _jnp.square(g)
    m_hat = m / (1.0 - ADAM_B1 ** ADAM_STEP)
    v_hat = v / (1.0 - ADAM_B2 ** ADAM_STEP)
    delta = -ADAM_LR * (m_hat / (_jnp.sqrt(v_hat) + ADAM_EPS) + ADAM_WD * w)
    return delta, m, v


def reference(x, norm_g, w_in, conv_w, conv_b, conv_ln_g, conv_ln_b, sgu_ln_g, sgu_ln_b, w_s, b_s, w_out, final_g, loss_target, m_norm_g, m_w_in, m_conv_w, m_conv_b, m_conv_ln_g, m_conv_ln_b, m_sgu_ln_g, m_sgu_ln_b, m_w_s, m_b_s, m_w_out, m_final_g, v_norm_g, v_w_in, v_conv_w, v_conv_b, v_conv_ln_g, v_conv_ln_b, v_sgu_ln_g, v_sgu_ln_b, v_w_s, v_b_s, v_w_out, v_final_g):
    given = dict(x=x, norm_g=norm_g, w_in=w_in, conv_w=conv_w, conv_b=conv_b, conv_ln_g=conv_ln_g, conv_ln_b=conv_ln_b, sgu_ln_g=sgu_ln_g, sgu_ln_b=sgu_ln_b, w_s=w_s, b_s=b_s, w_out=w_out, final_g=final_g, loss_target=loss_target, m_norm_g=m_norm_g, m_w_in=m_w_in, m_conv_w=m_conv_w, m_conv_b=m_conv_b, m_conv_ln_g=m_conv_ln_g, m_conv_ln_b=m_conv_ln_b, m_sgu_ln_g=m_sgu_ln_g, m_sgu_ln_b=m_sgu_ln_b, m_w_s=m_w_s, m_b_s=m_b_s, m_w_out=m_w_out, m_final_g=m_final_g, v_norm_g=v_norm_g, v_w_in=v_w_in, v_conv_w=v_conv_w, v_conv_b=v_conv_b, v_conv_ln_g=v_conv_ln_g, v_conv_ln_b=v_conv_ln_b, v_sgu_ln_g=v_sgu_ln_g, v_sgu_ln_b=v_sgu_ln_b, v_w_s=v_w_s, v_b_s=v_b_s, v_w_out=v_w_out, v_final_g=v_final_g)
    weights = {n: given[n] for n in TWIN_WEIGHTS}
    shared = {n: given[n] for n in SHARED_INPUTS}
    per_example = {n: given[n] for n in ['x']}
    grad_fn = _jax.value_and_grad(_loss, argnums=(0, 1))

    def one_microbatch(ex, loss_target):
        ex = dict(ex)
        diff = ex.pop(TWIN_DIFF_INPUT)
        return grad_fn(weights, diff, {**shared, **ex}, loss_target)

    if N_MICROBATCH == 1:
        loss, (grad_w, grad_x) = one_microbatch(per_example, given["loss_target"])
    else:
        def body(carry, xs):
            loss_sum, grad_sum = carry
            l_k, (gw_k, gx_k) = one_microbatch(xs[0], xs[1])
            with _jax.named_scope("update"):
                return (loss_sum + l_k, _jax.tree.map(_jnp.add, grad_sum, gw_k)), gx_k

        init = (_jnp.zeros((), _jnp.float32), _jax.tree.map(_jnp.zeros_like, weights))
        (loss, grad_w), grad_x = _jax.lax.scan(body, init, (per_example, given["loss_target"]))
    with _jax.named_scope("update"):
        delta_w, new_m, new_v = {}, {}, {}
        for n in TWIN_WEIGHTS:
            delta_w[n], new_m[n], new_v[n] = _adamw(weights[n], grad_w[n], given["m_" + n], given["v_" + n])
    return (loss, grad_x, *[grad_w[n] for n in TWIN_WEIGHTS], *[delta_w[n] for n in TWIN_WEIGHTS],
            *[new_m[n] for n in TWIN_WEIGHTS], *[new_v[n] for n in TWIN_WEIGHTS])
```

```python
import functools

import jax
import jax.numpy as jnp
from jax import lax
from jax.experimental import pallas as pl
from jax.experimental.pallas import tpu as pltpu

F32 = jnp.float32
BF16 = jnp.bfloat16
ACT = jnp.bfloat16

D_MODEL = 1024
C_BR = 1024
D_IN = 6 * C_BR
N_DEV = 8
W_BLK = D_IN // N_DEV
HEADS = 8
HEAD_DIM = 128
CHUNK = 128
CONV_WIDTH = 31
CONV_PAD = CONV_WIDTH // 2
HALO = 16
CONV_ROWS = 32
EPS = 1e-6

ADAM_LR = 0.001
ADAM_B1 = 0.9
ADAM_B2 = 0.999
ADAM_EPS = 1e-08
ADAM_WD = 0.01
ADAM_STEP = 10

VMEM_LIMIT = 56 * 1024 * 1024
MESH = pl.DeviceIdType.MESH

ROW_NORM_G, ROW_CONV_B, ROW_CLN_G, ROW_CLN_B, ROW_SLN_G, ROW_SLN_B, ROW_FINAL_G, ROW_B_S, ROW_LOSS = range(9)
ROW_CONV_W = 16
SMALL_ROWS = ROW_CONV_W + CONV_ROWS


def _params(sem=None, **kw):
    return pltpu.CompilerParams(dimension_semantics=sem, vmem_limit_bytes=VMEM_LIMIT, **kw)


def _fold8(a):
    r, n = a.shape
    return a.reshape(r // 8, 8, n).sum(axis=0)


def _sigmoid(z):
    return 1.0 / (1.0 + jnp.exp(-z))


def _ln_norm(xf):
    mu = jnp.mean(xf, axis=-1, keepdims=True)
    xc = xf - mu
    var = jnp.mean(xc * xc, axis=-1, keepdims=True)
    rstd = lax.rsqrt(var + EPS)
    return xc * rstd, rstd


def _ln_bwd(dy, xhat, rstd, g):
    dxhat = dy * g
    m1 = jnp.mean(dxhat, axis=-1, keepdims=True)
    m2 = jnp.mean(dxhat * xhat, axis=-1, keepdims=True)
    return rstd * (dxhat - m1 - xhat * m2)


def _place():
    return lax.axis_index("x"), lax.axis_index("y"), lax.axis_index("c")


def _ag_weights(w_in, w_out, conv_w):
    def body(win_ref, wout_ref, cw_ref, win_all, wout_all, cw_all, send_sems, recv_sems):
        x, y, c = _place()
        me, sibling = (x, y, c), (x, y, 1 - c)
        chips = [(1 - x, y), (x, 1 - y), (1 - x, 1 - y)]
        outs = (win_all, wout_all, cw_all)

        def blk(a, px, py, pc):
            return outs[a].at[4 * px + 2 * py + pc]

        def copy(a, k, block, to):
            return pltpu.make_async_remote_copy(
                src_ref=blk(a, *block), dst_ref=blk(a, *block),
                send_sem=send_sems.at[a, k], recv_sem=recv_sems.at[a, k],
                device_id=to, device_id_type=MESH)

        mine = 4 * x + 2 * y + c
        win_all[mine] = win_ref[...].astype(BF16)
        wout_all[mine] = wout_ref[...].astype(BF16)
        cw_all[mine] = cw_ref[...]

        arrays = range(3)
        first = []
        for a in arrays:
            first.append(copy(a, 0, me, sibling))
            first += [copy(a, 1 + j, me, (*chip, c)) for j, chip in enumerate(chips)]
        for cp in first:
            cp.start()
        passed = []
        for j, chip in enumerate(chips):
            for a in arrays:
                copy(a, 1 + j, (*chip, c), me).wait_recv()
                fwd = copy(a, 4 + j, (*chip, c), sibling)
                fwd.start()
                passed.append(fwd)
        for a in arrays:
            copy(a, 0, sibling, me).wait_recv()
            for j, chip in enumerate(chips):
                copy(a, 4 + j, (*chip, 1 - c), me).wait_recv()
        for cp in first + passed:
            cp.wait_send()

    vm = pl.BlockSpec(memory_space=pltpu.VMEM)
    return pl.pallas_call(
        body, name="ag_weights",
        out_shape=(jax.ShapeDtypeStruct((N_DEV,) + w_in.shape, BF16),
                   jax.ShapeDtypeStruct((N_DEV,) + w_out.shape, BF16),
                   jax.ShapeDtypeStruct((N_DEV,) + conv_w.shape, F32)),
        in_specs=[vm, vm, vm], out_specs=(vm, vm, vm),
        scratch_shapes=[pltpu.SemaphoreType.DMA((3, 7)), pltpu.SemaphoreType.DMA((3, 7))],
        compiler_params=_params(),
    )(w_in, w_out, conv_w)


def _proj(x, norm_g, win_all, tm):
    s = x.shape[0]

    def body(x_ref, g_ref, w_ref, proj_ref, ht_ref):
        xf = x_ref[...]
        r = lax.rsqrt(jnp.mean(xf * xf, axis=-1, keepdims=True) + EPS)
        hf = xf * r * g_ref[...]
        h = hf.astype(BF16)
        ht_ref[...] = hf.T.astype(BF16)
        for j in range(N_DEV):
            proj_ref[:, j * W_BLK:(j + 1) * W_BLK] = jnp.dot(
                h, w_ref[j], preferred_element_type=F32).astype(ACT)

    return pl.pallas_call(
        body, name="proj", grid=(s // tm,),
        out_shape=(jax.ShapeDtypeStruct((s, D_IN), ACT), jax.ShapeDtypeStruct((D_MODEL, s), BF16)),
        in_specs=[pl.BlockSpec((tm, D_MODEL), lambda i: (i, 0)),
                  pl.BlockSpec((1, D_MODEL), lambda i: (0, 0)),
                  pl.BlockSpec((N_DEV, D_MODEL, W_BLK), lambda i: (0, 0, 0))],
        out_specs=(pl.BlockSpec((tm, D_IN), lambda i: (i, 0)),
                   pl.BlockSpec((D_MODEL, tm), lambda i: (0, i))),
        compiler_params=_params(("parallel",)),
    )(x, norm_g, win_all)


def _halo_specs(tm, s, col):
    per = tm // HALO
    last = s // HALO - 1
    return [pl.BlockSpec((HALO, C_BR), lambda i: (jnp.maximum(i * per - 1, 0), col)),
            pl.BlockSpec((tm, C_BR), lambda i: (i, col)),
            pl.BlockSpec((HALO, C_BR), lambda i: (jnp.minimum((i + 1) * per, last), col))]


def _shifted_sum(ext_ref, w_ref, tm, rb, offset):
    outs = []
    for r0 in range(0, tm, rb):
        acc = None
        for k in range(CONV_WIDTH):
            term = w_ref[k:k + 1, :] * ext_ref[r0 + offset(k):r0 + offset(k) + rb, :]
            acc = term if acc is None else acc + term
        outs.append(acc)
    return jnp.concatenate(outs, axis=0)


def _conv_fwd(proj, conv_w, conv_b, ln_g, ln_b, tm):
    s = proj.shape[0]
    nt = s // tm

    def body(av_p, av_m, av_n, ag_p, ag_m, ag_n, gc_ref, w_ref, cb_ref, lg_ref, lb_ref,
             y_ref, c_ref, ext_ref):
        i = pl.program_id(0)

        def glu(a_ref, g_ref):
            return a_ref[...].astype(F32) * _sigmoid(g_ref[...].astype(F32))

        ext_ref[0:HALO, :] = jnp.where(i > 0, glu(av_p, ag_p), 0.0)
        ext_ref[HALO:HALO + tm, :] = glu(av_m, ag_m)
        ext_ref[HALO + tm:, :] = jnp.where(i < nt - 1, glu(av_n, ag_n), 0.0)
        cv = _shifted_sum(ext_ref, w_ref, tm, 32, lambda k: k + HALO - CONV_PAD) + cb_ref[...]
        c_ref[...] = cv.astype(ACT)
        xhat, _ = _ln_norm(cv)
        ln = xhat * lg_ref[...] + lb_ref[...]
        gc = gc_ref[...].astype(F32)
        y_ref[...] = (ln * _sigmoid(ln) * (gc * _sigmoid(gc))).astype(ACT)

    vec = pl.BlockSpec((1, C_BR), lambda i: (0, 0))
    return pl.pallas_call(
        body, name="conv_fwd", grid=(nt,),
        out_shape=(jax.ShapeDtypeStruct((s, 2 * C_BR), ACT), jax.ShapeDtypeStruct((s, C_BR), ACT)),
        in_specs=_halo_specs(tm, s, 0) + _halo_specs(tm, s, 1)
        + [pl.BlockSpec((tm, C_BR), lambda i: (i, 2)),
           pl.BlockSpec((CONV_ROWS, C_BR), lambda i: (0, 0)), vec, vec, vec],
        out_specs=(pl.BlockSpec((tm, C_BR), lambda i: (i, 0)), pl.BlockSpec((tm, C_BR), lambda i: (i, 0))),
        scratch_shapes=[pltpu.VMEM((tm + 2 * HALO, C_BR), F32)],
        compiler_params=_params(("parallel",)),
    )(proj, proj, proj, proj, proj, proj, proj, conv_w, conv_b, ln_g, ln_b)


def _sgu_fwd(proj, y, ln_g, ln_b, ws, bsb, tm):
    s = proj.shape[0]

    def body(u_ref, v_ref, gs_ref, y_in, lg_ref, lb_ref, ws_ref, bsb_ref, y_ref):
        del y_in
        xhat, _ = _ln_norm(v_ref[...].astype(F32))
        vn = (xhat * lg_ref[...] + lb_ref[...]).astype(BF16)
        for cidx in range(tm // CHUNK):
            rows = slice(cidx * CHUNK, (cidx + 1) * CHUNK)
            for h in range(HEADS):
                cols = slice(h * HEAD_DIM, (h + 1) * HEAD_DIM)
                mixed = jnp.dot(ws_ref[h], vn[rows, cols], preferred_element_type=F32) + bsb_ref[:, cols]
                gs = gs_ref[rows, cols].astype(F32)
                y_ref[rows, cols] = (u_ref[rows, cols].astype(F32) * mixed * (gs * _sigmoid(gs))).astype(ACT)

    vec = pl.BlockSpec((1, C_BR), lambda i: (0, 0))
    return pl.pallas_call(
        body, name="sgu_fwd", grid=(s // tm,),
        out_shape=jax.ShapeDtypeStruct((s, 2 * C_BR), ACT),
        in_specs=[pl.BlockSpec((tm, C_BR), lambda i: (i, 3)),
                  pl.BlockSpec((tm, C_BR), lambda i: (i, 4)),
                  pl.BlockSpec((tm, C_BR), lambda i: (i, 5)),
                  pl.BlockSpec(memory_space=pl.ANY),
                  vec, vec,
                  pl.BlockSpec((HEADS, CHUNK, CHUNK), lambda i: (0, 0, 0)),
                  pl.BlockSpec((CHUNK, C_BR), lambda i: (0, 0))],
        out_specs=pl.BlockSpec((tm, C_BR), lambda i: (i, 1)),
        input_output_aliases={3: 0},
        compiler_params=_params(("parallel",)),
    )(proj, proj, proj, y, ln_g, ln_b, ws, bsb)


def _out_loss(x, y, wout, final_g, target, tm):
    s = x.shape[0]
    nt = s // tm
    inv_d = 1.0 / D_MODEL

    def body(x_ref, y_ref, w_ref, g_ref, t_ref, dx2_ref, dy_ref, dw_ref, loss_ref, dfg_ref, acc_ref):
        i = pl.program_id(0)

        @pl.when(i == 0)
        def _():
            acc_ref[...] = jnp.zeros_like(acc_ref)
            loss_ref[...] = jnp.zeros_like(loss_ref)
            dfg_ref[...] = jnp.zeros_like(dfg_ref)

        yb = y_ref[...]
        x2 = x_ref[...] + jnp.dot(yb, w_ref[...], preferred_element_type=F32)
        r2 = lax.rsqrt(jnp.mean(x2 * x2, axis=-1, keepdims=True) + EPS)
        n = x2 * r2
        g = g_ref[...]
        e = n * g - t_ref[...]
        loss_ref[...] += _fold8(e * e)
        dout = e * inv_d
        dfg_ref[...] += _fold8(dout * n)
        dn = dout * g
        dx2 = r2 * (dn - n * jnp.mean(dn * n, axis=-1, keepdims=True))
        dx2_ref[...] = dx2
        dxb = dx2.astype(BF16)
        dy_ref[...] = lax.dot_general(dxb, w_ref[...], (((1,), (1,)), ((), ())),
                                      preferred_element_type=F32).astype(ACT)
        acc_ref[...] += lax.dot_general(yb, dxb, (((0,), (0,)), ((), ())), preferred_element_type=F32)

        @pl.when(i == nt - 1)
        def _():
            dw_ref[...] = acc_ref[...].astype(BF16)

    part = pl.BlockSpec((8, D_MODEL), lambda i: (0, 0))
    return pl.pallas_call(
        body, name="out_loss", grid=(nt,),
        out_shape=(jax.ShapeDtypeStruct((s, D_MODEL), F32), jax.ShapeDtypeStruct((s, 2 * C_BR), ACT),
                   jax.ShapeDtypeStruct((2 * C_BR, D_MODEL), BF16),
                   jax.ShapeDtypeStruct((8, D_MODEL), F32), jax.ShapeDtypeStruct((8, D_MODEL), F32)),
        in_specs=[pl.BlockSpec((tm, D_MODEL), lambda i: (i, 0)),
                  pl.BlockSpec((tm, 2 * C_BR), lambda i: (i, 0)),
                  pl.BlockSpec((2 * C_BR, D_MODEL), lambda i: (0, 0)),
                  pl.BlockSpec((1, D_MODEL), lambda i: (0, 0)),
                  pl.BlockSpec((tm, D_MODEL), lambda i: (i, 0))],
        out_specs=(pl.BlockSpec((tm, D_MODEL), lambda i: (i, 0)),
                   pl.BlockSpec((tm, 2 * C_BR), lambda i: (i, 0)),
                   pl.BlockSpec((2 * C_BR, D_MODEL), lambda i: (0, 0)), part, part),
        scratch_shapes=[pltpu.VMEM((2 * C_BR, D_MODEL), F32)],
        compiler_params=_params(("arbitrary",)),
    )(x, y, wout, final_g, target)


def _conv_bwd(proj, cv, dy, conv_w, ln_g, ln_b, tm):
    s = proj.shape[0]
    nt = s // tm
    rb = 32

    def body(av_ref, ag_ref, gc_p, gc_m, gc_n, c_p, c_m, c_n, dy_p, dy_m, dy_n, w_ref, lg_ref, lb_ref,
             dp_ref, dwc_ref, dcb_ref, dlg_ref, dlb_ref, dce_ref):
        i = pl.program_id(0)

        @pl.when(i == 0)
        def _():
            dwc_ref[...] = jnp.zeros_like(dwc_ref)
            dcb_ref[...] = jnp.zeros_like(dcb_ref)
            dlg_ref[...] = jnp.zeros_like(dlg_ref)
            dlb_ref[...] = jnp.zeros_like(dlb_ref)

        def ext(p, m, n):
            return jnp.concatenate([p[...], m[...], n[...]], axis=0).astype(F32)

        main = slice(HALO, HALO + tm)
        cf, gc, dyc = ext(c_p, c_m, c_n), ext(gc_p, gc_m, gc_n), ext(dy_p, dy_m, dy_n)
        xhat, rstd = _ln_norm(cf)
        lg = lg_ref[...]
        ln = xhat * lg + lb_ref[...]
        s_ln, s_gc = _sigmoid(ln), _sigmoid(gc)
        dln = dyc * (gc * s_gc) * (s_ln * (1.0 + ln * (1.0 - s_ln)))
        dp_ref[:, 2 * C_BR:] = (dyc[main] * (ln[main] * s_ln[main])
                                * (s_gc[main] * (1.0 + gc[main] * (1.0 - s_gc[main])))).astype(ACT)
        dlg_ref[...] += _fold8(dln[main] * xhat[main])
        dlb_ref[...] += _fold8(dln[main])
        dc = _ln_bwd(dln, xhat, rstd, lg)
        dcb_ref[...] += _fold8(dc[main])
        dce_ref[0:HALO, :] = jnp.where(i > 0, dc[0:HALO], 0.0)
        dce_ref[main, :] = dc[main]
        dce_ref[HALO + tm:, :] = jnp.where(i < nt - 1, dc[HALO + tm:], 0.0)

        av = av_ref[...].astype(F32)
        ag = ag_ref[...].astype(F32)
        s_ag = _sigmoid(ag)
        glu = av * s_ag
        for r0 in range(0, tm, rb):
            acc = None
            glu_blk = glu[r0:r0 + rb]
            for k in range(CONV_WIDTH):
                off = r0 + HALO + CONV_PAD - k
                sh = dce_ref[off:off + rb, :]
                term = w_ref[k:k + 1, :] * sh
                acc = term if acc is None else acc + term
                dwc_ref[8 * k:8 * k + 8, :] += _fold8(glu_blk * sh)
            sa = s_ag[r0:r0 + rb]
            dp_ref[r0:r0 + rb, 0:C_BR] = (acc * sa).astype(ACT)
            dp_ref[r0:r0 + rb, C_BR:2 * C_BR] = (acc * av[r0:r0 + rb] * sa * (1.0 - sa)).astype(ACT)

    vec = pl.BlockSpec((1, C_BR), lambda i: (0, 0))
    part = pl.BlockSpec((8, C_BR), lambda i: (0, 0))
    return pl.pallas_call(
        body, name="conv_bwd", grid=(nt,),
        out_shape=(jax.ShapeDtypeStruct((s, D_IN), ACT), jax.ShapeDtypeStruct((8 * CONV_ROWS, C_BR), F32),
                   jax.ShapeDtypeStruct((8, C_BR), F32), jax.ShapeDtypeStruct((8, C_BR), F32),
                   jax.ShapeDtypeStruct((8, C_BR), F32)),
        in_specs=[pl.BlockSpec((tm, C_BR), lambda i: (i, 0)), pl.BlockSpec((tm, C_BR), lambda i: (i, 1))]
        + _halo_specs(tm, s, 2) + _halo_specs(tm, s, 0) + _halo_specs(tm, s, 0)
        + [pl.BlockSpec((CONV_ROWS, C_BR), lambda i: (0, 0)), vec, vec],
        out_specs=(pl.BlockSpec((tm, 3 * C_BR), lambda i: (i, 0)),
                   pl.BlockSpec((8 * CONV_ROWS, C_BR), lambda i: (0, 0)), part, part, part),
        scratch_shapes=[pltpu.VMEM((tm + 2 * HALO, C_BR), F32)],
        compiler_params=_params(("arbitrary",)),
    )(proj, proj, proj, proj, proj, cv, cv, cv, dy, dy, dy, conv_w, ln_g, ln_b)


def _sgu_bwd(proj, dy, dproj, ln_g, ln_b, ws, wst, bsb, tm):
    s = proj.shape[0]

    def body(u_ref, v_ref, gs_ref, dy_ref, dp_in, lg_ref, lb_ref, ws_ref, wst_ref, bsb_ref,
             dp_ref, dws_ref, dba_ref, dlg_ref, dlb_ref, dvn_ref):
        del dp_in
        i = pl.program_id(0)

        @pl.when(i == 0)
        def _():
            dws_ref[...] = jnp.zeros_like(dws_ref)
            dba_ref[...] = jnp.zeros_like(dba_ref)
            dlg_ref[...] = jnp.zeros_like(dlg_ref)
            dlb_ref[...] = jnp.zeros_like(dlb_ref)

        xhat, rstd = _ln_norm(v_ref[...].astype(F32))
        lg = lg_ref[...]
        vn = (xhat * lg + lb_ref[...]).astype(BF16)
        for cidx in range(tm // CHUNK):
            rows = slice(cidx * CHUNK, (cidx + 1) * CHUNK)
            for h in range(HEADS):
                cols = slice(h * HEAD_DIM, (h + 1) * HEAD_DIM)
                ocols = slice(C_BR + h * HEAD_DIM, C_BR + (h + 1) * HEAD_DIM)
                gcols = slice(2 * C_BR + h * HEAD_DIM, 2 * C_BR + (h + 1) * HEAD_DIM)
                vb = vn[rows, cols]
                mixed = jnp.dot(ws_ref[h], vb, preferred_element_type=F32) + bsb_ref[:, cols]
                gs = gs_ref[rows, cols].astype(F32)
                sg = _sigmoid(gs)
                u = u_ref[rows, cols].astype(F32)
                dyb = dy_ref[rows, cols].astype(F32)
                t = dyb * (gs * sg)
                dp_ref[rows, cols] = (t * mixed).astype(ACT)
                dp_ref[rows, gcols] = (dyb * u * mixed * (sg * (1.0 + gs * (1.0 - sg)))).astype(ACT)
                dm = t * u
                dmb = dm.astype(BF16)
                dvn_ref[rows, cols] = jnp.dot(wst_ref[h], dmb, preferred_element_type=F32)
                dws_ref[h] += lax.dot_general(dmb, vb, (((1,), (1,)), ((), ())), preferred_element_type=F32)
                dba_ref[:, cols] += dm
        dvn = dvn_ref[...]
        dlg_ref[...] += _fold8(dvn * xhat)
        dlb_ref[...] += _fold8(dvn)
        dp_ref[:, C_BR:2 * C_BR] = _ln_bwd(dvn, xhat, rstd, lg).astype(ACT)

    vec = pl.BlockSpec((1, C_BR), lambda i: (0, 0))
    part = pl.BlockSpec((8, C_BR), lambda i: (0, 0))
    wsp = pl.BlockSpec((HEADS, CHUNK, CHUNK), lambda i: (0, 0, 0))
    return pl.pallas_call(
        body, name="sgu_bwd", grid=(s // tm,),
        out_shape=(jax.ShapeDtypeStruct((s, D_IN), ACT), jax.ShapeDtypeStruct((HEADS, CHUNK, CHUNK), F32),
                   jax.ShapeDtypeStruct((CHUNK, C_BR), F32), jax.ShapeDtypeStruct((8, C_BR), F32),
                   jax.ShapeDtypeStruct((8, C_BR), F32)),
        in_specs=[pl.BlockSpec((tm, C_BR), lambda i: (i, 3)),
                  pl.BlockSpec((tm, C_BR), lambda i: (i, 4)),
                  pl.BlockSpec((tm, C_BR), lambda i: (i, 5)),
                  pl.BlockSpec((tm, C_BR), lambda i: (i, 1)),
                  pl.BlockSpec(memory_space=pl.ANY),
                  vec, vec, wsp, wsp, pl.BlockSpec((CHUNK, C_BR), lambda i: (0, 0))],
        out_specs=(pl.BlockSpec((tm, 3 * C_BR), lambda i: (i, 1)), wsp,
                   pl.BlockSpec((CHUNK, C_BR), lambda i: (0, 0)), part, part),
        scratch_shapes=[pltpu.VMEM((tm, C_BR), F32)],
        input_output_aliases={4: 0},
        compiler_params=_params(("arbitrary",)),
    )(proj, proj, proj, dy, dproj, ln_g, ln_b, ws, wst, bsb)


def _dx(dproj, win_all, x, norm_g, dx2, tm):
    s = x.shape[0]

    def body(dp_ref, w_ref, x_ref, g_ref, dx2_ref, gx_ref, dng_ref):
        i = pl.program_id(0)

        @pl.when(i == 0)
        def _():
            dng_ref[...] = jnp.zeros_like(dng_ref)

        dh = None
        for j in range(N_DEV):
            term = lax.dot_general(dp_ref[:, j * W_BLK:(j + 1) * W_BLK], w_ref[j],
                                   (((1,), (1,)), ((), ())), preferred_element_type=F32)
            dh = term if dh is None else dh + term
        xf = x_ref[...]
        r = lax.rsqrt(jnp.mean(xf * xf, axis=-1, keepdims=True) + EPS)
        n = xf * r
        dng_ref[...] += _fold8(dh * n)
        dn = dh * g_ref[...]
        gx_ref[...] = dx2_ref[...] + r * (dn - n * jnp.mean(dn * n, axis=-1, keepdims=True))

    return pl.pallas_call(
        body, name="dx", grid=(s // tm,),
        out_shape=(jax.ShapeDtypeStruct((s, D_MODEL), F32), jax.ShapeDtypeStruct((8, D_MODEL), F32)),
        in_specs=[pl.BlockSpec((tm, D_IN), lambda i: (i, 0)),
                  pl.BlockSpec((N_DEV, D_MODEL, W_BLK), lambda i: (0, 0, 0)),
                  pl.BlockSpec((tm, D_MODEL), lambda i: (i, 0)),
                  pl.BlockSpec((1, D_MODEL), lambda i: (0, 0)),
                  pl.BlockSpec((tm, D_MODEL), lambda i: (i, 0))],
        out_specs=(pl.BlockSpec((tm, D_MODEL), lambda i: (i, 0)), pl.BlockSpec((8, D_MODEL), lambda i: (0, 0))),
        compiler_params=_params(("arbitrary",)),
    )(dproj, win_all, x, norm_g, dx2)


def _dwin(ht, dproj, tk):
    s = ht.shape[1]
    nk = s // tk

    def body(ht_ref, dp_ref, dw_ref, acc_ref):
        k = pl.program_id(1)

        @pl.when(k == 0)
        def _():
            acc_ref[...] = jnp.zeros_like(acc_ref)

        acc_ref[...] += jnp.dot(ht_ref[...], dp_ref[...], preferred_element_type=F32)

        @pl.when(k == nk - 1)
        def _():
            dw_ref[0] = acc_ref[...].astype(BF16)

    return pl.pallas_call(
        body, name="dwin", grid=(N_DEV, nk),
        out_shape=jax.ShapeDtypeStruct((N_DEV, D_MODEL, W_BLK), BF16),
        in_specs=[pl.BlockSpec((D_MODEL, tk), lambda j, k: (0, k)),
                  pl.BlockSpec((tk, W_BLK), lambda j, k: (k, j))],
        out_specs=pl.BlockSpec((1, D_MODEL, W_BLK), lambda j, k: (j, 0, 0)),
        scratch_shapes=[pltpu.VMEM((D_MODEL, W_BLK), F32)],
        compiler_params=_params(("parallel", "arbitrary")),
    )(ht, dproj)


def _reduce_scatter(g, name):
    _, rows, cols = g.shape
    rchunk = 64

    def body(g_ref, out_ref, own_ref, land1_ref, s1_ref, land2_ref, loc_sems, send1, recv1, send2, recv2):
        x, y, c = _place()
        sibling = (x, y, 1 - c)
        chips = [(x, y), (1 - x, y), (x, 1 - y), (1 - x, 1 - y)]

        def blk(chip, pc):
            return g_ref.at[4 * chip[0] + 2 * chip[1] + pc]

        loads = [pltpu.make_async_copy(blk(chip, c), own_ref.at[r], loc_sems.at[r]) for r, chip in enumerate(chips)]
        to_sib = [pltpu.make_async_remote_copy(src_ref=blk(chip, 1 - c), dst_ref=land1_ref.at[r],
                                               send_sem=send1.at[r], recv_sem=recv1.at[r],
                                               device_id=sibling, device_id_type=MESH)
                  for r, chip in enumerate(chips)]
        order = [1, 2, 3, 0]
        for r in order:
            to_sib[r].start()
            loads[r].start()

        def add_rows(dst_ref, srcs, dtype):
            def step(t, carry):
                sl = pl.ds(pl.multiple_of(t * rchunk, rchunk), rchunk)
                acc = srcs[0][sl, :].astype(F32)
                for src in srcs[1:]:
                    acc = acc + src[sl, :].astype(F32)
                dst_ref[sl, :] = acc.astype(dtype)
                return carry
            lax.fori_loop(0, rows // rchunk, step, 0)

        to_chip = []
        for r in (1, 2, 3):
            loads[r].wait()
            to_sib[r].wait_recv()
            add_rows(s1_ref.at[r - 1], [own_ref.at[r], land1_ref.at[r]], BF16)
            cp = pltpu.make_async_remote_copy(src_ref=s1_ref.at[r - 1], dst_ref=land2_ref.at[r - 1],
                                              send_sem=send2.at[r - 1], recv_sem=recv2.at[r - 1],
                                              device_id=(*chips[r], c), device_id_type=MESH)
            cp.start()
            to_chip.append(cp)
        loads[0].wait()
        to_sib[0].wait_recv()
        for cp in to_chip:
            cp.wait_recv()
        add_rows(out_ref, [own_ref.at[0], land1_ref.at[0], land2_ref.at[0], land2_ref.at[1], land2_ref.at[2]], F32)
        for cp in to_sib + to_chip:
            cp.wait_send()

    return pl.pallas_call(
        body, name=name,
        out_shape=jax.ShapeDtypeStruct((rows, cols), F32),
        in_specs=[pl.BlockSpec(memory_space=pl.ANY)],
        out_specs=pl.BlockSpec(memory_space=pltpu.VMEM),
        scratch_shapes=[pltpu.VMEM((4, rows, cols), BF16), pltpu.VMEM((4, rows, cols), BF16),
                        pltpu.VMEM((3, rows, cols), BF16), pltpu.VMEM((3, rows, cols), BF16),
                        pltpu.SemaphoreType.DMA((4,)), pltpu.SemaphoreType.DMA((4,)), pltpu.SemaphoreType.DMA((4,)),
                        pltpu.SemaphoreType.DMA((3,)), pltpu.SemaphoreType.DMA((3,))],
        compiler_params=_params(),
    )(g)


def _ar_small(parts, dwc, dba, dws):
    n_part = len(parts)

    def body(*refs):
        part_refs = refs[:n_part]
        dwc_ref, dba_ref, dws_ref = refs[n_part:n_part + 3]
        red_ref, wsr_ref, cws_ref, loss_ref = refs[n_part + 3:n_part + 7]
        all1_ref, all2_ref, send_sems, recv_sems = refs[n_part + 7:]
        x, y, c = _place()
        me = 4 * x + 2 * y + c

        all1_ref[me] = jnp.zeros((SMALL_ROWS, C_BR), F32)
        for row, p_ref in zip((ROW_NORM_G, ROW_CONV_B, ROW_CLN_G, ROW_CLN_B, ROW_SLN_G, ROW_SLN_B,
                               ROW_FINAL_G, ROW_LOSS), part_refs):
            all1_ref[me, row:row + 1, :] = jnp.sum(p_ref[...], axis=0, keepdims=True)
        ones = jnp.ones((8, HEAD_DIM), F32)
        brow = [lax.dot_general(ones, dba_ref[:, h * HEAD_DIM:(h + 1) * HEAD_DIM], (((1,), (1,)), ((), ())),
                                precision=lax.Precision.HIGHEST, preferred_element_type=F32)[0:1]
                for h in range(HEADS)]
        all1_ref[me, ROW_B_S:ROW_B_S + 1, :] = jnp.concatenate(brow, axis=1)
        for k in range(CONV_WIDTH):
            all1_ref[me, ROW_CONV_W + k:ROW_CONV_W + k + 1, :] = jnp.sum(
                dwc_ref[8 * k:8 * k + 8, :], axis=0, keepdims=True)
        all2_ref[me] = dws_ref[...]

        copies = []
        for rel in range(1, N_DEV):
            peer = (x ^ (rel >> 2), y ^ ((rel >> 1) & 1), c ^ (rel & 1))
            for a, buf in enumerate((all1_ref, all2_ref)):
                copies.append(pltpu.make_async_remote_copy(
                    src_ref=buf.at[me], dst_ref=buf.at[me],
                    send_sem=send_sems.at[a, rel - 1], recv_sem=recv_sems.at[a, rel - 1],
                    device_id=peer, device_id_type=MESH))
        for cp in copies:
            cp.start()
        for cp in copies:
            cp.wait_recv()

        tot = all1_ref[0]
        for d in range(1, N_DEV):
            tot = tot + all1_ref[d]
        red_ref[...] = tot
        loss_ref[...] = jnp.broadcast_to(jnp.sum(tot[ROW_LOSS:ROW_LOSS + 1, :], axis=1, keepdims=True) * (0.5 / D_MODEL), loss_ref.shape)
        shard = jnp.zeros(cws_ref.shape, F32)
        for d in range(N_DEV):
            shard = jnp.where(me == d, tot[ROW_CONV_W:ROW_CONV_W + CONV_ROWS, d * 128:(d + 1) * 128], shard)
        cws_ref[...] = shard
        tot2 = all2_ref[0]
        for d in range(1, N_DEV):
            tot2 = tot2 + all2_ref[d]
        wsr_ref[...] = tot2
        for cp in copies:
            cp.wait_send()

    vm = pl.BlockSpec(memory_space=pltpu.VMEM)
    return pl.pallas_call(
        body, name="ar_small",
        out_shape=(jax.ShapeDtypeStruct((SMALL_ROWS, C_BR), F32), jax.ShapeDtypeStruct(dws.shape, F32),
                   jax.ShapeDtypeStruct((CONV_ROWS, 128), F32), jax.ShapeDtypeStruct((8, 128), F32)),
        in_specs=[vm] * (n_part + 3), out_specs=(vm, vm, vm, vm),
        scratch_shapes=[pltpu.VMEM((N_DEV, SMALL_ROWS, C_BR), F32), pltpu.VMEM((N_DEV,) + dws.shape, F32),
                        pltpu.SemaphoreType.DMA((2, N_DEV - 1)), pltpu.SemaphoreType.DMA((2, N_DEV - 1))],
        compiler_params=_params(),
    )(*parts, dwc, dba, dws)


def _adamw_math(w, g, m, v):
    m = ADAM_B1 * m + (1.0 - ADAM_B1) * g
    v = ADAM_B2 * v + (1.0 - ADAM_B2) * (g * g)
    m_hat = m / (1.0 - ADAM_B1 ** ADAM_STEP)
    v_hat = v / (1.0 - ADAM_B2 ** ADAM_STEP)
    delta = -ADAM_LR * (m_hat / (jnp.sqrt(v_hat) + ADAM_EPS) + ADAM_WD * w)
    return delta, m, v


def _adamw(g, w, m, v, name, tr):
    rows, cols = w.shape

    def body(g_ref, w_ref, m_ref, v_ref, d_ref, mo_ref, vo_ref):
        d_ref[...], mo_ref[...], vo_ref[...] = _adamw_math(w_ref[...], g_ref[...], m_ref[...], v_ref[...])

    spec = pl.BlockSpec((tr, cols), lambda i: (i, 0))
    sds = jax.ShapeDtypeStruct((rows, cols), F32)
    return pl.pallas_call(
        body, name=name, grid=(rows // tr,), out_shape=(sds, sds, sds),
        in_specs=[spec] * 4, out_specs=(spec, spec, spec),
        compiler_params=_params(("parallel",)),
    )(g, w, m, v)


def _adamw_rows(red, rows, ws, ms, vs):
    n = len(rows)

    def body(*refs):
        red_ref = refs[0]
        w_refs, m_refs, v_refs = refs[1:1 + n], refs[1 + n:1 + 2 * n], refs[1 + 2 * n:1 + 3 * n]
        outs = refs[1 + 3 * n:]
        for t, row in enumerate(rows):
            g = red_ref[row:row + 1, :]
            delta, m, v = _adamw_math(w_refs[t][...], g, m_refs[t][...], v_refs[t][...])
            outs[t][...] = g
            outs[n + t][...] = delta
            outs[2 * n + t][...] = m
            outs[3 * n + t][...] = v

    vm = pl.BlockSpec(memory_space=pltpu.VMEM)
    sds = jax.ShapeDtypeStruct((1, C_BR), F32)
    res = pl.pallas_call(
        body, name="adamw_rows", out_shape=(sds,) * (4 * n),
        in_specs=[vm] * (1 + 3 * n), out_specs=(vm,) * (4 * n),
        compiler_params=_params(),
    )(red, *ws, *ms, *vs)
    return res[:n], res[n:2 * n], res[2 * n:3 * n], res[3 * n:]


def kernel(x, norm_g, w_in, conv_w, conv_b, conv_ln_g, conv_ln_b, sgu_ln_g, sgu_ln_b, w_s, b_s, w_out, final_g, loss_target, m_norm_g, m_w_in, m_conv_w, m_conv_b, m_conv_ln_g, m_conv_ln_b, m_sgu_ln_g, m_sgu_ln_b, m_w_s, m_b_s, m_w_out, m_final_g, v_norm_g, v_w_in, v_conv_w, v_conv_b, v_conv_ln_g, v_conv_ln_b, v_sgu_ln_g, v_sgu_ln_b, v_w_s, v_b_s, v_w_out, v_final_g):
    s = x.shape[1]
    xs = x.reshape(s, D_MODEL)
    tgt = loss_target.reshape(s, D_MODEL)
    tm = min(256, s)

    cw_pad = jnp.pad(conv_w[0], ((0, CONV_ROWS - CONV_WIDTH), (0, 0)))
    win_all, wout_all, cw_all = _ag_weights(w_in[0], w_out[0], cw_pad)
    wout_full = wout_all.reshape(2 * C_BR, D_MODEL)
    cw_full = jnp.transpose(cw_all, (1, 0, 2)).reshape(CONV_ROWS, C_BR)

    ws = w_s[0].astype(BF16)
    wst = jnp.transpose(w_s[0], (0, 2, 1)).astype(BF16)
    bsb = jnp.repeat(jnp.transpose(b_s[0]), HEAD_DIM, axis=1)
    fg = final_g.reshape(1, D_MODEL)

    proj, ht = _proj(xs, norm_g, win_all, tm)
    y, cv = _conv_fwd(proj, cw_full, conv_b, conv_ln_g, conv_ln_b, tm)
    y = _sgu_fwd(proj, y, sgu_ln_g, sgu_ln_b, ws, bsb, tm)
    dx2, dy, dwout, loss_p, dfg_p = _out_loss(xs, y, wout_full, fg, tgt, tm)
    dproj, dwc, dcb_p, dclg_p, dclb_p = _conv_bwd(proj, cv, dy, cw_full, conv_ln_g, conv_ln_b, tm)
    dproj, dws, dba, dslg_p, dslb_p = _sgu_bwd(proj, dy, dproj, sgu_ln_g, sgu_ln_b, ws, wst, bsb, tm)
    grad_x, dng_p = _dx(dproj, win_all, xs, norm_g, dx2, tm)
    dwin = _dwin(ht, dproj, min(512, s))

    g_w_in = _reduce_scatter(dwin, "rs_w_in")
    g_w_out = _reduce_scatter(dwout.reshape(N_DEV, 2 * C_BR // N_DEV, D_MODEL), "rs_w_out")
    red, g_w_s, g_cw, loss8 = _ar_small(
        [dng_p, dcb_p, dclg_p, dclb_p, dslg_p, dslb_p, dfg_p, loss_p], dwc, dba,
        dws.reshape(HEADS * CHUNK, CHUNK))
    loss = loss8[0, 0]

    d_w_in, nm_w_in, nv_w_in = _adamw(g_w_in, w_in[0], m_w_in[0], v_w_in[0], "adamw_w_in", 256)
    d_w_out, nm_w_out, nv_w_out = _adamw(g_w_out, w_out[0], m_w_out[0], v_w_out[0], "adamw_w_out", 256)
    g_cw = g_cw[:CONV_WIDTH]
    d_cw, nm_cw, nv_cw = _adamw(g_cw, conv_w[0], m_conv_w[0], v_conv_w[0], "adamw_conv_w", CONV_WIDTH)
    flat = lambda a: a.reshape(HEADS * CHUNK, CHUNK)
    d_ws, nm_ws, nv_ws = _adamw(g_w_s, flat(w_s), flat(m_w_s), flat(v_w_s), "adamw_w_s", HEADS * CHUNK)
    row = lambda a: a.reshape(1, C_BR)
    rows = (ROW_NORM_G, ROW_CONV_B, ROW_CLN_G, ROW_CLN_B, ROW_SLN_G, ROW_SLN_B, ROW_B_S, ROW_FINAL_G)
    g_r, d_r, m_r, v_r = _adamw_rows(
        red, rows,
        [norm_g, conv_b, conv_ln_g, conv_ln_b, sgu_ln_g, sgu_ln_b, row(b_s), row(final_g)],
        [m_norm_g, m_conv_b, m_conv_ln_g, m_conv_ln_b, m_sgu_ln_g, m_sgu_ln_b, row(m_b_s), row(m_final_g)],
        [v_norm_g, v_conv_b, v_conv_ln_g, v_conv_ln_b, v_sgu_ln_g, v_sgu_ln_b, row(v_b_s), row(v_final_g)])

    def leaves(r, w_in_l, cw_l, ws_l, w_out_l):
        return (r[0], w_in_l[None], cw_l[None], r[1], r[2], r[3], r[4], r[5],
                ws_l.reshape(1, HEADS, CHUNK, CHUNK), r[6].reshape(1, HEADS, CHUNK), w_out_l[None],
                r[7].reshape(D_MODEL))

    return (loss, grad_x.reshape(1, s, D_MODEL),
            *leaves(g_r, g_w_in, g_cw, g_w_s, g_w_out),
            *leaves(d_r, d_w_in, d_cw, d_ws, d_w_out),
            *leaves(m_r, nm_w_in, nm_cw, nm_ws, nm_w_out),
            *leaves(v_r, nv_w_in, nv_cw, nv_ws, nv_w_out))
```

```python
import functools

import jax
import jax.numpy as jnp
from jax import lax
from jax.experimental import pallas as pl
from jax.experimental.pallas import tpu as pltpu

F32 = jnp.float32
BF16 = jnp.bfloat16
ACT = jnp.bfloat16

D_MODEL = 1024
C_BR = 1024
D_IN = 6 * C_BR
N_DEV = 8
W_BLK = D_IN // N_DEV
HEADS = 8
HEAD_DIM = 128
CHUNK = 128
CONV_WIDTH = 31
CONV_PAD = CONV_WIDTH // 2
HALO = 16
CONV_ROWS = 32
EPS = 1e-6

ADAM_LR = 0.001
ADAM_B1 = 0.9
ADAM_B2 = 0.999
ADAM_EPS = 1e-08
ADAM_WD = 0.01
ADAM_STEP = 10

VMEM_LIMIT = 56 * 1024 * 1024
MESH = pl.DeviceIdType.MESH

ROW_NORM_G, ROW_CONV_B, ROW_CLN_G, ROW_CLN_B, ROW_SLN_G, ROW_SLN_B, ROW_FINAL_G, ROW_B_S, ROW_LOSS = range(9)
ROW_CONV_W = 16
SMALL_ROWS = ROW_CONV_W + CONV_ROWS


def _params(sem=None, **kw):
    return pltpu.CompilerParams(dimension_semantics=sem, vmem_limit_bytes=VMEM_LIMIT, **kw)


def _fold8(a):
    r, n = a.shape
    return a.reshape(r // 8, 8, n).sum(axis=0)


def _sigmoid(z):
    return 1.0 / (1.0 + jnp.exp(-z))


def _ln_norm(xf):
    mu = jnp.mean(xf, axis=-1, keepdims=True)
    xc = xf - mu
    var = jnp.mean(xc * xc, axis=-1, keepdims=True)
    rstd = lax.rsqrt(var + EPS)
    return xc * rstd, rstd


def _ln_bwd(dy, xhat, rstd, g):
    dxhat = dy * g
    m1 = jnp.mean(dxhat, axis=-1, keepdims=True)
    m2 = jnp.mean(dxhat * xhat, axis=-1, keepdims=True)
    return rstd * (dxhat - m1 - xhat * m2)


def _place():
    return lax.axis_index("x"), lax.axis_index("y"), lax.axis_index("c")


def _ag_weights(w_in, w_out, conv_w):
    def body(win_ref, wout_ref, cw_ref, win_all, wout_all, cw_all, send_sems, recv_sems):
        x, y, c = _place()
        me, sibling = (x, y, c), (x, y, 1 - c)
        chips = [(1 - x, y), (x, 1 - y), (1 - x, 1 - y)]
        outs = (win_all, wout_all, cw_all)

        def blk(a, px, py, pc):
            return outs[a].at[4 * px + 2 * py + pc]

        def copy(a, k, block, to):
            return pltpu.make_async_remote_copy(
                src_ref=blk(a, *block), dst_ref=blk(a, *block),
                send_sem=send_sems.at[a, k], recv_sem=recv_sems.at[a, k],
                device_id=to, device_id_type=MESH)

        mine = 4 * x + 2 * y + c
        win_all[mine] = win_ref[...].astype(BF16)
        wout_all[mine] = wout_ref[...].astype(BF16)
        cw_all[mine] = cw_ref[...]

        arrays = range(3)
        first = []
        for a in arrays:
            first.append(copy(a, 0, me, sibling))
            first += [copy(a, 1 + j, me, (*chip, c)) for j, chip in enumerate(chips)]
        for cp in first:
            cp.start()
        passed = []
        for j, chip in enumerate(chips):
            for a in arrays:
                copy(a, 1 + j, (*chip, c), me).wait_recv()
                fwd = copy(a, 4 + j, (*chip, c), sibling)
                fwd.start()
                passed.append(fwd)
        for a in arrays:
            copy(a, 0, sibling, me).wait_recv()
            for j, chip in enumerate(chips):
                copy(a, 4 + j, (*chip, 1 - c), me).wait_recv()
        for cp in first + passed:
            cp.wait_send()

    vm = pl.BlockSpec(memory_space=pltpu.VMEM)
    return pl.pallas_call(
        body, name="ag_weights",
        out_shape=(jax.ShapeDtypeStruct((N_DEV,) + w_in.shape, BF16),
                   jax.ShapeDtypeStruct((N_DEV,) + w_out.shape, BF16),
                   jax.ShapeDtypeStruct((N_DEV,) + conv_w.shape, F32)),
        in_specs=[vm, vm, vm], out_specs=(vm, vm, vm),
        scratch_shapes=[pltpu.SemaphoreType.DMA((3, 7)), pltpu.SemaphoreType.DMA((3, 7))],
        compiler_params=_params(),
    )(w_in, w_out, conv_w)


def _proj(x, norm_g, win_all, tm):
    s = x.shape[0]

    def body(x_ref, g_ref, w_ref, proj_ref, ht_ref):
        xf = x_ref[...]
        r = lax.rsqrt(jnp.mean(xf * xf, axis=-1, keepdims=True) + EPS)
        hf = xf * r * g_ref[...]
        h = hf.astype(BF16)
        ht_ref[...] = hf.T.astype(BF16)
        for j in range(N_DEV):
            proj_ref[:, j * W_BLK:(j + 1) * W_BLK] = jnp.dot(
                h, w_ref[j], preferred_element_type=F32).astype(ACT)

    return pl.pallas_call(
        body, name="proj", grid=(s // tm,),
        out_shape=(jax.ShapeDtypeStruct((s, D_IN), ACT), jax.ShapeDtypeStruct((D_MODEL, s), BF16)),
        in_specs=[pl.BlockSpec((tm, D_MODEL), lambda i: (i, 0)),
                  pl.BlockSpec((1, D_MODEL), lambda i: (0, 0)),
                  pl.BlockSpec((N_DEV, D_MODEL, W_BLK), lambda i: (0, 0, 0))],
        out_specs=(pl.BlockSpec((tm, D_IN), lambda i: (i, 0)),
                   pl.BlockSpec((D_MODEL, tm), lambda i: (0, i))),
        compiler_params=_params(("parallel",)),
    )(x, norm_g, win_all)


def _halo_specs(tm, s, col):
    per = tm // HALO
    last = s // HALO - 1
    return [pl.BlockSpec((HALO, C_BR), lambda i: (jnp.maximum(i * per - 1, 0), col)),
            pl.BlockSpec((tm, C_BR), lambda i: (i, col)),
            pl.BlockSpec((HALO, C_BR), lambda i: (jnp.minimum((i + 1) * per, last), col))]


PHASE_ROWS_LESS = 8


def _fill_phases(ext_ref, ph_ref):
    n = ext_ref.shape[0] - PHASE_ROWS_LESS
    for b in range(1, 8):
        ph_ref[b - 1] = ext_ref[b:b + n, :]


def _for_taps(ext_ref, ph_ref, tm, rb, offset, visit):
    by_phase = {}
    for k in range(CONV_WIDTH):
        a, b = divmod(offset(k), 8)
        by_phase.setdefault(b, []).append((k, a))

    def taps(l0, r0):
        for b, ks in by_phase.items():
            amax = max(a for _, a in ks)
            src = ext_ref if b == 0 else ph_ref.at[b - 1]
            big = src[pl.ds(r0, rb + 8 * amax), l0:l0 + 128]
            for k, a in ks:
                yield k, big[8 * a:8 * a + rb]

    for l0 in range(0, C_BR, 128):
        def step(t, carry, l0=l0):
            r0 = pl.multiple_of(t * rb, rb)
            visit(l0, r0, taps(l0, r0))
            return carry

        lax.fori_loop(0, tm // rb, step, 0)


def _conv_fwd(proj, conv_w, conv_b, ln_g, ln_b, tm):
    s = proj.shape[0]
    nt = s // tm
    rb = 64

    def body(av_p, av_m, av_n, ag_p, ag_m, ag_n, gc_ref, w_ref, cb_ref, lg_ref, lb_ref,
             y_ref, c_ref, ext_ref, ph_ref, cv_ref):
        i = pl.program_id(0)

        def glu(a_ref, g_ref):
            return a_ref[...].astype(F32) * _sigmoid(g_ref[...].astype(F32))

        ext_ref[0:HALO, :] = jnp.where(i > 0, glu(av_p, ag_p), 0.0)
        ext_ref[HALO:HALO + tm, :] = glu(av_m, ag_m)
        ext_ref[HALO + tm:, :] = jnp.where(i < nt - 1, glu(av_n, ag_n), 0.0)
        _fill_phases(ext_ref, ph_ref)

        def visit(l0, r0, taps):
            lanes = slice(l0, l0 + 128)
            accs = [cb_ref[:, lanes], None]
            for n, (k, slab) in enumerate(taps):
                term = w_ref[k:k + 1, lanes] * slab
                accs[n % 2] = term if accs[n % 2] is None else accs[n % 2] + term
            cv_ref[pl.ds(r0, rb), lanes] = accs[0] + accs[1]

        _for_taps(ext_ref, ph_ref, tm, rb, lambda k: k + HALO - CONV_PAD, visit)
        cv = cv_ref[...]
        c_ref[...] = cv.astype(ACT)
        xhat, _ = _ln_norm(cv)
        ln = xhat * lg_ref[...] + lb_ref[...]
        gc = gc_ref[...].astype(F32)
        y_ref[...] = (ln * _sigmoid(ln) * (gc * _sigmoid(gc))).astype(ACT)

    vec = pl.BlockSpec((1, C_BR), lambda i: (0, 0))
    return pl.pallas_call(
        body, name="conv_fwd", grid=(nt,),
        out_shape=(jax.ShapeDtypeStruct((s, 2 * C_BR), ACT), jax.ShapeDtypeStruct((s, C_BR), ACT)),
        in_specs=_halo_specs(tm, s, 0) + _halo_specs(tm, s, 1)
        + [pl.BlockSpec((tm, C_BR), lambda i: (i, 2)),
           pl.BlockSpec((CONV_ROWS, C_BR), lambda i: (0, 0)), vec, vec, vec],
        out_specs=(pl.BlockSpec((tm, C_BR), lambda i: (i, 0)), pl.BlockSpec((tm, C_BR), lambda i: (i, 0))),
        scratch_shapes=[pltpu.VMEM((tm + 2 * HALO, C_BR), F32),
                        pltpu.VMEM((7, tm + 2 * HALO - PHASE_ROWS_LESS, C_BR), F32),
                        pltpu.VMEM((tm, C_BR), F32)],
        compiler_params=_params(("parallel",)),
    )(proj, proj, proj, proj, proj, proj, proj, conv_w, conv_b, ln_g, ln_b)


def _sgu_fwd(proj, y, ln_g, ln_b, ws, bsb, tm):
    s = proj.shape[0]

    def body(u_ref, v_ref, gs_ref, y_in, lg_ref, lb_ref, ws_ref, bsb_ref, y_ref):
        del y_in
        xhat, _ = _ln_norm(v_ref[...].astype(F32))
        vn = (xhat * lg_ref[...] + lb_ref[...]).astype(BF16)
        for cidx in range(tm // CHUNK):
            rows = slice(cidx * CHUNK, (cidx + 1) * CHUNK)
            for h in range(HEADS):
                cols = slice(h * HEAD_DIM, (h + 1) * HEAD_DIM)
                mixed = jnp.dot(ws_ref[h], vn[rows, cols], preferred_element_type=F32) + bsb_ref[:, cols]
                gs = gs_ref[rows, cols].astype(F32)
                y_ref[rows, cols] = (u_ref[rows, cols].astype(F32) * mixed * (gs * _sigmoid(gs))).astype(ACT)

    vec = pl.BlockSpec((1, C_BR), lambda i: (0, 0))
    return pl.pallas_call(
        body, name="sgu_fwd", grid=(s // tm,),
        out_shape=jax.ShapeDtypeStruct((s, 2 * C_BR), ACT),
        in_specs=[pl.BlockSpec((tm, C_BR), lambda i: (i, 3)),
                  pl.BlockSpec((tm, C_BR), lambda i: (i, 4)),
                  pl.BlockSpec((tm, C_BR), lambda i: (i, 5)),
                  pl.BlockSpec(memory_space=pl.ANY),
                  vec, vec,
                  pl.BlockSpec((HEADS, CHUNK, CHUNK), lambda i: (0, 0, 0)),
                  pl.BlockSpec((CHUNK, C_BR), lambda i: (0, 0))],
        out_specs=pl.BlockSpec((tm, C_BR), lambda i: (i, 1)),
        input_output_aliases={3: 0},
        compiler_params=_params(("parallel",)),
    )(proj, proj, proj, y, ln_g, ln_b, ws, bsb)


def _out_loss(x, y, wout, final_g, target, tm):
    s = x.shape[0]
    nt = s // tm
    inv_d = 1.0 / D_MODEL

    def body(x_ref, y_ref, w_ref, g_ref, t_ref, dx2_ref, dy_ref, dw_ref, loss_ref, dfg_ref, acc_ref):
        i = pl.program_id(0)

        @pl.when(i == 0)
        def _():
            acc_ref[...] = jnp.zeros_like(acc_ref)
            loss_ref[...] = jnp.zeros_like(loss_ref)
            dfg_ref[...] = jnp.zeros_like(dfg_ref)

        yb = y_ref[...]
        x2 = x_ref[...] + jnp.dot(yb, w_ref[...], preferred_element_type=F32)
        r2 = lax.rsqrt(jnp.mean(x2 * x2, axis=-1, keepdims=True) + EPS)
        n = x2 * r2
        g = g_ref[...]
        e = n * g - t_ref[...]
        loss_ref[...] += _fold8(e * e)
        dout = e * inv_d
        dfg_ref[...] += _fold8(dout * n)
        dn = dout * g
        dx2 = r2 * (dn - n * jnp.mean(dn * n, axis=-1, keepdims=True))
        dx2_ref[...] = dx2
        dxb = dx2.astype(BF16)
        dy_ref[...] = lax.dot_general(dxb, w_ref[...], (((1,), (1,)), ((), ())),
                                      preferred_element_type=F32).astype(ACT)
        acc_ref[...] += lax.dot_general(yb, dxb, (((0,), (0,)), ((), ())), preferred_element_type=F32)

        @pl.when(i == nt - 1)
        def _():
            dw_ref[...] = acc_ref[...].astype(BF16)

    part = pl.BlockSpec((8, D_MODEL), lambda i: (0, 0))
    return pl.pallas_call(
        body, name="out_loss", grid=(nt,),
        out_shape=(jax.ShapeDtypeStruct((s, D_MODEL), F32), jax.ShapeDtypeStruct((s, 2 * C_BR), ACT),
                   jax.ShapeDtypeStruct((2 * C_BR, D_MODEL), BF16),
                   jax.ShapeDtypeStruct((8, D_MODEL), F32), jax.ShapeDtypeStruct((8, D_MODEL), F32)),
        in_specs=[pl.BlockSpec((tm, D_MODEL), lambda i: (i, 0)),
                  pl.BlockSpec((tm, 2 * C_BR), lambda i: (i, 0)),
                  pl.BlockSpec((2 * C_BR, D_MODEL), lambda i: (0, 0), pipeline_mode=pl.Buffered(1)),
                  pl.BlockSpec((1, D_MODEL), lambda i: (0, 0)),
                  pl.BlockSpec((tm, D_MODEL), lambda i: (i, 0))],
        out_specs=(pl.BlockSpec((tm, D_MODEL), lambda i: (i, 0)),
                   pl.BlockSpec((tm, 2 * C_BR), lambda i: (i, 0)),
                   pl.BlockSpec((2 * C_BR, D_MODEL), lambda i: (0, 0), pipeline_mode=pl.Buffered(1)), part, part),
        scratch_shapes=[pltpu.VMEM((2 * C_BR, D_MODEL), F32)],
        compiler_params=_params(("arbitrary",)),
    )(x, y, wout, final_g, target)


def _conv_bwd(proj, cv, dy, conv_w, ln_g, ln_b, tm):
    s = proj.shape[0]
    nt = s // tm
    rb = 64

    def body(av_ref, ag_ref, gc_p, gc_m, gc_n, c_p, c_m, c_n, dy_p, dy_m, dy_n, w_ref, lg_ref, lb_ref,
             dp_ref, dwc_ref, dcb_ref, dlg_ref, dlb_ref, dce_ref, ph_ref):
        i = pl.program_id(0)

        @pl.when(i == 0)
        def _():
            dwc_ref[...] = jnp.zeros_like(dwc_ref)
            dcb_ref[...] = jnp.zeros_like(dcb_ref)
            dlg_ref[...] = jnp.zeros_like(dlg_ref)
            dlb_ref[...] = jnp.zeros_like(dlb_ref)

        def ext(p, m, n):
            return jnp.concatenate([p[...], m[...], n[...]], axis=0).astype(F32)

        main = slice(HALO, HALO + tm)
        cf, gc, dyc = ext(c_p, c_m, c_n), ext(gc_p, gc_m, gc_n), ext(dy_p, dy_m, dy_n)
        xhat, rstd = _ln_norm(cf)
        lg = lg_ref[...]
        ln = xhat * lg + lb_ref[...]
        s_ln, s_gc = _sigmoid(ln), _sigmoid(gc)
        dln = dyc * (gc * s_gc) * (s_ln * (1.0 + ln * (1.0 - s_ln)))
        dp_ref[:, 2 * C_BR:] = (dyc[main] * (ln[main] * s_ln[main])
                                * (s_gc[main] * (1.0 + gc[main] * (1.0 - s_gc[main])))).astype(ACT)
        dlg_ref[...] += _fold8(dln[main] * xhat[main])
        dlb_ref[...] += _fold8(dln[main])
        dc = _ln_bwd(dln, xhat, rstd, lg)
        dcb_ref[...] += _fold8(dc[main])
        dce_ref[0:HALO, :] = jnp.where(i > 0, dc[0:HALO], 0.0)
        dce_ref[main, :] = dc[main]
        dce_ref[HALO + tm:, :] = jnp.where(i < nt - 1, dc[HALO + tm:], 0.0)
        _fill_phases(dce_ref, ph_ref)


        def visit(l0, r0, taps):
            rows, lanes = pl.ds(r0, rb), slice(l0, l0 + 128)
            av = av_ref[rows, lanes].astype(F32)
            sa = _sigmoid(ag_ref[rows, lanes].astype(F32))
            glu_blk = av * sa
            accs = [None, None]
            for n, (k, slab) in enumerate(taps):
                term = w_ref[k:k + 1, lanes] * slab
                accs[n % 2] = term if accs[n % 2] is None else accs[n % 2] + term
                dwc_ref[8 * k:8 * k + 8, lanes] += _fold8(glu_blk * slab)
            acc = accs[0] + accs[1]
            dp_ref[rows, lanes] = (acc * sa).astype(ACT)
            dp_ref[rows, C_BR + l0:C_BR + l0 + 128] = (acc * av * sa * (1.0 - sa)).astype(ACT)

        _for_taps(dce_ref, ph_ref, tm, rb, lambda k: HALO + CONV_PAD - k, visit)

    vec = pl.BlockSpec((1, C_BR), lambda i: (0, 0))
    part = pl.BlockSpec((8, C_BR), lambda i: (0, 0))
    return pl.pallas_call(
        body, name="conv_bwd", grid=(nt,),
        out_shape=(jax.ShapeDtypeStruct((s, D_IN), ACT), jax.ShapeDtypeStruct((8 * CONV_ROWS, C_BR), F32),
                   jax.ShapeDtypeStruct((8, C_BR), F32), jax.ShapeDtypeStruct((8, C_BR), F32),
                   jax.ShapeDtypeStruct((8, C_BR), F32)),
        in_specs=[pl.BlockSpec((tm, C_BR), lambda i: (i, 0)), pl.BlockSpec((tm, C_BR), lambda i: (i, 1))]
        + _halo_specs(tm, s, 2) + _halo_specs(tm, s, 0) + _halo_specs(tm, s, 0)
        + [pl.BlockSpec((CONV_ROWS, C_BR), lambda i: (0, 0)), vec, vec],
        out_specs=(pl.BlockSpec((tm, 3 * C_BR), lambda i: (i, 0)),
                   pl.BlockSpec((8 * CONV_ROWS, C_BR), lambda i: (0, 0)), part, part, part),
        scratch_shapes=[pltpu.VMEM((tm + 2 * HALO, C_BR), F32),
                        pltpu.VMEM((7, tm + 2 * HALO - PHASE_ROWS_LESS, C_BR), F32)],
        compiler_params=_params(("arbitrary",)),
    )(proj, proj, proj, proj, proj, cv, cv, cv, dy, dy, dy, conv_w, ln_g, ln_b)


def _sgu_bwd(proj, dy, dproj, ln_g, ln_b, ws, wst, bsb, tm):
    s = proj.shape[0]

    def body(u_ref, v_ref, gs_ref, dy_ref, dp_in, lg_ref, lb_ref, ws_ref, wst_ref, bsb_ref,
             dp_ref, dws_ref, dba_ref, dlg_ref, dlb_ref, dvn_ref):
        del dp_in
        i = pl.program_id(0)

        @pl.when(i == 0)
        def _():
            dws_ref[...] = jnp.zeros_like(dws_ref)
            dba_ref[...] = jnp.zeros_like(dba_ref)
            dlg_ref[...] = jnp.zeros_like(dlg_ref)
            dlb_ref[...] = jnp.zeros_like(dlb_ref)

        xhat, rstd = _ln_norm(v_ref[...].astype(F32))
        lg = lg_ref[...]
        vn = (xhat * lg + lb_ref[...]).astype(BF16)
        for cidx in range(tm // CHUNK):
            rows = slice(cidx * CHUNK, (cidx + 1) * CHUNK)
            for h in range(HEADS):
                cols = slice(h * HEAD_DIM, (h + 1) * HEAD_DIM)
                ocols = slice(C_BR + h * HEAD_DIM, C_BR + (h + 1) * HEAD_DIM)
                gcols = slice(2 * C_BR + h * HEAD_DIM, 2 * C_BR + (h + 1) * HEAD_DIM)
                vb = vn[rows, cols]
                mixed = jnp.dot(ws_ref[h], vb, preferred_element_type=F32) + bsb_ref[:, cols]
                gs = gs_ref[rows, cols].astype(F32)
                sg = _sigmoid(gs)
                u = u_ref[rows, cols].astype(F32)
                dyb = dy_ref[rows, cols].astype(F32)
                t = dyb * (gs * sg)
                dp_ref[rows, cols] = (t * mixed).astype(ACT)
                dp_ref[rows, gcols] = (dyb * u * mixed * (sg * (1.0 + gs * (1.0 - sg)))).astype(ACT)
                dm = t * u
                dmb = dm.astype(BF16)
                dvn_ref[rows, cols] = jnp.dot(wst_ref[h], dmb, preferred_element_type=F32)
                dws_ref[h] += lax.dot_general(dmb, vb, (((1,), (1,)), ((), ())), preferred_element_type=F32)
                dba_ref[:, cols] += dm
        dvn = dvn_ref[...]
        dlg_ref[...] += _fold8(dvn * xhat)
        dlb_ref[...] += _fold8(dvn)
        dp_ref[:, C_BR:2 * C_BR] = _ln_bwd(dvn, xhat, rstd, lg).astype(ACT)

    vec = pl.BlockSpec((1, C_BR), lambda i: (0, 0))
    part = pl.BlockSpec((8, C_BR), lambda i: (0, 0))
    wsp = pl.BlockSpec((HEADS, CHUNK, CHUNK), lambda i: (0, 0, 0))
    return pl.pallas_call(
        body, name="sgu_bwd", grid=(s // tm,),
        out_shape=(jax.ShapeDtypeStruct((s, D_IN), ACT), jax.ShapeDtypeStruct((HEADS, CHUNK, CHUNK), F32),
                   jax.ShapeDtypeStruct((CHUNK, C_BR), F32), jax.ShapeDtypeStruct((8, C_BR), F32),
                   jax.ShapeDtypeStruct((8, C_BR), F32)),
        in_specs=[pl.BlockSpec((tm, C_BR), lambda i: (i, 3)),
                  pl.BlockSpec((tm, C_BR), lambda i: (i, 4)),
                  pl.BlockSpec((tm, C_BR), lambda i: (i, 5)),
                  pl.BlockSpec((tm, C_BR), lambda i: (i, 1)),
                  pl.BlockSpec(memory_space=pl.ANY),
                  vec, vec, wsp, wsp, pl.BlockSpec((CHUNK, C_BR), lambda i: (0, 0))],
        out_specs=(pl.BlockSpec((tm, 3 * C_BR), lambda i: (i, 1)), wsp,
                   pl.BlockSpec((CHUNK, C_BR), lambda i: (0, 0)), part, part),
        scratch_shapes=[pltpu.VMEM((tm, C_BR), F32)],
        input_output_aliases={4: 0},
        compiler_params=_params(("arbitrary",)),
    )(proj, proj, proj, dy, dproj, ln_g, ln_b, ws, wst, bsb)


def _dx(dproj, win_all, x, norm_g, dx2, tm):
    s = x.shape[0]

    def body(dp_ref, w_ref, x_ref, g_ref, dx2_ref, gx_ref, dng_ref):
        i = pl.program_id(0)

        @pl.when(i == 0)
        def _():
            dng_ref[...] = jnp.zeros_like(dng_ref)

        dh = None
        for j in range(N_DEV):
            term = lax.dot_general(dp_ref[:, j * W_BLK:(j + 1) * W_BLK], w_ref[j],
                                   (((1,), (1,)), ((), ())), preferred_element_type=F32)
            dh = term if dh is None else dh + term
        xf = x_ref[...]
        r = lax.rsqrt(jnp.mean(xf * xf, axis=-1, keepdims=True) + EPS)
        n = xf * r
        dng_ref[...] += _fold8(dh * n)
        dn = dh * g_ref[...]
        gx_ref[...] = dx2_ref[...] + r * (dn - n * jnp.mean(dn * n, axis=-1, keepdims=True))

    return pl.pallas_call(
        body, name="dx", grid=(s // tm,),
        out_shape=(jax.ShapeDtypeStruct((s, D_MODEL), F32), jax.ShapeDtypeStruct((8, D_MODEL), F32)),
        in_specs=[pl.BlockSpec((tm, D_IN), lambda i: (i, 0)),
                  pl.BlockSpec((N_DEV, D_MODEL, W_BLK), lambda i: (0, 0, 0)),
                  pl.BlockSpec((tm, D_MODEL), lambda i: (i, 0)),
                  pl.BlockSpec((1, D_MODEL), lambda i: (0, 0)),
                  pl.BlockSpec((tm, D_MODEL), lambda i: (i, 0))],
        out_specs=(pl.BlockSpec((tm, D_MODEL), lambda i: (i, 0)), pl.BlockSpec((8, D_MODEL), lambda i: (0, 0))),
        compiler_params=_params(("arbitrary",)),
    )(dproj, win_all, x, norm_g, dx2)


def _dwin(ht, dproj, tk):
    s = ht.shape[1]
    nk = s // tk

    def body(ht_ref, dp_ref, dw_ref, acc_ref):
        k = pl.program_id(1)

        @pl.when(k == 0)
        def _():
            acc_ref[...] = jnp.zeros_like(acc_ref)

        acc_ref[...] += jnp.dot(ht_ref[...], dp_ref[...], preferred_element_type=F32)

        @pl.when(k == nk - 1)
        def _():
            dw_ref[0] = acc_ref[...].astype(BF16)

    return pl.pallas_call(
        body, name="dwin", grid=(N_DEV, nk),
        out_shape=jax.ShapeDtypeStruct((N_DEV, D_MODEL, W_BLK), BF16),
        in_specs=[pl.BlockSpec((D_MODEL, tk), lambda j, k: (0, k)),
                  pl.BlockSpec((tk, W_BLK), lambda j, k: (k, j))],
        out_specs=pl.BlockSpec((1, D_MODEL, W_BLK), lambda j, k: (j, 0, 0)),
        scratch_shapes=[pltpu.VMEM((D_MODEL, W_BLK), F32)],
        compiler_params=_params(("parallel", "arbitrary")),
    )(ht, dproj)


def _reduce_scatter(g, name):
    _, rows, cols = g.shape
    rchunk = 64

    def body(g_ref, out_ref, own_ref, land1_ref, s1_ref, land2_ref, loc_sems, send1, recv1, send2, recv2):
        x, y, c = _place()
        sibling = (x, y, 1 - c)
        chips = [(x, y), (1 - x, y), (x, 1 - y), (1 - x, 1 - y)]

        def blk(chip, pc):
            return g_ref.at[4 * chip[0] + 2 * chip[1] + pc]

        loads = [pltpu.make_async_copy(blk(chip, c), own_ref.at[r], loc_sems.at[r]) for r, chip in enumerate(chips)]
        to_sib = [pltpu.make_async_remote_copy(src_ref=blk(chip, 1 - c), dst_ref=land1_ref.at[r],
                                               send_sem=send1.at[r], recv_sem=recv1.at[r],
                                               device_id=sibling, device_id_type=MESH)
                  for r, chip in enumerate(chips)]
        order = [1, 2, 3, 0]
        for r in order:
            to_sib[r].start()
            loads[r].start()

        def add_rows(dst_ref, srcs, dtype):
            def step(t, carry):
                sl = pl.ds(pl.multiple_of(t * rchunk, rchunk), rchunk)
                acc = srcs[0][sl, :].astype(F32)
                for src in srcs[1:]:
                    acc = acc + src[sl, :].astype(F32)
                dst_ref[sl, :] = acc.astype(dtype)
                return carry
            lax.fori_loop(0, rows // rchunk, step, 0)

        to_chip = []
        for r in (1, 2, 3):
            loads[r].wait()
            to_sib[r].wait_recv()
            add_rows(s1_ref.at[r - 1], [own_ref.at[r], land1_ref.at[r]], BF16)
            cp = pltpu.make_async_remote_copy(src_ref=s1_ref.at[r - 1], dst_ref=land2_ref.at[r - 1],
                                              send_sem=send2.at[r - 1], recv_sem=recv2.at[r - 1],
                                              device_id=(*chips[r], c), device_id_type=MESH)
            cp.start()
            to_chip.append(cp)
        loads[0].wait()
        to_sib[0].wait_recv()
        for cp in to_chip:
            cp.wait_recv()
        add_rows(out_ref, [own_ref.at[0], land1_ref.at[0], land2_ref.at[0], land2_ref.at[1], land2_ref.at[2]], F32)
        for cp in to_sib + to_chip:
            cp.wait_send()

    return pl.pallas_call(
        body, name=name,
        out_shape=jax.ShapeDtypeStruct((rows, cols), F32),
        in_specs=[pl.BlockSpec(memory_space=pl.ANY)],
        out_specs=pl.BlockSpec(memory_space=pltpu.VMEM),
        scratch_shapes=[pltpu.VMEM((4, rows, cols), BF16), pltpu.VMEM((4, rows, cols), BF16),
                        pltpu.VMEM((3, rows, cols), BF16), pltpu.VMEM((3, rows, cols), BF16),
                        pltpu.SemaphoreType.DMA((4,)), pltpu.SemaphoreType.DMA((4,)), pltpu.SemaphoreType.DMA((4,)),
                        pltpu.SemaphoreType.DMA((3,)), pltpu.SemaphoreType.DMA((3,))],
        compiler_params=_params(),
    )(g)


def _ar_small(parts, dwc, dba, dws):
    n_part = len(parts)

    def body(*refs):
        part_refs = refs[:n_part]
        dwc_ref, dba_ref, dws_ref = refs[n_part:n_part + 3]
        red_ref, wsr_ref, cws_ref, loss_ref = refs[n_part + 3:n_part + 7]
        all1_ref, all2_ref, send_sems, recv_sems = refs[n_part + 7:]
        x, y, c = _place()
        me = 4 * x + 2 * y + c

        all1_ref[me] = jnp.zeros((SMALL_ROWS, C_BR), F32)
        for row, p_ref in zip((ROW_NORM_G, ROW_CONV_B, ROW_CLN_G, ROW_CLN_B, ROW_SLN_G, ROW_SLN_B,
                               ROW_FINAL_G, ROW_LOSS), part_refs):
            all1_ref[me, row:row + 1, :] = jnp.sum(p_ref[...], axis=0, keepdims=True)
        ones = jnp.ones((8, HEAD_DIM), F32)
        brow = [lax.dot_general(ones, dba_ref[:, h * HEAD_DIM:(h + 1) * HEAD_DIM], (((1,), (1,)), ((), ())),
                                precision=lax.Precision.HIGHEST, preferred_element_type=F32)[0:1]
                for h in range(HEADS)]
        all1_ref[me, ROW_B_S:ROW_B_S + 1, :] = jnp.concatenate(brow, axis=1)
        for k in range(CONV_WIDTH):
            all1_ref[me, ROW_CONV_W + k:ROW_CONV_W + k + 1, :] = jnp.sum(
                dwc_ref[8 * k:8 * k + 8, :], axis=0, keepdims=True)
        all2_ref[me] = dws_ref[...]

        copies = []
        for rel in range(1, N_DEV):
            peer = (x ^ (rel >> 2), y ^ ((rel >> 1) & 1), c ^ (rel & 1))
            for a, buf in enumerate((all1_ref, all2_ref)):
                copies.append(pltpu.make_async_remote_copy(
                    src_ref=buf.at[me], dst_ref=buf.at[me],
                    send_sem=send_sems.at[a, rel - 1], recv_sem=recv_sems.at[a, rel - 1],
                    device_id=peer, device_id_type=MESH))
        for cp in copies:
            cp.start()
        for cp in copies:
            cp.wait_recv()

        tot = all1_ref[0]
        for d in range(1, N_DEV):
            tot = tot + all1_ref[d]
        red_ref[...] = tot
        loss_ref[...] = jnp.broadcast_to(jnp.sum(tot[ROW_LOSS:ROW_LOSS + 1, :], axis=1, keepdims=True) * (0.5 / D_MODEL), loss_ref.shape)
        shard = jnp.zeros(cws_ref.shape, F32)
        for d in range(N_DEV):
            shard = jnp.where(me == d, tot[ROW_CONV_W:ROW_CONV_W + CONV_ROWS, d * 128:(d + 1) * 128], shard)
        cws_ref[...] = shard
        tot2 = all2_ref[0]
        for d in range(1, N_DEV):
            tot2 = tot2 + all2_ref[d]
        wsr_ref[...] = tot2
        for cp in copies:
            cp.wait_send()

    vm = pl.BlockSpec(memory_space=pltpu.VMEM)
    return pl.pallas_call(
        body, name="ar_small",
        out_shape=(jax.ShapeDtypeStruct((SMALL_ROWS, C_BR), F32), jax.ShapeDtypeStruct(dws.shape, F32),
                   jax.ShapeDtypeStruct((CONV_ROWS, 128), F32), jax.ShapeDtypeStruct((8, 128), F32)),
        in_specs=[vm] * (n_part + 3), out_specs=(vm, vm, vm, vm),
        scratch_shapes=[pltpu.VMEM((N_DEV, SMALL_ROWS, C_BR), F32), pltpu.VMEM((N_DEV,) + dws.shape, F32),
                        pltpu.SemaphoreType.DMA((2, N_DEV - 1)), pltpu.SemaphoreType.DMA((2, N_DEV - 1))],
        compiler_params=_params(),
    )(*parts, dwc, dba, dws)


def _adamw_math(w, g, m, v):
    m = ADAM_B1 * m + (1.0 - ADAM_B1) * g
    v = ADAM_B2 * v + (1.0 - ADAM_B2) * (g * g)
    m_hat = m / (1.0 - ADAM_B1 ** ADAM_STEP)
    v_hat = v / (1.0 - ADAM_B2 ** ADAM_STEP)
    delta = -ADAM_LR * (m_hat / (jnp.sqrt(v_hat) + ADAM_EPS) + ADAM_WD * w)
    return delta, m, v


def _adamw(g, w, m, v, name, tr):
    rows, cols = w.shape

    def body(g_ref, w_ref, m_ref, v_ref, d_ref, mo_ref, vo_ref):
        d_ref[...], mo_ref[...], vo_ref[...] = _adamw_math(w_ref[...], g_ref[...], m_ref[...], v_ref[...])

    spec = pl.BlockSpec((tr, cols), lambda i: (i, 0))
    sds = jax.ShapeDtypeStruct((rows, cols), F32)
    return pl.pallas_call(
        body, name=name, grid=(rows // tr,), out_shape=(sds, sds, sds),
        in_specs=[spec] * 4, out_specs=(spec, spec, spec),
        compiler_params=_params(("parallel",)),
    )(g, w, m, v)


def _adamw_rows(red, rows, ws, ms, vs):
    n = len(rows)

    def body(*refs):
        red_ref = refs[0]
        w_refs, m_refs, v_refs = refs[1:1 + n], refs[1 + n:1 + 2 * n], refs[1 + 2 * n:1 + 3 * n]
        outs = refs[1 + 3 * n:]
        for t, row in enumerate(rows):
            g = red_ref[row:row + 1, :]
            delta, m, v = _adamw_math(w_refs[t][...], g, m_refs[t][...], v_refs[t][...])
            outs[t][...] = g
            outs[n + t][...] = delta
            outs[2 * n + t][...] = m
            outs[3 * n + t][...] = v

    vm = pl.BlockSpec(memory_space=pltpu.VMEM)
    sds = jax.ShapeDtypeStruct((1, C_BR), F32)
    res = pl.pallas_call(
        body, name="adamw_rows", out_shape=(sds,) * (4 * n),
        in_specs=[vm] * (1 + 3 * n), out_specs=(vm,) * (4 * n),
        compiler_params=_params(),
    )(red, *ws, *ms, *vs)
    return res[:n], res[n:2 * n], res[2 * n:3 * n], res[3 * n:]


def kernel(x, norm_g, w_in, conv_w, conv_b, conv_ln_g, conv_ln_b, sgu_ln_g, sgu_ln_b, w_s, b_s, w_out, final_g, loss_target, m_norm_g, m_w_in, m_conv_w, m_conv_b, m_conv_ln_g, m_conv_ln_b, m_sgu_ln_g, m_sgu_ln_b, m_w_s, m_b_s, m_w_out, m_final_g, v_norm_g, v_w_in, v_conv_w, v_conv_b, v_conv_ln_g, v_conv_ln_b, v_sgu_ln_g, v_sgu_ln_b, v_w_s, v_b_s, v_w_out, v_final_g):
    s = x.shape[1]
    xs = x.reshape(s, D_MODEL)
    tgt = loss_target.reshape(s, D_MODEL)
    tm = min(256, s)

    cw_pad = jnp.pad(conv_w[0], ((0, CONV_ROWS - CONV_WIDTH), (0, 0)))
    win_all, wout_all, cw_all = _ag_weights(w_in[0], w_out[0], cw_pad)
    wout_full = wout_all.reshape(2 * C_BR, D_MODEL)
    cw_full = jnp.transpose(cw_all, (1, 0, 2)).reshape(CONV_ROWS, C_BR)

    ws = w_s[0].astype(BF16)
    wst = jnp.transpose(w_s[0], (0, 2, 1)).astype(BF16)
    bsb = jnp.repeat(jnp.transpose(b_s[0]), HEAD_DIM, axis=1)
    fg = final_g.reshape(1, D_MODEL)

    proj, ht = _proj(xs, norm_g, win_all, tm)
    y, cv = _conv_fwd(proj, cw_full, conv_b, conv_ln_g, conv_ln_b, tm)
    y = _sgu_fwd(proj, y, sgu_ln_g, sgu_ln_b, ws, bsb, tm)
    dx2, dy, dwout, loss_p, dfg_p = _out_loss(xs, y, wout_full, fg, tgt, min(512, s))
    dproj, dwc, dcb_p, dclg_p, dclb_p = _conv_bwd(proj, cv, dy, cw_full, conv_ln_g, conv_ln_b, tm)
    dproj, dws, dba, dslg_p, dslb_p = _sgu_bwd(proj, dy, dproj, sgu_ln_g, sgu_ln_b, ws, wst, bsb, tm)
    grad_x, dng_p = _dx(dproj, win_all, xs, norm_g, dx2, tm)
    dwin = _dwin(ht, dproj, min(2048, s))

    g_w_in = _reduce_scatter(dwin, "rs_w_in")
    g_w_out = _reduce_scatter(dwout.reshape(N_DEV, 2 * C_BR // N_DEV, D_MODEL), "rs_w_out")
    red, g_w_s, g_cw, loss8 = _ar_small(
        [dng_p, dcb_p, dclg_p, dclb_p, dslg_p, dslb_p, dfg_p, loss_p], dwc, dba,
        dws.reshape(HEADS * CHUNK, CHUNK))
    loss = loss8[0, 0]

    d_w_in, nm_w_in, nv_w_in = _adamw(g_w_in, w_in[0], m_w_in[0], v_w_in[0], "adamw_w_in", 256)
    d_w_out, nm_w_out, nv_w_out = _adamw(g_w_out, w_out[0], m_w_out[0], v_w_out[0], "adamw_w_out", 256)
    g_cw = g_cw[:CONV_WIDTH]
    d_cw, nm_cw, nv_cw = _adamw(g_cw, conv_w[0], m_conv_w[0], v_conv_w[0], "adamw_conv_w", CONV_WIDTH)
    flat = lambda a: a.reshape(HEADS * CHUNK, CHUNK)
    d_ws, nm_ws, nv_ws = _adamw(g_w_s, flat(w_s), flat(m_w_s), flat(v_w_s), "adamw_w_s", HEADS * CHUNK)
    row = lambda a: a.reshape(1, C_BR)
    rows = (ROW_NORM_G, ROW_CONV_B, ROW_CLN_G, ROW_CLN_B, ROW_SLN_G, ROW_SLN_B, ROW_B_S, ROW_FINAL_G)
    g_r, d_r, m_r, v_r = _adamw_rows(
        red, rows,
        [norm_g, conv_b, conv_ln_g, conv_ln_b, sgu_ln_g, sgu_ln_b, row(b_s), row(final_g)],
        [m_norm_g, m_conv_b, m_conv_ln_g, m_conv_ln_b, m_sgu_ln_g, m_sgu_ln_b, row(m_b_s), row(m_final_g)],
        [v_norm_g, v_conv_b, v_conv_ln_g, v_conv_ln_b, v_sgu_ln_g, v_sgu_ln_b, row(v_b_s), row(v_final_g)])

    def leaves(r, w_in_l, cw_l, ws_l, w_out_l):
        return (r[0], w_in_l[None], cw_l[None], r[1], r[2], r[3], r[4], r[5],
                ws_l.reshape(1, HEADS, CHUNK, CHUNK), r[6].reshape(1, HEADS, CHUNK), w_out_l[None],
                r[7].reshape(D_MODEL))

    return (loss, grad_x.reshape(1, s, D_MODEL),
            *leaves(g_r, g_w_in, g_cw, g_w_s, g_w_out),
            *leaves(d_r, d_w_in, d_cw, d_ws, d_w_out),
            *leaves(m_r, nm_w_in, nm_cw, nm_ws, nm_w_out),
            *leaves(v_r, nv_w_in, nv_cw, nv_ws, nv_w_out))
```

```python
import functools

import jax
import jax.numpy as jnp
from jax import lax
from jax.experimental import pallas as pl
from jax.experimental.pallas import tpu as pltpu

F32 = jnp.float32
BF16 = jnp.bfloat16
ACT = jnp.bfloat16

D_MODEL = 1024
C_BR = 1024
D_IN = 6 * C_BR
N_DEV = 8
W_BLK = D_IN // N_DEV
HEADS = 8
HEAD_DIM = 128
CHUNK = 128
CONV_WIDTH = 31
CONV_PAD = CONV_WIDTH // 2
HALO = 16
CONV_ROWS = 32
EPS = 1e-6

ADAM_LR = 0.001
ADAM_B1 = 0.9
ADAM_B2 = 0.999
ADAM_EPS = 1e-08
ADAM_WD = 0.01
ADAM_STEP = 10

VMEM_LIMIT = 56 * 1024 * 1024
MESH = pl.DeviceIdType.MESH

ROW_NORM_G, ROW_CONV_B, ROW_CLN_G, ROW_CLN_B, ROW_SLN_G, ROW_SLN_B, ROW_FINAL_G, ROW_B_S, ROW_LOSS = range(9)
ROW_CONV_W = 16
SMALL_ROWS = ROW_CONV_W + CONV_ROWS


def _params(sem=None, **kw):
    return pltpu.CompilerParams(dimension_semantics=sem, vmem_limit_bytes=VMEM_LIMIT, **kw)


def _fold8(a):
    r, n = a.shape
    return a.reshape(r // 8, 8, n).sum(axis=0)


def _sigmoid(z):
    return 1.0 / (1.0 + jnp.exp(-z))


def _ln_norm(xf):
    mu = jnp.mean(xf, axis=-1, keepdims=True)
    xc = xf - mu
    var = jnp.mean(xc * xc, axis=-1, keepdims=True)
    rstd = lax.rsqrt(var + EPS)
    return xc * rstd, rstd


def _ln_bwd(dy, xhat, rstd, g):
    dxhat = dy * g
    m1 = jnp.mean(dxhat, axis=-1, keepdims=True)
    m2 = jnp.mean(dxhat * xhat, axis=-1, keepdims=True)
    return rstd * (dxhat - m1 - xhat * m2)


def _place():
    return lax.axis_index("x"), lax.axis_index("y"), lax.axis_index("c")


def _ag_weights(w_in, w_out, conv_w):
    def body(win_ref, wout_ref, cw_ref, win_all, wout_all, cw_all, send_sems, recv_sems):
        x, y, c = _place()
        me, sibling = (x, y, c), (x, y, 1 - c)
        chips = [(1 - x, y), (x, 1 - y), (1 - x, 1 - y)]
        outs = (win_all, wout_all, cw_all)

        def blk(a, px, py, pc):
            return outs[a].at[4 * px + 2 * py + pc]

        def copy(a, k, block, to):
            return pltpu.make_async_remote_copy(
                src_ref=blk(a, *block), dst_ref=blk(a, *block),
                send_sem=send_sems.at[a, k], recv_sem=recv_sems.at[a, k],
                device_id=to, device_id_type=MESH)

        mine = 4 * x + 2 * y + c
        win_all[mine] = win_ref[...].astype(BF16)
        wout_all[mine] = wout_ref[...].astype(BF16)
        cw_all[mine] = cw_ref[...]

        arrays = range(3)
        first = []
        for a in arrays:
            first.append(copy(a, 0, me, sibling))
            first += [copy(a, 1 + j, me, (*chip, c)) for j, chip in enumerate(chips)]
        for cp in first:
            cp.start()
        passed = []
        for j, chip in enumerate(chips):
            for a in arrays:
                copy(a, 1 + j, (*chip, c), me).wait_recv()
                fwd = copy(a, 4 + j, (*chip, c), sibling)
                fwd.start()
                passed.append(fwd)
        for a in arrays:
            copy(a, 0, sibling, me).wait_recv()
            for j, chip in enumerate(chips):
                copy(a, 4 + j, (*chip, 1 - c), me).wait_recv()
        for cp in first + passed:
            cp.wait_send()

    vm = pl.BlockSpec(memory_space=pltpu.VMEM)
    return pl.pallas_call(
        body, name="ag_weights",
        out_shape=(jax.ShapeDtypeStruct((N_DEV,) + w_in.shape, BF16),
                   jax.ShapeDtypeStruct((N_DEV,) + w_out.shape, BF16),
                   jax.ShapeDtypeStruct((N_DEV,) + conv_w.shape, F32)),
        in_specs=[vm, vm, vm], out_specs=(vm, vm, vm),
        scratch_shapes=[pltpu.SemaphoreType.DMA((3, 7)), pltpu.SemaphoreType.DMA((3, 7))],
        compiler_params=_params(),
    )(w_in, w_out, conv_w)


def _proj(x, norm_g, win_all, tm):
    s = x.shape[0]

    def body(x_ref, g_ref, w_ref, proj_ref, ht_ref):
        xf = x_ref[...]
        r = lax.rsqrt(jnp.mean(xf * xf, axis=-1, keepdims=True) + EPS)
        hf = xf * r * g_ref[...]
        h = hf.astype(BF16)
        ht_ref[...] = hf.T.astype(BF16)
        for j in range(N_DEV):
            proj_ref[:, j * W_BLK:(j + 1) * W_BLK] = jnp.dot(
                h, w_ref[j], preferred_element_type=F32).astype(ACT)

    return pl.pallas_call(
        body, name="proj", grid=(s // tm,),
        out_shape=(jax.ShapeDtypeStruct((s, D_IN), ACT), jax.ShapeDtypeStruct((D_MODEL, s), BF16)),
        in_specs=[pl.BlockSpec((tm, D_MODEL), lambda i: (i, 0)),
                  pl.BlockSpec((1, D_MODEL), lambda i: (0, 0)),
                  pl.BlockSpec((N_DEV, D_MODEL, W_BLK), lambda i: (0, 0, 0))],
        out_specs=(pl.BlockSpec((tm, D_IN), lambda i: (i, 0)),
                   pl.BlockSpec((D_MODEL, tm), lambda i: (0, i))),
        compiler_params=_params(("parallel",)),
    )(x, norm_g, win_all)


def _halo_specs(tm, s, col):
    per = tm // HALO
    last = s // HALO - 1
    return [pl.BlockSpec((HALO, C_BR), lambda i: (jnp.maximum(i * per - 1, 0), col)),
            pl.BlockSpec((tm, C_BR), lambda i: (i, col)),
            pl.BlockSpec((HALO, C_BR), lambda i: (jnp.minimum((i + 1) * per, last), col))]


PHASE_ROWS_LESS = 8


def _fill_phases(ext_ref, ph_ref):
    n = ext_ref.shape[0] - PHASE_ROWS_LESS
    for b in range(1, 8):
        ph_ref[b - 1] = ext_ref[b:b + n, :]


def _for_taps(ext_ref, ph_ref, tm, rb, offset, visit):
    by_phase = {}
    for k in range(CONV_WIDTH):
        a, b = divmod(offset(k), 8)
        by_phase.setdefault(b, []).append((k, a))

    def taps(l0, r0):
        for b, ks in by_phase.items():
            amax = max(a for _, a in ks)
            src = ext_ref if b == 0 else ph_ref.at[b - 1]
            big = src[pl.ds(r0, rb + 8 * amax), l0:l0 + 128]
            for k, a in ks:
                yield k, big[8 * a:8 * a + rb]

    for l0 in range(0, C_BR, 128):
        def step(t, carry, l0=l0):
            r0 = pl.multiple_of(t * rb, rb)
            visit(l0, r0, taps(l0, r0))
            return carry

        lax.fori_loop(0, tm // rb, step, 0)


def _conv_fwd(proj, conv_w, conv_b, ln_g, ln_b, tm):
    s = proj.shape[0]
    nt = s // tm
    rb = 64

    def body(av_p, av_m, av_n, ag_p, ag_m, ag_n, gc_ref, w_ref, cb_ref, lg_ref, lb_ref,
             y_ref, c_ref, ext_ref, ph_ref, cv_ref):
        i = pl.program_id(0)

        def glu(a_ref, g_ref):
            return a_ref[...].astype(F32) * _sigmoid(g_ref[...].astype(F32))

        ext_ref[0:HALO, :] = jnp.where(i > 0, glu(av_p, ag_p), 0.0)
        ext_ref[HALO:HALO + tm, :] = glu(av_m, ag_m)
        ext_ref[HALO + tm:, :] = jnp.where(i < nt - 1, glu(av_n, ag_n), 0.0)
        _fill_phases(ext_ref, ph_ref)

        def visit(l0, r0, taps):
            lanes = slice(l0, l0 + 128)
            accs = [cb_ref[:, lanes], None]
            for n, (k, slab) in enumerate(taps):
                term = w_ref[k:k + 1, lanes] * slab
                accs[n % 2] = term if accs[n % 2] is None else accs[n % 2] + term
            cv_ref[pl.ds(r0, rb), lanes] = accs[0] + accs[1]

        _for_taps(ext_ref, ph_ref, tm, rb, lambda k: k + HALO - CONV_PAD, visit)
        cv = cv_ref[...]
        c_ref[...] = cv.astype(ACT)
        xhat, _ = _ln_norm(cv)
        ln = xhat * lg_ref[...] + lb_ref[...]
        gc = gc_ref[...].astype(F32)
        y_ref[...] = (ln * _sigmoid(ln) * (gc * _sigmoid(gc))).astype(ACT)

    vec = pl.BlockSpec((1, C_BR), lambda i: (0, 0))
    return pl.pallas_call(
        body, name="conv_fwd", grid=(nt,),
        out_shape=(jax.ShapeDtypeStruct((s, 2 * C_BR), ACT), jax.ShapeDtypeStruct((s, C_BR), ACT)),
        in_specs=_halo_specs(tm, s, 0) + _halo_specs(tm, s, 1)
        + [pl.BlockSpec((tm, C_BR), lambda i: (i, 2)),
           pl.BlockSpec((CONV_ROWS, C_BR), lambda i: (0, 0)), vec, vec, vec],
        out_specs=(pl.BlockSpec((tm, C_BR), lambda i: (i, 0)), pl.BlockSpec((tm, C_BR), lambda i: (i, 0))),
        scratch_shapes=[pltpu.VMEM((tm + 2 * HALO, C_BR), F32),
                        pltpu.VMEM((7, tm + 2 * HALO - PHASE_ROWS_LESS, C_BR), F32),
                        pltpu.VMEM((tm, C_BR), F32)],
        compiler_params=_params(("parallel",)),
    )(proj, proj, proj, proj, proj, proj, proj, conv_w, conv_b, ln_g, ln_b)


def _sgu_fwd(proj, y, ln_g, ln_b, ws, bsb, tm):
    s = proj.shape[0]

    def body(u_ref, v_ref, gs_ref, y_in, lg_ref, lb_ref, ws_ref, bsb_ref, y_ref):
        del y_in
        xhat, _ = _ln_norm(v_ref[...].astype(F32))
        vn = (xhat * lg_ref[...] + lb_ref[...]).astype(BF16)
        for cidx in range(tm // CHUNK):
            rows = slice(cidx * CHUNK, (cidx + 1) * CHUNK)
            for h in range(HEADS):
                cols = slice(h * HEAD_DIM, (h + 1) * HEAD_DIM)
                mixed = jnp.dot(ws_ref[h], vn[rows, cols], preferred_element_type=F32) + bsb_ref[:, cols]
                gs = gs_ref[rows, cols].astype(F32)
                y_ref[rows, cols] = (u_ref[rows, cols].astype(F32) * mixed * (gs * _sigmoid(gs))).astype(ACT)

    vec = pl.BlockSpec((1, C_BR), lambda i: (0, 0))
    return pl.pallas_call(
        body, name="sgu_fwd", grid=(s // tm,),
        out_shape=jax.ShapeDtypeStruct((s, 2 * C_BR), ACT),
        in_specs=[pl.BlockSpec((tm, C_BR), lambda i: (i, 3)),
                  pl.BlockSpec((tm, C_BR), lambda i: (i, 4)),
                  pl.BlockSpec((tm, C_BR), lambda i: (i, 5)),
                  pl.BlockSpec(memory_space=pl.ANY),
                  vec, vec,
                  pl.BlockSpec((HEADS, CHUNK, CHUNK), lambda i: (0, 0, 0)),
                  pl.BlockSpec((CHUNK, C_BR), lambda i: (0, 0))],
        out_specs=pl.BlockSpec((tm, C_BR), lambda i: (i, 1)),
        input_output_aliases={3: 0},
        compiler_params=_params(("parallel",)),
    )(proj, proj, proj, y, ln_g, ln_b, ws, bsb)


def _out_loss(x, y, wout, final_g, target, tm):
    s = x.shape[0]
    nt = s // tm
    inv_d = 1.0 / D_MODEL

    def body(x_ref, y_ref, w_ref, g_ref, t_ref, dx2_ref, dy_ref, dw_ref, loss_ref, dfg_ref, acc_ref):
        i = pl.program_id(0)

        @pl.when(i == 0)
        def _():
            acc_ref[...] = jnp.zeros_like(acc_ref)
            loss_ref[...] = jnp.zeros_like(loss_ref)
            dfg_ref[...] = jnp.zeros_like(dfg_ref)

        yb = y_ref[...]
        x2 = x_ref[...] + jnp.dot(yb, w_ref[...], preferred_element_type=F32)
        r2 = lax.rsqrt(jnp.mean(x2 * x2, axis=-1, keepdims=True) + EPS)
        n = x2 * r2
        g = g_ref[...]
        e = n * g - t_ref[...]
        loss_ref[...] += _fold8(e * e)
        dout = e * inv_d
        dfg_ref[...] += _fold8(dout * n)
        dn = dout * g
        dx2 = r2 * (dn - n * jnp.mean(dn * n, axis=-1, keepdims=True))
        dx2_ref[...] = dx2
        dxb = dx2.astype(BF16)
        dy_ref[...] = lax.dot_general(dxb, w_ref[...], (((1,), (1,)), ((), ())),
                                      preferred_element_type=F32).astype(ACT)
        acc_ref[...] += lax.dot_general(yb, dxb, (((0,), (0,)), ((), ())), preferred_element_type=F32)

        @pl.when(i == nt - 1)
        def _():
            dw_ref[...] = acc_ref[...].astype(BF16)

    part = pl.BlockSpec((8, D_MODEL), lambda i: (0, 0))
    return pl.pallas_call(
        body, name="out_loss", grid=(nt,),
        out_shape=(jax.ShapeDtypeStruct((s, D_MODEL), F32), jax.ShapeDtypeStruct((s, 2 * C_BR), ACT),
                   jax.ShapeDtypeStruct((2 * C_BR, D_MODEL), BF16),
                   jax.ShapeDtypeStruct((8, D_MODEL), F32), jax.ShapeDtypeStruct((8, D_MODEL), F32)),
        in_specs=[pl.BlockSpec((tm, D_MODEL), lambda i: (i, 0)),
                  pl.BlockSpec((tm, 2 * C_BR), lambda i: (i, 0)),
                  pl.BlockSpec((2 * C_BR, D_MODEL), lambda i: (0, 0), pipeline_mode=pl.Buffered(1)),
                  pl.BlockSpec((1, D_MODEL), lambda i: (0, 0)),
                  pl.BlockSpec((tm, D_MODEL), lambda i: (i, 0))],
        out_specs=(pl.BlockSpec((tm, D_MODEL), lambda i: (i, 0)),
                   pl.BlockSpec((tm, 2 * C_BR), lambda i: (i, 0)),
                   pl.BlockSpec((2 * C_BR, D_MODEL), lambda i: (0, 0), pipeline_mode=pl.Buffered(1)), part, part),
        scratch_shapes=[pltpu.VMEM((2 * C_BR, D_MODEL), F32)],
        compiler_params=_params(("arbitrary",)),
    )(x, y, wout, final_g, target)


def _conv_bwd(proj, cv, dy, conv_w, ln_g, ln_b, tm):
    s = proj.shape[0]
    nt = s // tm
    rb = 64

    def body(av_ref, ag_ref, gc_p, gc_m, gc_n, c_p, c_m, c_n, dy_p, dy_m, dy_n, w_ref, lg_ref, lb_ref,
             dp_ref, dwc_ref, dcb_ref, dlg_ref, dlb_ref, dce_ref, ph_ref):
        i = pl.program_id(0)

        @pl.when(i == 0)
        def _():
            dwc_ref[...] = jnp.zeros_like(dwc_ref)
            dcb_ref[...] = jnp.zeros_like(dcb_ref)
            dlg_ref[...] = jnp.zeros_like(dlg_ref)
            dlb_ref[...] = jnp.zeros_like(dlb_ref)

        def ext(p, m, n):
            return jnp.concatenate([p[...], m[...], n[...]], axis=0).astype(F32)

        main = slice(HALO, HALO + tm)
        cf, gc, dyc = ext(c_p, c_m, c_n), ext(gc_p, gc_m, gc_n), ext(dy_p, dy_m, dy_n)
        xhat, rstd = _ln_norm(cf)
        lg = lg_ref[...]
        ln = xhat * lg + lb_ref[...]
        s_ln, s_gc = _sigmoid(ln), _sigmoid(gc)
        dln = dyc * (gc * s_gc) * (s_ln * (1.0 + ln * (1.0 - s_ln)))
        dp_ref[:, 2 * C_BR:] = (dyc[main] * (ln[main] * s_ln[main])
                                * (s_gc[main] * (1.0 + gc[main] * (1.0 - s_gc[main])))).astype(ACT)
        dlg_ref[...] += _fold8(dln[main] * xhat[main])
        dlb_ref[...] += _fold8(dln[main])
        dc = _ln_bwd(dln, xhat, rstd, lg)
        dcb_ref[...] += _fold8(dc[main])
        dce_ref[0:HALO, :] = jnp.where(i > 0, dc[0:HALO], 0.0)
        dce_ref[main, :] = dc[main]
        dce_ref[HALO + tm:, :] = jnp.where(i < nt - 1, dc[HALO + tm:], 0.0)
        _fill_phases(dce_ref, ph_ref)


        def visit(l0, r0, taps):
            rows, lanes = pl.ds(r0, rb), slice(l0, l0 + 128)
            av = av_ref[rows, lanes].astype(F32)
            sa = _sigmoid(ag_ref[rows, lanes].astype(F32))
            glu_blk = av * sa
            accs = [None, None]
            for n, (k, slab) in enumerate(taps):
                term = w_ref[k:k + 1, lanes] * slab
                accs[n % 2] = term if accs[n % 2] is None else accs[n % 2] + term
                dwc_ref[8 * k:8 * k + 8, lanes] += _fold8(glu_blk * slab)
            acc = accs[0] + accs[1]
            dp_ref[rows, lanes] = (acc * sa).astype(ACT)
            dp_ref[rows, C_BR + l0:C_BR + l0 + 128] = (acc * av * sa * (1.0 - sa)).astype(ACT)

        _for_taps(dce_ref, ph_ref, tm, rb, lambda k: HALO + CONV_PAD - k, visit)

    vec = pl.BlockSpec((1, C_BR), lambda i: (0, 0))
    part = pl.BlockSpec((8, C_BR), lambda i: (0, 0))
    return pl.pallas_call(
        body, name="conv_bwd", grid=(nt,),
        out_shape=(jax.ShapeDtypeStruct((s, D_IN), ACT), jax.ShapeDtypeStruct((8 * CONV_ROWS, C_BR), F32),
                   jax.ShapeDtypeStruct((8, C_BR), F32), jax.ShapeDtypeStruct((8, C_BR), F32),
                   jax.ShapeDtypeStruct((8, C_BR), F32)),
        in_specs=[pl.BlockSpec((tm, C_BR), lambda i: (i, 0)), pl.BlockSpec((tm, C_BR), lambda i: (i, 1))]
        + _halo_specs(tm, s, 2) + _halo_specs(tm, s, 0) + _halo_specs(tm, s, 0)
        + [pl.BlockSpec((CONV_ROWS, C_BR), lambda i: (0, 0)), vec, vec],
        out_specs=(pl.BlockSpec((tm, 3 * C_BR), lambda i: (i, 0)),
                   pl.BlockSpec((8 * CONV_ROWS, C_BR), lambda i: (0, 0)), part, part, part),
        scratch_shapes=[pltpu.VMEM((tm + 2 * HALO, C_BR), F32),
                        pltpu.VMEM((7, tm + 2 * HALO - PHASE_ROWS_LESS, C_BR), F32)],
        compiler_params=_params(("arbitrary",)),
    )(proj, proj, proj, proj, proj, cv, cv, cv, dy, dy, dy, conv_w, ln_g, ln_b)


def _sgu_bwd(proj, dy, dproj, ln_g, ln_b, ws, wst, bsb, tm):
    s = proj.shape[0]

    def body(u_ref, v_ref, gs_ref, dy_ref, dp_in, lg_ref, lb_ref, ws_ref, wst_ref, bsb_ref,
             dp_ref, dws_ref, dba_ref, dlg_ref, dlb_ref, dvn_ref):
        del dp_in
        i = pl.program_id(0)

        @pl.when(i == 0)
        def _():
            dws_ref[...] = jnp.zeros_like(dws_ref)
            dba_ref[...] = jnp.zeros_like(dba_ref)
            dlg_ref[...] = jnp.zeros_like(dlg_ref)
            dlb_ref[...] = jnp.zeros_like(dlb_ref)

        xhat, rstd = _ln_norm(v_ref[...].astype(F32))
        lg = lg_ref[...]
        vn = (xhat * lg + lb_ref[...]).astype(BF16)
        for cidx in range(tm // CHUNK):
            rows = slice(cidx * CHUNK, (cidx + 1) * CHUNK)
            for h in range(HEADS):
                cols = slice(h * HEAD_DIM, (h + 1) * HEAD_DIM)
                ocols = slice(C_BR + h * HEAD_DIM, C_BR + (h + 1) * HEAD_DIM)
                gcols = slice(2 * C_BR + h * HEAD_DIM, 2 * C_BR + (h + 1) * HEAD_DIM)
                vb = vn[rows, cols]
                mixed = jnp.dot(ws_ref[h], vb, preferred_element_type=F32) + bsb_ref[:, cols]
                gs = gs_ref[rows, cols].astype(F32)
                sg = _sigmoid(gs)
                u = u_ref[rows, cols].astype(F32)
                dyb = dy_ref[rows, cols].astype(F32)
                t = dyb * (gs * sg)
                dp_ref[rows, cols] = (t * mixed).astype(ACT)
                dp_ref[rows, gcols] = (dyb * u * mixed * (sg * (1.0 + gs * (1.0 - sg)))).astype(ACT)
                dm = t * u
                dmb = dm.astype(BF16)
                dvn_ref[rows, cols] = jnp.dot(wst_ref[h], dmb, preferred_element_type=F32)
                dws_ref[h] += lax.dot_general(dmb, vb, (((1,), (1,)), ((), ())), preferred_element_type=F32)
                dba_ref[:, cols] += dm
        dvn = dvn_ref[...]
        dlg_ref[...] += _fold8(dvn * xhat)
        dlb_ref[...] += _fold8(dvn)
        dp_ref[:, C_BR:2 * C_BR] = _ln_bwd(dvn, xhat, rstd, lg).astype(ACT)

    vec = pl.BlockSpec((1, C_BR), lambda i: (0, 0))
    part = pl.BlockSpec((8, C_BR), lambda i: (0, 0))
    wsp = pl.BlockSpec((HEADS, CHUNK, CHUNK), lambda i: (0, 0, 0))
    return pl.pallas_call(
        body, name="sgu_bwd", grid=(s // tm,),
        out_shape=(jax.ShapeDtypeStruct((s, D_IN), ACT), jax.ShapeDtypeStruct((HEADS, CHUNK, CHUNK), F32),
                   jax.ShapeDtypeStruct((CHUNK, C_BR), F32), jax.ShapeDtypeStruct((8, C_BR), F32),
                   jax.ShapeDtypeStruct((8, C_BR), F32)),
        in_specs=[pl.BlockSpec((tm, C_BR), lambda i: (i, 3)),
                  pl.BlockSpec((tm, C_BR), lambda i: (i, 4)),
                  pl.BlockSpec((tm, C_BR), lambda i: (i, 5)),
                  pl.BlockSpec((tm, C_BR), lambda i: (i, 1)),
                  pl.BlockSpec(memory_space=pl.ANY),
                  vec, vec, wsp, wsp, pl.BlockSpec((CHUNK, C_BR), lambda i: (0, 0))],
        out_specs=(pl.BlockSpec((tm, 3 * C_BR), lambda i: (i, 1)), wsp,
                   pl.BlockSpec((CHUNK, C_BR), lambda i: (0, 0)), part, part),
        scratch_shapes=[pltpu.VMEM((tm, C_BR), F32)],
        input_output_aliases={4: 0},
        compiler_params=_params(("arbitrary",)),
    )(proj, proj, proj, dy, dproj, ln_g, ln_b, ws, wst, bsb)


def _dx(dproj, win_all, x, norm_g, dx2, tm):
    s = x.shape[0]

    def body(dp_ref, w_ref, x_ref, g_ref, dx2_ref, gx_ref, dng_ref):
        i = pl.program_id(0)

        @pl.when(i == 0)
        def _():
            dng_ref[...] = jnp.zeros_like(dng_ref)

        dh = None
        for j in range(N_DEV):
            term = lax.dot_general(dp_ref[:, j * W_BLK:(j + 1) * W_BLK], w_ref[j],
                                   (((1,), (1,)), ((), ())), preferred_element_type=F32)
            dh = term if dh is None else dh + term
        xf = x_ref[...]
        r = lax.rsqrt(jnp.mean(xf * xf, axis=-1, keepdims=True) + EPS)
        n = xf * r
        dng_ref[...] += _fold8(dh * n)
        dn = dh * g_ref[...]
        gx_ref[...] = dx2_ref[...] + r * (dn - n * jnp.mean(dn * n, axis=-1, keepdims=True))

    return pl.pallas_call(
        body, name="dx", grid=(s // tm,),
        out_shape=(jax.ShapeDtypeStruct((s, D_MODEL), F32), jax.ShapeDtypeStruct((8, D_MODEL), F32)),
        in_specs=[pl.BlockSpec((tm, D_IN), lambda i: (i, 0)),
                  pl.BlockSpec((N_DEV, D_MODEL, W_BLK), lambda i: (0, 0, 0)),
                  pl.BlockSpec((tm, D_MODEL), lambda i: (i, 0)),
                  pl.BlockSpec((1, D_MODEL), lambda i: (0, 0)),
                  pl.BlockSpec((tm, D_MODEL), lambda i: (i, 0))],
        out_specs=(pl.BlockSpec((tm, D_MODEL), lambda i: (i, 0)), pl.BlockSpec((8, D_MODEL), lambda i: (0, 0))),
        compiler_params=_params(("arbitrary",)),
    )(dproj, win_all, x, norm_g, dx2)


def _dwin_comm(ht, dproj, dwout, parts, dwc, dba, dws, tk):
    s = ht.shape[1]
    nk = s // tk
    n_part = len(parts)
    wo_rows = dwout.shape[1]
    rchunk = 64

    def body(*refs):
        ht_ref, dp_ref, dwout_ref = refs[:3]
        part_refs = refs[3:3 + n_part]
        dwc_ref, dba_ref, dws_ref = refs[3 + n_part:6 + n_part]
        dw_ref, red_ref, wsr_ref, cws_ref, loss_ref, gwo_ref = refs[6 + n_part:12 + n_part]
        acc_ref, all1_ref, all2_ref, land_ref, send_sems, recv_sems, loc_sem = refs[12 + n_part:]
        j, k = pl.program_id(0), pl.program_id(1)
        x, y, c = _place()
        me = 4 * x + 2 * y + c

        def exchanges():
            out = []
            for rel in range(1, N_DEV):
                px, py, pc = x ^ (rel >> 2), y ^ ((rel >> 1) & 1), c ^ (rel & 1)
                srcs = (all1_ref.at[me], all2_ref.at[me], dwout_ref.at[4 * px + 2 * py + pc])
                dsts = (all1_ref.at[me], all2_ref.at[me], land_ref.at[me])
                for a in range(3):
                    out.append(pltpu.make_async_remote_copy(
                        src_ref=srcs[a], dst_ref=dsts[a],
                        send_sem=send_sems.at[a, rel - 1], recv_sem=recv_sems.at[a, rel - 1],
                        device_id=(px, py, pc), device_id_type=MESH))
            return out

        own = pltpu.make_async_copy(dwout_ref.at[me], land_ref.at[me], loc_sem)

        @pl.when((j == 0) & (k == 0))
        def _():
            all1_ref[me] = jnp.zeros((SMALL_ROWS, C_BR), F32)
            for row, p_ref in zip((ROW_NORM_G, ROW_CONV_B, ROW_CLN_G, ROW_CLN_B, ROW_SLN_G, ROW_SLN_B,
                                   ROW_FINAL_G, ROW_LOSS), part_refs):
                all1_ref[me, row:row + 1, :] = jnp.sum(p_ref[...], axis=0, keepdims=True)
            ones = jnp.ones((8, HEAD_DIM), F32)
            brow = [lax.dot_general(ones, dba_ref[:, h * HEAD_DIM:(h + 1) * HEAD_DIM], (((1,), (1,)), ((), ())),
                                    precision=lax.Precision.HIGHEST, preferred_element_type=F32)[0:1]
                    for h in range(HEADS)]
            all1_ref[me, ROW_B_S:ROW_B_S + 1, :] = jnp.concatenate(brow, axis=1)
            for t in range(CONV_WIDTH):
                all1_ref[me, ROW_CONV_W + t:ROW_CONV_W + t + 1, :] = jnp.sum(
                    dwc_ref[8 * t:8 * t + 8, :], axis=0, keepdims=True)
            all2_ref[me] = dws_ref[...]
            own.start()
            for cp in exchanges():
                cp.start()

        @pl.when(k == 0)
        def _():
            acc_ref[...] = jnp.zeros_like(acc_ref)

        acc_ref[...] += jnp.dot(ht_ref[...], dp_ref[...], preferred_element_type=F32)

        @pl.when(k == nk - 1)
        def _():
            dw_ref[0] = acc_ref[...].astype(BF16)

        @pl.when((j == N_DEV - 1) & (k == nk - 1))
        def _():
            copies = exchanges()
            own.wait()
            for cp in copies:
                cp.wait_recv()
            tot = all1_ref[0]
            for d in range(1, N_DEV):
                tot = tot + all1_ref[d]
            red_ref[...] = tot
            loss_ref[...] = jnp.broadcast_to(
                jnp.sum(tot[ROW_LOSS:ROW_LOSS + 1, :], axis=1, keepdims=True) * (0.5 / D_MODEL), loss_ref.shape)
            shard = jnp.zeros(cws_ref.shape, F32)
            for d in range(N_DEV):
                shard = jnp.where(me == d, tot[ROW_CONV_W:ROW_CONV_W + CONV_ROWS, d * 128:(d + 1) * 128], shard)
            cws_ref[...] = shard
            tot2 = all2_ref[0]
            for d in range(1, N_DEV):
                tot2 = tot2 + all2_ref[d]
            wsr_ref[...] = tot2

            def step(t, carry):
                sl = pl.ds(pl.multiple_of(t * rchunk, rchunk), rchunk)
                g = land_ref[0, sl, :].astype(F32)
                for d in range(1, N_DEV):
                    g = g + land_ref[d, sl, :].astype(F32)
                gwo_ref[sl, :] = g
                return carry

            lax.fori_loop(0, wo_rows // rchunk, step, 0)
            for cp in copies:
                cp.wait_send()

    vm = pl.BlockSpec(memory_space=pltpu.VMEM)
    return pl.pallas_call(
        body, name="dwin", grid=(N_DEV, nk),
        out_shape=(jax.ShapeDtypeStruct((N_DEV, D_MODEL, W_BLK), BF16),
                   jax.ShapeDtypeStruct((SMALL_ROWS, C_BR), F32), jax.ShapeDtypeStruct(dws.shape, F32),
                   jax.ShapeDtypeStruct((CONV_ROWS, 128), F32), jax.ShapeDtypeStruct((8, 128), F32),
                   jax.ShapeDtypeStruct(dwout.shape[1:], F32)),
        in_specs=[pl.BlockSpec((D_MODEL, tk), lambda j, k: (0, k)),
                  pl.BlockSpec((tk, W_BLK), lambda j, k: (k, j)),
                  pl.BlockSpec(memory_space=pl.ANY)] + [vm] * (n_part + 3),
        out_specs=(pl.BlockSpec((1, D_MODEL, W_BLK), lambda j, k: (j, 0, 0)), vm, vm, vm, vm, vm),
        scratch_shapes=[pltpu.VMEM((D_MODEL, W_BLK), F32),
                        pltpu.VMEM((N_DEV, SMALL_ROWS, C_BR), F32), pltpu.VMEM((N_DEV,) + dws.shape, F32),
                        pltpu.VMEM(dwout.shape, BF16),
                        pltpu.SemaphoreType.DMA((3, N_DEV - 1)), pltpu.SemaphoreType.DMA((3, N_DEV - 1)),
                        pltpu.SemaphoreType.DMA],
        compiler_params=_params(("arbitrary", "arbitrary")),
    )(ht, dproj, dwout, *parts, dwc, dba, dws)


def _reduce_scatter(g, name):
    _, rows, cols = g.shape
    rchunk = 64

    def body(g_ref, out_ref, own_ref, land1_ref, s1_ref, land2_ref, loc_sems, send1, recv1, send2, recv2):
        x, y, c = _place()
        sibling = (x, y, 1 - c)
        chips = [(x, y), (1 - x, y), (x, 1 - y), (1 - x, 1 - y)]

        def blk(chip, pc):
            return g_ref.at[4 * chip[0] + 2 * chip[1] + pc]

        loads = [pltpu.make_async_copy(blk(chip, c), own_ref.at[r], loc_sems.at[r]) for r, chip in enumerate(chips)]
        to_sib = [pltpu.make_async_remote_copy(src_ref=blk(chip, 1 - c), dst_ref=land1_ref.at[r],
                                               send_sem=send1.at[r], recv_sem=recv1.at[r],
                                               device_id=sibling, device_id_type=MESH)
                  for r, chip in enumerate(chips)]
        order = [1, 2, 3, 0]
        for r in order:
            to_sib[r].start()
            loads[r].start()

        def add_rows(dst_ref, srcs, dtype):
            def step(t, carry):
                sl = pl.ds(pl.multiple_of(t * rchunk, rchunk), rchunk)
                acc = srcs[0][sl, :].astype(F32)
                for src in srcs[1:]:
                    acc = acc + src[sl, :].astype(F32)
                dst_ref[sl, :] = acc.astype(dtype)
                return carry
            lax.fori_loop(0, rows // rchunk, step, 0)

        to_chip = []
        for r in (1, 2, 3):
            loads[r].wait()
            to_sib[r].wait_recv()
            add_rows(s1_ref.at[r - 1], [own_ref.at[r], land1_ref.at[r]], BF16)
            cp = pltpu.make_async_remote_copy(src_ref=s1_ref.at[r - 1], dst_ref=land2_ref.at[r - 1],
                                              send_sem=send2.at[r - 1], recv_sem=recv2.at[r - 1],
                                              device_id=(*chips[r], c), device_id_type=MESH)
            cp.start()
            to_chip.append(cp)
        loads[0].wait()
        to_sib[0].wait_recv()
        for cp in to_chip:
            cp.wait_recv()
        add_rows(out_ref, [own_ref.at[0], land1_ref.at[0], land2_ref.at[0], land2_ref.at[1], land2_ref.at[2]], F32)
        for cp in to_sib + to_chip:
            cp.wait_send()

    return pl.pallas_call(
        body, name=name,
        out_shape=jax.ShapeDtypeStruct((rows, cols), F32),
        in_specs=[pl.BlockSpec(memory_space=pl.ANY)],
        out_specs=pl.BlockSpec(memory_space=pltpu.VMEM),
        scratch_shapes=[pltpu.VMEM((4, rows, cols), BF16), pltpu.VMEM((4, rows, cols), BF16),
                        pltpu.VMEM((3, rows, cols), BF16), pltpu.VMEM((3, rows, cols), BF16),
                        pltpu.SemaphoreType.DMA((4,)), pltpu.SemaphoreType.DMA((4,)), pltpu.SemaphoreType.DMA((4,)),
                        pltpu.SemaphoreType.DMA((3,)), pltpu.SemaphoreType.DMA((3,))],
        compiler_params=_params(),
    )(g)


def _adamw_math(w, g, m, v):
    m = ADAM_B1 * m + (1.0 - ADAM_B1) * g
    v = ADAM_B2 * v + (1.0 - ADAM_B2) * (g * g)
    m_hat = m / (1.0 - ADAM_B1 ** ADAM_STEP)
    v_hat = v / (1.0 - ADAM_B2 ** ADAM_STEP)
    delta = -ADAM_LR * (m_hat / (jnp.sqrt(v_hat) + ADAM_EPS) + ADAM_WD * w)
    return delta, m, v


def _adamw(g, w, m, v, name, tr):
    rows, cols = w.shape

    def body(g_ref, w_ref, m_ref, v_ref, d_ref, mo_ref, vo_ref):
        d_ref[...], mo_ref[...], vo_ref[...] = _adamw_math(w_ref[...], g_ref[...], m_ref[...], v_ref[...])

    spec = pl.BlockSpec((tr, cols), lambda i: (i, 0))
    sds = jax.ShapeDtypeStruct((rows, cols), F32)
    return pl.pallas_call(
        body, name=name, grid=(rows // tr,), out_shape=(sds, sds, sds),
        in_specs=[spec] * 4, out_specs=(spec, spec, spec),
        compiler_params=_params(("parallel",)),
    )(g, w, m, v)


def _adamw_rows(red, rows, ws, ms, vs):
    n = len(rows)

    def body(*refs):
        red_ref = refs[0]
        w_refs, m_refs, v_refs = refs[1:1 + n], refs[1 + n:1 + 2 * n], refs[1 + 2 * n:1 + 3 * n]
        outs = refs[1 + 3 * n:]
        for t, row in enumerate(rows):
            g = red_ref[row:row + 1, :]
            delta, m, v = _adamw_math(w_refs[t][...], g, m_refs[t][...], v_refs[t][...])
            outs[t][...] = g
            outs[n + t][...] = delta
            outs[2 * n + t][...] = m
            outs[3 * n + t][...] = v

    vm = pl.BlockSpec(memory_space=pltpu.VMEM)
    sds = jax.ShapeDtypeStruct((1, C_BR), F32)
    res = pl.pallas_call(
        body, name="adamw_rows", out_shape=(sds,) * (4 * n),
        in_specs=[vm] * (1 + 3 * n), out_specs=(vm,) * (4 * n),
        compiler_params=_params(),
    )(red, *ws, *ms, *vs)
    return res[:n], res[n:2 * n], res[2 * n:3 * n], res[3 * n:]


def kernel(x, norm_g, w_in, conv_w, conv_b, conv_ln_g, conv_ln_b, sgu_ln_g, sgu_ln_b, w_s, b_s, w_out, final_g, loss_target, m_norm_g, m_w_in, m_conv_w, m_conv_b, m_conv_ln_g, m_conv_ln_b, m_sgu_ln_g, m_sgu_ln_b, m_w_s, m_b_s, m_w_out, m_final_g, v_norm_g, v_w_in, v_conv_w, v_conv_b, v_conv_ln_g, v_conv_ln_b, v_sgu_ln_g, v_sgu_ln_b, v_w_s, v_b_s, v_w_out, v_final_g):
    s = x.shape[1]
    xs = x.reshape(s, D_MODEL)
    tgt = loss_target.reshape(s, D_MODEL)
    tm = min(256, s)

    cw_pad = jnp.pad(conv_w[0], ((0, CONV_ROWS - CONV_WIDTH), (0, 0)))
    win_all, wout_all, cw_all = _ag_weights(w_in[0], w_out[0], cw_pad)
    wout_full = wout_all.reshape(2 * C_BR, D_MODEL)
    cw_full = jnp.transpose(cw_all, (1, 0, 2)).reshape(CONV_ROWS, C_BR)

    ws = w_s[0].astype(BF16)
    wst = jnp.transpose(w_s[0], (0, 2, 1)).astype(BF16)
    bsb = jnp.repeat(jnp.transpose(b_s[0]), HEAD_DIM, axis=1)
    fg = final_g.reshape(1, D_MODEL)

    proj, ht = _proj(xs, norm_g, win_all, tm)
    y, cv = _conv_fwd(proj, cw_full, conv_b, conv_ln_g, conv_ln_b, tm)
    y = _sgu_fwd(proj, y, sgu_ln_g, sgu_ln_b, ws, bsb, tm)
    dx2, dy, dwout, loss_p, dfg_p = _out_loss(xs, y, wout_full, fg, tgt, min(512, s))
    dproj, dwc, dcb_p, dclg_p, dclb_p = _conv_bwd(proj, cv, dy, cw_full, conv_ln_g, conv_ln_b, tm)
    dproj, dws, dba, dslg_p, dslb_p = _sgu_bwd(proj, dy, dproj, sgu_ln_g, sgu_ln_b, ws, wst, bsb, tm)
    grad_x, dng_p = _dx(dproj, win_all, xs, norm_g, dx2, tm)
    dwin, red, g_w_s, g_cw, loss8, g_w_out = _dwin_comm(
        ht, dproj, dwout.reshape(N_DEV, 2 * C_BR // N_DEV, D_MODEL),
        [dng_p, dcb_p, dclg_p, dclb_p, dslg_p, dslb_p, dfg_p, loss_p], dwc, dba,
        dws.reshape(HEADS * CHUNK, CHUNK), min(2048, s))
    g_w_in = _reduce_scatter(dwin, "rs_w_in")
    loss = loss8[0, 0]

    d_w_in, nm_w_in, nv_w_in = _adamw(g_w_in, w_in[0], m_w_in[0], v_w_in[0], "adamw_w_in", 256)
    d_w_out, nm_w_out, nv_w_out = _adamw(g_w_out, w_out[0], m_w_out[0], v_w_out[0], "adamw_w_out", 256)
    g_cw = g_cw[:CONV_WIDTH]
    d_cw, nm_cw, nv_cw = _adamw(g_cw, conv_w[0], m_conv_w[0], v_conv_w[0], "adamw_conv_w", CONV_WIDTH)
    flat = lambda a: a.reshape(HEADS * CHUNK, CHUNK)
    d_ws, nm_ws, nv_ws = _adamw(g_w_s, flat(w_s), flat(m_w_s), flat(v_w_s), "adamw_w_s", HEADS * CHUNK)
    row = lambda a: a.reshape(1, C_BR)
    rows = (ROW_NORM_G, ROW_CONV_B, ROW_CLN_G, ROW_CLN_B, ROW_SLN_G, ROW_SLN_B, ROW_B_S, ROW_FINAL_G)
    g_r, d_r, m_r, v_r = _adamw_rows(
        red, rows,
        [norm_g, conv_b, conv_ln_g, conv_ln_b, sgu_ln_g, sgu_ln_b, row(b_s), row(final_g)],
        [m_norm_g, m_conv_b, m_conv_ln_g, m_conv_ln_b, m_sgu_ln_g, m_sgu_ln_b, row(m_b_s), row(m_final_g)],
        [v_norm_g, v_conv_b, v_conv_ln_g, v_conv_ln_b, v_sgu_ln_g, v_sgu_ln_b, row(v_b_s), row(v_final_g)])

    def leaves(r, w_in_l, cw_l, ws_l, w_out_l):
        return (r[0], w_in_l[None], cw_l[None], r[1], r[2], r[3], r[4], r[5],
                ws_l.reshape(1, HEADS, CHUNK, CHUNK), r[6].reshape(1, HEADS, CHUNK), w_out_l[None],
                r[7].reshape(D_MODEL))

    return (loss, grad_x.reshape(1, s, D_MODEL),
            *leaves(g_r, g_w_in, g_cw, g_w_s, g_w_out),
            *leaves(d_r, d_w_in, d_cw, d_ws, d_w_out),
            *leaves(m_r, nm_w_in, nm_cw, nm_ws, nm_w_out),
            *leaves(v_r, nv_w_in, nv_cw, nv_ws, nv_w_out))
```

```python
import functools

import jax
import jax.numpy as jnp
from jax import lax
from jax.experimental import pallas as pl
from jax.experimental.pallas import tpu as pltpu

F32 = jnp.float32
BF16 = jnp.bfloat16
ACT = jnp.bfloat16

D_MODEL = 1024
C_BR = 1024
D_IN = 6 * C_BR
N_DEV = 8
W_BLK = D_IN // N_DEV
HEADS = 8
HEAD_DIM = 128
CHUNK = 128
CONV_WIDTH = 31
CONV_PAD = CONV_WIDTH // 2
HALO = 16
CONV_ROWS = 32
EPS = 1e-6

ADAM_LR = 0.001
ADAM_B1 = 0.9
ADAM_B2 = 0.999
ADAM_EPS = 1e-08
ADAM_WD = 0.01
ADAM_STEP = 10

VMEM_LIMIT = 56 * 1024 * 1024
MESH = pl.DeviceIdType.MESH

ROW_NORM_G, ROW_CONV_B, ROW_CLN_G, ROW_CLN_B, ROW_SLN_G, ROW_SLN_B, ROW_FINAL_G, ROW_B_S, ROW_LOSS = range(9)
ROW_CONV_W = 16
SMALL_ROWS = ROW_CONV_W + CONV_ROWS


def _params(sem=None, **kw):
    return pltpu.CompilerParams(dimension_semantics=sem, vmem_limit_bytes=VMEM_LIMIT, **kw)


def _fold8(a):
    r, n = a.shape
    return a.reshape(r // 8, 8, n).sum(axis=0)


def _sigmoid(z):
    return 1.0 / (1.0 + jnp.exp(-z))


def _ln_norm(xf):
    mu = jnp.mean(xf, axis=-1, keepdims=True)
    xc = xf - mu
    var = jnp.mean(xc * xc, axis=-1, keepdims=True)
    rstd = lax.rsqrt(var + EPS)
    return xc * rstd, rstd


def _ln_bwd(dy, xhat, rstd, g):
    dxhat = dy * g
    m1 = jnp.mean(dxhat, axis=-1, keepdims=True)
    m2 = jnp.mean(dxhat * xhat, axis=-1, keepdims=True)
    return rstd * (dxhat - m1 - xhat * m2)


def _place():
    return lax.axis_index("x"), lax.axis_index("y"), lax.axis_index("c")


def _proj_ag(x, norm_g, w_in, w_out, conv_w, order, tm):
    s = x.shape[0]
    nt = s // tm

    def body(order_ref, x_ref, g_ref, win_ref, wout_ref, cw_ref,
             proj_ref, ht_ref, win_all, wout_all, cw_all,
             h_ref, win_buf, wout_buf, cw_buf, send_sems, recv_sems, save_sems):
        jj, i = pl.program_id(0), pl.program_id(1)
        x_, y_, c_ = _place()
        me, sibling = (x_, y_, c_), (x_, y_, 1 - c_)
        chips = [(1 - x_, y_), (x_, 1 - y_), (1 - x_, 1 - y_)]
        bufs = (win_buf, wout_buf, cw_buf)
        outs = (win_all, wout_all, cw_all)
        start = i == 0

        def index(px, py, pc):
            return 4 * px + 2 * py + pc

        def copy(a, k, block, to):
            return pltpu.make_async_remote_copy(
                src_ref=bufs[a].at[index(*block)], dst_ref=bufs[a].at[index(*block)],
                send_sem=send_sems.at[a, k], recv_sem=recv_sems.at[a, k],
                device_id=to, device_id_type=MESH)

        def save(a, slot, block):
            return pltpu.make_async_copy(bufs[a].at[index(*block)], outs[a].at[index(*block)], save_sems.at[a, slot])

        def first(a):
            return [copy(a, 0, me, sibling)] + [copy(a, 1 + j, me, (*chip, c_)) for j, chip in enumerate(chips)]

        def saves(a):
            blocks = [me, sibling] + [(*chip, c_) for chip in chips] + [(*chip, 1 - c_) for chip in chips]
            return [save(a, slot, block) for slot, block in enumerate(blocks)]

        @pl.when((jj == 0) & start)
        def _():
            win_buf[index(*me)] = win_ref[...].astype(BF16)
            wout_buf[index(*me)] = wout_ref[...].astype(BF16)
            cw_buf[index(*me)] = cw_ref[...]
            for a in range(3):
                for cp in first(a):
                    cp.start()
                saves(a)[0].start()

        @pl.when((jj == 1) & start)
        def _():
            copy(0, 0, sibling, me).wait_recv()
            saves(0)[1].start()

        for j, chip in enumerate(chips):
            @pl.when((jj == 2 + j) & start)
            def _(j=j, chip=chip):
                copy(0, 1 + j, (*chip, c_), me).wait_recv()
                copy(0, 4 + j, (*chip, c_), sibling).start()
                saves(0)[2 + j].start()

            @pl.when((jj == 5 + j) & start)
            def _(j=j, chip=chip):
                copy(0, 4 + j, (*chip, 1 - c_), me).wait_recv()
                saves(0)[5 + j].start()

        @pl.when(jj == 0)
        def _():
            xf = x_ref[...]
            r = lax.rsqrt(jnp.mean(xf * xf, axis=-1, keepdims=True) + EPS)
            hf = xf * r * g_ref[...]
            h_ref[i] = hf.astype(BF16)
            ht_ref[...] = hf.T.astype(BF16)

        proj_ref[...] = jnp.dot(h_ref[i], win_buf[order_ref[jj]], preferred_element_type=F32).astype(ACT)

        @pl.when((jj == N_DEV - 1) & (i == nt - 1))
        def _():
            passed = [copy(0, 4 + j, (*chip, c_), sibling) for j, chip in enumerate(chips)]
            for a in (1, 2):
                for j, chip in enumerate(chips):
                    copy(a, 1 + j, (*chip, c_), me).wait_recv()
                    fwd = copy(a, 4 + j, (*chip, c_), sibling)
                    fwd.start()
                    passed.append(fwd)
                    saves(a)[2 + j].start()
            for a in (1, 2):
                copy(a, 0, sibling, me).wait_recv()
                saves(a)[1].start()
                for j, chip in enumerate(chips):
                    copy(a, 4 + j, (*chip, 1 - c_), me).wait_recv()
                    saves(a)[5 + j].start()
            for cp in saves(0) + saves(1) + saves(2):
                cp.wait()
            for cp in first(0) + first(1) + first(2) + passed:
                cp.wait_send()

    vm = pl.BlockSpec(memory_space=pltpu.VMEM)
    hbm = pl.BlockSpec(memory_space=pl.ANY)
    once = lambda jj, i: jnp.where(jj == 0, i, nt - 1)
    stacked = [(N_DEV,) + w.shape for w in (w_in, w_out, conv_w)]
    return pl.pallas_call(
        body, name="proj_ag",
        grid_spec=pltpu.PrefetchScalarGridSpec(
            num_scalar_prefetch=1, grid=(N_DEV, nt),
            in_specs=[pl.BlockSpec((tm, D_MODEL), lambda jj, i, o: (once(jj, i), 0)),
                      pl.BlockSpec((1, D_MODEL), lambda jj, i, o: (0, 0)), vm, vm, vm],
            out_specs=(pl.BlockSpec((tm, W_BLK), lambda jj, i, o: (i, o[jj])),
                       pl.BlockSpec((D_MODEL, tm), lambda jj, i, o: (0, once(jj, i))), hbm, hbm, hbm),
            scratch_shapes=[pltpu.VMEM((nt, tm, D_MODEL), BF16),
                            pltpu.VMEM(stacked[0], BF16), pltpu.VMEM(stacked[1], BF16), pltpu.VMEM(stacked[2], F32),
                            pltpu.SemaphoreType.DMA((3, 7)), pltpu.SemaphoreType.DMA((3, 7)),
                            pltpu.SemaphoreType.DMA((3, N_DEV))]),
        out_shape=(jax.ShapeDtypeStruct((s, D_IN), ACT), jax.ShapeDtypeStruct((D_MODEL, s), BF16),
                   jax.ShapeDtypeStruct(stacked[0], BF16), jax.ShapeDtypeStruct(stacked[1], BF16),
                   jax.ShapeDtypeStruct(stacked[2], F32)),
        compiler_params=_params(("arbitrary", "arbitrary")),
    )(order, x, norm_g, w_in, w_out, conv_w)


def _halo_specs(tm, s, col):
    per = tm // HALO
    last = s // HALO - 1
    return [pl.BlockSpec((HALO, C_BR), lambda i: (jnp.maximum(i * per - 1, 0), col)),
            pl.BlockSpec((tm, C_BR), lambda i: (i, col)),
            pl.BlockSpec((HALO, C_BR), lambda i: (jnp.minimum((i + 1) * per, last), col))]


PHASE_ROWS_LESS = 8


def _fill_phases(ext_ref, ph_ref):
    n = ext_ref.shape[0] - PHASE_ROWS_LESS
    for b in range(1, 8):
        ph_ref[b - 1] = ext_ref[b:b + n, :]


def _for_taps(ext_ref, ph_ref, tm, rb, offset, visit):
    by_phase = {}
    for k in range(CONV_WIDTH):
        a, b = divmod(offset(k), 8)
        by_phase.setdefault(b, []).append((k, a))

    def taps(l0, r0):
        for b, ks in by_phase.items():
            amax = max(a for _, a in ks)
            src = ext_ref if b == 0 else ph_ref.at[b - 1]
            big = src[pl.ds(r0, rb + 8 * amax), l0:l0 + 128]
            for k, a in ks:
                yield k, big[8 * a:8 * a + rb]

    for l0 in range(0, C_BR, 128):
        def step(t, carry, l0=l0):
            r0 = pl.multiple_of(t * rb, rb)
            visit(l0, r0, taps(l0, r0))
            return carry

        lax.fori_loop(0, tm // rb, step, 0)


def _conv_fwd(proj, conv_w, conv_b, ln_g, ln_b, tm):
    s = proj.shape[0]
    nt = s // tm
    rb = 64

    def body(av_p, av_m, av_n, ag_p, ag_m, ag_n, gc_ref, w_ref, cb_ref, lg_ref, lb_ref,
             y_ref, c_ref, ext_ref, ph_ref, cv_ref):
        i = pl.program_id(0)

        def glu(a_ref, g_ref):
            return a_ref[...].astype(F32) * _sigmoid(g_ref[...].astype(F32))

        ext_ref[0:HALO, :] = jnp.where(i > 0, glu(av_p, ag_p), 0.0)
        ext_ref[HALO:HALO + tm, :] = glu(av_m, ag_m)
        ext_ref[HALO + tm:, :] = jnp.where(i < nt - 1, glu(av_n, ag_n), 0.0)
        _fill_phases(ext_ref, ph_ref)

        def visit(l0, r0, taps):
            lanes = slice(l0, l0 + 128)
            accs = [cb_ref[:, lanes], None]
            for n, (k, slab) in enumerate(taps):
                term = w_ref[k:k + 1, lanes] * slab
                accs[n % 2] = term if accs[n % 2] is None else accs[n % 2] + term
            cv_ref[pl.ds(r0, rb), lanes] = accs[0] + accs[1]

        _for_taps(ext_ref, ph_ref, tm, rb, lambda k: k + HALO - CONV_PAD, visit)
        cv = cv_ref[...]
        c_ref[...] = cv.astype(ACT)
        xhat, _ = _ln_norm(cv)
        ln = xhat * lg_ref[...] + lb_ref[...]
        gc = gc_ref[...].astype(F32)
        y_ref[...] = (ln * _sigmoid(ln) * (gc * _sigmoid(gc))).astype(ACT)

    vec = pl.BlockSpec((1, C_BR), lambda i: (0, 0))
    return pl.pallas_call(
        body, name="conv_fwd", grid=(nt,),
        out_shape=(jax.ShapeDtypeStruct((s, 2 * C_BR), ACT), jax.ShapeDtypeStruct((s, C_BR), ACT)),
        in_specs=_halo_specs(tm, s, 0) + _halo_specs(tm, s, 1)
        + [pl.BlockSpec((tm, C_BR), lambda i: (i, 2)),
           pl.BlockSpec((CONV_ROWS, C_BR), lambda i: (0, 0)), vec, vec, vec],
        out_specs=(pl.BlockSpec((tm, C_BR), lambda i: (i, 0)), pl.BlockSpec((tm, C_BR), lambda i: (i, 0))),
        scratch_shapes=[pltpu.VMEM((tm + 2 * HALO, C_BR), F32),
                        pltpu.VMEM((7, tm + 2 * HALO - PHASE_ROWS_LESS, C_BR), F32),
                        pltpu.VMEM((tm, C_BR), F32)],
        compiler_params=_params(("parallel",)),
    )(proj, proj, proj, proj, proj, proj, proj, conv_w, conv_b, ln_g, ln_b)


def _sgu_fwd(proj, y, ln_g, ln_b, ws, bsb, tm):
    s = proj.shape[0]

    def body(u_ref, v_ref, gs_ref, y_in, lg_ref, lb_ref, ws_ref, bsb_ref, y_ref):
        del y_in
        xhat, _ = _ln_norm(v_ref[...].astype(F32))
        vn = (xhat * lg_ref[...] + lb_ref[...]).astype(BF16)
        for cidx in range(tm // CHUNK):
            rows = slice(cidx * CHUNK, (cidx + 1) * CHUNK)
            for h in range(HEADS):
                cols = slice(h * HEAD_DIM, (h + 1) * HEAD_DIM)
                mixed = jnp.dot(ws_ref[h], vn[rows, cols], preferred_element_type=F32) + bsb_ref[:, cols]
                gs = gs_ref[rows, cols].astype(F32)
                y_ref[rows, cols] = (u_ref[rows, cols].astype(F32) * mixed * (gs * _sigmoid(gs))).astype(ACT)

    vec = pl.BlockSpec((1, C_BR), lambda i: (0, 0))
    return pl.pallas_call(
        body, name="sgu_fwd", grid=(s // tm,),
        out_shape=jax.ShapeDtypeStruct((s, 2 * C_BR), ACT),
        in_specs=[pl.BlockSpec((tm, C_BR), lambda i: (i, 3)),
                  pl.BlockSpec((tm, C_BR), lambda i: (i, 4)),
                  pl.BlockSpec((tm, C_BR), lambda i: (i, 5)),
                  pl.BlockSpec(memory_space=pl.ANY),
                  vec, vec,
                  pl.BlockSpec((HEADS, CHUNK, CHUNK), lambda i: (0, 0, 0)),
                  pl.BlockSpec((CHUNK, C_BR), lambda i: (0, 0))],
        out_specs=pl.BlockSpec((tm, C_BR), lambda i: (i, 1)),
        input_output_aliases={3: 0},
        compiler_params=_params(("parallel",)),
    )(proj, proj, proj, y, ln_g, ln_b, ws, bsb)


def _out_loss(x, y, wout, final_g, target, tm):
    s = x.shape[0]
    nt = s // tm
    inv_d = 1.0 / D_MODEL

    def body(x_ref, y_ref, w_ref, g_ref, t_ref, dx2_ref, dy_ref, dw_ref, loss_ref, dfg_ref, acc_ref):
        i = pl.program_id(0)

        @pl.when(i == 0)
        def _():
            acc_ref[...] = jnp.zeros_like(acc_ref)
            loss_ref[...] = jnp.zeros_like(loss_ref)
            dfg_ref[...] = jnp.zeros_like(dfg_ref)

        yb = y_ref[...]
        x2 = x_ref[...] + jnp.dot(yb, w_ref[...], preferred_element_type=F32)
        r2 = lax.rsqrt(jnp.mean(x2 * x2, axis=-1, keepdims=True) + EPS)
        n = x2 * r2
        g = g_ref[...]
        e = n * g - t_ref[...]
        loss_ref[...] += _fold8(e * e)
        dout = e * inv_d
        dfg_ref[...] += _fold8(dout * n)
        dn = dout * g
        dx2 = r2 * (dn - n * jnp.mean(dn * n, axis=-1, keepdims=True))
        dx2_ref[...] = dx2
        dxb = dx2.astype(BF16)
        dy_ref[...] = lax.dot_general(dxb, w_ref[...], (((1,), (1,)), ((), ())),
                                      preferred_element_type=F32).astype(ACT)
        acc_ref[...] += lax.dot_general(yb, dxb, (((0,), (0,)), ((), ())), preferred_element_type=F32)

        @pl.when(i == nt - 1)
        def _():
            dw_ref[...] = acc_ref[...].astype(BF16)

    part = pl.BlockSpec((8, D_MODEL), lambda i: (0, 0))
    return pl.pallas_call(
        body, name="out_loss", grid=(nt,),
        out_shape=(jax.ShapeDtypeStruct((s, D_MODEL), F32), jax.ShapeDtypeStruct((s, 2 * C_BR), ACT),
                   jax.ShapeDtypeStruct((2 * C_BR, D_MODEL), BF16),
                   jax.ShapeDtypeStruct((8, D_MODEL), F32), jax.ShapeDtypeStruct((8, D_MODEL), F32)),
        in_specs=[pl.BlockSpec((tm, D_MODEL), lambda i: (i, 0)),
                  pl.BlockSpec((tm, 2 * C_BR), lambda i: (i, 0)),
                  pl.BlockSpec((2 * C_BR, D_MODEL), lambda i: (0, 0), pipeline_mode=pl.Buffered(1)),
                  pl.BlockSpec((1, D_MODEL), lambda i: (0, 0)),
                  pl.BlockSpec((tm, D_MODEL), lambda i: (i, 0))],
        out_specs=(pl.BlockSpec((tm, D_MODEL), lambda i: (i, 0)),
                   pl.BlockSpec((tm, 2 * C_BR), lambda i: (i, 0)),
                   pl.BlockSpec((2 * C_BR, D_MODEL), lambda i: (0, 0), pipeline_mode=pl.Buffered(1)), part, part),
        scratch_shapes=[pltpu.VMEM((2 * C_BR, D_MODEL), F32)],
        compiler_params=_params(("arbitrary",)),
    )(x, y, wout, final_g, target)


def _conv_bwd(proj, cv, dy, conv_w, ln_g, ln_b, tm):
    s = proj.shape[0]
    nt = s // tm
    rb = 64

    def body(av_ref, ag_ref, gc_p, gc_m, gc_n, c_p, c_m, c_n, dy_p, dy_m, dy_n, w_ref, lg_ref, lb_ref,
             dp_ref, dwc_ref, dcb_ref, dlg_ref, dlb_ref, dce_ref, ph_ref):
        i = pl.program_id(0)

        @pl.when(i == 0)
        def _():
            dwc_ref[...] = jnp.zeros_like(dwc_ref)
            dcb_ref[...] = jnp.zeros_like(dcb_ref)
            dlg_ref[...] = jnp.zeros_like(dlg_ref)
            dlb_ref[...] = jnp.zeros_like(dlb_ref)

        def ext(p, m, n):
            return jnp.concatenate([p[...], m[...], n[...]], axis=0).astype(F32)

        main = slice(HALO, HALO + tm)
        cf, gc, dyc = ext(c_p, c_m, c_n), ext(gc_p, gc_m, gc_n), ext(dy_p, dy_m, dy_n)
        xhat, rstd = _ln_norm(cf)
        lg = lg_ref[...]
        ln = xhat * lg + lb_ref[...]
        s_ln, s_gc = _sigmoid(ln), _sigmoid(gc)
        dln = dyc * (gc * s_gc) * (s_ln * (1.0 + ln * (1.0 - s_ln)))
        dp_ref[:, 2 * C_BR:] = (dyc[main] * (ln[main] * s_ln[main])
                                * (s_gc[main] * (1.0 + gc[main] * (1.0 - s_gc[main])))).astype(ACT)
        dlg_ref[...] += _fold8(dln[main] * xhat[main])
        dlb_ref[...] += _fold8(dln[main])
        dc = _ln_bwd(dln, xhat, rstd, lg)
        dcb_ref[...] += _fold8(dc[main])
        dce_ref[0:HALO, :] = jnp.where(i > 0, dc[0:HALO], 0.0)
        dce_ref[main, :] = dc[main]
        dce_ref[HALO + tm:, :] = jnp.where(i < nt - 1, dc[HALO + tm:], 0.0)
        _fill_phases(dce_ref, ph_ref)


        def visit(l0, r0, taps):
            rows, lanes = pl.ds(r0, rb), slice(l0, l0 + 128)
            av = av_ref[rows, lanes].astype(F32)
            sa = _sigmoid(ag_ref[rows, lanes].astype(F32))
            glu_blk = av * sa
            accs = [None, None]
            for n, (k, slab) in enumerate(taps):
                term = w_ref[k:k + 1, lanes] * slab
                accs[n % 2] = term if accs[n % 2] is None else accs[n % 2] + term
                dwc_ref[8 * k:8 * k + 8, lanes] += _fold8(glu_blk * slab)
            acc = accs[0] + accs[1]
            dp_ref[rows, lanes] = (acc * sa).astype(ACT)
            dp_ref[rows, C_BR + l0:C_BR + l0 + 128] = (acc * av * sa * (1.0 - sa)).astype(ACT)

        _for_taps(dce_ref, ph_ref, tm, rb, lambda k: HALO + CONV_PAD - k, visit)

    vec = pl.BlockSpec((1, C_BR), lambda i: (0, 0))
    part = pl.BlockSpec((8, C_BR), lambda i: (0, 0))
    return pl.pallas_call(
        body, name="conv_bwd", grid=(nt,),
        out_shape=(jax.ShapeDtypeStruct((s, D_IN), ACT), jax.ShapeDtypeStruct((8 * CONV_ROWS, C_BR), F32),
                   jax.ShapeDtypeStruct((8, C_BR), F32), jax.ShapeDtypeStruct((8, C_BR), F32),
                   jax.ShapeDtypeStruct((8, C_BR), F32)),
        in_specs=[pl.BlockSpec((tm, C_BR), lambda i: (i, 0)), pl.BlockSpec((tm, C_BR), lambda i: (i, 1))]
        + _halo_specs(tm, s, 2) + _halo_specs(tm, s, 0) + _halo_specs(tm, s, 0)
        + [pl.BlockSpec((CONV_ROWS, C_BR), lambda i: (0, 0)), vec, vec],
        out_specs=(pl.BlockSpec((tm, 3 * C_BR), lambda i: (i, 0)),
                   pl.BlockSpec((8 * CONV_ROWS, C_BR), lambda i: (0, 0)), part, part, part),
        scratch_shapes=[pltpu.VMEM((tm + 2 * HALO, C_BR), F32),
                        pltpu.VMEM((7, tm + 2 * HALO - PHASE_ROWS_LESS, C_BR), F32)],
        compiler_params=_params(("arbitrary",)),
    )(proj, proj, proj, proj, proj, cv, cv, cv, dy, dy, dy, conv_w, ln_g, ln_b)


def _sgu_bwd(proj, dy, dproj, ln_g, ln_b, ws, wst, bsb, tm):
    s = proj.shape[0]

    def body(u_ref, v_ref, gs_ref, dy_ref, dp_in, lg_ref, lb_ref, ws_ref, wst_ref, bsb_ref,
             dp_ref, dws_ref, dba_ref, dlg_ref, dlb_ref, dvn_ref):
        del dp_in
        i = pl.program_id(0)

        @pl.when(i == 0)
        def _():
            dws_ref[...] = jnp.zeros_like(dws_ref)
            dba_ref[...] = jnp.zeros_like(dba_ref)
            dlg_ref[...] = jnp.zeros_like(dlg_ref)
            dlb_ref[...] = jnp.zeros_like(dlb_ref)

        xhat, rstd = _ln_norm(v_ref[...].astype(F32))
        lg = lg_ref[...]
        vn = (xhat * lg + lb_ref[...]).astype(BF16)
        for cidx in range(tm // CHUNK):
            rows = slice(cidx * CHUNK, (cidx + 1) * CHUNK)
            for h in range(HEADS):
                cols = slice(h * HEAD_DIM, (h + 1) * HEAD_DIM)
                ocols = slice(C_BR + h * HEAD_DIM, C_BR + (h + 1) * HEAD_DIM)
                gcols = slice(2 * C_BR + h * HEAD_DIM, 2 * C_BR + (h + 1) * HEAD_DIM)
                vb = vn[rows, cols]
                mixed = jnp.dot(ws_ref[h], vb, preferred_element_type=F32) + bsb_ref[:, cols]
                gs = gs_ref[rows, cols].astype(F32)
                sg = _sigmoid(gs)
                u = u_ref[rows, cols].astype(F32)
                dyb = dy_ref[rows, cols].astype(F32)
                t = dyb * (gs * sg)
                dp_ref[rows, cols] = (t * mixed).astype(ACT)
                dp_ref[rows, gcols] = (dyb * u * mixed * (sg * (1.0 + gs * (1.0 - sg)))).astype(ACT)
                dm = t * u
                dmb = dm.astype(BF16)
                dvn_ref[rows, cols] = jnp.dot(wst_ref[h], dmb, preferred_element_type=F32)
                dws_ref[h] += lax.dot_general(dmb, vb, (((1,), (1,)), ((), ())), preferred_element_type=F32)
                dba_ref[:, cols] += dm
        dvn = dvn_ref[...]
        dlg_ref[...] += _fold8(dvn * xhat)
        dlb_ref[...] += _fold8(dvn)
        dp_ref[:, C_BR:2 * C_BR] = _ln_bwd(dvn, xhat, rstd, lg).astype(ACT)

    vec = pl.BlockSpec((1, C_BR), lambda i: (0, 0))
    part = pl.BlockSpec((8, C_BR), lambda i: (0, 0))
    wsp = pl.BlockSpec((HEADS, CHUNK, CHUNK), lambda i: (0, 0, 0))
    return pl.pallas_call(
        body, name="sgu_bwd", grid=(s // tm,),
        out_shape=(jax.ShapeDtypeStruct((s, D_IN), ACT), jax.ShapeDtypeStruct((HEADS, CHUNK, CHUNK), F32),
                   jax.ShapeDtypeStruct((CHUNK, C_BR), F32), jax.ShapeDtypeStruct((8, C_BR), F32),
                   jax.ShapeDtypeStruct((8, C_BR), F32)),
        in_specs=[pl.BlockSpec((tm, C_BR), lambda i: (i, 3)),
                  pl.BlockSpec((tm, C_BR), lambda i: (i, 4)),
                  pl.BlockSpec((tm, C_BR), lambda i: (i, 5)),
                  pl.BlockSpec((tm, C_BR), lambda i: (i, 1)),
                  pl.BlockSpec(memory_space=pl.ANY),
                  vec, vec, wsp, wsp, pl.BlockSpec((CHUNK, C_BR), lambda i: (0, 0))],
        out_specs=(pl.BlockSpec((tm, 3 * C_BR), lambda i: (i, 1)), wsp,
                   pl.BlockSpec((CHUNK, C_BR), lambda i: (0, 0)), part, part),
        scratch_shapes=[pltpu.VMEM((tm, C_BR), F32)],
        input_output_aliases={4: 0},
        compiler_params=_params(("arbitrary",)),
    )(proj, proj, proj, dy, dproj, ln_g, ln_b, ws, wst, bsb)


def _dx(dproj, win_all, x, norm_g, dx2, tm):
    s = x.shape[0]

    def body(dp_ref, w_ref, x_ref, g_ref, dx2_ref, gx_ref, dng_ref):
        i = pl.program_id(0)

        @pl.when(i == 0)
        def _():
            dng_ref[...] = jnp.zeros_like(dng_ref)

        dh = None
        for j in range(N_DEV):
            term = lax.dot_general(dp_ref[:, j * W_BLK:(j + 1) * W_BLK], w_ref[j],
                                   (((1,), (1,)), ((), ())), preferred_element_type=F32)
            dh = term if dh is None else dh + term
        xf = x_ref[...]
        r = lax.rsqrt(jnp.mean(xf * xf, axis=-1, keepdims=True) + EPS)
        n = xf * r
        dng_ref[...] += _fold8(dh * n)
        dn = dh * g_ref[...]
        gx_ref[...] = dx2_ref[...] + r * (dn - n * jnp.mean(dn * n, axis=-1, keepdims=True))

    return pl.pallas_call(
        body, name="dx", grid=(s // tm,),
        out_shape=(jax.ShapeDtypeStruct((s, D_MODEL), F32), jax.ShapeDtypeStruct((8, D_MODEL), F32)),
        in_specs=[pl.BlockSpec((tm, D_IN), lambda i: (i, 0)),
                  pl.BlockSpec((N_DEV, D_MODEL, W_BLK), lambda i: (0, 0, 0)),
                  pl.BlockSpec((tm, D_MODEL), lambda i: (i, 0)),
                  pl.BlockSpec((1, D_MODEL), lambda i: (0, 0)),
                  pl.BlockSpec((tm, D_MODEL), lambda i: (i, 0))],
        out_specs=(pl.BlockSpec((tm, D_MODEL), lambda i: (i, 0)), pl.BlockSpec((8, D_MODEL), lambda i: (0, 0))),
        compiler_params=_params(("arbitrary",)),
    )(dproj, win_all, x, norm_g, dx2)


def _dwin_comm(ht, dproj, dwout, parts, dwc, dba, dws, tk):
    s = ht.shape[1]
    nk = s // tk
    n_part = len(parts)
    wo_rows = dwout.shape[1]
    rchunk = 64

    def body(*refs):
        ht_ref, dp_ref, dwout_ref = refs[:3]
        part_refs = refs[3:3 + n_part]
        dwc_ref, dba_ref, dws_ref = refs[3 + n_part:6 + n_part]
        dw_ref, red_ref, wsr_ref, cws_ref, loss_ref, gwo_ref = refs[6 + n_part:12 + n_part]
        acc_ref, all1_ref, all2_ref, land_ref, send_sems, recv_sems, loc_sem = refs[12 + n_part:]
        j, k = pl.program_id(0), pl.program_id(1)
        x, y, c = _place()
        me = 4 * x + 2 * y + c

        def exchanges():
            out = []
            for rel in range(1, N_DEV):
                px, py, pc = x ^ (rel >> 2), y ^ ((rel >> 1) & 1), c ^ (rel & 1)
                srcs = (all1_ref.at[me], all2_ref.at[me], dwout_ref.at[4 * px + 2 * py + pc])
                dsts = (all1_ref.at[me], all2_ref.at[me], land_ref.at[me])
                for a in range(3):
                    out.append(pltpu.make_async_remote_copy(
                        src_ref=srcs[a], dst_ref=dsts[a],
                        send_sem=send_sems.at[a, rel - 1], recv_sem=recv_sems.at[a, rel - 1],
                        device_id=(px, py, pc), device_id_type=MESH))
            return out

        own = pltpu.make_async_copy(dwout_ref.at[me], land_ref.at[me], loc_sem)

        @pl.when((j == 0) & (k == 0))
        def _():
            all1_ref[me] = jnp.zeros((SMALL_ROWS, C_BR), F32)
            for row, p_ref in zip((ROW_NORM_G, ROW_CONV_B, ROW_CLN_G, ROW_CLN_B, ROW_SLN_G, ROW_SLN_B,
                                   ROW_FINAL_G, ROW_LOSS), part_refs):
                all1_ref[me, row:row + 1, :] = jnp.sum(p_ref[...], axis=0, keepdims=True)
            ones = jnp.ones((8, HEAD_DIM), F32)
            brow = [lax.dot_general(ones, dba_ref[:, h * HEAD_DIM:(h + 1) * HEAD_DIM], (((1,), (1,)), ((), ())),
                                    precision=lax.Precision.HIGHEST, preferred_element_type=F32)[0:1]
                    for h in range(HEADS)]
            all1_ref[me, ROW_B_S:ROW_B_S + 1, :] = jnp.concatenate(brow, axis=1)
            for t in range(CONV_WIDTH):
                all1_ref[me, ROW_CONV_W + t:ROW_CONV_W + t + 1, :] = jnp.sum(
                    dwc_ref[8 * t:8 * t + 8, :], axis=0, keepdims=True)
            all2_ref[me] = dws_ref[...]
            own.start()
            for cp in exchanges():
                cp.start()

        @pl.when(k == 0)
        def _():
            acc_ref[...] = jnp.zeros_like(acc_ref)

        acc_ref[...] += jnp.dot(ht_ref[...], dp_ref[...], preferred_element_type=F32)

        @pl.when(k == nk - 1)
        def _():
            dw_ref[0] = acc_ref[...].astype(BF16)

        @pl.when((j == N_DEV - 1) & (k == nk - 1))
        def _():
            copies = exchanges()
            own.wait()
            for cp in copies:
                cp.wait_recv()
            tot = all1_ref[0]
            for d in range(1, N_DEV):
                tot = tot + all1_ref[d]
            red_ref[...] = tot
            loss_ref[...] = jnp.broadcast_to(
                jnp.sum(tot[ROW_LOSS:ROW_LOSS + 1, :], axis=1, keepdims=True) * (0.5 / D_MODEL), loss_ref.shape)
            shard = jnp.zeros(cws_ref.shape, F32)
            for d in range(N_DEV):
                shard = jnp.where(me == d, tot[ROW_CONV_W:ROW_CONV_W + CONV_ROWS, d * 128:(d + 1) * 128], shard)
            cws_ref[...] = shard
            tot2 = all2_ref[0]
            for d in range(1, N_DEV):
                tot2 = tot2 + all2_ref[d]
            wsr_ref[...] = tot2

            def step(t, carry):
                sl = pl.ds(pl.multiple_of(t * rchunk, rchunk), rchunk)
                g = land_ref[0, sl, :].astype(F32)
                for d in range(1, N_DEV):
                    g = g + land_ref[d, sl, :].astype(F32)
                gwo_ref[sl, :] = g
                return carry

            lax.fori_loop(0, wo_rows // rchunk, step, 0)
            for cp in copies:
                cp.wait_send()

    vm = pl.BlockSpec(memory_space=pltpu.VMEM)
    return pl.pallas_call(
        body, name="dwin", grid=(N_DEV, nk),
        out_shape=(jax.ShapeDtypeStruct((N_DEV, D_MODEL, W_BLK), BF16),
                   jax.ShapeDtypeStruct((SMALL_ROWS, C_BR), F32), jax.ShapeDtypeStruct(dws.shape, F32),
                   jax.ShapeDtypeStruct((CONV_ROWS, 128), F32), jax.ShapeDtypeStruct((8, 128), F32),
                   jax.ShapeDtypeStruct(dwout.shape[1:], F32)),
        in_specs=[pl.BlockSpec((D_MODEL, tk), lambda j, k: (0, k)),
                  pl.BlockSpec((tk, W_BLK), lambda j, k: (k, j)),
                  pl.BlockSpec(memory_space=pl.ANY)] + [vm] * (n_part + 3),
        out_specs=(pl.BlockSpec((1, D_MODEL, W_BLK), lambda j, k: (j, 0, 0)), vm, vm, vm, vm, vm),
        scratch_shapes=[pltpu.VMEM((D_MODEL, W_BLK), F32),
                        pltpu.VMEM((N_DEV, SMALL_ROWS, C_BR), F32), pltpu.VMEM((N_DEV,) + dws.shape, F32),
                        pltpu.VMEM(dwout.shape, BF16),
                        pltpu.SemaphoreType.DMA((3, N_DEV - 1)), pltpu.SemaphoreType.DMA((3, N_DEV - 1)),
                        pltpu.SemaphoreType.DMA],
        compiler_params=_params(("arbitrary", "arbitrary")),
    )(ht, dproj, dwout, *parts, dwc, dba, dws)


def _reduce_scatter(g, name):
    _, rows, cols = g.shape
    rchunk = 64

    def body(g_ref, out_ref, own_ref, land1_ref, s1_ref, land2_ref, loc_sems, send1, recv1, send2, recv2):
        x, y, c = _place()
        sibling = (x, y, 1 - c)
        chips = [(x, y), (1 - x, y), (x, 1 - y), (1 - x, 1 - y)]

        def blk(chip, pc):
            return g_ref.at[4 * chip[0] + 2 * chip[1] + pc]

        loads = [pltpu.make_async_copy(blk(chip, c), own_ref.at[r], loc_sems.at[r]) for r, chip in enumerate(chips)]
        to_sib = [pltpu.make_async_remote_copy(src_ref=blk(chip, 1 - c), dst_ref=land1_ref.at[r],
                                               send_sem=send1.at[r], recv_sem=recv1.at[r],
                                               device_id=sibling, device_id_type=MESH)
                  for r, chip in enumerate(chips)]
        order = [1, 2, 3, 0]
        for r in order:
            to_sib[r].start()
            loads[r].start()

        def add_rows(dst_ref, srcs, dtype):
            def step(t, carry):
                sl = pl.ds(pl.multiple_of(t * rchunk, rchunk), rchunk)
                acc = srcs[0][sl, :].astype(F32)
                for src in srcs[1:]:
                    acc = acc + src[sl, :].astype(F32)
                dst_ref[sl, :] = acc.astype(dtype)
                return carry
            lax.fori_loop(0, rows // rchunk, step, 0)

        to_chip = []
        for r in (1, 2, 3):
            loads[r].wait()
            to_sib[r].wait_recv()
            add_rows(s1_ref.at[r - 1], [own_ref.at[r], land1_ref.at[r]], BF16)
            cp = pltpu.make_async_remote_copy(src_ref=s1_ref.at[r - 1], dst_ref=land2_ref.at[r - 1],
                                              send_sem=send2.at[r - 1], recv_sem=recv2.at[r - 1],
                                              device_id=(*chips[r], c), device_id_type=MESH)
            cp.start()
            to_chip.append(cp)
        loads[0].wait()
        to_sib[0].wait_recv()
        for cp in to_chip:
            cp.wait_recv()
        add_rows(out_ref, [own_ref.at[0], land1_ref.at[0], land2_ref.at[0], land2_ref.at[1], land2_ref.at[2]], F32)
        for cp in to_sib + to_chip:
            cp.wait_send()

    return pl.pallas_call(
        body, name=name,
        out_shape=jax.ShapeDtypeStruct((rows, cols), F32),
        in_specs=[pl.BlockSpec(memory_space=pl.ANY)],
        out_specs=pl.BlockSpec(memory_space=pltpu.VMEM),
        scratch_shapes=[pltpu.VMEM((4, rows, cols), BF16), pltpu.VMEM((4, rows, cols), BF16),
                        pltpu.VMEM((3, rows, cols), BF16), pltpu.VMEM((3, rows, cols), BF16),
                        pltpu.SemaphoreType.DMA((4,)), pltpu.SemaphoreType.DMA((4,)), pltpu.SemaphoreType.DMA((4,)),
                        pltpu.SemaphoreType.DMA((3,)), pltpu.SemaphoreType.DMA((3,))],
        compiler_params=_params(),
    )(g)


def _adamw_math(w, g, m, v):
    m = ADAM_B1 * m + (1.0 - ADAM_B1) * g
    v = ADAM_B2 * v + (1.0 - ADAM_B2) * (g * g)
    m_hat = m / (1.0 - ADAM_B1 ** ADAM_STEP)
    v_hat = v / (1.0 - ADAM_B2 ** ADAM_STEP)
    delta = -ADAM_LR * (m_hat / (jnp.sqrt(v_hat) + ADAM_EPS) + ADAM_WD * w)
    return delta, m, v


def _adamw(g, w, m, v, name, tr):
    rows, cols = w.shape

    def body(g_ref, w_ref, m_ref, v_ref, d_ref, mo_ref, vo_ref):
        d_ref[...], mo_ref[...], vo_ref[...] = _adamw_math(w_ref[...], g_ref[...], m_ref[...], v_ref[...])

    spec = pl.BlockSpec((tr, cols), lambda i: (i, 0))
    sds = jax.ShapeDtypeStruct((rows, cols), F32)
    return pl.pallas_call(
        body, name=name, grid=(rows // tr,), out_shape=(sds, sds, sds),
        in_specs=[spec] * 4, out_specs=(spec, spec, spec),
        compiler_params=_params(("parallel",)),
    )(g, w, m, v)


def _adamw_rows(red, rows, ws, ms, vs):
    n = len(rows)

    def body(*refs):
        red_ref = refs[0]
        w_refs, m_refs, v_refs = refs[1:1 + n], refs[1 + n:1 + 2 * n], refs[1 + 2 * n:1 + 3 * n]
        outs = refs[1 + 3 * n:]
        for t, row in enumerate(rows):
            g = red_ref[row:row + 1, :]
            delta, m, v = _adamw_math(w_refs[t][...], g, m_refs[t][...], v_refs[t][...])
            outs[t][...] = g
            outs[n + t][...] = delta
            outs[2 * n + t][...] = m
            outs[3 * n + t][...] = v

    vm = pl.BlockSpec(memory_space=pltpu.VMEM)
    sds = jax.ShapeDtypeStruct((1, C_BR), F32)
    res = pl.pallas_call(
        body, name="adamw_rows", out_shape=(sds,) * (4 * n),
        in_specs=[vm] * (1 + 3 * n), out_specs=(vm,) * (4 * n),
        compiler_params=_params(),
    )(red, *ws, *ms, *vs)
    return res[:n], res[n:2 * n], res[2 * n:3 * n], res[3 * n:]


def kernel(x, norm_g, w_in, conv_w, conv_b, conv_ln_g, conv_ln_b, sgu_ln_g, sgu_ln_b, w_s, b_s, w_out, final_g, loss_target, m_norm_g, m_w_in, m_conv_w, m_conv_b, m_conv_ln_g, m_conv_ln_b, m_sgu_ln_g, m_sgu_ln_b, m_w_s, m_b_s, m_w_out, m_final_g, v_norm_g, v_w_in, v_conv_w, v_conv_b, v_conv_ln_g, v_conv_ln_b, v_sgu_ln_g, v_sgu_ln_b, v_w_s, v_b_s, v_w_out, v_final_g):
    s = x.shape[1]
    xs = x.reshape(s, D_MODEL)
    tgt = loss_target.reshape(s, D_MODEL)
    tm = min(256, s)

    cw_pad = jnp.pad(conv_w[0], ((0, CONV_ROWS - CONV_WIDTH), (0, 0)))
    px, py, pc = _place()
    blocks = [(px, py, pc), (px, py, 1 - pc)]
    blocks += [(*chip, core) for core in (pc, 1 - pc) for chip in ((1 - px, py), (px, 1 - py), (1 - px, 1 - py))]
    order = jnp.stack([4 * bx + 2 * by + bc for bx, by, bc in blocks]).astype(jnp.int32)
    proj, ht, win_all, wout_all, cw_all = _proj_ag(xs, norm_g, w_in[0], w_out[0], cw_pad, order, min(512, s))
    wout_full = wout_all.reshape(2 * C_BR, D_MODEL)
    cw_full = jnp.transpose(cw_all, (1, 0, 2)).reshape(CONV_ROWS, C_BR)

    ws = w_s[0].astype(BF16)
    wst = jnp.transpose(w_s[0], (0, 2, 1)).astype(BF16)
    bsb = jnp.repeat(jnp.transpose(b_s[0]), HEAD_DIM, axis=1)
    fg = final_g.reshape(1, D_MODEL)

    y, cv = _conv_fwd(proj, cw_full, conv_b, conv_ln_g, conv_ln_b, tm)
    y = _sgu_fwd(proj, y, sgu_ln_g, sgu_ln_b, ws, bsb, tm)
    dx2, dy, dwout, loss_p, dfg_p = _out_loss(xs, y, wout_full, fg, tgt, min(512, s))
    dproj, dwc, dcb_p, dclg_p, dclb_p = _conv_bwd(proj, cv, dy, cw_full, conv_ln_g, conv_ln_b, tm)
    dproj, dws, dba, dslg_p, dslb_p = _sgu_bwd(proj, dy, dproj, sgu_ln_g, sgu_ln_b, ws, wst, bsb, tm)
    grad_x, dng_p = _dx(dproj, win_all, xs, norm_g, dx2, tm)
    dwin, red, g_w_s, g_cw, loss8, g_w_out = _dwin_comm(
        ht, dproj, dwout.reshape(N_DEV, 2 * C_BR // N_DEV, D_MODEL),
        [dng_p, dcb_p, dclg_p, dclb_p, dslg_p, dslb_p, dfg_p, loss_p], dwc, dba,
        dws.reshape(HEADS * CHUNK, CHUNK), min(2048, s))
    g_w_in = _reduce_scatter(dwin, "rs_w_in")
    loss = loss8[0, 0]

    d_w_in, nm_w_in, nv_w_in = _adamw(g_w_in, w_in[0], m_w_in[0], v_w_in[0], "adamw_w_in", 256)
    d_w_out, nm_w_out, nv_w_out = _adamw(g_w_out, w_out[0], m_w_out[0], v_w_out[0], "adamw_w_out", 256)
    g_cw = g_cw[:CONV_WIDTH]
    d_cw, nm_cw, nv_cw = _adamw(g_cw, conv_w[0], m_conv_w[0], v_conv_w[0], "adamw_conv_w", CONV_WIDTH)
    flat = lambda a: a.reshape(HEADS * CHUNK, CHUNK)
    d_ws, nm_ws, nv_ws = _adamw(g_w_s, flat(w_s), flat(m_w_s), flat(v_w_s), "adamw_w_s", HEADS * CHUNK)
    row = lambda a: a.reshape(1, C_BR)
    rows = (ROW_NORM_G, ROW_CONV_B, ROW_CLN_G, ROW_CLN_B, ROW_SLN_G, ROW_SLN_B, ROW_B_S, ROW_FINAL_G)
    g_r, d_r, m_r, v_r = _adamw_rows(
        red, rows,
        [norm_g, conv_b, conv_ln_g, conv_ln_b, sgu_ln_g, sgu_ln_b, row(b_s), row(final_g)],
        [m_norm_g, m_conv_b, m_conv_ln_g, m_conv_ln_b, m_sgu_ln_g, m_sgu_ln_b, row(m_b_s), row(m_final_g)],
        [v_norm_g, v_conv_b, v_conv_ln_g, v_conv_ln_b, v_sgu_ln_g, v_sgu_ln_b, row(v_b_s), row(v_final_g)])

    def leaves(r, w_in_l, cw_l, ws_l, w_out_l):
        return (r[0], w_in_l[None], cw_l[None], r[1], r[2], r[3], r[4], r[5],
                ws_l.reshape(1, HEADS, CHUNK, CHUNK), r[6].reshape(1, HEADS, CHUNK), w_out_l[None],
                r[7].reshape(D_MODEL))

    return (loss, grad_x.reshape(1, s, D_MODEL),
            *leaves(g_r, g_w_in, g_cw, g_w_s, g_w_out),
            *leaves(d_r, d_w_in, d_cw, d_ws, d_w_out),
            *leaves(m_r, nm_w_in, nm_cw, nm_ws, nm_w_out),
            *leaves(v_r, nv_w_in, nv_cw, nv_ws, nv_w_out))
```

```python
import functools

import jax
import jax.numpy as jnp
from jax import lax
from jax.experimental import pallas as pl
from jax.experimental.pallas import tpu as pltpu

F32 = jnp.float32
BF16 = jnp.bfloat16
ACT = jnp.bfloat16

D_MODEL = 1024
C_BR = 1024
D_IN = 6 * C_BR
N_DEV = 8
W_BLK = D_IN // N_DEV
HEADS = 8
HEAD_DIM = 128
CHUNK = 128
CONV_WIDTH = 31
CONV_PAD = CONV_WIDTH // 2
HALO = 16
CONV_ROWS = 32
EPS = 1e-6

ADAM_LR = 0.001
ADAM_B1 = 0.9
ADAM_B2 = 0.999
ADAM_EPS = 1e-08
ADAM_WD = 0.01
ADAM_STEP = 10

VMEM_LIMIT = 56 * 1024 * 1024
MESH = pl.DeviceIdType.MESH

ROW_NORM_G, ROW_CONV_B, ROW_CLN_G, ROW_CLN_B, ROW_SLN_G, ROW_SLN_B, ROW_FINAL_G, ROW_B_S, ROW_LOSS = range(9)
ROW_CONV_W = 16
SMALL_ROWS = ROW_CONV_W + CONV_ROWS


def _params(sem=None, **kw):
    return pltpu.CompilerParams(dimension_semantics=sem, vmem_limit_bytes=VMEM_LIMIT, **kw)


def _fold8(a):
    r, n = a.shape
    return a.reshape(r // 8, 8, n).sum(axis=0)


def _sigmoid(z):
    return 1.0 / (1.0 + jnp.exp(-z))


def _ln_norm(xf):
    mu = jnp.mean(xf, axis=-1, keepdims=True)
    xc = xf - mu
    var = jnp.mean(xc * xc, axis=-1, keepdims=True)
    rstd = lax.rsqrt(var + EPS)
    return xc * rstd, rstd


def _ln_bwd(dy, xhat, rstd, g):
    dxhat = dy * g
    m1 = jnp.mean(dxhat, axis=-1, keepdims=True)
    m2 = jnp.mean(dxhat * xhat, axis=-1, keepdims=True)
    return rstd * (dxhat - m1 - xhat * m2)


def _place():
    return lax.axis_index("x"), lax.axis_index("y"), lax.axis_index("c")


def _proj_ag(x, norm_g, w_in, w_out, conv_w, order, tm):
    s = x.shape[0]
    nt = s // tm

    def body(order_ref, x_ref, g_ref, win_ref, wout_ref, cw_ref,
             proj_ref, ht_ref, win_all, wout_all, cw_all,
             h_ref, win_buf, wout_buf, cw_buf, send_sems, recv_sems, save_sems):
        jj, i = pl.program_id(0), pl.program_id(1)
        x_, y_, c_ = _place()
        me, sibling = (x_, y_, c_), (x_, y_, 1 - c_)
        chips = [(1 - x_, y_), (x_, 1 - y_), (1 - x_, 1 - y_)]
        bufs = (win_buf, wout_buf, cw_buf)
        outs = (win_all, wout_all, cw_all)
        start = i == 0

        def index(px, py, pc):
            return 4 * px + 2 * py + pc

        def copy(a, k, block, to):
            return pltpu.make_async_remote_copy(
                src_ref=bufs[a].at[index(*block)], dst_ref=bufs[a].at[index(*block)],
                send_sem=send_sems.at[a, k], recv_sem=recv_sems.at[a, k],
                device_id=to, device_id_type=MESH)

        def save(a, slot, block):
            return pltpu.make_async_copy(bufs[a].at[index(*block)], outs[a].at[index(*block)], save_sems.at[a, slot])

        def first(a):
            return [copy(a, 0, me, sibling)] + [copy(a, 1 + j, me, (*chip, c_)) for j, chip in enumerate(chips)]

        def saves(a):
            blocks = [me, sibling] + [(*chip, c_) for chip in chips] + [(*chip, 1 - c_) for chip in chips]
            return [save(a, slot, block) for slot, block in enumerate(blocks)]

        @pl.when((jj == 0) & start)
        def _():
            win_buf[index(*me)] = win_ref[...].astype(BF16)
            wout_buf[index(*me)] = wout_ref[...].astype(BF16)
            cw_buf[index(*me)] = cw_ref[...]
            for a in range(3):
                for cp in first(a):
                    cp.start()
                saves(a)[0].start()

        @pl.when((jj == 1) & start)
        def _():
            copy(0, 0, sibling, me).wait_recv()
            saves(0)[1].start()

        for j, chip in enumerate(chips):
            @pl.when((jj == 2 + j) & start)
            def _(j=j, chip=chip):
                copy(0, 1 + j, (*chip, c_), me).wait_recv()
                copy(0, 4 + j, (*chip, c_), sibling).start()
                saves(0)[2 + j].start()

            @pl.when((jj == 5 + j) & start)
            def _(j=j, chip=chip):
                copy(0, 4 + j, (*chip, 1 - c_), me).wait_recv()
                saves(0)[5 + j].start()

        @pl.when(jj == 0)
        def _():
            xf = x_ref[...]
            r = lax.rsqrt(jnp.mean(xf * xf, axis=-1, keepdims=True) + EPS)
            hf = xf * r * g_ref[...]
            h_ref[i] = hf.astype(BF16)
            ht_ref[...] = hf.T.astype(BF16)

        proj_ref[...] = jnp.dot(h_ref[i], win_buf[order_ref[jj]], preferred_element_type=F32).astype(ACT)

        @pl.when((jj == N_DEV - 1) & (i == nt - 1))
        def _():
            passed = [copy(0, 4 + j, (*chip, c_), sibling) for j, chip in enumerate(chips)]
            for a in (1, 2):
                for j, chip in enumerate(chips):
                    copy(a, 1 + j, (*chip, c_), me).wait_recv()
                    fwd = copy(a, 4 + j, (*chip, c_), sibling)
                    fwd.start()
                    passed.append(fwd)
                    saves(a)[2 + j].start()
            for a in (1, 2):
                copy(a, 0, sibling, me).wait_recv()
                saves(a)[1].start()
                for j, chip in enumerate(chips):
                    copy(a, 4 + j, (*chip, 1 - c_), me).wait_recv()
                    saves(a)[5 + j].start()
            for cp in saves(0) + saves(1) + saves(2):
                cp.wait()
            for cp in first(0) + first(1) + first(2) + passed:
                cp.wait_send()

    vm = pl.BlockSpec(memory_space=pltpu.VMEM)
    hbm = pl.BlockSpec(memory_space=pl.ANY)
    once = lambda jj, i: jnp.where(jj == 0, i, nt - 1)
    stacked = [(N_DEV,) + w.shape for w in (w_in, w_out, conv_w)]
    return pl.pallas_call(
        body, name="proj_ag",
        grid_spec=pltpu.PrefetchScalarGridSpec(
            num_scalar_prefetch=1, grid=(N_DEV, nt),
            in_specs=[pl.BlockSpec((tm, D_MODEL), lambda jj, i, o: (once(jj, i), 0)),
                      pl.BlockSpec((1, D_MODEL), lambda jj, i, o: (0, 0)), vm, vm, vm],
            out_specs=(pl.BlockSpec((tm, W_BLK), lambda jj, i, o: (i, o[jj])),
                       pl.BlockSpec((D_MODEL, tm), lambda jj, i, o: (0, once(jj, i))), hbm, hbm, hbm),
            scratch_shapes=[pltpu.VMEM((nt, tm, D_MODEL), BF16),
                            pltpu.VMEM(stacked[0], BF16), pltpu.VMEM(stacked[1], BF16), pltpu.VMEM(stacked[2], F32),
                            pltpu.SemaphoreType.DMA((3, 7)), pltpu.SemaphoreType.DMA((3, 7)),
                            pltpu.SemaphoreType.DMA((3, N_DEV))]),
        out_shape=(jax.ShapeDtypeStruct((s, D_IN), ACT), jax.ShapeDtypeStruct((D_MODEL, s), BF16),
                   jax.ShapeDtypeStruct(stacked[0], BF16), jax.ShapeDtypeStruct(stacked[1], BF16),
                   jax.ShapeDtypeStruct(stacked[2], F32)),
        compiler_params=_params(("arbitrary", "arbitrary")),
    )(order, x, norm_g, w_in, w_out, conv_w)


def _halo_specs(tm, s, col):
    per = tm // HALO
    last = s // HALO - 1
    return [pl.BlockSpec((HALO, C_BR), lambda i: (jnp.maximum(i * per - 1, 0), col)),
            pl.BlockSpec((tm, C_BR), lambda i: (i, col)),
            pl.BlockSpec((HALO, C_BR), lambda i: (jnp.minimum((i + 1) * per, last), col))]


PHASE_ROWS_LESS = 8


def _fill_phases(ext_ref, ph_ref):
    n = ext_ref.shape[0] - PHASE_ROWS_LESS
    for b in range(1, 8):
        ph_ref[b - 1] = ext_ref[b:b + n, :]


def _for_taps(ext_ref, ph_ref, tm, rb, offset, visit):
    by_phase = {}
    for k in range(CONV_WIDTH):
        a, b = divmod(offset(k), 8)
        by_phase.setdefault(b, []).append((k, a))

    def taps(l0, r0):
        for b, ks in by_phase.items():
            amax = max(a for _, a in ks)
            src = ext_ref if b == 0 else ph_ref.at[b - 1]
            big = src[pl.ds(r0, rb + 8 * amax), l0:l0 + 128]
            for k, a in ks:
                yield k, big[8 * a:8 * a + rb]

    for l0 in range(0, C_BR, 128):
        def step(t, carry, l0=l0):
            r0 = pl.multiple_of(t * rb, rb)
            visit(l0, r0, taps(l0, r0))
            return carry

        lax.fori_loop(0, tm // rb, step, 0)


def _conv_fwd(proj, conv_w, conv_b, ln_g, ln_b, tm):
    s = proj.shape[0]
    nt = s // tm
    rb = 64

    def body(av_p, av_m, av_n, ag_p, ag_m, ag_n, gc_ref, w_ref, cb_ref, lg_ref, lb_ref,
             y_ref, c_ref, ext_ref, ph_ref, cv_ref):
        i = pl.program_id(0)

        def glu(a_ref, g_ref):
            return a_ref[...].astype(F32) * _sigmoid(g_ref[...].astype(F32))

        ext_ref[0:HALO, :] = jnp.where(i > 0, glu(av_p, ag_p), 0.0)
        ext_ref[HALO:HALO + tm, :] = glu(av_m, ag_m)
        ext_ref[HALO + tm:, :] = jnp.where(i < nt - 1, glu(av_n, ag_n), 0.0)
        _fill_phases(ext_ref, ph_ref)

        def visit(l0, r0, taps):
            lanes = slice(l0, l0 + 128)
            accs = [cb_ref[:, lanes], None]
            for n, (k, slab) in enumerate(taps):
                term = w_ref[k:k + 1, lanes] * slab
                accs[n % 2] = term if accs[n % 2] is None else accs[n % 2] + term
            cv_ref[pl.ds(r0, rb), lanes] = accs[0] + accs[1]

        _for_taps(ext_ref, ph_ref, tm, rb, lambda k: k + HALO - CONV_PAD, visit)
        cv = cv_ref[...]
        c_ref[...] = cv.astype(ACT)
        xhat, _ = _ln_norm(cv)
        ln = xhat * lg_ref[...] + lb_ref[...]
        gc = gc_ref[...].astype(F32)
        y_ref[...] = (ln * _sigmoid(ln) * (gc * _sigmoid(gc))).astype(ACT)

    vec = pl.BlockSpec((1, C_BR), lambda i: (0, 0))
    return pl.pallas_call(
        body, name="conv_fwd", grid=(nt,),
        out_shape=(jax.ShapeDtypeStruct((s, 2 * C_BR), ACT), jax.ShapeDtypeStruct((s, C_BR), ACT)),
        in_specs=_halo_specs(tm, s, 0) + _halo_specs(tm, s, 1)
        + [pl.BlockSpec((tm, C_BR), lambda i: (i, 2)),
           pl.BlockSpec((CONV_ROWS, C_BR), lambda i: (0, 0)), vec, vec, vec],
        out_specs=(pl.BlockSpec((tm, C_BR), lambda i: (i, 0)), pl.BlockSpec((tm, C_BR), lambda i: (i, 0))),
        scratch_shapes=[pltpu.VMEM((tm + 2 * HALO, C_BR), F32),
                        pltpu.VMEM((7, tm + 2 * HALO - PHASE_ROWS_LESS, C_BR), F32),
                        pltpu.VMEM((tm, C_BR), F32)],
        compiler_params=_params(("parallel",)),
    )(proj, proj, proj, proj, proj, proj, proj, conv_w, conv_b, ln_g, ln_b)


def _sgu_fwd(proj, y, ln_g, ln_b, ws, bsb, tm):
    s = proj.shape[0]

    def body(u_ref, v_ref, gs_ref, y_in, lg_ref, lb_ref, ws_ref, bsb_ref, y_ref):
        del y_in
        xhat, _ = _ln_norm(v_ref[...].astype(F32))
        vn = (xhat * lg_ref[...] + lb_ref[...]).astype(BF16)
        for cidx in range(tm // CHUNK):
            rows = slice(cidx * CHUNK, (cidx + 1) * CHUNK)
            for h in range(HEADS):
                cols = slice(h * HEAD_DIM, (h + 1) * HEAD_DIM)
                mixed = jnp.dot(ws_ref[h], vn[rows, cols], preferred_element_type=F32) + bsb_ref[:, cols]
                gs = gs_ref[rows, cols].astype(F32)
                y_ref[rows, cols] = (u_ref[rows, cols].astype(F32) * mixed * (gs * _sigmoid(gs))).astype(ACT)

    vec = pl.BlockSpec((1, C_BR), lambda i: (0, 0))
    return pl.pallas_call(
        body, name="sgu_fwd", grid=(s // tm,),
        out_shape=jax.ShapeDtypeStruct((s, 2 * C_BR), ACT),
        in_specs=[pl.BlockSpec((tm, C_BR), lambda i: (i, 3)),
                  pl.BlockSpec((tm, C_BR), lambda i: (i, 4)),
                  pl.BlockSpec((tm, C_BR), lambda i: (i, 5)),
                  pl.BlockSpec(memory_space=pl.ANY),
                  vec, vec,
                  pl.BlockSpec((HEADS, CHUNK, CHUNK), lambda i: (0, 0, 0)),
                  pl.BlockSpec((CHUNK, C_BR), lambda i: (0, 0))],
        out_specs=pl.BlockSpec((tm, C_BR), lambda i: (i, 1)),
        input_output_aliases={3: 0},
        compiler_params=_params(("parallel",)),
    )(proj, proj, proj, y, ln_g, ln_b, ws, bsb)


def _out_loss(x, y, wout, final_g, target, tm):
    s = x.shape[0]
    nt = s // tm
    inv_d = 1.0 / D_MODEL

    def body(x_ref, y_ref, w_ref, g_ref, t_ref, dx2_ref, dy_ref, dw_ref, loss_ref, dfg_ref, acc_ref):
        i = pl.program_id(0)

        @pl.when(i == 0)
        def _():
            acc_ref[...] = jnp.zeros_like(acc_ref)
            loss_ref[...] = jnp.zeros_like(loss_ref)
            dfg_ref[...] = jnp.zeros_like(dfg_ref)

        yb = y_ref[...]
        x2 = x_ref[...] + jnp.dot(yb, w_ref[...], preferred_element_type=F32)
        r2 = lax.rsqrt(jnp.mean(x2 * x2, axis=-1, keepdims=True) + EPS)
        n = x2 * r2
        g = g_ref[...]
        e = n * g - t_ref[...]
        loss_ref[...] += _fold8(e * e)
        dout = e * inv_d
        dfg_ref[...] += _fold8(dout * n)
        dn = dout * g
        dx2 = r2 * (dn - n * jnp.mean(dn * n, axis=-1, keepdims=True))
        dx2_ref[...] = dx2
        dxb = dx2.astype(BF16)
        dy_ref[...] = lax.dot_general(dxb, w_ref[...], (((1,), (1,)), ((), ())),
                                      preferred_element_type=F32).astype(ACT)
        acc_ref[...] += lax.dot_general(yb, dxb, (((0,), (0,)), ((), ())), preferred_element_type=F32)

        @pl.when(i == nt - 1)
        def _():
            dw_ref[...] = acc_ref[...].astype(BF16)

    part = pl.BlockSpec((8, D_MODEL), lambda i: (0, 0))
    return pl.pallas_call(
        body, name="out_loss", grid=(nt,),
        out_shape=(jax.ShapeDtypeStruct((s, D_MODEL), F32), jax.ShapeDtypeStruct((s, 2 * C_BR), ACT),
                   jax.ShapeDtypeStruct((2 * C_BR, D_MODEL), BF16),
                   jax.ShapeDtypeStruct((8, D_MODEL), F32), jax.ShapeDtypeStruct((8, D_MODEL), F32)),
        in_specs=[pl.BlockSpec((tm, D_MODEL), lambda i: (i, 0)),
                  pl.BlockSpec((tm, 2 * C_BR), lambda i: (i, 0)),
                  pl.BlockSpec((2 * C_BR, D_MODEL), lambda i: (0, 0), pipeline_mode=pl.Buffered(1)),
                  pl.BlockSpec((1, D_MODEL), lambda i: (0, 0)),
                  pl.BlockSpec((tm, D_MODEL), lambda i: (i, 0))],
        out_specs=(pl.BlockSpec((tm, D_MODEL), lambda i: (i, 0)),
                   pl.BlockSpec((tm, 2 * C_BR), lambda i: (i, 0)),
                   pl.BlockSpec((2 * C_BR, D_MODEL), lambda i: (0, 0), pipeline_mode=pl.Buffered(1)), part, part),
        scratch_shapes=[pltpu.VMEM((2 * C_BR, D_MODEL), F32)],
        compiler_params=_params(("arbitrary",)),
    )(x, y, wout, final_g, target)


def _conv_bwd(proj, cv, dy, conv_w, ln_g, ln_b, dwout, tm):
    s = proj.shape[0]
    nt = s // tm
    rb = 64
    wo_rows = dwout.shape[1]

    def body(av_ref, ag_ref, gc_p, gc_m, gc_n, c_p, c_m, c_n, dy_p, dy_m, dy_n, w_ref, lg_ref, lb_ref,
             dwout_ref, dp_ref, dwc_ref, dcb_ref, dlg_ref, dlb_ref, gwo_ref,
             dce_ref, ph_ref, land_ref, send_sems, recv_sems, loc_sem):
        i = pl.program_id(0)
        px, py, pc = _place()
        me = 4 * px + 2 * py + pc

        def exchanges():
            out = []
            for rel in range(1, N_DEV):
                qx, qy, qc = px ^ (rel >> 2), py ^ ((rel >> 1) & 1), pc ^ (rel & 1)
                out.append(pltpu.make_async_remote_copy(
                    src_ref=dwout_ref.at[4 * qx + 2 * qy + qc], dst_ref=land_ref.at[me],
                    send_sem=send_sems.at[rel - 1], recv_sem=recv_sems.at[rel - 1],
                    device_id=(qx, qy, qc), device_id_type=MESH))
            return out

        own = pltpu.make_async_copy(dwout_ref.at[me], land_ref.at[me], loc_sem)

        @pl.when(i == 0)
        def _():
            dwc_ref[...] = jnp.zeros_like(dwc_ref)
            dcb_ref[...] = jnp.zeros_like(dcb_ref)
            dlg_ref[...] = jnp.zeros_like(dlg_ref)
            dlb_ref[...] = jnp.zeros_like(dlb_ref)
            own.start()
            for cp in exchanges():
                cp.start()

        def ext(p, m, n):
            return jnp.concatenate([p[...], m[...], n[...]], axis=0).astype(F32)

        main = slice(HALO, HALO + tm)
        cf, gc, dyc = ext(c_p, c_m, c_n), ext(gc_p, gc_m, gc_n), ext(dy_p, dy_m, dy_n)
        xhat, rstd = _ln_norm(cf)
        lg = lg_ref[...]
        ln = xhat * lg + lb_ref[...]
        s_ln, s_gc = _sigmoid(ln), _sigmoid(gc)
        dln = dyc * (gc * s_gc) * (s_ln * (1.0 + ln * (1.0 - s_ln)))
        dp_ref[:, 2 * C_BR:] = (dyc[main] * (ln[main] * s_ln[main])
                                * (s_gc[main] * (1.0 + gc[main] * (1.0 - s_gc[main])))).astype(ACT)
        dlg_ref[...] += _fold8(dln[main] * xhat[main])
        dlb_ref[...] += _fold8(dln[main])
        dc = _ln_bwd(dln, xhat, rstd, lg)
        dcb_ref[...] += _fold8(dc[main])
        dce_ref[0:HALO, :] = jnp.where(i > 0, dc[0:HALO], 0.0)
        dce_ref[main, :] = dc[main]
        dce_ref[HALO + tm:, :] = jnp.where(i < nt - 1, dc[HALO + tm:], 0.0)
        _fill_phases(dce_ref, ph_ref)


        def visit(l0, r0, taps):
            rows, lanes = pl.ds(r0, rb), slice(l0, l0 + 128)
            av = av_ref[rows, lanes].astype(F32)
            sa = _sigmoid(ag_ref[rows, lanes].astype(F32))
            glu_blk = av * sa
            accs = [None, None]
            for n, (k, slab) in enumerate(taps):
                term = w_ref[k:k + 1, lanes] * slab
                accs[n % 2] = term if accs[n % 2] is None else accs[n % 2] + term
                dwc_ref[8 * k:8 * k + 8, lanes] += _fold8(glu_blk * slab)
            acc = accs[0] + accs[1]
            dp_ref[rows, lanes] = (acc * sa).astype(ACT)
            dp_ref[rows, C_BR + l0:C_BR + l0 + 128] = (acc * av * sa * (1.0 - sa)).astype(ACT)

        _for_taps(dce_ref, ph_ref, tm, rb, lambda k: HALO + CONV_PAD - k, visit)

        @pl.when(i == nt - 1)
        def _():
            copies = exchanges()
            own.wait()
            for cp in copies:
                cp.wait_recv()

            def step(t, carry):
                sl = pl.ds(pl.multiple_of(t * 64, 64), 64)
                g = land_ref[0, sl, :].astype(F32)
                for d in range(1, N_DEV):
                    g = g + land_ref[d, sl, :].astype(F32)
                gwo_ref[sl, :] = g
                return carry

            lax.fori_loop(0, wo_rows // 64, step, 0)
            for cp in copies:
                cp.wait_send()

    vec = pl.BlockSpec((1, C_BR), lambda i: (0, 0))
    part = pl.BlockSpec((8, C_BR), lambda i: (0, 0))
    return pl.pallas_call(
        body, name="conv_bwd", grid=(nt,),
        out_shape=(jax.ShapeDtypeStruct((s, D_IN), ACT), jax.ShapeDtypeStruct((8 * CONV_ROWS, C_BR), F32),
                   jax.ShapeDtypeStruct((8, C_BR), F32), jax.ShapeDtypeStruct((8, C_BR), F32),
                   jax.ShapeDtypeStruct((8, C_BR), F32), jax.ShapeDtypeStruct(dwout.shape[1:], F32)),
        in_specs=[pl.BlockSpec((tm, C_BR), lambda i: (i, 0)), pl.BlockSpec((tm, C_BR), lambda i: (i, 1))]
        + _halo_specs(tm, s, 2) + _halo_specs(tm, s, 0) + _halo_specs(tm, s, 0)
        + [pl.BlockSpec((CONV_ROWS, C_BR), lambda i: (0, 0)), vec, vec, pl.BlockSpec(memory_space=pl.ANY)],
        out_specs=(pl.BlockSpec((tm, 3 * C_BR), lambda i: (i, 0)),
                   pl.BlockSpec((8 * CONV_ROWS, C_BR), lambda i: (0, 0)), part, part, part,
                   pl.BlockSpec(memory_space=pltpu.VMEM)),
        scratch_shapes=[pltpu.VMEM((tm + 2 * HALO, C_BR), F32),
                        pltpu.VMEM((7, tm + 2 * HALO - PHASE_ROWS_LESS, C_BR), F32),
                        pltpu.VMEM(dwout.shape, BF16),
                        pltpu.SemaphoreType.DMA((N_DEV - 1,)), pltpu.SemaphoreType.DMA((N_DEV - 1,)),
                        pltpu.SemaphoreType.DMA],
        compiler_params=_params(("arbitrary",)),
    )(proj, proj, proj, proj, proj, cv, cv, cv, dy, dy, dy, conv_w, ln_g, ln_b, dwout)


def _sgu_bwd(proj, dy, dproj, ln_g, ln_b, ws, wst, bsb, tm):
    s = proj.shape[0]

    def body(u_ref, v_ref, gs_ref, dy_ref, dp_in, lg_ref, lb_ref, ws_ref, wst_ref, bsb_ref,
             dp_ref, dws_ref, dba_ref, dlg_ref, dlb_ref, dvn_ref):
        del dp_in
        i = pl.program_id(0)

        @pl.when(i == 0)
        def _():
            dws_ref[...] = jnp.zeros_like(dws_ref)
            dba_ref[...] = jnp.zeros_like(dba_ref)
            dlg_ref[...] = jnp.zeros_like(dlg_ref)
            dlb_ref[...] = jnp.zeros_like(dlb_ref)

        xhat, rstd = _ln_norm(v_ref[...].astype(F32))
        lg = lg_ref[...]
        vn = (xhat * lg + lb_ref[...]).astype(BF16)
        for cidx in range(tm // CHUNK):
            rows = slice(cidx * CHUNK, (cidx + 1) * CHUNK)
            for h in range(HEADS):
                cols = slice(h * HEAD_DIM, (h + 1) * HEAD_DIM)
                ocols = slice(C_BR + h * HEAD_DIM, C_BR + (h + 1) * HEAD_DIM)
                gcols = slice(2 * C_BR + h * HEAD_DIM, 2 * C_BR + (h + 1) * HEAD_DIM)
                vb = vn[rows, cols]
                mixed = jnp.dot(ws_ref[h], vb, preferred_element_type=F32) + bsb_ref[:, cols]
                gs = gs_ref[rows, cols].astype(F32)
                sg = _sigmoid(gs)
                u = u_ref[rows, cols].astype(F32)
                dyb = dy_ref[rows, cols].astype(F32)
                t = dyb * (gs * sg)
                dp_ref[rows, cols] = (t * mixed).astype(ACT)
                dp_ref[rows, gcols] = (dyb * u * mixed * (sg * (1.0 + gs * (1.0 - sg)))).astype(ACT)
                dm = t * u
                dmb = dm.astype(BF16)
                dvn_ref[rows, cols] = jnp.dot(wst_ref[h], dmb, preferred_element_type=F32)
                dws_ref[h] += lax.dot_general(dmb, vb, (((1,), (1,)), ((), ())), preferred_element_type=F32)
                dba_ref[:, cols] += dm
        dvn = dvn_ref[...]
        dlg_ref[...] += _fold8(dvn * xhat)
        dlb_ref[...] += _fold8(dvn)
        dp_ref[:, C_BR:2 * C_BR] = _ln_bwd(dvn, xhat, rstd, lg).astype(ACT)

    vec = pl.BlockSpec((1, C_BR), lambda i: (0, 0))
    part = pl.BlockSpec((8, C_BR), lambda i: (0, 0))
    wsp = pl.BlockSpec((HEADS, CHUNK, CHUNK), lambda i: (0, 0, 0))
    return pl.pallas_call(
        body, name="sgu_bwd", grid=(s // tm,),
        out_shape=(jax.ShapeDtypeStruct((s, D_IN), ACT), jax.ShapeDtypeStruct((HEADS, CHUNK, CHUNK), F32),
                   jax.ShapeDtypeStruct((CHUNK, C_BR), F32), jax.ShapeDtypeStruct((8, C_BR), F32),
                   jax.ShapeDtypeStruct((8, C_BR), F32)),
        in_specs=[pl.BlockSpec((tm, C_BR), lambda i: (i, 3)),
                  pl.BlockSpec((tm, C_BR), lambda i: (i, 4)),
                  pl.BlockSpec((tm, C_BR), lambda i: (i, 5)),
                  pl.BlockSpec((tm, C_BR), lambda i: (i, 1)),
                  pl.BlockSpec(memory_space=pl.ANY),
                  vec, vec, wsp, wsp, pl.BlockSpec((CHUNK, C_BR), lambda i: (0, 0))],
        out_specs=(pl.BlockSpec((tm, 3 * C_BR), lambda i: (i, 1)), wsp,
                   pl.BlockSpec((CHUNK, C_BR), lambda i: (0, 0)), part, part),
        scratch_shapes=[pltpu.VMEM((tm, C_BR), F32)],
        input_output_aliases={4: 0},
        compiler_params=_params(("arbitrary",)),
    )(proj, proj, proj, dy, dproj, ln_g, ln_b, ws, wst, bsb)


def _dx(dproj, win_all, x, norm_g, dx2, tm):
    s = x.shape[0]

    def body(dp_ref, w_ref, x_ref, g_ref, dx2_ref, gx_ref, dng_ref):
        i = pl.program_id(0)

        @pl.when(i == 0)
        def _():
            dng_ref[...] = jnp.zeros_like(dng_ref)

        dh = None
        for j in range(N_DEV):
            term = lax.dot_general(dp_ref[:, j * W_BLK:(j + 1) * W_BLK], w_ref[j],
                                   (((1,), (1,)), ((), ())), preferred_element_type=F32)
            dh = term if dh is None else dh + term
        xf = x_ref[...]
        r = lax.rsqrt(jnp.mean(xf * xf, axis=-1, keepdims=True) + EPS)
        n = xf * r
        dng_ref[...] += _fold8(dh * n)
        dn = dh * g_ref[...]
        gx_ref[...] = dx2_ref[...] + r * (dn - n * jnp.mean(dn * n, axis=-1, keepdims=True))

    return pl.pallas_call(
        body, name="dx", grid=(s // tm,),
        out_shape=(jax.ShapeDtypeStruct((s, D_MODEL), F32), jax.ShapeDtypeStruct((8, D_MODEL), F32)),
        in_specs=[pl.BlockSpec((tm, D_IN), lambda i: (i, 0)),
                  pl.BlockSpec((N_DEV, D_MODEL, W_BLK), lambda i: (0, 0, 0)),
                  pl.BlockSpec((tm, D_MODEL), lambda i: (i, 0)),
                  pl.BlockSpec((1, D_MODEL), lambda i: (0, 0)),
                  pl.BlockSpec((tm, D_MODEL), lambda i: (i, 0))],
        out_specs=(pl.BlockSpec((tm, D_MODEL), lambda i: (i, 0)), pl.BlockSpec((8, D_MODEL), lambda i: (0, 0))),
        compiler_params=_params(("arbitrary",)),
    )(dproj, win_all, x, norm_g, dx2)


def _dwin_comm(ht, dproj, order, parts, dwc, dba, dws, tk):
    s = ht.shape[1]
    nk = s // tk
    n_part = len(parts)

    def body(*refs):
        order_ref, ht_ref, dp_ref = refs[:3]
        del order_ref
        part_refs = refs[3:3 + n_part]
        dwc_ref, dba_ref, dws_ref = refs[3 + n_part:6 + n_part]
        gw_ref, red_ref, wsr_ref, cws_ref, loss_ref = refs[6 + n_part:11 + n_part]
        (acc_ref, all1_ref, all2_ref, out_s, land_s, out_x, land_x,
         send_sems, recv_sems, send_s, recv_s, send_x, recv_x) = refs[11 + n_part:]
        jj, k = pl.program_id(0), pl.program_id(1)
        x, y, c = _place()
        me = 4 * x + 2 * y + c
        sibling = (x, y, 1 - c)
        chips = [(1 - x, 1 - y), (1 - x, y), (x, 1 - y)]
        last = k == nk - 1

        def exchanges():
            out = []
            for rel in range(1, N_DEV):
                peer = (x ^ (rel >> 2), y ^ ((rel >> 1) & 1), c ^ (rel & 1))
                for a, buf in enumerate((all1_ref, all2_ref)):
                    out.append(pltpu.make_async_remote_copy(
                        src_ref=buf.at[me], dst_ref=buf.at[me],
                        send_sem=send_sems.at[a, rel - 1], recv_sem=recv_sems.at[a, rel - 1],
                        device_id=peer, device_id_type=MESH))
            return out

        def to_sibling(slot):
            return pltpu.make_async_remote_copy(
                src_ref=out_s.at[slot], dst_ref=land_s.at[slot],
                send_sem=send_s.at[slot], recv_sem=recv_s.at[slot], device_id=sibling, device_id_type=MESH)

        def to_chip(slot):
            return pltpu.make_async_remote_copy(
                src_ref=out_x.at[slot], dst_ref=land_x.at[slot],
                send_sem=send_x.at[slot], recv_sem=recv_x.at[slot],
                device_id=(*chips[slot], c), device_id_type=MESH)

        @pl.when((jj == 0) & (k == 0))
        def _():
            all1_ref[me] = jnp.zeros((SMALL_ROWS, C_BR), F32)
            for row, p_ref in zip((ROW_NORM_G, ROW_CONV_B, ROW_CLN_G, ROW_CLN_B, ROW_SLN_G, ROW_SLN_B,
                                   ROW_FINAL_G, ROW_LOSS), part_refs):
                all1_ref[me, row:row + 1, :] = jnp.sum(p_ref[...], axis=0, keepdims=True)
            ones = jnp.ones((8, HEAD_DIM), F32)
            brow = [lax.dot_general(ones, dba_ref[:, h * HEAD_DIM:(h + 1) * HEAD_DIM], (((1,), (1,)), ((), ())),
                                    precision=lax.Precision.HIGHEST, preferred_element_type=F32)[0:1]
                    for h in range(HEADS)]
            all1_ref[me, ROW_B_S:ROW_B_S + 1, :] = jnp.concatenate(brow, axis=1)
            for t in range(CONV_WIDTH):
                all1_ref[me, ROW_CONV_W + t:ROW_CONV_W + t + 1, :] = jnp.sum(
                    dwc_ref[8 * t:8 * t + 8, :], axis=0, keepdims=True)
            all2_ref[me] = dws_ref[...]
            for cp in exchanges():
                cp.start()

        @pl.when(k == 0)
        def _():
            acc_ref[...] = jnp.zeros_like(acc_ref)

        acc_ref[...] += jnp.dot(ht_ref[...], dp_ref[...], preferred_element_type=F32)

        for slot in range(4):
            @pl.when((jj == 2 * slot) & last)
            def _(slot=slot):
                out_s[slot] = acc_ref[...].astype(BF16)
                to_sibling(slot).start()

        for slot in range(3):
            @pl.when((jj == 2 * slot + 1) & last)
            def _(slot=slot):
                to_sibling(slot).wait_recv()
                out_x[slot] = (acc_ref[...] + land_s[slot].astype(F32)).astype(BF16)
                to_chip(slot).start()

        @pl.when((jj == N_DEV - 1) & last)
        def _():
            to_sibling(3).wait_recv()
            total = acc_ref[...] + land_s[3].astype(F32)
            for slot in range(3):
                to_chip(slot).wait_recv()
                total = total + land_x[slot].astype(F32)
            gw_ref[...] = total

            copies = exchanges()
            for cp in copies:
                cp.wait_recv()
            tot = all1_ref[0]
            for d in range(1, N_DEV):
                tot = tot + all1_ref[d]
            red_ref[...] = tot
            loss_ref[...] = jnp.broadcast_to(
                jnp.sum(tot[ROW_LOSS:ROW_LOSS + 1, :], axis=1, keepdims=True) * (0.5 / D_MODEL), loss_ref.shape)
            shard = jnp.zeros(cws_ref.shape, F32)
            for d in range(N_DEV):
                shard = jnp.where(me == d, tot[ROW_CONV_W:ROW_CONV_W + CONV_ROWS, d * 128:(d + 1) * 128], shard)
            cws_ref[...] = shard
            tot2 = all2_ref[0]
            for d in range(1, N_DEV):
                tot2 = tot2 + all2_ref[d]
            wsr_ref[...] = tot2
            for cp in copies + [to_sibling(slot) for slot in range(4)] + [to_chip(slot) for slot in range(3)]:
                cp.wait_send()

    vm = pl.BlockSpec(memory_space=pltpu.VMEM)
    blk = (D_MODEL, W_BLK)
    return pl.pallas_call(
        body, name="dwin",
        grid_spec=pltpu.PrefetchScalarGridSpec(
            num_scalar_prefetch=1, grid=(N_DEV, nk),
            in_specs=[pl.BlockSpec((D_MODEL, tk), lambda jj, k, o: (0, k)),
                      pl.BlockSpec((tk, W_BLK), lambda jj, k, o: (k, o[jj]))] + [vm] * (n_part + 3),
            out_specs=(vm, vm, vm, vm, vm),
            scratch_shapes=[pltpu.VMEM(blk, F32),
                            pltpu.VMEM((N_DEV, SMALL_ROWS, C_BR), F32), pltpu.VMEM((N_DEV,) + dws.shape, F32),
                            pltpu.VMEM((4,) + blk, BF16), pltpu.VMEM((4,) + blk, BF16),
                            pltpu.VMEM((3,) + blk, BF16), pltpu.VMEM((3,) + blk, BF16),
                            pltpu.SemaphoreType.DMA((2, N_DEV - 1)), pltpu.SemaphoreType.DMA((2, N_DEV - 1)),
                            pltpu.SemaphoreType.DMA((4,)), pltpu.SemaphoreType.DMA((4,)),
                            pltpu.SemaphoreType.DMA((3,)), pltpu.SemaphoreType.DMA((3,))]),
        out_shape=(jax.ShapeDtypeStruct(blk, F32),
                   jax.ShapeDtypeStruct((SMALL_ROWS, C_BR), F32), jax.ShapeDtypeStruct(dws.shape, F32),
                   jax.ShapeDtypeStruct((CONV_ROWS, 128), F32), jax.ShapeDtypeStruct((8, 128), F32)),
        compiler_params=_params(("arbitrary", "arbitrary")),
    )(order, ht, dproj, *parts, dwc, dba, dws)


def _adamw_math(w, g, m, v):
    m = ADAM_B1 * m + (1.0 - ADAM_B1) * g
    v = ADAM_B2 * v + (1.0 - ADAM_B2) * (g * g)
    m_hat = m / (1.0 - ADAM_B1 ** ADAM_STEP)
    v_hat = v / (1.0 - ADAM_B2 ** ADAM_STEP)
    delta = -ADAM_LR * (m_hat / (jnp.sqrt(v_hat) + ADAM_EPS) + ADAM_WD * w)
    return delta, m, v


def _adamw(g, w, m, v, name, tr):
    rows, cols = w.shape

    def body(g_ref, w_ref, m_ref, v_ref, d_ref, mo_ref, vo_ref):
        d_ref[...], mo_ref[...], vo_ref[...] = _adamw_math(w_ref[...], g_ref[...], m_ref[...], v_ref[...])

    spec = pl.BlockSpec((tr, cols), lambda i: (i, 0))
    sds = jax.ShapeDtypeStruct((rows, cols), F32)
    return pl.pallas_call(
        body, name=name, grid=(rows // tr,), out_shape=(sds, sds, sds),
        in_specs=[spec] * 4, out_specs=(spec, spec, spec),
        compiler_params=_params(("parallel",)),
    )(g, w, m, v)


def _adamw_rows(red, rows, ws, ms, vs):
    n = len(rows)

    def body(*refs):
        red_ref = refs[0]
        w_refs, m_refs, v_refs = refs[1:1 + n], refs[1 + n:1 + 2 * n], refs[1 + 2 * n:1 + 3 * n]
        outs = refs[1 + 3 * n:]
        for t, row in enumerate(rows):
            g = red_ref[row:row + 1, :]
            delta, m, v = _adamw_math(w_refs[t][...], g, m_refs[t][...], v_refs[t][...])
            outs[t][...] = g
            outs[n + t][...] = delta
            outs[2 * n + t][...] = m
            outs[3 * n + t][...] = v

    vm = pl.BlockSpec(memory_space=pltpu.VMEM)
    sds = jax.ShapeDtypeStruct((1, C_BR), F32)
    res = pl.pallas_call(
        body, name="adamw_rows", out_shape=(sds,) * (4 * n),
        in_specs=[vm] * (1 + 3 * n), out_specs=(vm,) * (4 * n),
        compiler_params=_params(),
    )(red, *ws, *ms, *vs)
    return res[:n], res[n:2 * n], res[2 * n:3 * n], res[3 * n:]


def kernel(x, norm_g, w_in, conv_w, conv_b, conv_ln_g, conv_ln_b, sgu_ln_g, sgu_ln_b, w_s, b_s, w_out, final_g, loss_target, m_norm_g, m_w_in, m_conv_w, m_conv_b, m_conv_ln_g, m_conv_ln_b, m_sgu_ln_g, m_sgu_ln_b, m_w_s, m_b_s, m_w_out, m_final_g, v_norm_g, v_w_in, v_conv_w, v_conv_b, v_conv_ln_g, v_conv_ln_b, v_sgu_ln_g, v_sgu_ln_b, v_w_s, v_b_s, v_w_out, v_final_g):
    s = x.shape[1]
    xs = x.reshape(s, D_MODEL)
    tgt = loss_target.reshape(s, D_MODEL)
    tm = min(256, s)

    cw_pad = jnp.pad(conv_w[0], ((0, CONV_ROWS - CONV_WIDTH), (0, 0)))
    px, py, pc = _place()
    blocks = [(px, py, pc), (px, py, 1 - pc)]
    blocks += [(*chip, core) for core in (pc, 1 - pc) for chip in ((1 - px, py), (px, 1 - py), (1 - px, 1 - py))]
    order = jnp.stack([4 * bx + 2 * by + bc for bx, by, bc in blocks]).astype(jnp.int32)
    proj, ht, win_all, wout_all, cw_all = _proj_ag(xs, norm_g, w_in[0], w_out[0], cw_pad, order, min(512, s))
    wout_full = wout_all.reshape(2 * C_BR, D_MODEL)
    cw_full = jnp.transpose(cw_all, (1, 0, 2)).reshape(CONV_ROWS, C_BR)

    ws = w_s[0].astype(BF16)
    wst = jnp.transpose(w_s[0], (0, 2, 1)).astype(BF16)
    bsb = jnp.repeat(jnp.transpose(b_s[0]), HEAD_DIM, axis=1)
    fg = final_g.reshape(1, D_MODEL)

    y, cv = _conv_fwd(proj, cw_full, conv_b, conv_ln_g, conv_ln_b, tm)
    y = _sgu_fwd(proj, y, sgu_ln_g, sgu_ln_b, ws, bsb, tm)
    dx2, dy, dwout, loss_p, dfg_p = _out_loss(xs, y, wout_full, fg, tgt, min(512, s))
    dproj, dwc, dcb_p, dclg_p, dclb_p, g_w_out = _conv_bwd(
        proj, cv, dy, cw_full, conv_ln_g, conv_ln_b, dwout.reshape(N_DEV, 2 * C_BR // N_DEV, D_MODEL), tm)
    dproj, dws, dba, dslg_p, dslb_p = _sgu_bwd(proj, dy, dproj, sgu_ln_g, sgu_ln_b, ws, wst, bsb, tm)
    grad_x, dng_p = _dx(dproj, win_all, xs, norm_g, dx2, tm)
    rs_blocks = [(*chip, core) for chip in ((1 - px, 1 - py), (1 - px, py), (px, 1 - py), (px, py))
                 for core in (1 - pc, pc)]
    rs_order = jnp.stack([4 * bx + 2 * by + bc for bx, by, bc in rs_blocks]).astype(jnp.int32)
    g_w_in, red, g_w_s, g_cw, loss8 = _dwin_comm(
        ht, dproj, rs_order, [dng_p, dcb_p, dclg_p, dclb_p, dslg_p, dslb_p, dfg_p, loss_p], dwc, dba,
        dws.reshape(HEADS * CHUNK, CHUNK), min(1024, s))
    loss = loss8[0, 0]

    d_w_in, nm_w_in, nv_w_in = _adamw(g_w_in, w_in[0], m_w_in[0], v_w_in[0], "adamw_w_in", 256)
    d_w_out, nm_w_out, nv_w_out = _adamw(g_w_out, w_out[0], m_w_out[0], v_w_out[0], "adamw_w_out", 256)
    g_cw = g_cw[:CONV_WIDTH]
    d_cw, nm_cw, nv_cw = _adamw(g_cw, conv_w[0], m_conv_w[0], v_conv_w[0], "adamw_conv_w", CONV_WIDTH)
    flat = lambda a: a.reshape(HEADS * CHUNK, CHUNK)
    d_ws, nm_ws, nv_ws = _adamw(g_w_s, flat(w_s), flat(m_w_s), flat(v_w_s), "adamw_w_s", HEADS * CHUNK)
    row = lambda a: a.reshape(1, C_BR)
    rows = (ROW_NORM_G, ROW_CONV_B, ROW_CLN_G, ROW_CLN_B, ROW_SLN_G, ROW_SLN_B, ROW_B_S, ROW_FINAL_G)
    g_r, d_r, m_r, v_r = _adamw_rows(
        red, rows,
        [norm_g, conv_b, conv_ln_g, conv_ln_b, sgu_ln_g, sgu_ln_b, row(b_s), row(final_g)],
        [m_norm_g, m_conv_b, m_conv_ln_g, m_conv_ln_b, m_sgu_ln_g, m_sgu_ln_b, row(m_b_s), row(m_final_g)],
        [v_norm_g, v_conv_b, v_conv_ln_g, v_conv_ln_b, v_sgu_ln_g, v_sgu_ln_b, row(v_b_s), row(v_final_g)])

    def leaves(r, w_in_l, cw_l, ws_l, w_out_l):
        return (r[0], w_in_l[None], cw_l[None], r[1], r[2], r[3], r[4], r[5],
                ws_l.reshape(1, HEADS, CHUNK, CHUNK), r[6].reshape(1, HEADS, CHUNK), w_out_l[None],
                r[7].reshape(D_MODEL))

    return (loss, grad_x.reshape(1, s, D_MODEL),
            *leaves(g_r, g_w_in, g_cw, g_w_s, g_w_out),
            *leaves(d_r, d_w_in, d_cw, d_ws, d_w_out),
            *leaves(m_r, nm_w_in, nm_cw, nm_ws, nm_w_out),
            *leaves(v_r, nv_w_in, nv_cw, nv_ws, nv_w_out))
```

```python
import functools

import jax
import jax.numpy as jnp
from jax import lax
from jax.experimental import pallas as pl
from jax.experimental.pallas import tpu as pltpu

F32 = jnp.float32
BF16 = jnp.bfloat16
ACT = jnp.bfloat16

D_MODEL = 1024
C_BR = 1024
D_IN = 6 * C_BR
N_DEV = 8
W_BLK = D_IN // N_DEV
HEADS = 8
HEAD_DIM = 128
CHUNK = 128
CONV_WIDTH = 31
CONV_PAD = CONV_WIDTH // 2
HALO = 16
CONV_ROWS = 32
EPS = 1e-6

ADAM_LR = 0.001
ADAM_B1 = 0.9
ADAM_B2 = 0.999
ADAM_EPS = 1e-08
ADAM_WD = 0.01
ADAM_STEP = 10

VMEM_LIMIT = 56 * 1024 * 1024
MESH = pl.DeviceIdType.MESH

ROW_NORM_G, ROW_CONV_B, ROW_CLN_G, ROW_CLN_B, ROW_SLN_G, ROW_SLN_B, ROW_FINAL_G, ROW_B_S, ROW_LOSS = range(9)
ROW_CONV_W = 16
SMALL_ROWS = ROW_CONV_W + CONV_ROWS


def _params(sem=None, **kw):
    return pltpu.CompilerParams(dimension_semantics=sem, vmem_limit_bytes=VMEM_LIMIT, **kw)


def _fold8(a):
    r, n = a.shape
    return a.reshape(r // 8, 8, n).sum(axis=0)


def _sigmoid(z):
    return 1.0 / (1.0 + jnp.exp(-z))


def _ln_norm(xf):
    mu = jnp.mean(xf, axis=-1, keepdims=True)
    xc = xf - mu
    var = jnp.mean(xc * xc, axis=-1, keepdims=True)
    rstd = lax.rsqrt(var + EPS)
    return xc * rstd, rstd


def _ln_bwd(dy, xhat, rstd, g):
    dxhat = dy * g
    m1 = jnp.mean(dxhat, axis=-1, keepdims=True)
    m2 = jnp.mean(dxhat * xhat, axis=-1, keepdims=True)
    return rstd * (dxhat - m1 - xhat * m2)


def _place():
    return lax.axis_index("x"), lax.axis_index("y"), lax.axis_index("c")


def _proj_ag(x, norm_g, w_in, w_out, conv_w, order, tm):
    s = x.shape[0]
    nt = s // tm

    def body(order_ref, x_ref, g_ref, win_ref, wout_ref, cw_ref,
             proj_ref, ht_ref, win_all, wout_all, cw_all,
             h_ref, win_buf, wout_buf, cw_buf, send_sems, recv_sems, save_sems):
        jj, i = pl.program_id(0), pl.program_id(1)
        x_, y_, c_ = _place()
        me, sibling = (x_, y_, c_), (x_, y_, 1 - c_)
        chips = [(1 - x_, y_), (x_, 1 - y_), (1 - x_, 1 - y_)]
        bufs = (win_buf, wout_buf, cw_buf)
        outs = (win_all, wout_all, cw_all)
        start = i == 0

        def index(px, py, pc):
            return 4 * px + 2 * py + pc

        def copy(a, k, block, to):
            return pltpu.make_async_remote_copy(
                src_ref=bufs[a].at[index(*block)], dst_ref=bufs[a].at[index(*block)],
                send_sem=send_sems.at[a, k], recv_sem=recv_sems.at[a, k],
                device_id=to, device_id_type=MESH)

        def save(a, slot, block):
            return pltpu.make_async_copy(bufs[a].at[index(*block)], outs[a].at[index(*block)], save_sems.at[a, slot])

        def first(a):
            return [copy(a, 0, me, sibling)] + [copy(a, 1 + j, me, (*chip, c_)) for j, chip in enumerate(chips)]

        def saves(a):
            blocks = [me, sibling] + [(*chip, core) for chip in chips for core in (c_, 1 - c_)]
            return [save(a, slot, block) for slot, block in enumerate(blocks)]

        @pl.when((jj == 0) & start)
        def _():
            win_buf[index(*me)] = win_ref[...].astype(BF16)
            wout_buf[index(*me)] = wout_ref[...].astype(BF16)
            cw_buf[index(*me)] = cw_ref[...]
            for a in range(3):
                for cp in first(a):
                    cp.start()
                saves(a)[0].start()

        @pl.when((jj == 1) & start)
        def _():
            copy(0, 0, sibling, me).wait_recv()
            saves(0)[1].start()

        for j, chip in enumerate(chips):
            @pl.when((jj == 2 + 2 * j) & start)
            def _(j=j, chip=chip):
                copy(0, 1 + j, (*chip, c_), me).wait_recv()
                copy(0, 4 + j, (*chip, c_), sibling).start()
                saves(0)[2 + 2 * j].start()

            @pl.when((jj == 3 + 2 * j) & start)
            def _(j=j, chip=chip):
                copy(0, 4 + j, (*chip, 1 - c_), me).wait_recv()
                saves(0)[3 + 2 * j].start()

        @pl.when(jj == 0)
        def _():
            xf = x_ref[...]
            r = lax.rsqrt(jnp.mean(xf * xf, axis=-1, keepdims=True) + EPS)
            hf = xf * r * g_ref[...]
            h_ref[i] = hf.astype(BF16)
            ht_ref[...] = hf.T.astype(BF16)

        proj_ref[...] = jnp.dot(h_ref[i], win_buf[order_ref[jj]], preferred_element_type=F32).astype(ACT)

        @pl.when((jj == N_DEV - 1) & (i == nt - 1))
        def _():
            passed = [copy(0, 4 + j, (*chip, c_), sibling) for j, chip in enumerate(chips)]
            for a in (1, 2):
                for j, chip in enumerate(chips):
                    copy(a, 1 + j, (*chip, c_), me).wait_recv()
                    fwd = copy(a, 4 + j, (*chip, c_), sibling)
                    fwd.start()
                    passed.append(fwd)
                    saves(a)[2 + 2 * j].start()
            for a in (1, 2):
                copy(a, 0, sibling, me).wait_recv()
                saves(a)[1].start()
                for j, chip in enumerate(chips):
                    copy(a, 4 + j, (*chip, 1 - c_), me).wait_recv()
                    saves(a)[3 + 2 * j].start()
            for cp in saves(0) + saves(1) + saves(2):
                cp.wait()
            for cp in first(0) + first(1) + first(2) + passed:
                cp.wait_send()

    vm = pl.BlockSpec(memory_space=pltpu.VMEM)
    hbm = pl.BlockSpec(memory_space=pl.ANY)
    once = lambda jj, i: jnp.where(jj == 0, i, nt - 1)
    stacked = [(N_DEV,) + w.shape for w in (w_in, w_out, conv_w)]
    return pl.pallas_call(
        body, name="proj_ag",
        grid_spec=pltpu.PrefetchScalarGridSpec(
            num_scalar_prefetch=1, grid=(N_DEV, nt),
            in_specs=[pl.BlockSpec((tm, D_MODEL), lambda jj, i, o: (once(jj, i), 0)),
                      pl.BlockSpec((1, D_MODEL), lambda jj, i, o: (0, 0)), vm, vm, vm],
            out_specs=(pl.BlockSpec((tm, W_BLK), lambda jj, i, o: (i, o[jj])),
                       pl.BlockSpec((D_MODEL, tm), lambda jj, i, o: (0, once(jj, i))), hbm, hbm, hbm),
            scratch_shapes=[pltpu.VMEM((nt, tm, D_MODEL), BF16),
                            pltpu.VMEM(stacked[0], BF16), pltpu.VMEM(stacked[1], BF16), pltpu.VMEM(stacked[2], F32),
                            pltpu.SemaphoreType.DMA((3, 7)), pltpu.SemaphoreType.DMA((3, 7)),
                            pltpu.SemaphoreType.DMA((3, N_DEV))]),
        out_shape=(jax.ShapeDtypeStruct((s, D_IN), ACT), jax.ShapeDtypeStruct((D_MODEL, s), BF16),
                   jax.ShapeDtypeStruct(stacked[0], BF16), jax.ShapeDtypeStruct(stacked[1], BF16),
                   jax.ShapeDtypeStruct(stacked[2], F32)),
        compiler_params=_params(("arbitrary", "arbitrary")),
    )(order, x, norm_g, w_in, w_out, conv_w)


def _halo_specs(tm, s, col):
    per = tm // HALO
    last = s // HALO - 1
    return [pl.BlockSpec((HALO, C_BR), lambda i: (jnp.maximum(i * per - 1, 0), col)),
            pl.BlockSpec((tm, C_BR), lambda i: (i, col)),
            pl.BlockSpec((HALO, C_BR), lambda i: (jnp.minimum((i + 1) * per, last), col))]


PHASE_ROWS_LESS = 8


def _fill_phases(ext_ref, ph_ref):
    n = ext_ref.shape[0] - PHASE_ROWS_LESS
    for b in range(1, 8):
        ph_ref[b - 1] = ext_ref[b:b + n, :]


def _for_taps(ext_ref, ph_ref, tm, rb, offset, visit):
    by_phase = {}
    for k in range(CONV_WIDTH):
        a, b = divmod(offset(k), 8)
        by_phase.setdefault(b, []).append((k, a))

    def taps(l0, r0):
        for b, ks in by_phase.items():
            amax = max(a for _, a in ks)
            src = ext_ref if b == 0 else ph_ref.at[b - 1]
            big = src[pl.ds(r0, rb + 8 * amax), l0:l0 + 128]
            for k, a in ks:
                yield k, big[8 * a:8 * a + rb]

    for l0 in range(0, C_BR, 128):
        def step(t, carry, l0=l0):
            r0 = pl.multiple_of(t * rb, rb)
            visit(l0, r0, taps(l0, r0))
            return carry

        lax.fori_loop(0, tm // rb, step, 0)


def _conv_fwd(proj, conv_w, conv_b, ln_g, ln_b, tm):
    s = proj.shape[0]
    nt = s // tm
    rb = 64

    def body(av_p, av_m, av_n, ag_p, ag_m, ag_n, gc_ref, w_ref, cb_ref, lg_ref, lb_ref,
             y_ref, c_ref, ext_ref, ph_ref, cv_ref):
        i = pl.program_id(0)

        def glu(a_ref, g_ref):
            return a_ref[...].astype(F32) * _sigmoid(g_ref[...].astype(F32))

        ext_ref[0:HALO, :] = jnp.where(i > 0, glu(av_p, ag_p), 0.0)
        ext_ref[HALO:HALO + tm, :] = glu(av_m, ag_m)
        ext_ref[HALO + tm:, :] = jnp.where(i < nt - 1, glu(av_n, ag_n), 0.0)
        _fill_phases(ext_ref, ph_ref)

        def visit(l0, r0, taps):
            lanes = slice(l0, l0 + 128)
            accs = [cb_ref[:, lanes], None]
            for n, (k, slab) in enumerate(taps):
                term = w_ref[k:k + 1, lanes] * slab
                accs[n % 2] = term if accs[n % 2] is None else accs[n % 2] + term
            cv_ref[pl.ds(r0, rb), lanes] = accs[0] + accs[1]

        _for_taps(ext_ref, ph_ref, tm, rb, lambda k: k + HALO - CONV_PAD, visit)
        cv = cv_ref[...]
        c_ref[...] = cv.astype(ACT)
        xhat, _ = _ln_norm(cv)
        ln = xhat * lg_ref[...] + lb_ref[...]
        gc = gc_ref[...].astype(F32)
        y_ref[...] = (ln * _sigmoid(ln) * (gc * _sigmoid(gc))).astype(ACT)

    vec = pl.BlockSpec((1, C_BR), lambda i: (0, 0))
    return pl.pallas_call(
        body, name="conv_fwd", grid=(nt,),
        out_shape=(jax.ShapeDtypeStruct((s, 2 * C_BR), ACT), jax.ShapeDtypeStruct((s, C_BR), ACT)),
        in_specs=_halo_specs(tm, s, 0) + _halo_specs(tm, s, 1)
        + [pl.BlockSpec((tm, C_BR), lambda i: (i, 2)),
           pl.BlockSpec((CONV_ROWS, C_BR), lambda i: (0, 0)), vec, vec, vec],
        out_specs=(pl.BlockSpec((tm, C_BR), lambda i: (i, 0)), pl.BlockSpec((tm, C_BR), lambda i: (i, 0))),
        scratch_shapes=[pltpu.VMEM((tm + 2 * HALO, C_BR), F32),
                        pltpu.VMEM((7, tm + 2 * HALO - PHASE_ROWS_LESS, C_BR), F32),
                        pltpu.VMEM((tm, C_BR), F32)],
        compiler_params=_params(("parallel",)),
    )(proj, proj, proj, proj, proj, proj, proj, conv_w, conv_b, ln_g, ln_b)


def _sgu_fwd(proj, y, ln_g, ln_b, ws, bsb, tm):
    s = proj.shape[0]

    def body(u_ref, v_ref, gs_ref, y_in, lg_ref, lb_ref, ws_ref, bsb_ref, y_ref):
        del y_in
        xhat, _ = _ln_norm(v_ref[...].astype(F32))
        vn = (xhat * lg_ref[...] + lb_ref[...]).astype(BF16)
        for cidx in range(tm // CHUNK):
            rows = slice(cidx * CHUNK, (cidx + 1) * CHUNK)
            for h in range(HEADS):
                cols = slice(h * HEAD_DIM, (h + 1) * HEAD_DIM)
                mixed = jnp.dot(ws_ref[h], vn[rows, cols], preferred_element_type=F32) + bsb_ref[:, cols]
                gs = gs_ref[rows, cols].astype(F32)
                y_ref[rows, cols] = (u_ref[rows, cols].astype(F32) * mixed * (gs * _sigmoid(gs))).astype(ACT)

    vec = pl.BlockSpec((1, C_BR), lambda i: (0, 0))
    return pl.pallas_call(
        body, name="sgu_fwd", grid=(s // tm,),
        out_shape=jax.ShapeDtypeStruct((s, 2 * C_BR), ACT),
        in_specs=[pl.BlockSpec((tm, C_BR), lambda i: (i, 3)),
                  pl.BlockSpec((tm, C_BR), lambda i: (i, 4)),
                  pl.BlockSpec((tm, C_BR), lambda i: (i, 5)),
                  pl.BlockSpec(memory_space=pl.ANY),
                  vec, vec,
                  pl.BlockSpec((HEADS, CHUNK, CHUNK), lambda i: (0, 0, 0)),
                  pl.BlockSpec((CHUNK, C_BR), lambda i: (0, 0))],
        out_specs=pl.BlockSpec((tm, C_BR), lambda i: (i, 1)),
        input_output_aliases={3: 0},
        compiler_params=_params(("parallel",)),
    )(proj, proj, proj, y, ln_g, ln_b, ws, bsb)


def _out_loss(x, y, wout, final_g, target, tm):
    s = x.shape[0]
    nt = s // tm
    inv_d = 1.0 / D_MODEL

    def body(x_ref, y_ref, w_ref, g_ref, t_ref, dx2_ref, dy_ref, dw_ref, loss_ref, dfg_ref, acc_ref):
        i = pl.program_id(0)

        @pl.when(i == 0)
        def _():
            acc_ref[...] = jnp.zeros_like(acc_ref)
            loss_ref[...] = jnp.zeros_like(loss_ref)
            dfg_ref[...] = jnp.zeros_like(dfg_ref)

        yb = y_ref[...]
        x2 = x_ref[...] + jnp.dot(yb, w_ref[...], preferred_element_type=F32)
        r2 = lax.rsqrt(jnp.mean(x2 * x2, axis=-1, keepdims=True) + EPS)
        n = x2 * r2
        g = g_ref[...]
        e = n * g - t_ref[...]
        loss_ref[...] += _fold8(e * e)
        dout = e * inv_d
        dfg_ref[...] += _fold8(dout * n)
        dn = dout * g
        dx2 = r2 * (dn - n * jnp.mean(dn * n, axis=-1, keepdims=True))
        dx2_ref[...] = dx2
        dxb = dx2.astype(BF16)
        dy_ref[...] = lax.dot_general(dxb, w_ref[...], (((1,), (1,)), ((), ())),
                                      preferred_element_type=F32).astype(ACT)
        acc_ref[...] += lax.dot_general(yb, dxb, (((0,), (0,)), ((), ())), preferred_element_type=F32)

        @pl.when(i == nt - 1)
        def _():
            dw_ref[...] = acc_ref[...].astype(BF16)

    part = pl.BlockSpec((8, D_MODEL), lambda i: (0, 0))
    return pl.pallas_call(
        body, name="out_loss", grid=(nt,),
        out_shape=(jax.ShapeDtypeStruct((s, D_MODEL), F32), jax.ShapeDtypeStruct((s, 2 * C_BR), ACT),
                   jax.ShapeDtypeStruct((2 * C_BR, D_MODEL), BF16),
                   jax.ShapeDtypeStruct((8, D_MODEL), F32), jax.ShapeDtypeStruct((8, D_MODEL), F32)),
        in_specs=[pl.BlockSpec((tm, D_MODEL), lambda i: (i, 0)),
                  pl.BlockSpec((tm, 2 * C_BR), lambda i: (i, 0)),
                  pl.BlockSpec((2 * C_BR, D_MODEL), lambda i: (0, 0), pipeline_mode=pl.Buffered(1)),
                  pl.BlockSpec((1, D_MODEL), lambda i: (0, 0)),
                  pl.BlockSpec((tm, D_MODEL), lambda i: (i, 0))],
        out_specs=(pl.BlockSpec((tm, D_MODEL), lambda i: (i, 0)),
                   pl.BlockSpec((tm, 2 * C_BR), lambda i: (i, 0)),
                   pl.BlockSpec((2 * C_BR, D_MODEL), lambda i: (0, 0), pipeline_mode=pl.Buffered(1)), part, part),
        scratch_shapes=[pltpu.VMEM((2 * C_BR, D_MODEL), F32)],
        compiler_params=_params(("arbitrary",)),
    )(x, y, wout, final_g, target)


def _conv_bwd(proj, cv, dy, conv_w, ln_g, ln_b, dwout, tm):
    s = proj.shape[0]
    nt = s // tm
    rb = 64
    wo_rows = dwout.shape[1]

    def body(av_ref, ag_ref, gc_p, gc_m, gc_n, c_p, c_m, c_n, dy_p, dy_m, dy_n, w_ref, lg_ref, lb_ref,
             dwout_ref, dp_ref, dwc_ref, dcb_ref, dlg_ref, dlb_ref, gwo_ref,
             dce_ref, ph_ref, land_ref, send_sems, recv_sems, loc_sem):
        i = pl.program_id(0)
        px, py, pc = _place()
        me = 4 * px + 2 * py + pc

        def exchanges():
            out = []
            for rel in range(1, N_DEV):
                qx, qy, qc = px ^ (rel >> 2), py ^ ((rel >> 1) & 1), pc ^ (rel & 1)
                out.append(pltpu.make_async_remote_copy(
                    src_ref=dwout_ref.at[4 * qx + 2 * qy + qc], dst_ref=land_ref.at[me],
                    send_sem=send_sems.at[rel - 1], recv_sem=recv_sems.at[rel - 1],
                    device_id=(qx, qy, qc), device_id_type=MESH))
            return out

        own = pltpu.make_async_copy(dwout_ref.at[me], land_ref.at[me], loc_sem)

        @pl.when(i == 0)
        def _():
            dwc_ref[...] = jnp.zeros_like(dwc_ref)
            dcb_ref[...] = jnp.zeros_like(dcb_ref)
            dlg_ref[...] = jnp.zeros_like(dlg_ref)
            dlb_ref[...] = jnp.zeros_like(dlb_ref)
            own.start()
            for cp in exchanges():
                cp.start()

        def ext(p, m, n):
            return jnp.concatenate([p[...], m[...], n[...]], axis=0).astype(F32)

        main = slice(HALO, HALO + tm)
        cf, gc, dyc = ext(c_p, c_m, c_n), ext(gc_p, gc_m, gc_n), ext(dy_p, dy_m, dy_n)
        xhat, rstd = _ln_norm(cf)
        lg = lg_ref[...]
        ln = xhat * lg + lb_ref[...]
        s_ln, s_gc = _sigmoid(ln), _sigmoid(gc)
        dln = dyc * (gc * s_gc) * (s_ln * (1.0 + ln * (1.0 - s_ln)))
        dp_ref[:, 2 * C_BR:] = (dyc[main] * (ln[main] * s_ln[main])
                                * (s_gc[main] * (1.0 + gc[main] * (1.0 - s_gc[main])))).astype(ACT)
        dlg_ref[...] += _fold8(dln[main] * xhat[main])
        dlb_ref[...] += _fold8(dln[main])
        dc = _ln_bwd(dln, xhat, rstd, lg)
        dcb_ref[...] += _fold8(dc[main])
        dce_ref[0:HALO, :] = jnp.where(i > 0, dc[0:HALO], 0.0)
        dce_ref[main, :] = dc[main]
        dce_ref[HALO + tm:, :] = jnp.where(i < nt - 1, dc[HALO + tm:], 0.0)
        _fill_phases(dce_ref, ph_ref)


        def visit(l0, r0, taps):
            rows, lanes = pl.ds(r0, rb), slice(l0, l0 + 128)
            av = av_ref[rows, lanes].astype(F32)
            sa = _sigmoid(ag_ref[rows, lanes].astype(F32))
            glu_blk = av * sa
            accs = [None, None]
            for n, (k, slab) in enumerate(taps):
                term = w_ref[k:k + 1, lanes] * slab
                accs[n % 2] = term if accs[n % 2] is None else accs[n % 2] + term
                dwc_ref[8 * k:8 * k + 8, lanes] += _fold8(glu_blk * slab)
            acc = accs[0] + accs[1]
            dp_ref[rows, lanes] = (acc * sa).astype(ACT)
            dp_ref[rows, C_BR + l0:C_BR + l0 + 128] = (acc * av * sa * (1.0 - sa)).astype(ACT)

        _for_taps(dce_ref, ph_ref, tm, rb, lambda k: HALO + CONV_PAD - k, visit)

        @pl.when(i == nt - 1)
        def _():
            copies = exchanges()
            own.wait()
            for cp in copies:
                cp.wait_recv()

            def step(t, carry):
                sl = pl.ds(pl.multiple_of(t * 64, 64), 64)
                g = land_ref[0, sl, :].astype(F32)
                for d in range(1, N_DEV):
                    g = g + land_ref[d, sl, :].astype(F32)
                gwo_ref[sl, :] = g
                return carry

            lax.fori_loop(0, wo_rows // 64, step, 0)
            for cp in copies:
                cp.wait_send()

    vec = pl.BlockSpec((1, C_BR), lambda i: (0, 0))
    part = pl.BlockSpec((8, C_BR), lambda i: (0, 0))
    return pl.pallas_call(
        body, name="conv_bwd", grid=(nt,),
        out_shape=(jax.ShapeDtypeStruct((s, D_IN), ACT), jax.ShapeDtypeStruct((8 * CONV_ROWS, C_BR), F32),
                   jax.ShapeDtypeStruct((8, C_BR), F32), jax.ShapeDtypeStruct((8, C_BR), F32),
                   jax.ShapeDtypeStruct((8, C_BR), F32), jax.ShapeDtypeStruct(dwout.shape[1:], F32)),
        in_specs=[pl.BlockSpec((tm, C_BR), lambda i: (i, 0)), pl.BlockSpec((tm, C_BR), lambda i: (i, 1))]
        + _halo_specs(tm, s, 2) + _halo_specs(tm, s, 0) + _halo_specs(tm, s, 0)
        + [pl.BlockSpec((CONV_ROWS, C_BR), lambda i: (0, 0)), vec, vec, pl.BlockSpec(memory_space=pl.ANY)],
        out_specs=(pl.BlockSpec((tm, 3 * C_BR), lambda i: (i, 0)),
                   pl.BlockSpec((8 * CONV_ROWS, C_BR), lambda i: (0, 0)), part, part, part,
                   pl.BlockSpec(memory_space=pltpu.VMEM)),
        scratch_shapes=[pltpu.VMEM((tm + 2 * HALO, C_BR), F32),
                        pltpu.VMEM((7, tm + 2 * HALO - PHASE_ROWS_LESS, C_BR), F32),
                        pltpu.VMEM(dwout.shape, BF16),
                        pltpu.SemaphoreType.DMA((N_DEV - 1,)), pltpu.SemaphoreType.DMA((N_DEV - 1,)),
                        pltpu.SemaphoreType.DMA],
        compiler_params=_params(("arbitrary",)),
    )(proj, proj, proj, proj, proj, cv, cv, cv, dy, dy, dy, conv_w, ln_g, ln_b, dwout)


def _sgu_bwd(proj, dy, dproj, ln_g, ln_b, ws, wst, bsb, tm):
    s = proj.shape[0]

    def body(u_ref, v_ref, gs_ref, dy_ref, dp_in, lg_ref, lb_ref, ws_ref, wst_ref, bsb_ref,
             dp_ref, dws_ref, dba_ref, dlg_ref, dlb_ref, dvn_ref):
        del dp_in
        i = pl.program_id(0)

        @pl.when(i == 0)
        def _():
            dws_ref[...] = jnp.zeros_like(dws_ref)
            dba_ref[...] = jnp.zeros_like(dba_ref)
            dlg_ref[...] = jnp.zeros_like(dlg_ref)
            dlb_ref[...] = jnp.zeros_like(dlb_ref)

        xhat, rstd = _ln_norm(v_ref[...].astype(F32))
        lg = lg_ref[...]
        vn = (xhat * lg + lb_ref[...]).astype(BF16)
        for cidx in range(tm // CHUNK):
            rows = slice(cidx * CHUNK, (cidx + 1) * CHUNK)
            for h in range(HEADS):
                cols = slice(h * HEAD_DIM, (h + 1) * HEAD_DIM)
                ocols = slice(C_BR + h * HEAD_DIM, C_BR + (h + 1) * HEAD_DIM)
                gcols = slice(2 * C_BR + h * HEAD_DIM, 2 * C_BR + (h + 1) * HEAD_DIM)
                vb = vn[rows, cols]
                mixed = jnp.dot(ws_ref[h], vb, preferred_element_type=F32) + bsb_ref[:, cols]
                gs = gs_ref[rows, cols].astype(F32)
                sg = _sigmoid(gs)
                u = u_ref[rows, cols].astype(F32)
                dyb = dy_ref[rows, cols].astype(F32)
                t = dyb * (gs * sg)
                dp_ref[rows, cols] = (t * mixed).astype(ACT)
                dp_ref[rows, gcols] = (dyb * u * mixed * (sg * (1.0 + gs * (1.0 - sg)))).astype(ACT)
                dm = t * u
                dmb = dm.astype(BF16)
                dvn_ref[rows, cols] = jnp.dot(wst_ref[h], dmb, preferred_element_type=F32)
                dws_ref[h] += lax.dot_general(dmb, vb, (((1,), (1,)), ((), ())), preferred_element_type=F32)
                dba_ref[:, cols] += dm
        dvn = dvn_ref[...]
        dlg_ref[...] += _fold8(dvn * xhat)
        dlb_ref[...] += _fold8(dvn)
        dp_ref[:, C_BR:2 * C_BR] = _ln_bwd(dvn, xhat, rstd, lg).astype(ACT)

    vec = pl.BlockSpec((1, C_BR), lambda i: (0, 0))
    part = pl.BlockSpec((8, C_BR), lambda i: (0, 0))
    wsp = pl.BlockSpec((HEADS, CHUNK, CHUNK), lambda i: (0, 0, 0))
    return pl.pallas_call(
        body, name="sgu_bwd", grid=(s // tm,),
        out_shape=(jax.ShapeDtypeStruct((s, D_IN), ACT), jax.ShapeDtypeStruct((HEADS, CHUNK, CHUNK), F32),
                   jax.ShapeDtypeStruct((CHUNK, C_BR), F32), jax.ShapeDtypeStruct((8, C_BR), F32),
                   jax.ShapeDtypeStruct((8, C_BR), F32)),
        in_specs=[pl.BlockSpec((tm, C_BR), lambda i: (i, 3)),
                  pl.BlockSpec((tm, C_BR), lambda i: (i, 4)),
                  pl.BlockSpec((tm, C_BR), lambda i: (i, 5)),
                  pl.BlockSpec((tm, C_BR), lambda i: (i, 1)),
                  pl.BlockSpec(memory_space=pl.ANY),
                  vec, vec, wsp, wsp, pl.BlockSpec((CHUNK, C_BR), lambda i: (0, 0))],
        out_specs=(pl.BlockSpec((tm, 3 * C_BR), lambda i: (i, 1)), wsp,
                   pl.BlockSpec((CHUNK, C_BR), lambda i: (0, 0)), part, part),
        scratch_shapes=[pltpu.VMEM((tm, C_BR), F32)],
        input_output_aliases={4: 0},
        compiler_params=_params(("arbitrary",)),
    )(proj, proj, proj, dy, dproj, ln_g, ln_b, ws, wst, bsb)


def _dx(dproj, win_all, x, norm_g, dx2, tm):
    s = x.shape[0]

    def body(dp_ref, w_ref, x_ref, g_ref, dx2_ref, gx_ref, dng_ref):
        i = pl.program_id(0)

        @pl.when(i == 0)
        def _():
            dng_ref[...] = jnp.zeros_like(dng_ref)

        dh = None
        for j in range(N_DEV):
            term = lax.dot_general(dp_ref[:, j * W_BLK:(j + 1) * W_BLK], w_ref[j],
                                   (((1,), (1,)), ((), ())), preferred_element_type=F32)
            dh = term if dh is None else dh + term
        xf = x_ref[...]
        r = lax.rsqrt(jnp.mean(xf * xf, axis=-1, keepdims=True) + EPS)
        n = xf * r
        dng_ref[...] += _fold8(dh * n)
        dn = dh * g_ref[...]
        gx_ref[...] = dx2_ref[...] + r * (dn - n * jnp.mean(dn * n, axis=-1, keepdims=True))

    return pl.pallas_call(
        body, name="dx", grid=(s // tm,),
        out_shape=(jax.ShapeDtypeStruct((s, D_MODEL), F32), jax.ShapeDtypeStruct((8, D_MODEL), F32)),
        in_specs=[pl.BlockSpec((tm, D_IN), lambda i: (i, 0)),
                  pl.BlockSpec((N_DEV, D_MODEL, W_BLK), lambda i: (0, 0, 0), pipeline_mode=pl.Buffered(1)),
                  pl.BlockSpec((tm, D_MODEL), lambda i: (i, 0)),
                  pl.BlockSpec((1, D_MODEL), lambda i: (0, 0)),
                  pl.BlockSpec((tm, D_MODEL), lambda i: (i, 0))],
        out_specs=(pl.BlockSpec((tm, D_MODEL), lambda i: (i, 0)), pl.BlockSpec((8, D_MODEL), lambda i: (0, 0))),
        compiler_params=_params(("arbitrary",)),
    )(dproj, win_all, x, norm_g, dx2)


def _dwin_comm(ht, dproj, order, parts, dwc, dba, dws, tk):
    s = ht.shape[1]
    nk = s // tk
    n_part = len(parts)

    def body(*refs):
        order_ref, ht_ref, dp_ref = refs[:3]
        del order_ref
        part_refs = refs[3:3 + n_part]
        dwc_ref, dba_ref, dws_ref = refs[3 + n_part:6 + n_part]
        gw_ref, red_ref, wsr_ref, cws_ref, loss_ref = refs[6 + n_part:11 + n_part]
        (acc_ref, all1_ref, all2_ref, out_s, land_s, out_x, land_x,
         send_sems, recv_sems, send_s, recv_s, send_x, recv_x) = refs[11 + n_part:]
        jj, k = pl.program_id(0), pl.program_id(1)
        x, y, c = _place()
        me = 4 * x + 2 * y + c
        sibling = (x, y, 1 - c)
        chips = [(1 - x, 1 - y), (1 - x, y), (x, 1 - y)]
        last = k == nk - 1

        def exchanges():
            out = []
            for rel in range(1, N_DEV):
                peer = (x ^ (rel >> 2), y ^ ((rel >> 1) & 1), c ^ (rel & 1))
                for a, buf in enumerate((all1_ref, all2_ref)):
                    out.append(pltpu.make_async_remote_copy(
                        src_ref=buf.at[me], dst_ref=buf.at[me],
                        send_sem=send_sems.at[a, rel - 1], recv_sem=recv_sems.at[a, rel - 1],
                        device_id=peer, device_id_type=MESH))
            return out

        def to_sibling(slot):
            return pltpu.make_async_remote_copy(
                src_ref=out_s.at[slot], dst_ref=land_s.at[slot],
                send_sem=send_s.at[slot], recv_sem=recv_s.at[slot], device_id=sibling, device_id_type=MESH)

        def to_chip(slot):
            return pltpu.make_async_remote_copy(
                src_ref=out_x.at[slot], dst_ref=land_x.at[slot],
                send_sem=send_x.at[slot], recv_sem=recv_x.at[slot],
                device_id=(*chips[slot], c), device_id_type=MESH)

        @pl.when((jj == 0) & (k == 0))
        def _():
            all1_ref[me] = jnp.zeros((SMALL_ROWS, C_BR), F32)
            for row, p_ref in zip((ROW_NORM_G, ROW_CONV_B, ROW_CLN_G, ROW_CLN_B, ROW_SLN_G, ROW_SLN_B,
                                   ROW_FINAL_G, ROW_LOSS), part_refs):
                all1_ref[me, row:row + 1, :] = jnp.sum(p_ref[...], axis=0, keepdims=True)
            ones = jnp.ones((8, HEAD_DIM), F32)
            brow = [lax.dot_general(ones, dba_ref[:, h * HEAD_DIM:(h + 1) * HEAD_DIM], (((1,), (1,)), ((), ())),
                                    precision=lax.Precision.HIGHEST, preferred_element_type=F32)[0:1]
                    for h in range(HEADS)]
            all1_ref[me, ROW_B_S:ROW_B_S + 1, :] = jnp.concatenate(brow, axis=1)
            for t in range(CONV_WIDTH):
                all1_ref[me, ROW_CONV_W + t:ROW_CONV_W + t + 1, :] = jnp.sum(
                    dwc_ref[8 * t:8 * t + 8, :], axis=0, keepdims=True)
            all2_ref[me] = dws_ref[...]
            for cp in exchanges():
                cp.start()

        @pl.when(k == 0)
        def _():
            acc_ref[...] = jnp.zeros_like(acc_ref)

        acc_ref[...] += jnp.dot(ht_ref[...], dp_ref[...], preferred_element_type=F32)

        for slot in range(4):
            @pl.when((jj == 2 * slot) & last)
            def _(slot=slot):
                out_s[slot] = acc_ref[...].astype(BF16)
                to_sibling(slot).start()

        for slot in range(3):
            @pl.when((jj == 2 * slot + 1) & last)
            def _(slot=slot):
                to_sibling(slot).wait_recv()
                out_x[slot] = (acc_ref[...] + land_s[slot].astype(F32)).astype(BF16)
                to_chip(slot).start()

        @pl.when((jj == N_DEV - 1) & last)
        def _():
            to_sibling(3).wait_recv()
            total = acc_ref[...] + land_s[3].astype(F32)
            for slot in range(3):
                to_chip(slot).wait_recv()
                total = total + land_x[slot].astype(F32)
            gw_ref[...] = total

            copies = exchanges()
            for cp in copies:
                cp.wait_recv()
            tot = all1_ref[0]
            for d in range(1, N_DEV):
                tot = tot + all1_ref[d]
            red_ref[...] = tot
            loss_ref[...] = jnp.broadcast_to(
                jnp.sum(tot[ROW_LOSS:ROW_LOSS + 1, :], axis=1, keepdims=True) * (0.5 / D_MODEL), loss_ref.shape)
            shard = jnp.zeros(cws_ref.shape, F32)
            for d in range(N_DEV):
                shard = jnp.where(me == d, tot[ROW_CONV_W:ROW_CONV_W + CONV_ROWS, d * 128:(d + 1) * 128], shard)
            cws_ref[...] = shard
            tot2 = all2_ref[0]
            for d in range(1, N_DEV):
                tot2 = tot2 + all2_ref[d]
            wsr_ref[...] = tot2
            for cp in copies + [to_sibling(slot) for slot in range(4)] + [to_chip(slot) for slot in range(3)]:
                cp.wait_send()

    vm = pl.BlockSpec(memory_space=pltpu.VMEM)
    blk = (D_MODEL, W_BLK)
    return pl.pallas_call(
        body, name="dwin",
        grid_spec=pltpu.PrefetchScalarGridSpec(
            num_scalar_prefetch=1, grid=(N_DEV, nk),
            in_specs=[pl.BlockSpec((D_MODEL, tk), lambda jj, k, o: (0, k)),
                      pl.BlockSpec((tk, W_BLK), lambda jj, k, o: (k, o[jj]))] + [vm] * (n_part + 3),
            out_specs=(vm, vm, vm, vm, vm),
            scratch_shapes=[pltpu.VMEM(blk, F32),
                            pltpu.VMEM((N_DEV, SMALL_ROWS, C_BR), F32), pltpu.VMEM((N_DEV,) + dws.shape, F32),
                            pltpu.VMEM((4,) + blk, BF16), pltpu.VMEM((4,) + blk, BF16),
                            pltpu.VMEM((3,) + blk, BF16), pltpu.VMEM((3,) + blk, BF16),
                            pltpu.SemaphoreType.DMA((2, N_DEV - 1)), pltpu.SemaphoreType.DMA((2, N_DEV - 1)),
                            pltpu.SemaphoreType.DMA((4,)), pltpu.SemaphoreType.DMA((4,)),
                            pltpu.SemaphoreType.DMA((3,)), pltpu.SemaphoreType.DMA((3,))]),
        out_shape=(jax.ShapeDtypeStruct(blk, F32),
                   jax.ShapeDtypeStruct((SMALL_ROWS, C_BR), F32), jax.ShapeDtypeStruct(dws.shape, F32),
                   jax.ShapeDtypeStruct((CONV_ROWS, 128), F32), jax.ShapeDtypeStruct((8, 128), F32)),
        compiler_params=_params(("arbitrary", "arbitrary")),
    )(order, ht, dproj, *parts, dwc, dba, dws)


def _adamw_math(w, g, m, v):
    m = ADAM_B1 * m + (1.0 - ADAM_B1) * g
    v = ADAM_B2 * v + (1.0 - ADAM_B2) * (g * g)
    m_hat = m / (1.0 - ADAM_B1 ** ADAM_STEP)
    v_hat = v / (1.0 - ADAM_B2 ** ADAM_STEP)
    delta = -ADAM_LR * (m_hat / (jnp.sqrt(v_hat) + ADAM_EPS) + ADAM_WD * w)
    return delta, m, v


def _adamw(g, w, m, v, name, tr):
    rows, cols = w.shape

    def body(g_ref, w_ref, m_ref, v_ref, d_ref, mo_ref, vo_ref):
        d_ref[...], mo_ref[...], vo_ref[...] = _adamw_math(w_ref[...], g_ref[...], m_ref[...], v_ref[...])

    spec = pl.BlockSpec((tr, cols), lambda i: (i, 0))
    sds = jax.ShapeDtypeStruct((rows, cols), F32)
    return pl.pallas_call(
        body, name=name, grid=(rows // tr,), out_shape=(sds, sds, sds),
        in_specs=[spec] * 4, out_specs=(spec, spec, spec),
        compiler_params=_params(("parallel",)),
    )(g, w, m, v)


def _adamw_rows(red, rows, ws, ms, vs):
    n = len(rows)

    def body(*refs):
        red_ref = refs[0]
        w_refs, m_refs, v_refs = refs[1:1 + n], refs[1 + n:1 + 2 * n], refs[1 + 2 * n:1 + 3 * n]
        outs = refs[1 + 3 * n:]
        for t, row in enumerate(rows):
            g = red_ref[row:row + 1, :]
            delta, m, v = _adamw_math(w_refs[t][...], g, m_refs[t][...], v_refs[t][...])
            outs[t][...] = g
            outs[n + t][...] = delta
            outs[2 * n + t][...] = m
            outs[3 * n + t][...] = v

    vm = pl.BlockSpec(memory_space=pltpu.VMEM)
    sds = jax.ShapeDtypeStruct((1, C_BR), F32)
    res = pl.pallas_call(
        body, name="adamw_rows", out_shape=(sds,) * (4 * n),
        in_specs=[vm] * (1 + 3 * n), out_specs=(vm,) * (4 * n),
        compiler_params=_params(),
    )(red, *ws, *ms, *vs)
    return res[:n], res[n:2 * n], res[2 * n:3 * n], res[3 * n:]


def kernel(x, norm_g, w_in, conv_w, conv_b, conv_ln_g, conv_ln_b, sgu_ln_g, sgu_ln_b, w_s, b_s, w_out, final_g, loss_target, m_norm_g, m_w_in, m_conv_w, m_conv_b, m_conv_ln_g, m_conv_ln_b, m_sgu_ln_g, m_sgu_ln_b, m_w_s, m_b_s, m_w_out, m_final_g, v_norm_g, v_w_in, v_conv_w, v_conv_b, v_conv_ln_g, v_conv_ln_b, v_sgu_ln_g, v_sgu_ln_b, v_w_s, v_b_s, v_w_out, v_final_g):
    s = x.shape[1]
    xs = x.reshape(s, D_MODEL)
    tgt = loss_target.reshape(s, D_MODEL)
    tm = min(256, s)

    cw_pad = jnp.pad(conv_w[0], ((0, CONV_ROWS - CONV_WIDTH), (0, 0)))
    px, py, pc = _place()
    blocks = [(px, py, pc), (px, py, 1 - pc)]
    blocks += [(*chip, core) for chip in ((1 - px, py), (px, 1 - py), (1 - px, 1 - py)) for core in (pc, 1 - pc)]
    order = jnp.stack([4 * bx + 2 * by + bc for bx, by, bc in blocks]).astype(jnp.int32)
    proj, ht, win_all, wout_all, cw_all = _proj_ag(xs, norm_g, w_in[0], w_out[0], cw_pad, order, min(1024, s))
    wout_full = wout_all.reshape(2 * C_BR, D_MODEL)
    cw_full = jnp.transpose(cw_all, (1, 0, 2)).reshape(CONV_ROWS, C_BR)

    ws = w_s[0].astype(BF16)
    wst = jnp.transpose(w_s[0], (0, 2, 1)).astype(BF16)
    bsb = jnp.repeat(jnp.transpose(b_s[0]), HEAD_DIM, axis=1)
    fg = final_g.reshape(1, D_MODEL)

    big = min(512, s)
    y, cv = _conv_fwd(proj, cw_full, conv_b, conv_ln_g, conv_ln_b, big)
    y = _sgu_fwd(proj, y, sgu_ln_g, sgu_ln_b, ws, bsb, big)
    dx2, dy, dwout, loss_p, dfg_p = _out_loss(xs, y, wout_full, fg, tgt, min(512, s))
    dproj, dwc, dcb_p, dclg_p, dclb_p, g_w_out = _conv_bwd(
        proj, cv, dy, cw_full, conv_ln_g, conv_ln_b, dwout.reshape(N_DEV, 2 * C_BR // N_DEV, D_MODEL), tm)
    dproj, dws, dba, dslg_p, dslb_p = _sgu_bwd(proj, dy, dproj, sgu_ln_g, sgu_ln_b, ws, wst, bsb, big)
    grad_x, dng_p = _dx(dproj, win_all, xs, norm_g, dx2, big)
    rs_blocks = [(*chip, core) for chip in ((1 - px, 1 - py), (1 - px, py), (px, 1 - py), (px, py))
                 for core in (1 - pc, pc)]
    rs_order = jnp.stack([4 * bx + 2 * by + bc for bx, by, bc in rs_blocks]).astype(jnp.int32)
    g_w_in, red, g_w_s, g_cw, loss8 = _dwin_comm(
        ht, dproj, rs_order, [dng_p, dcb_p, dclg_p, dclb_p, dslg_p, dslb_p, dfg_p, loss_p], dwc, dba,
        dws.reshape(HEADS * CHUNK, CHUNK), min(1024, s))
    loss = loss8[0, 0]

    d_w_in, nm_w_in, nv_w_in = _adamw(g_w_in, w_in[0], m_w_in[0], v_w_in[0], "adamw_w_in", 256)
    d_w_out, nm_w_out, nv_w_out = _adamw(g_w_out, w_out[0], m_w_out[0], v_w_out[0], "adamw_w_out", 256)
    g_cw = g_cw[:CONV_WIDTH]
    d_cw, nm_cw, nv_cw = _adamw(g_cw, conv_w[0], m_conv_w[0], v_conv_w[0], "adamw_conv_w", CONV_WIDTH)
    flat = lambda a: a.reshape(HEADS * CHUNK, CHUNK)
    d_ws, nm_ws, nv_ws = _adamw(g_w_s, flat(w_s), flat(m_w_s), flat(v_w_s), "adamw_w_s", HEADS * CHUNK)
    row = lambda a: a.reshape(1, C_BR)
    rows = (ROW_NORM_G, ROW_CONV_B, ROW_CLN_G, ROW_CLN_B, ROW_SLN_G, ROW_SLN_B, ROW_B_S, ROW_FINAL_G)
    g_r, d_r, m_r, v_r = _adamw_rows(
        red, rows,
        [norm_g, conv_b, conv_ln_g, conv_ln_b, sgu_ln_g, sgu_ln_b, row(b_s), row(final_g)],
        [m_norm_g, m_conv_b, m_conv_ln_g, m_conv_ln_b, m_sgu_ln_g, m_sgu_ln_b, row(m_b_s), row(m_final_g)],
        [v_norm_g, v_conv_b, v_conv_ln_g, v_conv_ln_b, v_sgu_ln_g, v_sgu_ln_b, row(v_b_s), row(v_final_g)])

    def leaves(r, w_in_l, cw_l, ws_l, w_out_l):
        return (r[0], w_in_l[None], cw_l[None], r[1], r[2], r[3], r[4], r[5],
                ws_l.reshape(1, HEADS, CHUNK, CHUNK), r[6].reshape(1, HEADS, CHUNK), w_out_l[None],
                r[7].reshape(D_MODEL))

    return (loss, grad_x.reshape(1, s, D_MODEL),
            *leaves(g_r, g_w_in, g_cw, g_w_s, g_w_out),
            *leaves(d_r, d_w_in, d_cw, d_ws, d_w_out),
            *leaves(m_r, nm_w_in, nm_cw, nm_ws, nm_w_out),
            *leaves(v_r, nv_w_in, nv_cw, nv_ws, nv_w_out))
```

```python
import functools

import jax
import jax.numpy as jnp
from jax import lax
from jax.experimental import pallas as pl
from jax.experimental.pallas import tpu as pltpu

F32 = jnp.float32
BF16 = jnp.bfloat16
ACT = jnp.bfloat16

D_MODEL = 1024
C_BR = 1024
D_IN = 6 * C_BR
N_DEV = 8
W_BLK = D_IN // N_DEV
HEADS = 8
HEAD_DIM = 128
CHUNK = 128
CONV_WIDTH = 31
CONV_PAD = CONV_WIDTH // 2
HALO = 16
CONV_ROWS = 32
EPS = 1e-6

ADAM_LR = 0.001
ADAM_B1 = 0.9
ADAM_B2 = 0.999
ADAM_EPS = 1e-08
ADAM_WD = 0.01
ADAM_STEP = 10

VMEM_LIMIT = 56 * 1024 * 1024
MESH = pl.DeviceIdType.MESH

ROW_NORM_G, ROW_CONV_B, ROW_CLN_G, ROW_CLN_B, ROW_SLN_G, ROW_SLN_B, ROW_FINAL_G, ROW_B_S, ROW_LOSS = range(9)
ROW_CONV_W = 16
SMALL_ROWS = ROW_CONV_W + CONV_ROWS


def _params(sem=None, **kw):
    return pltpu.CompilerParams(dimension_semantics=sem, vmem_limit_bytes=VMEM_LIMIT, **kw)


def _fold8(a):
    r, n = a.shape
    return a.reshape(r // 8, 8, n).sum(axis=0)


def _sigmoid(z):
    return 1.0 / (1.0 + jnp.exp(-z))


def _ln_norm(xf):
    mu = jnp.mean(xf, axis=-1, keepdims=True)
    xc = xf - mu
    var = jnp.mean(xc * xc, axis=-1, keepdims=True)
    rstd = lax.rsqrt(var + EPS)
    return xc * rstd, rstd


def _ln_bwd(dy, xhat, rstd, g):
    dxhat = dy * g
    m1 = jnp.mean(dxhat, axis=-1, keepdims=True)
    m2 = jnp.mean(dxhat * xhat, axis=-1, keepdims=True)
    return rstd * (dxhat - m1 - xhat * m2)


def _place():
    return lax.axis_index("x"), lax.axis_index("y"), lax.axis_index("c")


def _proj_ag(x, norm_g, w_in, w_out, conv_w, order, tm):
    s = x.shape[0]
    nt = s // tm

    def body(order_ref, x_ref, g_ref, win_ref, wout_ref, cw_ref,
             proj_ref, ht_ref, win_all, wout_all, cw_all,
             h_ref, win_buf, wout_buf, cw_buf, send_sems, recv_sems, save_sems):
        jj, i = pl.program_id(0), pl.program_id(1)
        x_, y_, c_ = _place()
        me, sibling = (x_, y_, c_), (x_, y_, 1 - c_)
        chips = [(1 - x_, y_), (x_, 1 - y_), (1 - x_, 1 - y_)]
        bufs = (win_buf, wout_buf, cw_buf)
        outs = (win_all, wout_all, cw_all)
        start = i == 0

        def index(px, py, pc):
            return 4 * px + 2 * py + pc

        def copy(a, k, block, to):
            return pltpu.make_async_remote_copy(
                src_ref=bufs[a].at[index(*block)], dst_ref=bufs[a].at[index(*block)],
                send_sem=send_sems.at[a, k], recv_sem=recv_sems.at[a, k],
                device_id=to, device_id_type=MESH)

        def save(a, slot, block):
            return pltpu.make_async_copy(bufs[a].at[index(*block)], outs[a].at[index(*block)], save_sems.at[a, slot])

        def first(a):
            return [copy(a, 0, me, sibling)] + [copy(a, 1 + j, me, (*chip, c_)) for j, chip in enumerate(chips)]

        def saves(a):
            blocks = [me, sibling] + [(*chip, core) for chip in chips for core in (c_, 1 - c_)]
            return [save(a, slot, block) for slot, block in enumerate(blocks)]

        @pl.when((jj == 0) & start)
        def _():
            win_buf[index(*me)] = win_ref[...].astype(BF16)
            wout_buf[index(*me)] = wout_ref[...].astype(BF16)
            cw_buf[index(*me)] = cw_ref[...]
            for a in range(3):
                for cp in first(a):
                    cp.start()
                saves(a)[0].start()

        @pl.when((jj == 1) & start)
        def _():
            copy(0, 0, sibling, me).wait_recv()
            saves(0)[1].start()

        for j, chip in enumerate(chips):
            @pl.when((jj == 2 + 2 * j) & start)
            def _(j=j, chip=chip):
                copy(0, 1 + j, (*chip, c_), me).wait_recv()
                copy(0, 4 + j, (*chip, c_), sibling).start()
                saves(0)[2 + 2 * j].start()

            @pl.when((jj == 3 + 2 * j) & start)
            def _(j=j, chip=chip):
                copy(0, 4 + j, (*chip, 1 - c_), me).wait_recv()
                saves(0)[3 + 2 * j].start()

        @pl.when(jj == 0)
        def _():
            xf = x_ref[...]
            r = lax.rsqrt(jnp.mean(xf * xf, axis=-1, keepdims=True) + EPS)
            hf = xf * r * g_ref[...]
            h_ref[i] = hf.astype(BF16)
            ht_ref[...] = hf.T.astype(BF16)

        proj_ref[...] = jnp.dot(h_ref[i], win_buf[order_ref[jj]], preferred_element_type=F32).astype(ACT)

        @pl.when((jj == N_DEV - 1) & (i == nt - 1))
        def _():
            passed = [copy(0, 4 + j, (*chip, c_), sibling) for j, chip in enumerate(chips)]
            for a in (1, 2):
                for j, chip in enumerate(chips):
                    copy(a, 1 + j, (*chip, c_), me).wait_recv()
                    fwd = copy(a, 4 + j, (*chip, c_), sibling)
                    fwd.start()
                    passed.append(fwd)
                    saves(a)[2 + 2 * j].start()
            for a in (1, 2):
                copy(a, 0, sibling, me).wait_recv()
                saves(a)[1].start()
                for j, chip in enumerate(chips):
                    copy(a, 4 + j, (*chip, 1 - c_), me).wait_recv()
                    saves(a)[3 + 2 * j].start()
            for cp in saves(0) + saves(1) + saves(2):
                cp.wait()
            for cp in first(0) + first(1) + first(2) + passed:
                cp.wait_send()

    vm = pl.BlockSpec(memory_space=pltpu.VMEM)
    hbm = pl.BlockSpec(memory_space=pl.ANY)
    once = lambda jj, i: jnp.where(jj == 0, i, nt - 1)
    stacked = [(N_DEV,) + w.shape for w in (w_in, w_out, conv_w)]
    return pl.pallas_call(
        body, name="proj_ag",
        grid_spec=pltpu.PrefetchScalarGridSpec(
            num_scalar_prefetch=1, grid=(N_DEV, nt),
            in_specs=[pl.BlockSpec((tm, D_MODEL), lambda jj, i, o: (once(jj, i), 0)),
                      pl.BlockSpec((1, D_MODEL), lambda jj, i, o: (0, 0)), vm, vm, vm],
            out_specs=(pl.BlockSpec((tm, W_BLK), lambda jj, i, o: (i, o[jj])),
                       pl.BlockSpec((D_MODEL, tm), lambda jj, i, o: (0, once(jj, i))), hbm, hbm, hbm),
            scratch_shapes=[pltpu.VMEM((nt, tm, D_MODEL), BF16),
                            pltpu.VMEM(stacked[0], BF16), pltpu.VMEM(stacked[1], BF16), pltpu.VMEM(stacked[2], F32),
                            pltpu.SemaphoreType.DMA((3, 7)), pltpu.SemaphoreType.DMA((3, 7)),
                            pltpu.SemaphoreType.DMA((3, N_DEV))]),
        out_shape=(jax.ShapeDtypeStruct((s, D_IN), ACT), jax.ShapeDtypeStruct((D_MODEL, s), BF16),
                   jax.ShapeDtypeStruct(stacked[0], BF16), jax.ShapeDtypeStruct(stacked[1], BF16),
                   jax.ShapeDtypeStruct(stacked[2], F32)),
        compiler_params=_params(("arbitrary", "arbitrary")),
    )(order, x, norm_g, w_in, w_out, conv_w)


def _halo_specs(tm, s, col):
    per = tm // HALO
    last = s // HALO - 1
    return [pl.BlockSpec((HALO, C_BR), lambda i: (jnp.maximum(i * per - 1, 0), col)),
            pl.BlockSpec((tm, C_BR), lambda i: (i, col)),
            pl.BlockSpec((HALO, C_BR), lambda i: (jnp.minimum((i + 1) * per, last), col))]


def _conv_fwd(proj, conv_w3, conv_b, ln_g, ln_b, tm):
    s = proj.shape[0]
    nt = s // tm

    def body(av_p, av_m, av_n, ag_p, ag_m, ag_n, gc_ref, w_ref, cb_ref, lg_ref, lb_ref,
             y_ref, c_ref, ext_ref, cv_ref):
        i = pl.program_id(0)

        def glu(a_ref, g_ref):
            return a_ref[...].astype(F32) * _sigmoid(g_ref[...].astype(F32))

        def tiles(a):
            return a.reshape(a.shape[0], 8, 128)

        ext_ref[0:HALO] = tiles(jnp.where(i > 0, glu(av_p, ag_p), 0.0))
        ext_ref[HALO:HALO + tm] = tiles(glu(av_m, ag_m))
        ext_ref[HALO + tm:] = tiles(jnp.where(i < nt - 1, glu(av_n, ag_n), 0.0))

        nb = 16

        def step(t, carry):
            s0 = t * nb
            accs = [None] * nb
            for k in range(CONV_WIDTH):
                w = w_ref[k]
                for j in range(nb):
                    term = w * ext_ref[s0 + HALO - CONV_PAD + j + k]
                    accs[j] = term if accs[j] is None else accs[j] + term
            for j in range(nb):
                cv_ref[s0 + j] = accs[j]
            return carry

        lax.fori_loop(0, tm // nb, step, 0)
        cv = cv_ref[...].reshape(tm, C_BR) + cb_ref[...]
        c_ref[...] = cv.astype(ACT)
        xhat, _ = _ln_norm(cv)
        ln = xhat * lg_ref[...] + lb_ref[...]
        gc = gc_ref[...].astype(F32)
        y_ref[...] = (ln * _sigmoid(ln) * (gc * _sigmoid(gc))).astype(ACT)

    vec = pl.BlockSpec((1, C_BR), lambda i: (0, 0))
    return pl.pallas_call(
        body, name="conv_fwd", grid=(nt,),
        out_shape=(jax.ShapeDtypeStruct((s, 2 * C_BR), ACT), jax.ShapeDtypeStruct((s, C_BR), ACT)),
        in_specs=_halo_specs(tm, s, 0) + _halo_specs(tm, s, 1)
        + [pl.BlockSpec((tm, C_BR), lambda i: (i, 2)),
           pl.BlockSpec((CONV_ROWS, 8, 128), lambda i: (0, 0, 0)), vec, vec, vec],
        out_specs=(pl.BlockSpec((tm, C_BR), lambda i: (i, 0)), pl.BlockSpec((tm, C_BR), lambda i: (i, 0))),
        scratch_shapes=[pltpu.VMEM((tm + 2 * HALO, 8, 128), F32),
                        pltpu.VMEM((tm, 8, 128), F32)],
        compiler_params=_params(("parallel",)),
    )(proj, proj, proj, proj, proj, proj, proj, conv_w3, conv_b, ln_g, ln_b)


def _sgu_fwd(proj, y, ln_g, ln_b, ws, bsb, tm):
    s = proj.shape[0]

    def body(u_ref, v_ref, gs_ref, y_in, lg_ref, lb_ref, ws_ref, bsb_ref, y_ref):
        del y_in
        xhat, _ = _ln_norm(v_ref[...].astype(F32))
        vn = (xhat * lg_ref[...] + lb_ref[...]).astype(BF16)
        for cidx in range(tm // CHUNK):
            rows = slice(cidx * CHUNK, (cidx + 1) * CHUNK)
            for h in range(HEADS):
                cols = slice(h * HEAD_DIM, (h + 1) * HEAD_DIM)
                mixed = jnp.dot(ws_ref[h], vn[rows, cols], preferred_element_type=F32) + bsb_ref[:, cols]
                gs = gs_ref[rows, cols].astype(F32)
                y_ref[rows, cols] = (u_ref[rows, cols].astype(F32) * mixed * (gs * _sigmoid(gs))).astype(ACT)

    vec = pl.BlockSpec((1, C_BR), lambda i: (0, 0))
    return pl.pallas_call(
        body, name="sgu_fwd", grid=(s // tm,),
        out_shape=jax.ShapeDtypeStruct((s, 2 * C_BR), ACT),
        in_specs=[pl.BlockSpec((tm, C_BR), lambda i: (i, 3)),
                  pl.BlockSpec((tm, C_BR), lambda i: (i, 4)),
                  pl.BlockSpec((tm, C_BR), lambda i: (i, 5)),
                  pl.BlockSpec(memory_space=pl.ANY),
                  vec, vec,
                  pl.BlockSpec((HEADS, CHUNK, CHUNK), lambda i: (0, 0, 0)),
                  pl.BlockSpec((CHUNK, C_BR), lambda i: (0, 0))],
        out_specs=pl.BlockSpec((tm, C_BR), lambda i: (i, 1)),
        input_output_aliases={3: 0},
        compiler_params=_params(("parallel",)),
    )(proj, proj, proj, y, ln_g, ln_b, ws, bsb)


def _out_loss(x, y, wout, final_g, target, tm):
    s = x.shape[0]
    nt = s // tm
    inv_d = 1.0 / D_MODEL

    def body(x_ref, y_ref, w_ref, g_ref, t_ref, dx2_ref, dy_ref, dw_ref, loss_ref, dfg_ref, acc_ref):
        i = pl.program_id(0)

        @pl.when(i == 0)
        def _():
            acc_ref[...] = jnp.zeros_like(acc_ref)
            loss_ref[...] = jnp.zeros_like(loss_ref)
            dfg_ref[...] = jnp.zeros_like(dfg_ref)

        yb = y_ref[...]
        x2 = x_ref[...] + jnp.dot(yb, w_ref[...], preferred_element_type=F32)
        r2 = lax.rsqrt(jnp.mean(x2 * x2, axis=-1, keepdims=True) + EPS)
        n = x2 * r2
        g = g_ref[...]
        e = n * g - t_ref[...]
        loss_ref[...] += _fold8(e * e)
        dout = e * inv_d
        dfg_ref[...] += _fold8(dout * n)
        dn = dout * g
        dx2 = r2 * (dn - n * jnp.mean(dn * n, axis=-1, keepdims=True))
        dx2_ref[...] = dx2
        dxb = dx2.astype(BF16)
        dy_ref[...] = lax.dot_general(dxb, w_ref[...], (((1,), (1,)), ((), ())),
                                      preferred_element_type=F32).astype(ACT)
        acc_ref[...] += lax.dot_general(yb, dxb, (((0,), (0,)), ((), ())), preferred_element_type=F32)

        @pl.when(i == nt - 1)
        def _():
            dw_ref[...] = acc_ref[...].astype(BF16)

    part = pl.BlockSpec((8, D_MODEL), lambda i: (0, 0))
    return pl.pallas_call(
        body, name="out_loss", grid=(nt,),
        out_shape=(jax.ShapeDtypeStruct((s, D_MODEL), F32), jax.ShapeDtypeStruct((s, 2 * C_BR), ACT),
                   jax.ShapeDtypeStruct((2 * C_BR, D_MODEL), BF16),
                   jax.ShapeDtypeStruct((8, D_MODEL), F32), jax.ShapeDtypeStruct((8, D_MODEL), F32)),
        in_specs=[pl.BlockSpec((tm, D_MODEL), lambda i: (i, 0)),
                  pl.BlockSpec((tm, 2 * C_BR), lambda i: (i, 0)),
                  pl.BlockSpec((2 * C_BR, D_MODEL), lambda i: (0, 0), pipeline_mode=pl.Buffered(1)),
                  pl.BlockSpec((1, D_MODEL), lambda i: (0, 0)),
                  pl.BlockSpec((tm, D_MODEL), lambda i: (i, 0))],
        out_specs=(pl.BlockSpec((tm, D_MODEL), lambda i: (i, 0)),
                   pl.BlockSpec((tm, 2 * C_BR), lambda i: (i, 0)),
                   pl.BlockSpec((2 * C_BR, D_MODEL), lambda i: (0, 0), pipeline_mode=pl.Buffered(1)), part, part),
        scratch_shapes=[pltpu.VMEM((2 * C_BR, D_MODEL), F32)],
        compiler_params=_params(("arbitrary",)),
    )(x, y, wout, final_g, target)


def _conv_bwd(proj, cv, dy, conv_w3, ln_g, ln_b, dwout, tm):
    s = proj.shape[0]
    nt = s // tm
    wo_rows = dwout.shape[1]

    def body(av_ref, ag_ref, gc_p, gc_m, gc_n, c_p, c_m, c_n, dy_p, dy_m, dy_n, w_ref, lg_ref, lb_ref,
             dwout_ref, dp_ref, dwc_ref, dcb_ref, dlg_ref, dlb_ref, gwo_ref,
             dce_ref, glu_ref, dgl_ref, land_ref, send_sems, recv_sems, loc_sem):
        i = pl.program_id(0)
        px, py, pc = _place()
        me = 4 * px + 2 * py + pc

        def exchanges():
            out = []
            for rel in range(1, N_DEV):
                qx, qy, qc = px ^ (rel >> 2), py ^ ((rel >> 1) & 1), pc ^ (rel & 1)
                out.append(pltpu.make_async_remote_copy(
                    src_ref=dwout_ref.at[4 * qx + 2 * qy + qc], dst_ref=land_ref.at[me],
                    send_sem=send_sems.at[rel - 1], recv_sem=recv_sems.at[rel - 1],
                    device_id=(qx, qy, qc), device_id_type=MESH))
            return out

        own = pltpu.make_async_copy(dwout_ref.at[me], land_ref.at[me], loc_sem)

        @pl.when(i == 0)
        def _():
            dwc_ref[...] = jnp.zeros_like(dwc_ref)
            dcb_ref[...] = jnp.zeros_like(dcb_ref)
            dlg_ref[...] = jnp.zeros_like(dlg_ref)
            dlb_ref[...] = jnp.zeros_like(dlb_ref)
            own.start()
            for cp in exchanges():
                cp.start()

        def ext(p, m, n):
            return jnp.concatenate([p[...], m[...], n[...]], axis=0).astype(F32)

        main = slice(HALO, HALO + tm)
        cf, gc, dyc = ext(c_p, c_m, c_n), ext(gc_p, gc_m, gc_n), ext(dy_p, dy_m, dy_n)
        xhat, rstd = _ln_norm(cf)
        lg = lg_ref[...]
        ln = xhat * lg + lb_ref[...]
        s_ln, s_gc = _sigmoid(ln), _sigmoid(gc)
        dln = dyc * (gc * s_gc) * (s_ln * (1.0 + ln * (1.0 - s_ln)))
        dp_ref[:, 2 * C_BR:] = (dyc[main] * (ln[main] * s_ln[main])
                                * (s_gc[main] * (1.0 + gc[main] * (1.0 - s_gc[main])))).astype(ACT)
        dlg_ref[...] += _fold8(dln[main] * xhat[main])
        dlb_ref[...] += _fold8(dln[main])
        dc = _ln_bwd(dln, xhat, rstd, lg)
        dcb_ref[...] += _fold8(dc[main])

        def tiles(a):
            return a.reshape(a.shape[0], 8, 128)

        dce_ref[0:HALO] = tiles(jnp.where(i > 0, dc[0:HALO], 0.0))
        dce_ref[HALO:HALO + tm] = tiles(dc[main])
        dce_ref[HALO + tm:] = tiles(jnp.where(i < nt - 1, dc[HALO + tm:], 0.0))
        av = av_ref[...].astype(F32)
        sa = _sigmoid(ag_ref[...].astype(F32))
        glu_ref[...] = tiles(av * sa)

        nb = 16

        def step(t, carry):
            s0 = t * nb
            accs = [None] * nb
            for k in range(CONV_WIDTH):
                w = w_ref[k]
                prods = []
                for j in range(nb):
                    v = dce_ref[s0 + HALO + CONV_PAD + j - k]
                    term = w * v
                    accs[j] = term if accs[j] is None else accs[j] + term
                    prods.append(glu_ref[s0 + j] * v)
                while len(prods) > 1:
                    prods = [p + q for p, q in zip(prods[::2], prods[1::2])]
                dwc_ref[k] += prods[0]
            for j in range(nb):
                dgl_ref[s0 + j] = accs[j]
            return carry

        lax.fori_loop(0, tm // nb, step, 0)
        dglu = dgl_ref[...].reshape(tm, C_BR)
        dp_ref[:, 0:C_BR] = (dglu * sa).astype(ACT)
        dp_ref[:, C_BR:2 * C_BR] = (dglu * av * sa * (1.0 - sa)).astype(ACT)

        @pl.when(i == nt - 1)
        def _():
            copies = exchanges()
            own.wait()
            for cp in copies:
                cp.wait_recv()

            def step(t, carry):
                sl = pl.ds(pl.multiple_of(t * 64, 64), 64)
                g = land_ref[0, sl, :].astype(F32)
                for d in range(1, N_DEV):
                    g = g + land_ref[d, sl, :].astype(F32)
                gwo_ref[sl, :] = g
                return carry

            lax.fori_loop(0, wo_rows // 64, step, 0)
            for cp in copies:
                cp.wait_send()

    vec = pl.BlockSpec((1, C_BR), lambda i: (0, 0))
    part = pl.BlockSpec((8, C_BR), lambda i: (0, 0))
    return pl.pallas_call(
        body, name="conv_bwd", grid=(nt,),
        out_shape=(jax.ShapeDtypeStruct((s, D_IN), ACT), jax.ShapeDtypeStruct((CONV_ROWS, 8, 128), F32),
                   jax.ShapeDtypeStruct((8, C_BR), F32), jax.ShapeDtypeStruct((8, C_BR), F32),
                   jax.ShapeDtypeStruct((8, C_BR), F32), jax.ShapeDtypeStruct(dwout.shape[1:], F32)),
        in_specs=[pl.BlockSpec((tm, C_BR), lambda i: (i, 0)), pl.BlockSpec((tm, C_BR), lambda i: (i, 1))]
        + _halo_specs(tm, s, 2) + _halo_specs(tm, s, 0) + _halo_specs(tm, s, 0)
        + [pl.BlockSpec((CONV_ROWS, 8, 128), lambda i: (0, 0, 0)), vec, vec, pl.BlockSpec(memory_space=pl.ANY)],
        out_specs=(pl.BlockSpec((tm, 3 * C_BR), lambda i: (i, 0)),
                   pl.BlockSpec((CONV_ROWS, 8, 128), lambda i: (0, 0, 0)), part, part, part,
                   pl.BlockSpec(memory_space=pltpu.VMEM)),
        scratch_shapes=[pltpu.VMEM((tm + 2 * HALO, 8, 128), F32),
                        pltpu.VMEM((tm, 8, 128), F32), pltpu.VMEM((tm, 8, 128), F32),
                        pltpu.VMEM(dwout.shape, BF16),
                        pltpu.SemaphoreType.DMA((N_DEV - 1,)), pltpu.SemaphoreType.DMA((N_DEV - 1,)),
                        pltpu.SemaphoreType.DMA],
        compiler_params=_params(("arbitrary",)),
    )(proj, proj, proj, proj, proj, cv, cv, cv, dy, dy, dy, conv_w3, ln_g, ln_b, dwout)


def _sgu_bwd(proj, dy, dproj, ln_g, ln_b, ws, wst, bsb, tm):
    s = proj.shape[0]

    def body(u_ref, v_ref, gs_ref, dy_ref, dp_in, lg_ref, lb_ref, ws_ref, wst_ref, bsb_ref,
             dp_ref, dws_ref, dba_ref, dlg_ref, dlb_ref, dvn_ref):
        del dp_in
        i = pl.program_id(0)

        @pl.when(i == 0)
        def _():
            dws_ref[...] = jnp.zeros_like(dws_ref)
            dba_ref[...] = jnp.zeros_like(dba_ref)
            dlg_ref[...] = jnp.zeros_like(dlg_ref)
            dlb_ref[...] = jnp.zeros_like(dlb_ref)

        xhat, rstd = _ln_norm(v_ref[...].astype(F32))
        lg = lg_ref[...]
        vn = (xhat * lg + lb_ref[...]).astype(BF16)
        for cidx in range(tm // CHUNK):
            rows = slice(cidx * CHUNK, (cidx + 1) * CHUNK)
            for h in range(HEADS):
                cols = slice(h * HEAD_DIM, (h + 1) * HEAD_DIM)
                ocols = slice(C_BR + h * HEAD_DIM, C_BR + (h + 1) * HEAD_DIM)
                gcols = slice(2 * C_BR + h * HEAD_DIM, 2 * C_BR + (h + 1) * HEAD_DIM)
                vb = vn[rows, cols]
                mixed = jnp.dot(ws_ref[h], vb, preferred_element_type=F32) + bsb_ref[:, cols]
                gs = gs_ref[rows, cols].astype(F32)
                sg = _sigmoid(gs)
                u = u_ref[rows, cols].astype(F32)
                dyb = dy_ref[rows, cols].astype(F32)
                t = dyb * (gs * sg)
                dp_ref[rows, cols] = (t * mixed).astype(ACT)
                dp_ref[rows, gcols] = (dyb * u * mixed * (sg * (1.0 + gs * (1.0 - sg)))).astype(ACT)
                dm = t * u
                dmb = dm.astype(BF16)
                dvn_ref[rows, cols] = jnp.dot(wst_ref[h], dmb, preferred_element_type=F32)
                dws_ref[h] += lax.dot_general(dmb, vb, (((1,), (1,)), ((), ())), preferred_element_type=F32)
                dba_ref[:, cols] += dm
        dvn = dvn_ref[...]
        dlg_ref[...] += _fold8(dvn * xhat)
        dlb_ref[...] += _fold8(dvn)
        dp_ref[:, C_BR:2 * C_BR] = _ln_bwd(dvn, xhat, rstd, lg).astype(ACT)

    vec = pl.BlockSpec((1, C_BR), lambda i: (0, 0))
    part = pl.BlockSpec((8, C_BR), lambda i: (0, 0))
    wsp = pl.BlockSpec((HEADS, CHUNK, CHUNK), lambda i: (0, 0, 0))
    return pl.pallas_call(
        body, name="sgu_bwd", grid=(s // tm,),
        out_shape=(jax.ShapeDtypeStruct((s, D_IN), ACT), jax.ShapeDtypeStruct((HEADS, CHUNK, CHUNK), F32),
                   jax.ShapeDtypeStruct((CHUNK, C_BR), F32), jax.ShapeDtypeStruct((8, C_BR), F32),
                   jax.ShapeDtypeStruct((8, C_BR), F32)),
        in_specs=[pl.BlockSpec((tm, C_BR), lambda i: (i, 3)),
                  pl.BlockSpec((tm, C_BR), lambda i: (i, 4)),
                  pl.BlockSpec((tm, C_BR), lambda i: (i, 5)),
                  pl.BlockSpec((tm, C_BR), lambda i: (i, 1)),
                  pl.BlockSpec(memory_space=pl.ANY),
                  vec, vec, wsp, wsp, pl.BlockSpec((CHUNK, C_BR), lambda i: (0, 0))],
        out_specs=(pl.BlockSpec((tm, 3 * C_BR), lambda i: (i, 1)), wsp,
                   pl.BlockSpec((CHUNK, C_BR), lambda i: (0, 0)), part, part),
        scratch_shapes=[pltpu.VMEM((tm, C_BR), F32)],
        input_output_aliases={4: 0},
        compiler_params=_params(("arbitrary",)),
    )(proj, proj, proj, dy, dproj, ln_g, ln_b, ws, wst, bsb)


def _dx(dproj, win_all, x, norm_g, dx2, tm):
    s = x.shape[0]

    def body(dp_ref, w_ref, x_ref, g_ref, dx2_ref, gx_ref, dng_ref):
        i = pl.program_id(0)

        @pl.when(i == 0)
        def _():
            dng_ref[...] = jnp.zeros_like(dng_ref)

        dh = None
        for j in range(N_DEV):
            term = lax.dot_general(dp_ref[:, j * W_BLK:(j + 1) * W_BLK], w_ref[j],
                                   (((1,), (1,)), ((), ())), preferred_element_type=F32)
            dh = term if dh is None else dh + term
        xf = x_ref[...]
        r = lax.rsqrt(jnp.mean(xf * xf, axis=-1, keepdims=True) + EPS)
        n = xf * r
        dng_ref[...] += _fold8(dh * n)
        dn = dh * g_ref[...]
        gx_ref[...] = dx2_ref[...] + r * (dn - n * jnp.mean(dn * n, axis=-1, keepdims=True))

    return pl.pallas_call(
        body, name="dx", grid=(s // tm,),
        out_shape=(jax.ShapeDtypeStruct((s, D_MODEL), F32), jax.ShapeDtypeStruct((8, D_MODEL), F32)),
        in_specs=[pl.BlockSpec((tm, D_IN), lambda i: (i, 0)),
                  pl.BlockSpec((N_DEV, D_MODEL, W_BLK), lambda i: (0, 0, 0), pipeline_mode=pl.Buffered(1)),
                  pl.BlockSpec((tm, D_MODEL), lambda i: (i, 0)),
                  pl.BlockSpec((1, D_MODEL), lambda i: (0, 0)),
                  pl.BlockSpec((tm, D_MODEL), lambda i: (i, 0))],
        out_specs=(pl.BlockSpec((tm, D_MODEL), lambda i: (i, 0)), pl.BlockSpec((8, D_MODEL), lambda i: (0, 0))),
        compiler_params=_params(("arbitrary",)),
    )(dproj, win_all, x, norm_g, dx2)


def _dwin_comm(ht, dproj, order, parts, dwc, dba, dws, tk):
    s = ht.shape[1]
    nk = s // tk
    n_part = len(parts)

    def body(*refs):
        order_ref, ht_ref, dp_ref = refs[:3]
        del order_ref
        part_refs = refs[3:3 + n_part]
        dwc_ref, dba_ref, dws_ref = refs[3 + n_part:6 + n_part]
        gw_ref, red_ref, wsr_ref, cws_ref, loss_ref = refs[6 + n_part:11 + n_part]
        (acc_ref, all1_ref, all2_ref, out_s, land_s, out_x, land_x,
         send_sems, recv_sems, send_s, recv_s, send_x, recv_x) = refs[11 + n_part:]
        jj, k = pl.program_id(0), pl.program_id(1)
        x, y, c = _place()
        me = 4 * x + 2 * y + c
        sibling = (x, y, 1 - c)
        chips = [(1 - x, 1 - y), (1 - x, y), (x, 1 - y)]
        last = k == nk - 1

        def exchanges():
            out = []
            for rel in range(1, N_DEV):
                peer = (x ^ (rel >> 2), y ^ ((rel >> 1) & 1), c ^ (rel & 1))
                for a, buf in enumerate((all1_ref, all2_ref)):
                    out.append(pltpu.make_async_remote_copy(
                        src_ref=buf.at[me], dst_ref=buf.at[me],
                        send_sem=send_sems.at[a, rel - 1], recv_sem=recv_sems.at[a, rel - 1],
                        device_id=peer, device_id_type=MESH))
            return out

        def to_sibling(slot):
            return pltpu.make_async_remote_copy(
                src_ref=out_s.at[slot], dst_ref=land_s.at[slot],
                send_sem=send_s.at[slot], recv_sem=recv_s.at[slot], device_id=sibling, device_id_type=MESH)

        def to_chip(slot):
            return pltpu.make_async_remote_copy(
                src_ref=out_x.at[slot], dst_ref=land_x.at[slot],
                send_sem=send_x.at[slot], recv_sem=recv_x.at[slot],
                device_id=(*chips[slot], c), device_id_type=MESH)

        @pl.when((jj == 0) & (k == 0))
        def _():
            all1_ref[me] = jnp.zeros((SMALL_ROWS, C_BR), F32)
            for row, p_ref in zip((ROW_NORM_G, ROW_CONV_B, ROW_CLN_G, ROW_CLN_B, ROW_SLN_G, ROW_SLN_B,
                                   ROW_FINAL_G, ROW_LOSS), part_refs):
                all1_ref[me, row:row + 1, :] = jnp.sum(p_ref[...], axis=0, keepdims=True)
            ones = jnp.ones((8, HEAD_DIM), F32)
            brow = [lax.dot_general(ones, dba_ref[:, h * HEAD_DIM:(h + 1) * HEAD_DIM], (((1,), (1,)), ((), ())),
                                    precision=lax.Precision.HIGHEST, preferred_element_type=F32)[0:1]
                    for h in range(HEADS)]
            all1_ref[me, ROW_B_S:ROW_B_S + 1, :] = jnp.concatenate(brow, axis=1)
            all1_ref[me, ROW_CONV_W:ROW_CONV_W + CONV_ROWS, :] = dwc_ref[...]
            all2_ref[me] = dws_ref[...]
            for cp in exchanges():
                cp.start()

        @pl.when(k == 0)
        def _():
            acc_ref[...] = jnp.zeros_like(acc_ref)

        acc_ref[...] += jnp.dot(ht_ref[...], dp_ref[...], preferred_element_type=F32)

        for slot in range(4):
            @pl.when((jj == 2 * slot) & last)
            def _(slot=slot):
                out_s[slot] = acc_ref[...].astype(BF16)
                to_sibling(slot).start()

        for slot in range(3):
            @pl.when((jj == 2 * slot + 1) & last)
            def _(slot=slot):
                to_sibling(slot).wait_recv()
                out_x[slot] = (acc_ref[...] + land_s[slot].astype(F32)).astype(BF16)
                to_chip(slot).start()

        @pl.when((jj == N_DEV - 1) & last)
        def _():
            to_sibling(3).wait_recv()
            total = acc_ref[...] + land_s[3].astype(F32)
            for slot in range(3):
                to_chip(slot).wait_recv()
                total = total + land_x[slot].astype(F32)
            gw_ref[...] = total

            copies = exchanges()
            for cp in copies:
                cp.wait_recv()
            tot = all1_ref[0]
            for d in range(1, N_DEV):
                tot = tot + all1_ref[d]
            red_ref[...] = tot
            loss_ref[...] = jnp.broadcast_to(
                jnp.sum(tot[ROW_LOSS:ROW_LOSS + 1, :], axis=1, keepdims=True) * (0.5 / D_MODEL), loss_ref.shape)
            shard = jnp.zeros(cws_ref.shape, F32)
            for d in range(N_DEV):
                shard = jnp.where(me == d, tot[ROW_CONV_W:ROW_CONV_W + CONV_ROWS, d * 128:(d + 1) * 128], shard)
            cws_ref[...] = shard
            tot2 = all2_ref[0]
            for d in range(1, N_DEV):
                tot2 = tot2 + all2_ref[d]
            wsr_ref[...] = tot2
            for cp in copies + [to_sibling(slot) for slot in range(4)] + [to_chip(slot) for slot in range(3)]:
                cp.wait_send()

    vm = pl.BlockSpec(memory_space=pltpu.VMEM)
    blk = (D_MODEL, W_BLK)
    return pl.pallas_call(
        body, name="dwin",
        grid_spec=pltpu.PrefetchScalarGridSpec(
            num_scalar_prefetch=1, grid=(N_DEV, nk),
            in_specs=[pl.BlockSpec((D_MODEL, tk), lambda jj, k, o: (0, k)),
                      pl.BlockSpec((tk, W_BLK), lambda jj, k, o: (k, o[jj]))] + [vm] * (n_part + 3),
            out_specs=(vm, vm, vm, vm, vm),
            scratch_shapes=[pltpu.VMEM(blk, F32),
                            pltpu.VMEM((N_DEV, SMALL_ROWS, C_BR), F32), pltpu.VMEM((N_DEV,) + dws.shape, F32),
                            pltpu.VMEM((4,) + blk, BF16), pltpu.VMEM((4,) + blk, BF16),
                            pltpu.VMEM((3,) + blk, BF16), pltpu.VMEM((3,) + blk, BF16),
                            pltpu.SemaphoreType.DMA((2, N_DEV - 1)), pltpu.SemaphoreType.DMA((2, N_DEV - 1)),
                            pltpu.SemaphoreType.DMA((4,)), pltpu.SemaphoreType.DMA((4,)),
                            pltpu.SemaphoreType.DMA((3,)), pltpu.SemaphoreType.DMA((3,))]),
        out_shape=(jax.ShapeDtypeStruct(blk, F32),
                   jax.ShapeDtypeStruct((SMALL_ROWS, C_BR), F32), jax.ShapeDtypeStruct(dws.shape, F32),
                   jax.ShapeDtypeStruct((CONV_ROWS, 128), F32), jax.ShapeDtypeStruct((8, 128), F32)),
        compiler_params=_params(("arbitrary", "arbitrary")),
    )(order, ht, dproj, *parts, dwc, dba, dws)


def _adamw_math(w, g, m, v):
    m = ADAM_B1 * m + (1.0 - ADAM_B1) * g
    v = ADAM_B2 * v + (1.0 - ADAM_B2) * (g * g)
    m_hat = m / (1.0 - ADAM_B1 ** ADAM_STEP)
    v_hat = v / (1.0 - ADAM_B2 ** ADAM_STEP)
    delta = -ADAM_LR * (m_hat / (jnp.sqrt(v_hat) + ADAM_EPS) + ADAM_WD * w)
    return delta, m, v


def _adamw(g, w, m, v, name, tr):
    rows, cols = w.shape

    def body(g_ref, w_ref, m_ref, v_ref, d_ref, mo_ref, vo_ref):
        d_ref[...], mo_ref[...], vo_ref[...] = _adamw_math(w_ref[...], g_ref[...], m_ref[...], v_ref[...])

    spec = pl.BlockSpec((tr, cols), lambda i: (i, 0))
    sds = jax.ShapeDtypeStruct((rows, cols), F32)
    return pl.pallas_call(
        body, name=name, grid=(rows // tr,), out_shape=(sds, sds, sds),
        in_specs=[spec] * 4, out_specs=(spec, spec, spec),
        compiler_params=_params(("parallel",)),
    )(g, w, m, v)


def _adamw_rows(red, rows, ws, ms, vs):
    n = len(rows)

    def body(*refs):
        red_ref = refs[0]
        w_refs, m_refs, v_refs = refs[1:1 + n], refs[1 + n:1 + 2 * n], refs[1 + 2 * n:1 + 3 * n]
        outs = refs[1 + 3 * n:]
        for t, row in enumerate(rows):
            g = red_ref[row:row + 1, :]
            delta, m, v = _adamw_math(w_refs[t][...], g, m_refs[t][...], v_refs[t][...])
            outs[t][...] = g
            outs[n + t][...] = delta
            outs[2 * n + t][...] = m
            outs[3 * n + t][...] = v

    vm = pl.BlockSpec(memory_space=pltpu.VMEM)
    sds = jax.ShapeDtypeStruct((1, C_BR), F32)
    res = pl.pallas_call(
        body, name="adamw_rows", out_shape=(sds,) * (4 * n),
        in_specs=[vm] * (1 + 3 * n), out_specs=(vm,) * (4 * n),
        compiler_params=_params(),
    )(red, *ws, *ms, *vs)
    return res[:n], res[n:2 * n], res[2 * n:3 * n], res[3 * n:]


def kernel(x, norm_g, w_in, conv_w, conv_b, conv_ln_g, conv_ln_b, sgu_ln_g, sgu_ln_b, w_s, b_s, w_out, final_g, loss_target, m_norm_g, m_w_in, m_conv_w, m_conv_b, m_conv_ln_g, m_conv_ln_b, m_sgu_ln_g, m_sgu_ln_b, m_w_s, m_b_s, m_w_out, m_final_g, v_norm_g, v_w_in, v_conv_w, v_conv_b, v_conv_ln_g, v_conv_ln_b, v_sgu_ln_g, v_sgu_ln_b, v_w_s, v_b_s, v_w_out, v_final_g):
    s = x.shape[1]
    xs = x.reshape(s, D_MODEL)
    tgt = loss_target.reshape(s, D_MODEL)
    tm = min(256, s)

    cw_pad = jnp.pad(conv_w[0], ((0, CONV_ROWS - CONV_WIDTH), (0, 0)))
    px, py, pc = _place()
    blocks = [(px, py, pc), (px, py, 1 - pc)]
    blocks += [(*chip, core) for chip in ((1 - px, py), (px, 1 - py), (1 - px, 1 - py)) for core in (pc, 1 - pc)]
    order = jnp.stack([4 * bx + 2 * by + bc for bx, by, bc in blocks]).astype(jnp.int32)
    proj, ht, win_all, wout_all, cw_all = _proj_ag(xs, norm_g, w_in[0], w_out[0], cw_pad, order, min(1024, s))
    wout_full = wout_all.reshape(2 * C_BR, D_MODEL)
    cw_tiles = jnp.transpose(cw_all, (1, 0, 2))

    ws = w_s[0].astype(BF16)
    wst = jnp.transpose(w_s[0], (0, 2, 1)).astype(BF16)
    bsb = jnp.repeat(jnp.transpose(b_s[0]), HEAD_DIM, axis=1)
    fg = final_g.reshape(1, D_MODEL)

    big = min(512, s)
    y, cv = _conv_fwd(proj, cw_tiles, conv_b, conv_ln_g, conv_ln_b, big)
    y = _sgu_fwd(proj, y, sgu_ln_g, sgu_ln_b, ws, bsb, big)
    dx2, dy, dwout, loss_p, dfg_p = _out_loss(xs, y, wout_full, fg, tgt, min(512, s))
    dproj, dwc, dcb_p, dclg_p, dclb_p, g_w_out = _conv_bwd(
        proj, cv, dy, cw_tiles, conv_ln_g, conv_ln_b, dwout.reshape(N_DEV, 2 * C_BR // N_DEV, D_MODEL), big)
    dwc = dwc.reshape(CONV_ROWS, C_BR)
    dproj, dws, dba, dslg_p, dslb_p = _sgu_bwd(proj, dy, dproj, sgu_ln_g, sgu_ln_b, ws, wst, bsb, big)
    grad_x, dng_p = _dx(dproj, win_all, xs, norm_g, dx2, big)
    rs_blocks = [(*chip, core) for chip in ((1 - px, 1 - py), (1 - px, py), (px, 1 - py), (px, py))
                 for core in (1 - pc, pc)]
    rs_order = jnp.stack([4 * bx + 2 * by + bc for bx, by, bc in rs_blocks]).astype(jnp.int32)
    g_w_in, red, g_w_s, g_cw, loss8 = _dwin_comm(
        ht, dproj, rs_order, [dng_p, dcb_p, dclg_p, dclb_p, dslg_p, dslb_p, dfg_p, loss_p], dwc, dba,
        dws.reshape(HEADS * CHUNK, CHUNK), min(1024, s))
    loss = loss8[0, 0]

    d_w_in, nm_w_in, nv_w_in = _adamw(g_w_in, w_in[0], m_w_in[0], v_w_in[0], "adamw_w_in", 256)
    d_w_out, nm_w_out, nv_w_out = _adamw(g_w_out, w_out[0], m_w_out[0], v_w_out[0], "adamw_w_out", 256)
    g_cw = g_cw[:CONV_WIDTH]
    d_cw, nm_cw, nv_cw = _adamw(g_cw, conv_w[0], m_conv_w[0], v_conv_w[0], "adamw_conv_w", CONV_WIDTH)
    flat = lambda a: a.reshape(HEADS * CHUNK, CHUNK)
    d_ws, nm_ws, nv_ws = _adamw(g_w_s, flat(w_s), flat(m_w_s), flat(v_w_s), "adamw_w_s", HEADS * CHUNK)
    row = lambda a: a.reshape(1, C_BR)
    rows = (ROW_NORM_G, ROW_CONV_B, ROW_CLN_G, ROW_CLN_B, ROW_SLN_G, ROW_SLN_B, ROW_B_S, ROW_FINAL_G)
    g_r, d_r, m_r, v_r = _adamw_rows(
        red, rows,
        [norm_g, conv_b, conv_ln_g, conv_ln_b, sgu_ln_g, sgu_ln_b, row(b_s), row(final_g)],
        [m_norm_g, m_conv_b, m_conv_ln_g, m_conv_ln_b, m_sgu_ln_g, m_sgu_ln_b, row(m_b_s), row(m_final_g)],
        [v_norm_g, v_conv_b, v_conv_ln_g, v_conv_ln_b, v_sgu_ln_g, v_sgu_ln_b, row(v_b_s), row(v_final_g)])

    def leaves(r, w_in_l, cw_l, ws_l, w_out_l):
        return (r[0], w_in_l[None], cw_l[None], r[1], r[2], r[3], r[4], r[5],
                ws_l.reshape(1, HEADS, CHUNK, CHUNK), r[6].reshape(1, HEADS, CHUNK), w_out_l[None],
                r[7].reshape(D_MODEL))

    return (loss, grad_x.reshape(1, s, D_MODEL),
            *leaves(g_r, g_w_in, g_cw, g_w_s, g_w_out),
            *leaves(d_r, d_w_in, d_cw, d_ws, d_w_out),
            *leaves(m_r, nm_w_in, nm_cw, nm_ws, nm_w_out),
            *leaves(v_r, nv_w_in, nv_cw, nv_ws, nv_w_out))
```

```python
import functools

import jax
import jax.numpy as jnp
from jax import lax
from jax.experimental import pallas as pl
from jax.experimental.pallas import tpu as pltpu

F32 = jnp.float32
BF16 = jnp.bfloat16
ACT = jnp.bfloat16

D_MODEL = 1024
C_BR = 1024
D_IN = 6 * C_BR
N_DEV = 8
W_BLK = D_IN // N_DEV
HEADS = 8
HEAD_DIM = 128
CHUNK = 128
CONV_WIDTH = 31
CONV_PAD = CONV_WIDTH // 2
HALO = 16
CONV_ROWS = 32
EPS = 1e-6

ADAM_LR = 0.001
ADAM_B1 = 0.9
ADAM_B2 = 0.999
ADAM_EPS = 1e-08
ADAM_WD = 0.01
ADAM_STEP = 10

VMEM_LIMIT = 56 * 1024 * 1024
MESH = pl.DeviceIdType.MESH

ROW_NORM_G, ROW_CONV_B, ROW_CLN_G, ROW_CLN_B, ROW_SLN_G, ROW_SLN_B, ROW_FINAL_G, ROW_B_S, ROW_LOSS = range(9)
ROW_CONV_W = 16
SMALL_ROWS = ROW_CONV_W + CONV_ROWS


def _params(sem=None, **kw):
    return pltpu.CompilerParams(dimension_semantics=sem, vmem_limit_bytes=VMEM_LIMIT, **kw)


def _fold8(a):
    r, n = a.shape
    return a.reshape(r // 8, 8, n).sum(axis=0)


def _sigmoid(z):
    return 1.0 / (1.0 + jnp.exp(-z))


def _ln_norm(xf):
    mu = jnp.mean(xf, axis=-1, keepdims=True)
    xc = xf - mu
    var = jnp.mean(xc * xc, axis=-1, keepdims=True)
    rstd = lax.rsqrt(var + EPS)
    return xc * rstd, rstd


def _ln_bwd(dy, xhat, rstd, g):
    dxhat = dy * g
    m1 = jnp.mean(dxhat, axis=-1, keepdims=True)
    m2 = jnp.mean(dxhat * xhat, axis=-1, keepdims=True)
    return rstd * (dxhat - m1 - xhat * m2)


def _place():
    return lax.axis_index("x"), lax.axis_index("y"), lax.axis_index("c")


def _proj_ag(x, norm_g, w_in, w_out, conv_w, order, tm):
    s = x.shape[0]
    nt = s // tm

    def body(order_ref, x_ref, g_ref, win_ref, wout_ref, cw_ref,
             proj_ref, ht_ref, win_all, wout_all, cw_all,
             h_ref, win_buf, wout_buf, cw_buf, send_sems, recv_sems, save_sems):
        jj, i = pl.program_id(0), pl.program_id(1)
        x_, y_, c_ = _place()
        me, sibling = (x_, y_, c_), (x_, y_, 1 - c_)
        chips = [(1 - x_, y_), (x_, 1 - y_), (1 - x_, 1 - y_)]
        bufs = (win_buf, wout_buf, cw_buf)
        outs = (win_all, wout_all, cw_all)
        start = i == 0

        def index(px, py, pc):
            return 4 * px + 2 * py + pc

        def copy(a, k, block, to):
            return pltpu.make_async_remote_copy(
                src_ref=bufs[a].at[index(*block)], dst_ref=bufs[a].at[index(*block)],
                send_sem=send_sems.at[a, k], recv_sem=recv_sems.at[a, k],
                device_id=to, device_id_type=MESH)

        def save(a, slot, block):
            return pltpu.make_async_copy(bufs[a].at[index(*block)], outs[a].at[index(*block)], save_sems.at[a, slot])

        def first(a):
            return [copy(a, 0, me, sibling)] + [copy(a, 1 + j, me, (*chip, c_)) for j, chip in enumerate(chips)]

        def saves(a):
            blocks = [me, sibling] + [(*chip, core) for chip in chips for core in (c_, 1 - c_)]
            return [save(a, slot, block) for slot, block in enumerate(blocks)]

        @pl.when((jj == 0) & start)
        def _():
            win_buf[index(*me)] = win_ref[...].astype(BF16)
            wout_buf[index(*me)] = wout_ref[...].astype(BF16)
            cw_buf[index(*me)] = cw_ref[...]
            for a in range(3):
                for cp in first(a):
                    cp.start()
                saves(a)[0].start()

        @pl.when((jj == 1) & start)
        def _():
            copy(0, 0, sibling, me).wait_recv()
            saves(0)[1].start()

        for j, chip in enumerate(chips):
            @pl.when((jj == 2 + 2 * j) & start)
            def _(j=j, chip=chip):
                copy(0, 1 + j, (*chip, c_), me).wait_recv()
                copy(0, 4 + j, (*chip, c_), sibling).start()
                saves(0)[2 + 2 * j].start()

            @pl.when((jj == 3 + 2 * j) & start)
            def _(j=j, chip=chip):
                copy(0, 4 + j, (*chip, 1 - c_), me).wait_recv()
                saves(0)[3 + 2 * j].start()

        @pl.when(jj == 0)
        def _():
            xf = x_ref[...]
            r = lax.rsqrt(jnp.mean(xf * xf, axis=-1, keepdims=True) + EPS)
            hf = xf * r * g_ref[...]
            h_ref[i] = hf.astype(BF16)
            ht_ref[...] = hf.T.astype(BF16)

        proj_ref[...] = jnp.dot(h_ref[i], win_buf[order_ref[jj]], preferred_element_type=F32).astype(ACT)

        @pl.when((jj == N_DEV - 1) & (i == nt - 1))
        def _():
            passed = [copy(0, 4 + j, (*chip, c_), sibling) for j, chip in enumerate(chips)]
            for a in (1, 2):
                for j, chip in enumerate(chips):
                    copy(a, 1 + j, (*chip, c_), me).wait_recv()
                    fwd = copy(a, 4 + j, (*chip, c_), sibling)
                    fwd.start()
                    passed.append(fwd)
                    saves(a)[2 + 2 * j].start()
            for a in (1, 2):
                copy(a, 0, sibling, me).wait_recv()
                saves(a)[1].start()
                for j, chip in enumerate(chips):
                    copy(a, 4 + j, (*chip, 1 - c_), me).wait_recv()
                    saves(a)[3 + 2 * j].start()
            for cp in saves(0) + saves(1) + saves(2):
                cp.wait()
            for cp in first(0) + first(1) + first(2) + passed:
                cp.wait_send()

    vm = pl.BlockSpec(memory_space=pltpu.VMEM)
    hbm = pl.BlockSpec(memory_space=pl.ANY)
    once = lambda jj, i: jnp.where(jj == 0, i, nt - 1)
    stacked = [(N_DEV,) + w.shape for w in (w_in, w_out, conv_w)]
    return pl.pallas_call(
        body, name="proj_ag",
        grid_spec=pltpu.PrefetchScalarGridSpec(
            num_scalar_prefetch=1, grid=(N_DEV, nt),
            in_specs=[pl.BlockSpec((tm, D_MODEL), lambda jj, i, o: (once(jj, i), 0)),
                      pl.BlockSpec((1, D_MODEL), lambda jj, i, o: (0, 0)), vm, vm, vm],
            out_specs=(pl.BlockSpec((tm, W_BLK), lambda jj, i, o: (i, o[jj])),
                       pl.BlockSpec((D_MODEL, tm), lambda jj, i, o: (0, once(jj, i))), hbm, hbm, hbm),
            scratch_shapes=[pltpu.VMEM((nt, tm, D_MODEL), BF16),
                            pltpu.VMEM(stacked[0], BF16), pltpu.VMEM(stacked[1], BF16), pltpu.VMEM(stacked[2], F32),
                            pltpu.SemaphoreType.DMA((3, 7)), pltpu.SemaphoreType.DMA((3, 7)),
                            pltpu.SemaphoreType.DMA((3, N_DEV))]),
        out_shape=(jax.ShapeDtypeStruct((s, D_IN), ACT), jax.ShapeDtypeStruct((D_MODEL, s), BF16),
                   jax.ShapeDtypeStruct(stacked[0], BF16), jax.ShapeDtypeStruct(stacked[1], BF16),
                   jax.ShapeDtypeStruct(stacked[2], F32)),
        compiler_params=_params(("arbitrary", "arbitrary")),
    )(order, x, norm_g, w_in, w_out, conv_w)


def _halo_specs(tm, s, col):
    per = tm // HALO
    last = s // HALO - 1
    return [pl.BlockSpec((HALO, C_BR), lambda i: (jnp.maximum(i * per - 1, 0), col)),
            pl.BlockSpec((tm, C_BR), lambda i: (i, col)),
            pl.BlockSpec((HALO, C_BR), lambda i: (jnp.minimum((i + 1) * per, last), col))]


def _conv_fwd(proj, conv_w3, conv_b, ln_g, ln_b, tm):
    s = proj.shape[0]
    nt = s // tm

    def body(av_p, av_m, av_n, ag_p, ag_m, ag_n, gc_ref, w_ref, cb_ref, lg_ref, lb_ref,
             y_ref, c_ref, ext_ref, cv_ref):
        i = pl.program_id(0)

        def glu(a_ref, g_ref):
            return a_ref[...].astype(F32) * _sigmoid(g_ref[...].astype(F32))

        def tiles(a):
            return a.reshape(a.shape[0], 8, 128)

        ext_ref[0:HALO] = tiles(jnp.where(i > 0, glu(av_p, ag_p), 0.0))
        ext_ref[HALO:HALO + tm] = tiles(glu(av_m, ag_m))
        ext_ref[HALO + tm:] = tiles(jnp.where(i < nt - 1, glu(av_n, ag_n), 0.0))

        nb = 16

        def step(t, carry):
            s0 = t * nb
            accs = [None] * nb
            for k in range(CONV_WIDTH):
                w = w_ref[k]
                for j in range(nb):
                    term = w * ext_ref[s0 + HALO - CONV_PAD + j + k]
                    accs[j] = term if accs[j] is None else accs[j] + term
            for j in range(nb):
                cv_ref[s0 + j] = accs[j]
            return carry

        lax.fori_loop(0, tm // nb, step, 0)
        cv = cv_ref[...].reshape(tm, C_BR) + cb_ref[...]
        c_ref[...] = cv.astype(ACT)
        xhat, _ = _ln_norm(cv)
        ln = xhat * lg_ref[...] + lb_ref[...]
        gc = gc_ref[...].astype(F32)
        y_ref[...] = (ln * _sigmoid(ln) * (gc * _sigmoid(gc))).astype(ACT)

    vec = pl.BlockSpec((1, C_BR), lambda i: (0, 0))
    return pl.pallas_call(
        body, name="conv_fwd", grid=(nt,),
        out_shape=(jax.ShapeDtypeStruct((s, 2 * C_BR), ACT), jax.ShapeDtypeStruct((s, C_BR), ACT)),
        in_specs=_halo_specs(tm, s, 0) + _halo_specs(tm, s, 1)
        + [pl.BlockSpec((tm, C_BR), lambda i: (i, 2)),
           pl.BlockSpec((CONV_ROWS, 8, 128), lambda i: (0, 0, 0)), vec, vec, vec],
        out_specs=(pl.BlockSpec((tm, C_BR), lambda i: (i, 0)), pl.BlockSpec((tm, C_BR), lambda i: (i, 0))),
        scratch_shapes=[pltpu.VMEM((tm + 2 * HALO, 8, 128), F32),
                        pltpu.VMEM((tm, 8, 128), F32)],
        compiler_params=_params(("parallel",)),
    )(proj, proj, proj, proj, proj, proj, proj, conv_w3, conv_b, ln_g, ln_b)


def _sgu_fwd(proj, y, ln_g, ln_b, ws, bsb, tm):
    s = proj.shape[0]

    def body(u_ref, v_ref, gs_ref, y_in, lg_ref, lb_ref, ws_ref, bsb_ref, y_ref):
        del y_in
        xhat, _ = _ln_norm(v_ref[...].astype(F32))
        vn = (xhat * lg_ref[...] + lb_ref[...]).astype(BF16)
        for cidx in range(tm // CHUNK):
            rows = slice(cidx * CHUNK, (cidx + 1) * CHUNK)
            for h in range(HEADS):
                cols = slice(h * HEAD_DIM, (h + 1) * HEAD_DIM)
                mixed = jnp.dot(ws_ref[h], vn[rows, cols], preferred_element_type=F32) + bsb_ref[:, cols]
                gs = gs_ref[rows, cols].astype(F32)
                y_ref[rows, cols] = (u_ref[rows, cols].astype(F32) * mixed * (gs * _sigmoid(gs))).astype(ACT)

    vec = pl.BlockSpec((1, C_BR), lambda i: (0, 0))
    return pl.pallas_call(
        body, name="sgu_fwd", grid=(s // tm,),
        out_shape=jax.ShapeDtypeStruct((s, 2 * C_BR), ACT),
        in_specs=[pl.BlockSpec((tm, C_BR), lambda i: (i, 3)),
                  pl.BlockSpec((tm, C_BR), lambda i: (i, 4)),
                  pl.BlockSpec((tm, C_BR), lambda i: (i, 5)),
                  pl.BlockSpec(memory_space=pl.ANY),
                  vec, vec,
                  pl.BlockSpec((HEADS, CHUNK, CHUNK), lambda i: (0, 0, 0)),
                  pl.BlockSpec((CHUNK, C_BR), lambda i: (0, 0))],
        out_specs=pl.BlockSpec((tm, C_BR), lambda i: (i, 1)),
        input_output_aliases={3: 0},
        compiler_params=_params(("parallel",)),
    )(proj, proj, proj, y, ln_g, ln_b, ws, bsb)


def _out_loss(x, y, wout, final_g, target, tm):
    s = x.shape[0]
    nt = s // tm
    inv_d = 1.0 / D_MODEL

    def body(x_ref, y_ref, w_ref, g_ref, t_ref, dx2_ref, dy_ref, dw_ref, loss_ref, dfg_ref, acc_ref):
        i = pl.program_id(0)

        @pl.when(i == 0)
        def _():
            acc_ref[...] = jnp.zeros_like(acc_ref)
            loss_ref[...] = jnp.zeros_like(loss_ref)
            dfg_ref[...] = jnp.zeros_like(dfg_ref)

        yb = y_ref[...]
        x2 = x_ref[...] + jnp.dot(yb, w_ref[...], preferred_element_type=F32)
        r2 = lax.rsqrt(jnp.mean(x2 * x2, axis=-1, keepdims=True) + EPS)
        n = x2 * r2
        g = g_ref[...]
        e = n * g - t_ref[...]
        loss_ref[...] += _fold8(e * e)
        dout = e * inv_d
        dfg_ref[...] += _fold8(dout * n)
        dn = dout * g
        dx2 = r2 * (dn - n * jnp.mean(dn * n, axis=-1, keepdims=True))
        dx2_ref[...] = dx2
        dxb = dx2.astype(BF16)
        dy_ref[...] = lax.dot_general(dxb, w_ref[...], (((1,), (1,)), ((), ())),
                                      preferred_element_type=F32).astype(ACT)
        acc_ref[...] += lax.dot_general(yb, dxb, (((0,), (0,)), ((), ())), preferred_element_type=F32)

        @pl.when(i == nt - 1)
        def _():
            dw_ref[...] = acc_ref[...].astype(BF16)

    part = pl.BlockSpec((8, D_MODEL), lambda i: (0, 0))
    return pl.pallas_call(
        body, name="out_loss", grid=(nt,),
        out_shape=(jax.ShapeDtypeStruct((s, D_MODEL), F32), jax.ShapeDtypeStruct((s, 2 * C_BR), ACT),
                   jax.ShapeDtypeStruct((2 * C_BR, D_MODEL), BF16),
                   jax.ShapeDtypeStruct((8, D_MODEL), F32), jax.ShapeDtypeStruct((8, D_MODEL), F32)),
        in_specs=[pl.BlockSpec((tm, D_MODEL), lambda i: (i, 0)),
                  pl.BlockSpec((tm, 2 * C_BR), lambda i: (i, 0)),
                  pl.BlockSpec((2 * C_BR, D_MODEL), lambda i: (0, 0), pipeline_mode=pl.Buffered(1)),
                  pl.BlockSpec((1, D_MODEL), lambda i: (0, 0)),
                  pl.BlockSpec((tm, D_MODEL), lambda i: (i, 0))],
        out_specs=(pl.BlockSpec((tm, D_MODEL), lambda i: (i, 0)),
                   pl.BlockSpec((tm, 2 * C_BR), lambda i: (i, 0)),
                   pl.BlockSpec((2 * C_BR, D_MODEL), lambda i: (0, 0), pipeline_mode=pl.Buffered(1)), part, part),
        scratch_shapes=[pltpu.VMEM((2 * C_BR, D_MODEL), F32)],
        compiler_params=_params(("arbitrary",)),
    )(x, y, wout, final_g, target)


def _conv_bwd(proj, cv, dy, conv_w3, ln_g, ln_b, dwout, tm):
    s = proj.shape[0]
    nt = s // tm
    wo_rows = dwout.shape[1]

    def body(av_ref, ag_ref, gc_p, gc_m, gc_n, c_p, c_m, c_n, dy_p, dy_m, dy_n, w_ref, lg_ref, lb_ref,
             dwout_ref, dp_ref, dwc_ref, dcb_ref, dlg_ref, dlb_ref, gwo_ref,
             dce_ref, glu_ref, dgl_ref, land_ref, send_sems, recv_sems, loc_sem):
        i = pl.program_id(0)
        px, py, pc = _place()
        me = 4 * px + 2 * py + pc

        def exchanges():
            out = []
            for rel in range(1, N_DEV):
                qx, qy, qc = px ^ (rel >> 2), py ^ ((rel >> 1) & 1), pc ^ (rel & 1)
                out.append(pltpu.make_async_remote_copy(
                    src_ref=dwout_ref.at[4 * qx + 2 * qy + qc], dst_ref=land_ref.at[me],
                    send_sem=send_sems.at[rel - 1], recv_sem=recv_sems.at[rel - 1],
                    device_id=(qx, qy, qc), device_id_type=MESH))
            return out

        own = pltpu.make_async_copy(dwout_ref.at[me], land_ref.at[me], loc_sem)

        @pl.when(i == 0)
        def _():
            dwc_ref[...] = jnp.zeros_like(dwc_ref)
            dcb_ref[...] = jnp.zeros_like(dcb_ref)
            dlg_ref[...] = jnp.zeros_like(dlg_ref)
            dlb_ref[...] = jnp.zeros_like(dlb_ref)
            own.start()
            for cp in exchanges():
                cp.start()

        def ext(p, m, n):
            return jnp.concatenate([p[...], m[...], n[...]], axis=0).astype(F32)

        main = slice(HALO, HALO + tm)
        cf, gc, dyc = ext(c_p, c_m, c_n), ext(gc_p, gc_m, gc_n), ext(dy_p, dy_m, dy_n)
        xhat, rstd = _ln_norm(cf)
        lg = lg_ref[...]
        ln = xhat * lg + lb_ref[...]
        s_ln, s_gc = _sigmoid(ln), _sigmoid(gc)
        dln = dyc * (gc * s_gc) * (s_ln * (1.0 + ln * (1.0 - s_ln)))
        dp_ref[:, 2 * C_BR:] = (dyc[main] * (ln[main] * s_ln[main])
                                * (s_gc[main] * (1.0 + gc[main] * (1.0 - s_gc[main])))).astype(ACT)
        dlg_ref[...] += _fold8(dln[main] * xhat[main])
        dlb_ref[...] += _fold8(dln[main])
        dc = _ln_bwd(dln, xhat, rstd, lg)
        dcb_ref[...] += _fold8(dc[main])

        def tiles(a):
            return a.reshape(a.shape[0], 8, 128)

        dce_ref[0:HALO] = tiles(jnp.where(i > 0, dc[0:HALO], 0.0))
        dce_ref[HALO:HALO + tm] = tiles(dc[main])
        dce_ref[HALO + tm:] = tiles(jnp.where(i < nt - 1, dc[HALO + tm:], 0.0))
        av = av_ref[...].astype(F32)
        sa = _sigmoid(ag_ref[...].astype(F32))
        glu_ref[...] = tiles(av * sa)

        nb = 16

        def step(t, carry):
            s0 = t * nb
            accs = [None] * nb
            for k in range(CONV_WIDTH):
                w = w_ref[k]
                prods = []
                for j in range(nb):
                    v = dce_ref[s0 + HALO + CONV_PAD + j - k]
                    term = w * v
                    accs[j] = term if accs[j] is None else accs[j] + term
                    prods.append(glu_ref[s0 + j] * v)
                while len(prods) > 1:
                    prods = [p + q for p, q in zip(prods[::2], prods[1::2])]
                dwc_ref[k] += prods[0]
            for j in range(nb):
                dgl_ref[s0 + j] = accs[j]
            return carry

        lax.fori_loop(0, tm // nb, step, 0)
        dglu = dgl_ref[...].reshape(tm, C_BR)
        dp_ref[:, 0:C_BR] = (dglu * sa).astype(ACT)
        dp_ref[:, C_BR:2 * C_BR] = (dglu * av * sa * (1.0 - sa)).astype(ACT)

        @pl.when(i == nt - 1)
        def _():
            copies = exchanges()
            own.wait()
            for cp in copies:
                cp.wait_recv()

            def step(t, carry):
                sl = pl.ds(pl.multiple_of(t * 64, 64), 64)
                g = land_ref[0, sl, :].astype(F32)
                for d in range(1, N_DEV):
                    g = g + land_ref[d, sl, :].astype(F32)
                gwo_ref[sl, :] = g
                return carry

            lax.fori_loop(0, wo_rows // 64, step, 0)
            for cp in copies:
                cp.wait_send()

    vec = pl.BlockSpec((1, C_BR), lambda i: (0, 0))
    part = pl.BlockSpec((8, C_BR), lambda i: (0, 0))
    return pl.pallas_call(
        body, name="conv_bwd", grid=(nt,),
        out_shape=(jax.ShapeDtypeStruct((s, D_IN), ACT), jax.ShapeDtypeStruct((CONV_ROWS, 8, 128), F32),
                   jax.ShapeDtypeStruct((8, C_BR), F32), jax.ShapeDtypeStruct((8, C_BR), F32),
                   jax.ShapeDtypeStruct((8, C_BR), F32), jax.ShapeDtypeStruct(dwout.shape[1:], F32)),
        in_specs=[pl.BlockSpec((tm, C_BR), lambda i: (i, 0)), pl.BlockSpec((tm, C_BR), lambda i: (i, 1))]
        + _halo_specs(tm, s, 2) + _halo_specs(tm, s, 0) + _halo_specs(tm, s, 0)
        + [pl.BlockSpec((CONV_ROWS, 8, 128), lambda i: (0, 0, 0)), vec, vec, pl.BlockSpec(memory_space=pl.ANY)],
        out_specs=(pl.BlockSpec((tm, 3 * C_BR), lambda i: (i, 0)),
                   pl.BlockSpec((CONV_ROWS, 8, 128), lambda i: (0, 0, 0)), part, part, part,
                   pl.BlockSpec(memory_space=pltpu.VMEM)),
        scratch_shapes=[pltpu.VMEM((tm + 2 * HALO, 8, 128), F32),
                        pltpu.VMEM((tm, 8, 128), F32), pltpu.VMEM((tm, 8, 128), F32),
                        pltpu.VMEM(dwout.shape, BF16),
                        pltpu.SemaphoreType.DMA((N_DEV - 1,)), pltpu.SemaphoreType.DMA((N_DEV - 1,)),
                        pltpu.SemaphoreType.DMA],
        compiler_params=_params(("arbitrary",)),
    )(proj, proj, proj, proj, proj, cv, cv, cv, dy, dy, dy, conv_w3, ln_g, ln_b, dwout)


def _sgu_bwd(proj, dy, dproj, ln_g, ln_b, ws, wst, bsb, tm):
    s = proj.shape[0]

    def body(u_ref, v_ref, gs_ref, dy_ref, dp_in, lg_ref, lb_ref, ws_ref, wst_ref, bsb_ref,
             dp_ref, dws_ref, dba_ref, dlg_ref, dlb_ref, dvn_ref):
        del dp_in
        i = pl.program_id(0)

        @pl.when(i == 0)
        def _():
            dws_ref[...] = jnp.zeros_like(dws_ref)
            dba_ref[...] = jnp.zeros_like(dba_ref)
            dlg_ref[...] = jnp.zeros_like(dlg_ref)
            dlb_ref[...] = jnp.zeros_like(dlb_ref)

        xhat, rstd = _ln_norm(v_ref[...].astype(F32))
        lg = lg_ref[...]
        vn = (xhat * lg + lb_ref[...]).astype(BF16)
        for cidx in range(tm // CHUNK):
            rows = slice(cidx * CHUNK, (cidx + 1) * CHUNK)
            for h in range(HEADS):
                cols = slice(h * HEAD_DIM, (h + 1) * HEAD_DIM)
                ocols = slice(C_BR + h * HEAD_DIM, C_BR + (h + 1) * HEAD_DIM)
                gcols = slice(2 * C_BR + h * HEAD_DIM, 2 * C_BR + (h + 1) * HEAD_DIM)
                vb = vn[rows, cols]
                mixed = jnp.dot(ws_ref[h], vb, preferred_element_type=F32) + bsb_ref[:, cols]
                gs = gs_ref[rows, cols].astype(F32)
                sg = _sigmoid(gs)
                u = u_ref[rows, cols].astype(F32)
                dyb = dy_ref[rows, cols].astype(F32)
                t = dyb * (gs * sg)
                dp_ref[rows, cols] = (t * mixed).astype(ACT)
                dp_ref[rows, gcols] = (dyb * u * mixed * (sg * (1.0 + gs * (1.0 - sg)))).astype(ACT)
                dm = t * u
                dmb = dm.astype(BF16)
                dvn_ref[rows, cols] = jnp.dot(wst_ref[h], dmb, preferred_element_type=F32)
                dws_ref[h] += lax.dot_general(dmb, vb, (((1,), (1,)), ((), ())), preferred_element_type=F32)
                dba_ref[:, cols] += dm
        dvn = dvn_ref[...]
        dlg_ref[...] += _fold8(dvn * xhat)
        dlb_ref[...] += _fold8(dvn)
        dp_ref[:, C_BR:2 * C_BR] = _ln_bwd(dvn, xhat, rstd, lg).astype(ACT)

    vec = pl.BlockSpec((1, C_BR), lambda i: (0, 0))
    part = pl.BlockSpec((8, C_BR), lambda i: (0, 0))
    wsp = pl.BlockSpec((HEADS, CHUNK, CHUNK), lambda i: (0, 0, 0))
    return pl.pallas_call(
        body, name="sgu_bwd", grid=(s // tm,),
        out_shape=(jax.ShapeDtypeStruct((s, D_IN), ACT), jax.ShapeDtypeStruct((HEADS, CHUNK, CHUNK), F32),
                   jax.ShapeDtypeStruct((CHUNK, C_BR), F32), jax.ShapeDtypeStruct((8, C_BR), F32),
                   jax.ShapeDtypeStruct((8, C_BR), F32)),
        in_specs=[pl.BlockSpec((tm, C_BR), lambda i: (i, 3)),
                  pl.BlockSpec((tm, C_BR), lambda i: (i, 4)),
                  pl.BlockSpec((tm, C_BR), lambda i: (i, 5)),
                  pl.BlockSpec((tm, C_BR), lambda i: (i, 1)),
                  pl.BlockSpec(memory_space=pl.ANY),
                  vec, vec, wsp, wsp, pl.BlockSpec((CHUNK, C_BR), lambda i: (0, 0))],
        out_specs=(pl.BlockSpec((tm, 3 * C_BR), lambda i: (i, 1)), wsp,
                   pl.BlockSpec((CHUNK, C_BR), lambda i: (0, 0)), part, part),
        scratch_shapes=[pltpu.VMEM((tm, C_BR), F32)],
        input_output_aliases={4: 0},
        compiler_params=_params(("arbitrary",)),
    )(proj, proj, proj, dy, dproj, ln_g, ln_b, ws, wst, bsb)


def _dx(dproj, win_all, x, norm_g, dx2, tm):
    s = x.shape[0]

    def body(dp_ref, w_ref, x_ref, g_ref, dx2_ref, gx_ref, dng_ref):
        i = pl.program_id(0)

        @pl.when(i == 0)
        def _():
            dng_ref[...] = jnp.zeros_like(dng_ref)

        dh = None
        for j in range(N_DEV):
            term = lax.dot_general(dp_ref[:, j * W_BLK:(j + 1) * W_BLK], w_ref[j],
                                   (((1,), (1,)), ((), ())), preferred_element_type=F32)
            dh = term if dh is None else dh + term
        xf = x_ref[...]
        r = lax.rsqrt(jnp.mean(xf * xf, axis=-1, keepdims=True) + EPS)
        n = xf * r
        dng_ref[...] += _fold8(dh * n)
        dn = dh * g_ref[...]
        gx_ref[...] = dx2_ref[...] + r * (dn - n * jnp.mean(dn * n, axis=-1, keepdims=True))

    return pl.pallas_call(
        body, name="dx", grid=(s // tm,),
        out_shape=(jax.ShapeDtypeStruct((s, D_MODEL), F32), jax.ShapeDtypeStruct((8, D_MODEL), F32)),
        in_specs=[pl.BlockSpec((tm, D_IN), lambda i: (i, 0)),
                  pl.BlockSpec((N_DEV, D_MODEL, W_BLK), lambda i: (0, 0, 0), pipeline_mode=pl.Buffered(1)),
                  pl.BlockSpec((tm, D_MODEL), lambda i: (i, 0)),
                  pl.BlockSpec((1, D_MODEL), lambda i: (0, 0)),
                  pl.BlockSpec((tm, D_MODEL), lambda i: (i, 0))],
        out_specs=(pl.BlockSpec((tm, D_MODEL), lambda i: (i, 0)), pl.BlockSpec((8, D_MODEL), lambda i: (0, 0))),
        compiler_params=_params(("arbitrary",)),
    )(dproj, win_all, x, norm_g, dx2)


def _dwin_comm(ht, dproj, order, parts, dwc, dba, dws, tk):
    s = ht.shape[1]
    nk = s // tk
    n_part = len(parts)

    def body(*refs):
        order_ref, ht_ref, dp_ref = refs[:3]
        del order_ref
        part_refs = refs[3:3 + n_part]
        dwc_ref, dba_ref, dws_ref = refs[3 + n_part:6 + n_part]
        gw_ref, red_ref, wsr_ref, cws_ref, loss_ref = refs[6 + n_part:11 + n_part]
        (acc_ref, all1_ref, all2_ref, out_s, land_s, out_x, land_x,
         send_sems, recv_sems, send_s, recv_s, send_x, recv_x) = refs[11 + n_part:]
        jj, k = pl.program_id(0), pl.program_id(1)
        x, y, c = _place()
        me = 4 * x + 2 * y + c
        sibling = (x, y, 1 - c)
        chips = [(1 - x, 1 - y), (1 - x, y), (x, 1 - y)]
        last = k == nk - 1

        def exchanges():
            out = []
            for rel in range(1, N_DEV):
                peer = (x ^ (rel >> 2), y ^ ((rel >> 1) & 1), c ^ (rel & 1))
                for a, buf in enumerate((all1_ref, all2_ref)):
                    out.append(pltpu.make_async_remote_copy(
                        src_ref=buf.at[me], dst_ref=buf.at[me],
                        send_sem=send_sems.at[a, rel - 1], recv_sem=recv_sems.at[a, rel - 1],
                        device_id=peer, device_id_type=MESH))
            return out

        def to_sibling(slot):
            return pltpu.make_async_remote_copy(
                src_ref=out_s.at[slot], dst_ref=land_s.at[slot],
                send_sem=send_s.at[slot], recv_sem=recv_s.at[slot], device_id=sibling, device_id_type=MESH)

        def to_chip(slot):
            return pltpu.make_async_remote_copy(
                src_ref=out_x.at[slot], dst_ref=land_x.at[slot],
                send_sem=send_x.at[slot], recv_sem=recv_x.at[slot],
                device_id=(*chips[slot], c), device_id_type=MESH)

        @pl.when((jj == 0) & (k == 0))
        def _():
            all1_ref[me] = jnp.zeros((SMALL_ROWS, C_BR), F32)
            for row, p_ref in zip((ROW_NORM_G, ROW_CONV_B, ROW_CLN_G, ROW_CLN_B, ROW_SLN_G, ROW_SLN_B,
                                   ROW_FINAL_G, ROW_LOSS), part_refs):
                all1_ref[me, row:row + 1, :] = jnp.sum(p_ref[...], axis=0, keepdims=True)
            ones = jnp.ones((8, HEAD_DIM), F32)
            brow = [lax.dot_general(ones, dba_ref[:, h * HEAD_DIM:(h + 1) * HEAD_DIM], (((1,), (1,)), ((), ())),
                                    precision=lax.Precision.HIGHEST, preferred_element_type=F32)[0:1]
                    for h in range(HEADS)]
            all1_ref[me, ROW_B_S:ROW_B_S + 1, :] = jnp.concatenate(brow, axis=1)
            all1_ref[me, ROW_CONV_W:ROW_CONV_W + CONV_ROWS, :] = dwc_ref[...]
            all2_ref[me] = dws_ref[...].astype(BF16)
            for cp in exchanges():
                cp.start()

        @pl.when(k == 0)
        def _():
            acc_ref[...] = jnp.zeros_like(acc_ref)

        acc_ref[...] += jnp.dot(ht_ref[...], dp_ref[...], preferred_element_type=F32)

        for slot in range(4):
            @pl.when((jj == 2 * slot) & last)
            def _(slot=slot):
                out_s[slot] = acc_ref[...].astype(BF16)
                to_sibling(slot).start()

        for slot in range(3):
            @pl.when((jj == 2 * slot + 1) & last)
            def _(slot=slot):
                to_sibling(slot).wait_recv()
                out_x[slot] = (acc_ref[...] + land_s[slot].astype(F32)).astype(BF16)
                to_chip(slot).start()

        @pl.when((jj == N_DEV - 1) & last)
        def _():
            to_sibling(3).wait_recv()
            total = acc_ref[...] + land_s[3].astype(F32)
            for slot in range(3):
                to_chip(slot).wait_recv()
                total = total + land_x[slot].astype(F32)
            gw_ref[...] = total

            copies = exchanges()
            for cp in copies:
                cp.wait_recv()
            tot = all1_ref[0]
            for d in range(1, N_DEV):
                tot = tot + all1_ref[d]
            red_ref[...] = tot
            loss_ref[...] = jnp.broadcast_to(
                jnp.sum(tot[ROW_LOSS:ROW_LOSS + 1, :], axis=1, keepdims=True) * (0.5 / D_MODEL), loss_ref.shape)
            shard = jnp.zeros(cws_ref.shape, F32)
            for d in range(N_DEV):
                shard = jnp.where(me == d, tot[ROW_CONV_W:ROW_CONV_W + CONV_ROWS, d * 128:(d + 1) * 128], shard)
            cws_ref[...] = shard
            tot2 = all2_ref[0].astype(F32)
            for d in range(1, N_DEV):
                tot2 = tot2 + all2_ref[d].astype(F32)
            wsr_ref[...] = tot2
            for cp in copies + [to_sibling(slot) for slot in range(4)] + [to_chip(slot) for slot in range(3)]:
                cp.wait_send()

    vm = pl.BlockSpec(memory_space=pltpu.VMEM)
    blk = (D_MODEL, W_BLK)
    return pl.pallas_call(
        body, name="dwin",
        grid_spec=pltpu.PrefetchScalarGridSpec(
            num_scalar_prefetch=1, grid=(N_DEV, nk),
            in_specs=[pl.BlockSpec((D_MODEL, tk), lambda jj, k, o: (0, k)),
                      pl.BlockSpec((tk, W_BLK), lambda jj, k, o: (k, o[jj]))] + [vm] * (n_part + 3),
            out_specs=(vm, vm, vm, vm, vm),
            scratch_shapes=[pltpu.VMEM(blk, F32),
                            pltpu.VMEM((N_DEV, SMALL_ROWS, C_BR), F32), pltpu.VMEM((N_DEV,) + dws.shape, BF16),
                            pltpu.VMEM((4,) + blk, BF16), pltpu.VMEM((4,) + blk, BF16),
                            pltpu.VMEM((3,) + blk, BF16), pltpu.VMEM((3,) + blk, BF16),
                            pltpu.SemaphoreType.DMA((2, N_DEV - 1)), pltpu.SemaphoreType.DMA((2, N_DEV - 1)),
                            pltpu.SemaphoreType.DMA((4,)), pltpu.SemaphoreType.DMA((4,)),
                            pltpu.SemaphoreType.DMA((3,)), pltpu.SemaphoreType.DMA((3,))]),
        out_shape=(jax.ShapeDtypeStruct(blk, F32),
                   jax.ShapeDtypeStruct((SMALL_ROWS, C_BR), F32), jax.ShapeDtypeStruct(dws.shape, F32),
                   jax.ShapeDtypeStruct((CONV_ROWS, 128), F32), jax.ShapeDtypeStruct((8, 128), F32)),
        compiler_params=_params(("arbitrary", "arbitrary")),
    )(order, ht, dproj, *parts, dwc, dba, dws)


def _adamw_math(w, g, m, v):
    m = ADAM_B1 * m + (1.0 - ADAM_B1) * g
    v = ADAM_B2 * v + (1.0 - ADAM_B2) * (g * g)
    m_hat = m / (1.0 - ADAM_B1 ** ADAM_STEP)
    v_hat = v / (1.0 - ADAM_B2 ** ADAM_STEP)
    delta = -ADAM_LR * (m_hat / (jnp.sqrt(v_hat) + ADAM_EPS) + ADAM_WD * w)
    return delta, m, v


def _adamw(g, w, m, v, name, tr):
    rows, cols = w.shape

    def body(g_ref, w_ref, m_ref, v_ref, d_ref, mo_ref, vo_ref):
        d_ref[...], mo_ref[...], vo_ref[...] = _adamw_math(w_ref[...], g_ref[...], m_ref[...], v_ref[...])

    spec = pl.BlockSpec((tr, cols), lambda i: (i, 0))
    sds = jax.ShapeDtypeStruct((rows, cols), F32)
    return pl.pallas_call(
        body, name=name, grid=(rows // tr,), out_shape=(sds, sds, sds),
        in_specs=[spec] * 4, out_specs=(spec, spec, spec),
        compiler_params=_params(("parallel",)),
    )(g, w, m, v)


def _adamw_rows(red, rows, ws, ms, vs):
    n = len(rows)

    def body(*refs):
        red_ref = refs[0]
        w_refs, m_refs, v_refs = refs[1:1 + n], refs[1 + n:1 + 2 * n], refs[1 + 2 * n:1 + 3 * n]
        outs = refs[1 + 3 * n:]
        for t, row in enumerate(rows):
            g = red_ref[row:row + 1, :]
            delta, m, v = _adamw_math(w_refs[t][...], g, m_refs[t][...], v_refs[t][...])
            outs[t][...] = g
            outs[n + t][...] = delta
            outs[2 * n + t][...] = m
            outs[3 * n + t][...] = v

    vm = pl.BlockSpec(memory_space=pltpu.VMEM)
    sds = jax.ShapeDtypeStruct((1, C_BR), F32)
    res = pl.pallas_call(
        body, name="adamw_rows", out_shape=(sds,) * (4 * n),
        in_specs=[vm] * (1 + 3 * n), out_specs=(vm,) * (4 * n),
        compiler_params=_params(),
    )(red, *ws, *ms, *vs)
    return res[:n], res[n:2 * n], res[2 * n:3 * n], res[3 * n:]


def kernel(x, norm_g, w_in, conv_w, conv_b, conv_ln_g, conv_ln_b, sgu_ln_g, sgu_ln_b, w_s, b_s, w_out, final_g, loss_target, m_norm_g, m_w_in, m_conv_w, m_conv_b, m_conv_ln_g, m_conv_ln_b, m_sgu_ln_g, m_sgu_ln_b, m_w_s, m_b_s, m_w_out, m_final_g, v_norm_g, v_w_in, v_conv_w, v_conv_b, v_conv_ln_g, v_conv_ln_b, v_sgu_ln_g, v_sgu_ln_b, v_w_s, v_b_s, v_w_out, v_final_g):
    s = x.shape[1]
    xs = x.reshape(s, D_MODEL)
    tgt = loss_target.reshape(s, D_MODEL)
    tm = min(256, s)

    cw_pad = jnp.pad(conv_w[0], ((0, CONV_ROWS - CONV_WIDTH), (0, 0)))
    px, py, pc = _place()
    blocks = [(px, py, pc), (px, py, 1 - pc)]
    blocks += [(*chip, core) for chip in ((1 - px, py), (px, 1 - py), (1 - px, 1 - py)) for core in (pc, 1 - pc)]
    order = jnp.stack([4 * bx + 2 * by + bc for bx, by, bc in blocks]).astype(jnp.int32)
    proj, ht, win_all, wout_all, cw_all = _proj_ag(xs, norm_g, w_in[0], w_out[0], cw_pad, order, min(1024, s))
    wout_full = wout_all.reshape(2 * C_BR, D_MODEL)
    cw_tiles = jnp.transpose(cw_all, (1, 0, 2))

    ws = w_s[0].astype(BF16)
    wst = jnp.transpose(w_s[0], (0, 2, 1)).astype(BF16)
    bsb = jnp.repeat(jnp.transpose(b_s[0]), HEAD_DIM, axis=1)
    fg = final_g.reshape(1, D_MODEL)

    big = min(512, s)
    y, cv = _conv_fwd(proj, cw_tiles, conv_b, conv_ln_g, conv_ln_b, big)
    y = _sgu_fwd(proj, y, sgu_ln_g, sgu_ln_b, ws, bsb, big)
    dx2, dy, dwout, loss_p, dfg_p = _out_loss(xs, y, wout_full, fg, tgt, min(512, s))
    dproj, dwc, dcb_p, dclg_p, dclb_p, g_w_out = _conv_bwd(
        proj, cv, dy, cw_tiles, conv_ln_g, conv_ln_b, dwout.reshape(N_DEV, 2 * C_BR // N_DEV, D_MODEL), big)
    dwc = dwc.reshape(CONV_ROWS, C_BR)
    dproj, dws, dba, dslg_p, dslb_p = _sgu_bwd(proj, dy, dproj, sgu_ln_g, sgu_ln_b, ws, wst, bsb, big)
    grad_x, dng_p = _dx(dproj, win_all, xs, norm_g, dx2, big)
    rs_blocks = [(*chip, core) for chip in ((1 - px, 1 - py), (1 - px, py), (px, 1 - py), (px, py))
                 for core in (1 - pc, pc)]
    rs_order = jnp.stack([4 * bx + 2 * by + bc for bx, by, bc in rs_blocks]).astype(jnp.int32)
    g_w_in, red, g_w_s, g_cw, loss8 = _dwin_comm(
        ht, dproj, rs_order, [dng_p, dcb_p, dclg_p, dclb_p, dslg_p, dslb_p, dfg_p, loss_p], dwc, dba,
        dws.reshape(HEADS * CHUNK, CHUNK), min(1024, s))
    loss = loss8[0, 0]

    d_w_in, nm_w_in, nv_w_in = _adamw(g_w_in, w_in[0], m_w_in[0], v_w_in[0], "adamw_w_in", 256)
    d_w_out, nm_w_out, nv_w_out = _adamw(g_w_out, w_out[0], m_w_out[0], v_w_out[0], "adamw_w_out", 256)
    g_cw = g_cw[:CONV_WIDTH]
    d_cw, nm_cw, nv_cw = _adamw(g_cw, conv_w[0], m_conv_w[0], v_conv_w[0], "adamw_conv_w", CONV_WIDTH)
    flat = lambda a: a.reshape(HEADS * CHUNK, CHUNK)
    d_ws, nm_ws, nv_ws = _adamw(g_w_s, flat(w_s), flat(m_w_s), flat(v_w_s), "adamw_w_s", HEADS * CHUNK)
    row = lambda a: a.reshape(1, C_BR)
    rows = (ROW_NORM_G, ROW_CONV_B, ROW_CLN_G, ROW_CLN_B, ROW_SLN_G, ROW_SLN_B, ROW_B_S, ROW_FINAL_G)
    g_r, d_r, m_r, v_r = _adamw_rows(
        red, rows,
        [norm_g, conv_b, conv_ln_g, conv_ln_b, sgu_ln_g, sgu_ln_b, row(b_s), row(final_g)],
        [m_norm_g, m_conv_b, m_conv_ln_g, m_conv_ln_b, m_sgu_ln_g, m_sgu_ln_b, row(m_b_s), row(m_final_g)],
        [v_norm_g, v_conv_b, v_conv_ln_g, v_conv_ln_b, v_sgu_ln_g, v_sgu_ln_b, row(v_b_s), row(v_final_g)])

    def leaves(r, w_in_l, cw_l, ws_l, w_out_l):
        return (r[0], w_in_l[None], cw_l[None], r[1], r[2], r[3], r[4], r[5],
                ws_l.reshape(1, HEADS, CHUNK, CHUNK), r[6].reshape(1, HEADS, CHUNK), w_out_l[None],
                r[7].reshape(D_MODEL))

    return (loss, grad_x.reshape(1, s, D_MODEL),
            *leaves(g_r, g_w_in, g_cw, g_w_s, g_w_out),
            *leaves(d_r, d_w_in, d_cw, d_ws, d_w_out),
            *leaves(m_r, nm_w_in, nm_cw, nm_ws, nm_w_out),
            *leaves(v_r, nv_w_in, nv_cw, nv_ws, nv_w_out))
```

```python
import functools

import jax
import jax.numpy as jnp
from jax import lax
from jax.experimental import pallas as pl
from jax.experimental.pallas import tpu as pltpu

F32 = jnp.float32
BF16 = jnp.bfloat16
ACT = jnp.bfloat16

D_MODEL = 1024
C_BR = 1024
D_IN = 6 * C_BR
N_DEV = 8
W_BLK = D_IN // N_DEV
HEADS = 8
HEAD_DIM = 128
CHUNK = 128
CONV_WIDTH = 31
CONV_PAD = CONV_WIDTH // 2
HALO = 16
CONV_ROWS = 32
EPS = 1e-6

ADAM_LR = 0.001
ADAM_B1 = 0.9
ADAM_B2 = 0.999
ADAM_EPS = 1e-08
ADAM_WD = 0.01
ADAM_STEP = 10

VMEM_LIMIT = 56 * 1024 * 1024
MESH = pl.DeviceIdType.MESH

ROW_NORM_G, ROW_CONV_B, ROW_CLN_G, ROW_CLN_B, ROW_SLN_G, ROW_SLN_B, ROW_FINAL_G, ROW_B_S, ROW_LOSS = range(9)
ROW_CONV_W = 16
SMALL_ROWS = ROW_CONV_W + CONV_ROWS


def _params(sem=None, **kw):
    return pltpu.CompilerParams(dimension_semantics=sem, vmem_limit_bytes=VMEM_LIMIT, **kw)


def _fold8(a):
    r, n = a.shape
    return a.reshape(r // 8, 8, n).sum(axis=0)


def _sigmoid(z):
    return 0.5 * jnp.tanh(0.5 * z) + 0.5


def _ln_norm(xf):
    mu = jnp.mean(xf, axis=-1, keepdims=True)
    xc = xf - mu
    var = jnp.mean(xc * xc, axis=-1, keepdims=True)
    rstd = lax.rsqrt(var + EPS)
    return xc * rstd, rstd


def _ln_bwd(dy, xhat, rstd, g):
    dxhat = dy * g
    m1 = jnp.mean(dxhat, axis=-1, keepdims=True)
    m2 = jnp.mean(dxhat * xhat, axis=-1, keepdims=True)
    return rstd * (dxhat - m1 - xhat * m2)


def _place():
    return lax.axis_index("x"), lax.axis_index("y"), lax.axis_index("c")


def _proj_ag(x, norm_g, w_in, w_out, conv_w, order, tm):
    s = x.shape[0]
    nt = s // tm

    def body(order_ref, x_ref, g_ref, win_ref, wout_ref, cw_ref,
             proj_ref, ht_ref, win_all, wout_all, cw_all,
             h_ref, win_buf, wout_buf, cw_buf, send_sems, recv_sems, save_sems):
        jj, i = pl.program_id(0), pl.program_id(1)
        x_, y_, c_ = _place()
        me, sibling = (x_, y_, c_), (x_, y_, 1 - c_)
        chips = [(1 - x_, y_), (x_, 1 - y_), (1 - x_, 1 - y_)]
        bufs = (win_buf, wout_buf, cw_buf)
        outs = (win_all, wout_all, cw_all)
        start = i == 0

        def index(px, py, pc):
            return 4 * px + 2 * py + pc

        def copy(a, k, block, to):
            return pltpu.make_async_remote_copy(
                src_ref=bufs[a].at[index(*block)], dst_ref=bufs[a].at[index(*block)],
                send_sem=send_sems.at[a, k], recv_sem=recv_sems.at[a, k],
                device_id=to, device_id_type=MESH)

        def save(a, slot, block):
            return pltpu.make_async_copy(bufs[a].at[index(*block)], outs[a].at[index(*block)], save_sems.at[a, slot])

        def first(a):
            return [copy(a, 0, me, sibling)] + [copy(a, 1 + j, me, (*chip, c_)) for j, chip in enumerate(chips)]

        def saves(a):
            blocks = [me, sibling] + [(*chip, core) for chip in chips for core in (c_, 1 - c_)]
            return [save(a, slot, block) for slot, block in enumerate(blocks)]

        @pl.when((jj == 0) & start)
        def _():
            win_buf[index(*me)] = win_ref[...].astype(BF16)
            wout_buf[index(*me)] = wout_ref[...].astype(BF16)
            cw_buf[index(*me)] = cw_ref[...]
            for a in range(3):
                for cp in first(a):
                    cp.start()
                saves(a)[0].start()

        @pl.when((jj == 1) & start)
        def _():
            copy(0, 0, sibling, me).wait_recv()
            saves(0)[1].start()

        for j, chip in enumerate(chips):
            @pl.when((jj == 2 + 2 * j) & start)
            def _(j=j, chip=chip):
                copy(0, 1 + j, (*chip, c_), me).wait_recv()
                copy(0, 4 + j, (*chip, c_), sibling).start()
                saves(0)[2 + 2 * j].start()

            @pl.when((jj == 3 + 2 * j) & start)
            def _(j=j, chip=chip):
                copy(0, 4 + j, (*chip, 1 - c_), me).wait_recv()
                saves(0)[3 + 2 * j].start()

        @pl.when(jj == 0)
        def _():
            xf = x_ref[...]
            r = lax.rsqrt(jnp.mean(xf * xf, axis=-1, keepdims=True) + EPS)
            hf = xf * r * g_ref[...]
            h_ref[i] = hf.astype(BF16)
            ht_ref[...] = hf.T.astype(BF16)

        proj_ref[...] = jnp.dot(h_ref[i], win_buf[order_ref[jj]], preferred_element_type=F32).astype(ACT)

        @pl.when((jj == N_DEV - 1) & (i == nt - 1))
        def _():
            passed = [copy(0, 4 + j, (*chip, c_), sibling) for j, chip in enumerate(chips)]
            for a in (1, 2):
                for j, chip in enumerate(chips):
                    copy(a, 1 + j, (*chip, c_), me).wait_recv()
                    fwd = copy(a, 4 + j, (*chip, c_), sibling)
                    fwd.start()
                    passed.append(fwd)
                    saves(a)[2 + 2 * j].start()
            for a in (1, 2):
                copy(a, 0, sibling, me).wait_recv()
                saves(a)[1].start()
                for j, chip in enumerate(chips):
                    copy(a, 4 + j, (*chip, 1 - c_), me).wait_recv()
                    saves(a)[3 + 2 * j].start()
            for cp in saves(0) + saves(1) + saves(2):
                cp.wait()
            for cp in first(0) + first(1) + first(2) + passed:
                cp.wait_send()

    vm = pl.BlockSpec(memory_space=pltpu.VMEM)
    hbm = pl.BlockSpec(memory_space=pl.ANY)
    once = lambda jj, i: jnp.where(jj == 0, i, nt - 1)
    stacked = [(N_DEV,) + w.shape for w in (w_in, w_out, conv_w)]
    return pl.pallas_call(
        body, name="proj_ag",
        grid_spec=pltpu.PrefetchScalarGridSpec(
            num_scalar_prefetch=1, grid=(N_DEV, nt),
            in_specs=[pl.BlockSpec((tm, D_MODEL), lambda jj, i, o: (once(jj, i), 0)),
                      pl.BlockSpec((1, D_MODEL), lambda jj, i, o: (0, 0)), vm, vm, vm],
            out_specs=(pl.BlockSpec((tm, W_BLK), lambda jj, i, o: (i, o[jj])),
                       pl.BlockSpec((D_MODEL, tm), lambda jj, i, o: (0, once(jj, i))), hbm, hbm, hbm),
            scratch_shapes=[pltpu.VMEM((nt, tm, D_MODEL), BF16),
                            pltpu.VMEM(stacked[0], BF16), pltpu.VMEM(stacked[1], BF16), pltpu.VMEM(stacked[2], F32),
                            pltpu.SemaphoreType.DMA((3, 7)), pltpu.SemaphoreType.DMA((3, 7)),
                            pltpu.SemaphoreType.DMA((3, N_DEV))]),
        out_shape=(jax.ShapeDtypeStruct((s, D_IN), ACT), jax.ShapeDtypeStruct((D_MODEL, s), BF16),
                   jax.ShapeDtypeStruct(stacked[0], BF16), jax.ShapeDtypeStruct(stacked[1], BF16),
                   jax.ShapeDtypeStruct(stacked[2], F32)),
        compiler_params=_params(("arbitrary", "arbitrary")),
    )(order, x, norm_g, w_in, w_out, conv_w)


def _halo_specs(tm, s, col):
    per = tm // HALO
    last = s // HALO - 1
    return [pl.BlockSpec((HALO, C_BR), lambda i: (jnp.maximum(i * per - 1, 0), col)),
            pl.BlockSpec((tm, C_BR), lambda i: (i, col)),
            pl.BlockSpec((HALO, C_BR), lambda i: (jnp.minimum((i + 1) * per, last), col))]


def _conv_fwd(proj, conv_w3, conv_b, ln_g, ln_b, tm):
    s = proj.shape[0]
    nt = s // tm

    def body(av_p, av_m, av_n, ag_p, ag_m, ag_n, gc_ref, w_ref, cb_ref, lg_ref, lb_ref,
             y_ref, c_ref, ext_ref, cv_ref):
        i = pl.program_id(0)

        def glu(a_ref, g_ref):
            return a_ref[...].astype(F32) * _sigmoid(g_ref[...].astype(F32))

        def tiles(a):
            return a.reshape(a.shape[0], 8, 128)

        ext_ref[0:HALO] = tiles(jnp.where(i > 0, glu(av_p, ag_p), 0.0))
        ext_ref[HALO:HALO + tm] = tiles(glu(av_m, ag_m))
        ext_ref[HALO + tm:] = tiles(jnp.where(i < nt - 1, glu(av_n, ag_n), 0.0))

        nb = 16

        def step(t, carry):
            s0 = t * nb
            accs = [None] * nb
            for k in range(CONV_WIDTH):
                w = w_ref[k]
                for j in range(nb):
                    term = w * ext_ref[s0 + HALO - CONV_PAD + j + k]
                    accs[j] = term if accs[j] is None else accs[j] + term
            for j in range(nb):
                cv_ref[s0 + j] = accs[j]
            return carry

        lax.fori_loop(0, tm // nb, step, 0)
        cv = cv_ref[...].reshape(tm, C_BR) + cb_ref[...]
        c_ref[...] = cv.astype(ACT)
        xhat, _ = _ln_norm(cv)
        ln = xhat * lg_ref[...] + lb_ref[...]
        gc = gc_ref[...].astype(F32)
        y_ref[...] = (ln * _sigmoid(ln) * (gc * _sigmoid(gc))).astype(ACT)

    vec = pl.BlockSpec((1, C_BR), lambda i: (0, 0))
    return pl.pallas_call(
        body, name="conv_fwd", grid=(nt,),
        out_shape=(jax.ShapeDtypeStruct((s, 2 * C_BR), ACT), jax.ShapeDtypeStruct((s, C_BR), ACT)),
        in_specs=_halo_specs(tm, s, 0) + _halo_specs(tm, s, 1)
        + [pl.BlockSpec((tm, C_BR), lambda i: (i, 2)),
           pl.BlockSpec((CONV_ROWS, 8, 128), lambda i: (0, 0, 0)), vec, vec, vec],
        out_specs=(pl.BlockSpec((tm, C_BR), lambda i: (i, 0)), pl.BlockSpec((tm, C_BR), lambda i: (i, 0))),
        scratch_shapes=[pltpu.VMEM((tm + 2 * HALO, 8, 128), F32),
                        pltpu.VMEM((tm, 8, 128), F32)],
        compiler_params=_params(("parallel",)),
    )(proj, proj, proj, proj, proj, proj, proj, conv_w3, conv_b, ln_g, ln_b)


def _sgu_fwd(proj, y, ln_g, ln_b, ws, bsb, tm):
    s = proj.shape[0]

    def body(u_ref, v_ref, gs_ref, y_in, lg_ref, lb_ref, ws_ref, bsb_ref, y_ref):
        del y_in
        xhat, _ = _ln_norm(v_ref[...].astype(F32))
        vn = (xhat * lg_ref[...] + lb_ref[...]).astype(BF16)
        for cidx in range(tm // CHUNK):
            rows = slice(cidx * CHUNK, (cidx + 1) * CHUNK)
            for h in range(HEADS):
                cols = slice(h * HEAD_DIM, (h + 1) * HEAD_DIM)
                mixed = jnp.dot(ws_ref[h], vn[rows, cols], preferred_element_type=F32) + bsb_ref[:, cols]
                gs = gs_ref[rows, cols].astype(F32)
                y_ref[rows, cols] = (u_ref[rows, cols].astype(F32) * mixed * (gs * _sigmoid(gs))).astype(ACT)

    vec = pl.BlockSpec((1, C_BR), lambda i: (0, 0))
    return pl.pallas_call(
        body, name="sgu_fwd", grid=(s // tm,),
        out_shape=jax.ShapeDtypeStruct((s, 2 * C_BR), ACT),
        in_specs=[pl.BlockSpec((tm, C_BR), lambda i: (i, 3)),
                  pl.BlockSpec((tm, C_BR), lambda i: (i, 4)),
                  pl.BlockSpec((tm, C_BR), lambda i: (i, 5)),
                  pl.BlockSpec(memory_space=pl.ANY),
                  vec, vec,
                  pl.BlockSpec((HEADS, CHUNK, CHUNK), lambda i: (0, 0, 0)),
                  pl.BlockSpec((CHUNK, C_BR), lambda i: (0, 0))],
        out_specs=pl.BlockSpec((tm, C_BR), lambda i: (i, 1)),
        input_output_aliases={3: 0},
        compiler_params=_params(("parallel",)),
    )(proj, proj, proj, y, ln_g, ln_b, ws, bsb)


def _out_loss(x, y, wout, final_g, target, tm):
    s = x.shape[0]
    nt = s // tm
    inv_d = 1.0 / D_MODEL

    def body(x_ref, y_ref, w_ref, g_ref, t_ref, dx2_ref, dy_ref, dw_ref, loss_ref, dfg_ref, acc_ref):
        i = pl.program_id(0)

        @pl.when(i == 0)
        def _():
            acc_ref[...] = jnp.zeros_like(acc_ref)
            loss_ref[...] = jnp.zeros_like(loss_ref)
            dfg_ref[...] = jnp.zeros_like(dfg_ref)

        yb = y_ref[...]
        x2 = x_ref[...] + jnp.dot(yb, w_ref[...], preferred_element_type=F32)
        r2 = lax.rsqrt(jnp.mean(x2 * x2, axis=-1, keepdims=True) + EPS)
        n = x2 * r2
        g = g_ref[...]
        e = n * g - t_ref[...]
        loss_ref[...] += _fold8(e * e)
        dout = e * inv_d
        dfg_ref[...] += _fold8(dout * n)
        dn = dout * g
        dx2 = r2 * (dn - n * jnp.mean(dn * n, axis=-1, keepdims=True))
        dx2_ref[...] = dx2
        dxb = dx2.astype(BF16)
        dy_ref[...] = lax.dot_general(dxb, w_ref[...], (((1,), (1,)), ((), ())),
                                      preferred_element_type=F32).astype(ACT)
        acc_ref[...] += lax.dot_general(yb, dxb, (((0,), (0,)), ((), ())), preferred_element_type=F32)

        @pl.when(i == nt - 1)
        def _():
            dw_ref[...] = acc_ref[...].astype(BF16)

    part = pl.BlockSpec((8, D_MODEL), lambda i: (0, 0))
    return pl.pallas_call(
        body, name="out_loss", grid=(nt,),
        out_shape=(jax.ShapeDtypeStruct((s, D_MODEL), F32), jax.ShapeDtypeStruct((s, 2 * C_BR), ACT),
                   jax.ShapeDtypeStruct((2 * C_BR, D_MODEL), BF16),
                   jax.ShapeDtypeStruct((8, D_MODEL), F32), jax.ShapeDtypeStruct((8, D_MODEL), F32)),
        in_specs=[pl.BlockSpec((tm, D_MODEL), lambda i: (i, 0)),
                  pl.BlockSpec((tm, 2 * C_BR), lambda i: (i, 0)),
                  pl.BlockSpec((2 * C_BR, D_MODEL), lambda i: (0, 0), pipeline_mode=pl.Buffered(1)),
                  pl.BlockSpec((1, D_MODEL), lambda i: (0, 0)),
                  pl.BlockSpec((tm, D_MODEL), lambda i: (i, 0))],
        out_specs=(pl.BlockSpec((tm, D_MODEL), lambda i: (i, 0)),
                   pl.BlockSpec((tm, 2 * C_BR), lambda i: (i, 0)),
                   pl.BlockSpec((2 * C_BR, D_MODEL), lambda i: (0, 0), pipeline_mode=pl.Buffered(1)), part, part),
        scratch_shapes=[pltpu.VMEM((2 * C_BR, D_MODEL), F32)],
        compiler_params=_params(("arbitrary",)),
    )(x, y, wout, final_g, target)


def _conv_bwd(proj, cv, dy, conv_w3, ln_g, ln_b, dwout, tm):
    s = proj.shape[0]
    nt = s // tm
    wo_rows = dwout.shape[1]

    def body(av_ref, ag_ref, gc_p, gc_m, gc_n, c_p, c_m, c_n, dy_p, dy_m, dy_n, w_ref, lg_ref, lb_ref,
             dwout_ref, dp_ref, dwc_ref, dcb_ref, dlg_ref, dlb_ref, gwo_ref,
             dce_ref, glu_ref, dgl_ref, land_ref, send_sems, recv_sems, loc_sem):
        i = pl.program_id(0)
        px, py, pc = _place()
        me = 4 * px + 2 * py + pc

        def exchanges():
            out = []
            for rel in range(1, N_DEV):
                qx, qy, qc = px ^ (rel >> 2), py ^ ((rel >> 1) & 1), pc ^ (rel & 1)
                out.append(pltpu.make_async_remote_copy(
                    src_ref=dwout_ref.at[4 * qx + 2 * qy + qc], dst_ref=land_ref.at[me],
                    send_sem=send_sems.at[rel - 1], recv_sem=recv_sems.at[rel - 1],
                    device_id=(qx, qy, qc), device_id_type=MESH))
            return out

        own = pltpu.make_async_copy(dwout_ref.at[me], land_ref.at[me], loc_sem)

        @pl.when(i == 0)
        def _():
            dwc_ref[...] = jnp.zeros_like(dwc_ref)
            dcb_ref[...] = jnp.zeros_like(dcb_ref)
            dlg_ref[...] = jnp.zeros_like(dlg_ref)
            dlb_ref[...] = jnp.zeros_like(dlb_ref)
            own.start()
            for cp in exchanges():
                cp.start()

        def ext(p, m, n):
            return jnp.concatenate([p[...], m[...], n[...]], axis=0).astype(F32)

        main = slice(HALO, HALO + tm)
        cf, gc, dyc = ext(c_p, c_m, c_n), ext(gc_p, gc_m, gc_n), ext(dy_p, dy_m, dy_n)
        xhat, rstd = _ln_norm(cf)
        lg = lg_ref[...]
        ln = xhat * lg + lb_ref[...]
        s_ln, s_gc = _sigmoid(ln), _sigmoid(gc)
        dln = dyc * (gc * s_gc) * (s_ln * (1.0 + ln * (1.0 - s_ln)))
        dp_ref[:, 2 * C_BR:] = (dyc[main] * (ln[main] * s_ln[main])
                                * (s_gc[main] * (1.0 + gc[main] * (1.0 - s_gc[main])))).astype(ACT)
        dlg_ref[...] += _fold8(dln[main] * xhat[main])
        dlb_ref[...] += _fold8(dln[main])
        dc = _ln_bwd(dln, xhat, rstd, lg)
        dcb_ref[...] += _fold8(dc[main])

        def tiles(a):
            return a.reshape(a.shape[0], 8, 128)

        dce_ref[0:HALO] = tiles(jnp.where(i > 0, dc[0:HALO], 0.0))
        dce_ref[HALO:HALO + tm] = tiles(dc[main])
        dce_ref[HALO + tm:] = tiles(jnp.where(i < nt - 1, dc[HALO + tm:], 0.0))
        av = av_ref[...].astype(F32)
        sa = _sigmoid(ag_ref[...].astype(F32))
        glu_ref[...] = tiles(av * sa)

        nb = 16

        def step(t, carry):
            s0 = t * nb
            accs = [None] * nb
            for k in range(CONV_WIDTH):
                w = w_ref[k]
                prods = []
                for j in range(nb):
                    v = dce_ref[s0 + HALO + CONV_PAD + j - k]
                    term = w * v
                    accs[j] = term if accs[j] is None else accs[j] + term
                    prods.append(glu_ref[s0 + j] * v)
                while len(prods) > 1:
                    prods = [p + q for p, q in zip(prods[::2], prods[1::2])]
                dwc_ref[k] += prods[0]
            for j in range(nb):
                dgl_ref[s0 + j] = accs[j]
            return carry

        lax.fori_loop(0, tm // nb, step, 0)
        dglu = dgl_ref[...].reshape(tm, C_BR)
        dp_ref[:, 0:C_BR] = (dglu * sa).astype(ACT)
        dp_ref[:, C_BR:2 * C_BR] = (dglu * av * sa * (1.0 - sa)).astype(ACT)

        @pl.when(i == nt - 1)
        def _():
            copies = exchanges()
            own.wait()
            for cp in copies:
                cp.wait_recv()

            def step(t, carry):
                sl = pl.ds(pl.multiple_of(t * 64, 64), 64)
                g = land_ref[0, sl, :].astype(F32)
                for d in range(1, N_DEV):
                    g = g + land_ref[d, sl, :].astype(F32)
                gwo_ref[sl, :] = g
                return carry

            lax.fori_loop(0, wo_rows // 64, step, 0)
            for cp in copies:
                cp.wait_send()

    vec = pl.BlockSpec((1, C_BR), lambda i: (0, 0))
    part = pl.BlockSpec((8, C_BR), lambda i: (0, 0))
    return pl.pallas_call(
        body, name="conv_bwd", grid=(nt,),
        out_shape=(jax.ShapeDtypeStruct((s, D_IN), ACT), jax.ShapeDtypeStruct((CONV_ROWS, 8, 128), F32),
                   jax.ShapeDtypeStruct((8, C_BR), F32), jax.ShapeDtypeStruct((8, C_BR), F32),
                   jax.ShapeDtypeStruct((8, C_BR), F32), jax.ShapeDtypeStruct(dwout.shape[1:], F32)),
        in_specs=[pl.BlockSpec((tm, C_BR), lambda i: (i, 0)), pl.BlockSpec((tm, C_BR), lambda i: (i, 1))]
        + _halo_specs(tm, s, 2) + _halo_specs(tm, s, 0) + _halo_specs(tm, s, 0)
        + [pl.BlockSpec((CONV_ROWS, 8, 128), lambda i: (0, 0, 0)), vec, vec, pl.BlockSpec(memory_space=pl.ANY)],
        out_specs=(pl.BlockSpec((tm, 3 * C_BR), lambda i: (i, 0)),
                   pl.BlockSpec((CONV_ROWS, 8, 128), lambda i: (0, 0, 0)), part, part, part,
                   pl.BlockSpec(memory_space=pltpu.VMEM)),
        scratch_shapes=[pltpu.VMEM((tm + 2 * HALO, 8, 128), F32),
                        pltpu.VMEM((tm, 8, 128), F32), pltpu.VMEM((tm, 8, 128), F32),
                        pltpu.VMEM(dwout.shape, BF16),
                        pltpu.SemaphoreType.DMA((N_DEV - 1,)), pltpu.SemaphoreType.DMA((N_DEV - 1,)),
                        pltpu.SemaphoreType.DMA],
        compiler_params=_params(("arbitrary",)),
    )(proj, proj, proj, proj, proj, cv, cv, cv, dy, dy, dy, conv_w3, ln_g, ln_b, dwout)


def _sgu_bwd(proj, dy, dproj, ln_g, ln_b, ws, wst, bsb, tm):
    s = proj.shape[0]

    def body(u_ref, v_ref, gs_ref, dy_ref, dp_in, lg_ref, lb_ref, ws_ref, wst_ref, bsb_ref,
             dp_ref, dws_ref, dba_ref, dlg_ref, dlb_ref, dvn_ref):
        del dp_in
        i = pl.program_id(0)

        @pl.when(i == 0)
        def _():
            dws_ref[...] = jnp.zeros_like(dws_ref)
            dba_ref[...] = jnp.zeros_like(dba_ref)
            dlg_ref[...] = jnp.zeros_like(dlg_ref)
            dlb_ref[...] = jnp.zeros_like(dlb_ref)

        xhat, rstd = _ln_norm(v_ref[...].astype(F32))
        lg = lg_ref[...]
        vn = (xhat * lg + lb_ref[...]).astype(BF16)
        for cidx in range(tm // CHUNK):
            rows = slice(cidx * CHUNK, (cidx + 1) * CHUNK)
            for h in range(HEADS):
                cols = slice(h * HEAD_DIM, (h + 1) * HEAD_DIM)
                ocols = slice(C_BR + h * HEAD_DIM, C_BR + (h + 1) * HEAD_DIM)
                gcols = slice(2 * C_BR + h * HEAD_DIM, 2 * C_BR + (h + 1) * HEAD_DIM)
                vb = vn[rows, cols]
                mixed = jnp.dot(ws_ref[h], vb, preferred_element_type=F32) + bsb_ref[:, cols]
                gs = gs_ref[rows, cols].astype(F32)
                sg = _sigmoid(gs)
                u = u_ref[rows, cols].astype(F32)
                dyb = dy_ref[rows, cols].astype(F32)
                t = dyb * (gs * sg)
                dp_ref[rows, cols] = (t * mixed).astype(ACT)
                dp_ref[rows, gcols] = (dyb * u * mixed * (sg * (1.0 + gs * (1.0 - sg)))).astype(ACT)
                dm = t * u
                dmb = dm.astype(BF16)
                dvn_ref[rows, cols] = jnp.dot(wst_ref[h], dmb, preferred_element_type=F32)
                dws_ref[h] += lax.dot_general(dmb, vb, (((1,), (1,)), ((), ())), preferred_element_type=F32)
                dba_ref[:, cols] += dm
        dvn = dvn_ref[...]
        dlg_ref[...] += _fold8(dvn * xhat)
        dlb_ref[...] += _fold8(dvn)
        dp_ref[:, C_BR:2 * C_BR] = _ln_bwd(dvn, xhat, rstd, lg).astype(ACT)

    vec = pl.BlockSpec((1, C_BR), lambda i: (0, 0))
    part = pl.BlockSpec((8, C_BR), lambda i: (0, 0))
    wsp = pl.BlockSpec((HEADS, CHUNK, CHUNK), lambda i: (0, 0, 0))
    return pl.pallas_call(
        body, name="sgu_bwd", grid=(s // tm,),
        out_shape=(jax.ShapeDtypeStruct((s, D_IN), ACT), jax.ShapeDtypeStruct((HEADS, CHUNK, CHUNK), F32),
                   jax.ShapeDtypeStruct((CHUNK, C_BR), F32), jax.ShapeDtypeStruct((8, C_BR), F32),
                   jax.ShapeDtypeStruct((8, C_BR), F32)),
        in_specs=[pl.BlockSpec((tm, C_BR), lambda i: (i, 3)),
                  pl.BlockSpec((tm, C_BR), lambda i: (i, 4)),
                  pl.BlockSpec((tm, C_BR), lambda i: (i, 5)),
                  pl.BlockSpec((tm, C_BR), lambda i: (i, 1)),
                  pl.BlockSpec(memory_space=pl.ANY),
                  vec, vec, wsp, wsp, pl.BlockSpec((CHUNK, C_BR), lambda i: (0, 0))],
        out_specs=(pl.BlockSpec((tm, 3 * C_BR), lambda i: (i, 1)), wsp,
                   pl.BlockSpec((CHUNK, C_BR), lambda i: (0, 0)), part, part),
        scratch_shapes=[pltpu.VMEM((tm, C_BR), F32)],
        input_output_aliases={4: 0},
        compiler_params=_params(("arbitrary",)),
    )(proj, proj, proj, dy, dproj, ln_g, ln_b, ws, wst, bsb)


def _dx(dproj, win_all, x, norm_g, dx2, tm):
    s = x.shape[0]

    def body(dp_ref, w_ref, x_ref, g_ref, dx2_ref, gx_ref, dng_ref):
        i = pl.program_id(0)

        @pl.when(i == 0)
        def _():
            dng_ref[...] = jnp.zeros_like(dng_ref)

        dh = None
        for j in range(N_DEV):
            term = lax.dot_general(dp_ref[:, j * W_BLK:(j + 1) * W_BLK], w_ref[j],
                                   (((1,), (1,)), ((), ())), preferred_element_type=F32)
            dh = term if dh is None else dh + term
        xf = x_ref[...]
        r = lax.rsqrt(jnp.mean(xf * xf, axis=-1, keepdims=True) + EPS)
        n = xf * r
        dng_ref[...] += _fold8(dh * n)
        dn = dh * g_ref[...]
        gx_ref[...] = dx2_ref[...] + r * (dn - n * jnp.mean(dn * n, axis=-1, keepdims=True))

    return pl.pallas_call(
        body, name="dx", grid=(s // tm,),
        out_shape=(jax.ShapeDtypeStruct((s, D_MODEL), F32), jax.ShapeDtypeStruct((8, D_MODEL), F32)),
        in_specs=[pl.BlockSpec((tm, D_IN), lambda i: (i, 0)),
                  pl.BlockSpec((N_DEV, D_MODEL, W_BLK), lambda i: (0, 0, 0), pipeline_mode=pl.Buffered(1)),
                  pl.BlockSpec((tm, D_MODEL), lambda i: (i, 0)),
                  pl.BlockSpec((1, D_MODEL), lambda i: (0, 0)),
                  pl.BlockSpec((tm, D_MODEL), lambda i: (i, 0))],
        out_specs=(pl.BlockSpec((tm, D_MODEL), lambda i: (i, 0)), pl.BlockSpec((8, D_MODEL), lambda i: (0, 0))),
        compiler_params=_params(("arbitrary",)),
    )(dproj, win_all, x, norm_g, dx2)


def _dwin_comm(ht, dproj, order, parts, dwc, dba, dws, tk):
    s = ht.shape[1]
    nk = s // tk
    n_part = len(parts)

    def body(*refs):
        order_ref, ht_ref, dp_ref = refs[:3]
        del order_ref
        part_refs = refs[3:3 + n_part]
        dwc_ref, dba_ref, dws_ref = refs[3 + n_part:6 + n_part]
        gw_ref, red_ref, wsr_ref, cws_ref, loss_ref = refs[6 + n_part:11 + n_part]
        (acc_ref, all1_ref, all2_ref, out_s, land_s, out_x, land_x,
         send_sems, recv_sems, send_s, recv_s, send_x, recv_x) = refs[11 + n_part:]
        jj, k = pl.program_id(0), pl.program_id(1)
        x, y, c = _place()
        me = 4 * x + 2 * y + c
        sibling = (x, y, 1 - c)
        chips = [(1 - x, 1 - y), (1 - x, y), (x, 1 - y)]
        last = k == nk - 1

        def exchanges():
            out = []
            for rel in range(1, N_DEV):
                peer = (x ^ (rel >> 2), y ^ ((rel >> 1) & 1), c ^ (rel & 1))
                for a, buf in enumerate((all1_ref, all2_ref)):
                    out.append(pltpu.make_async_remote_copy(
                        src_ref=buf.at[me], dst_ref=buf.at[me],
                        send_sem=send_sems.at[a, rel - 1], recv_sem=recv_sems.at[a, rel - 1],
                        device_id=peer, device_id_type=MESH))
            return out

        def to_sibling(slot):
            return pltpu.make_async_remote_copy(
                src_ref=out_s.at[slot], dst_ref=land_s.at[slot],
                send_sem=send_s.at[slot], recv_sem=recv_s.at[slot], device_id=sibling, device_id_type=MESH)

        def to_chip(slot):
            return pltpu.make_async_remote_copy(
                src_ref=out_x.at[slot], dst_ref=land_x.at[slot],
                send_sem=send_x.at[slot], recv_sem=recv_x.at[slot],
                device_id=(*chips[slot], c), device_id_type=MESH)

        @pl.when((jj == 0) & (k == 0))
        def _():
            all1_ref[me] = jnp.zeros((SMALL_ROWS, C_BR), F32)
            for row, p_ref in zip((ROW_NORM_G, ROW_CONV_B, ROW_CLN_G, ROW_CLN_B, ROW_SLN_G, ROW_SLN_B,
                                   ROW_FINAL_G, ROW_LOSS), part_refs):
                all1_ref[me, row:row + 1, :] = jnp.sum(p_ref[...], axis=0, keepdims=True)
            ones = jnp.ones((8, HEAD_DIM), F32)
            brow = [lax.dot_general(ones, dba_ref[:, h * HEAD_DIM:(h + 1) * HEAD_DIM], (((1,), (1,)), ((), ())),
                                    precision=lax.Precision.HIGHEST, preferred_element_type=F32)[0:1]
                    for h in range(HEADS)]
            all1_ref[me, ROW_B_S:ROW_B_S + 1, :] = jnp.concatenate(brow, axis=1)
            all1_ref[me, ROW_CONV_W:ROW_CONV_W + CONV_ROWS, :] = dwc_ref[...]
            all2_ref[me] = dws_ref[...]
            for cp in exchanges():
                cp.start()

        @pl.when(k == 0)
        def _():
            acc_ref[...] = jnp.zeros_like(acc_ref)

        acc_ref[...] += jnp.dot(ht_ref[...], dp_ref[...], preferred_element_type=F32)

        for slot in range(4):
            @pl.when((jj == 2 * slot) & last)
            def _(slot=slot):
                out_s[slot] = acc_ref[...].astype(BF16)
                to_sibling(slot).start()

        for slot in range(3):
            @pl.when((jj == 2 * slot + 1) & last)
            def _(slot=slot):
                to_sibling(slot).wait_recv()
                out_x[slot] = (acc_ref[...] + land_s[slot].astype(F32)).astype(BF16)
                to_chip(slot).start()

        @pl.when((jj == N_DEV - 1) & last)
        def _():
            to_sibling(3).wait_recv()
            total = acc_ref[...] + land_s[3].astype(F32)
            for slot in range(3):
                to_chip(slot).wait_recv()
                total = total + land_x[slot].astype(F32)
            gw_ref[...] = total

            copies = exchanges()
            for cp in copies:
                cp.wait_recv()
            tot = all1_ref[0]
            for d in range(1, N_DEV):
                tot = tot + all1_ref[d]
            red_ref[...] = tot
            loss_ref[...] = jnp.broadcast_to(
                jnp.sum(tot[ROW_LOSS:ROW_LOSS + 1, :], axis=1, keepdims=True) * (0.5 / D_MODEL), loss_ref.shape)
            shard = jnp.zeros(cws_ref.shape, F32)
            for d in range(N_DEV):
                shard = jnp.where(me == d, tot[ROW_CONV_W:ROW_CONV_W + CONV_ROWS, d * 128:(d + 1) * 128], shard)
            cws_ref[...] = shard
            tot2 = all2_ref[0]
            for d in range(1, N_DEV):
                tot2 = tot2 + all2_ref[d]
            wsr_ref[...] = tot2
            for cp in copies + [to_sibling(slot) for slot in range(4)] + [to_chip(slot) for slot in range(3)]:
                cp.wait_send()

    vm = pl.BlockSpec(memory_space=pltpu.VMEM)
    blk = (D_MODEL, W_BLK)
    return pl.pallas_call(
        body, name="dwin",
        grid_spec=pltpu.PrefetchScalarGridSpec(
            num_scalar_prefetch=1, grid=(N_DEV, nk),
            in_specs=[pl.BlockSpec((D_MODEL, tk), lambda jj, k, o: (0, k)),
                      pl.BlockSpec((tk, W_BLK), lambda jj, k, o: (k, o[jj]))] + [vm] * (n_part + 3),
            out_specs=(vm, vm, vm, vm, vm),
            scratch_shapes=[pltpu.VMEM(blk, F32),
                            pltpu.VMEM((N_DEV, SMALL_ROWS, C_BR), F32), pltpu.VMEM((N_DEV,) + dws.shape, F32),
                            pltpu.VMEM((4,) + blk, BF16), pltpu.VMEM((4,) + blk, BF16),
                            pltpu.VMEM((3,) + blk, BF16), pltpu.VMEM((3,) + blk, BF16),
                            pltpu.SemaphoreType.DMA((2, N_DEV - 1)), pltpu.SemaphoreType.DMA((2, N_DEV - 1)),
                            pltpu.SemaphoreType.DMA((4,)), pltpu.SemaphoreType.DMA((4,)),
                            pltpu.SemaphoreType.DMA((3,)), pltpu.SemaphoreType.DMA((3,))]),
        out_shape=(jax.ShapeDtypeStruct(blk, F32),
                   jax.ShapeDtypeStruct((SMALL_ROWS, C_BR), F32), jax.ShapeDtypeStruct(dws.shape, F32),
                   jax.ShapeDtypeStruct((CONV_ROWS, 128), F32), jax.ShapeDtypeStruct((8, 128), F32)),
        compiler_params=_params(("arbitrary", "arbitrary")),
    )(order, ht, dproj, *parts, dwc, dba, dws)


def _adamw_math(w, g, m, v):
    m = ADAM_B1 * m + (1.0 - ADAM_B1) * g
    v = ADAM_B2 * v + (1.0 - ADAM_B2) * (g * g)
    m_hat = m / (1.0 - ADAM_B1 ** ADAM_STEP)
    v_hat = v / (1.0 - ADAM_B2 ** ADAM_STEP)
    delta = -ADAM_LR * (m_hat / (jnp.sqrt(v_hat) + ADAM_EPS) + ADAM_WD * w)
    return delta, m, v


def _adamw_all(groups, g_conv_w, conv_wmv, red, rows, ws, ms, vs):
    ng, n = len(groups), len(rows)
    rchunk = 64

    def body(*refs):
        big = refs[:4 * ng]
        gcw_ref, cw_ref, cm_ref, cv_ref, red_ref = refs[4 * ng:4 * ng + 5]
        small = refs[4 * ng + 5:4 * ng + 5 + 3 * n]
        w_refs, m_refs, v_refs = small[:n], small[n:2 * n], small[2 * n:]
        outs = refs[4 * ng + 5 + 3 * n:]
        for t in range(ng):
            g_ref, w_ref, m_ref, v_ref = big[4 * t:4 * t + 4]
            d_ref, mo_ref, vo_ref = outs[3 * t:3 * t + 3]
            lead = w_ref.shape[0]
            chunks = lead // rchunk if len(w_ref.shape) == 2 else lead

            def step(c, carry, refs_=(g_ref, w_ref, m_ref, v_ref, d_ref, mo_ref, vo_ref), two_d=len(w_ref.shape) == 2):
                sl = pl.ds(pl.multiple_of(c * rchunk, rchunk), rchunk) if two_d else c
                g_, w_, m_, v_, d_, mo_, vo_ = refs_
                d_[sl], mo_[sl], vo_[sl] = _adamw_math(w_[sl], g_[sl], m_[sl], v_[sl])
                return carry

            lax.fori_loop(0, chunks, step, 0)
        g = gcw_ref[0:CONV_WIDTH, :]
        o = 3 * ng
        outs[o][...] = g
        outs[o + 1][...], outs[o + 2][...], outs[o + 3][...] = _adamw_math(cw_ref[...], g, cm_ref[...], cv_ref[...])
        o += 4
        for t, row in enumerate(rows):
            g = red_ref[row:row + 1, :]
            delta, m, v = _adamw_math(w_refs[t][...], g, m_refs[t][...], v_refs[t][...])
            outs[o + t][...] = g
            outs[o + n + t][...] = delta
            outs[o + 2 * n + t][...] = m
            outs[o + 3 * n + t][...] = v

    vm = pl.BlockSpec(memory_space=pltpu.VMEM)
    sds = lambda a: jax.ShapeDtypeStruct(a.shape, F32)
    out_shape = []
    for grp in groups:
        out_shape += [sds(grp[1])] * 3
    out_shape += [sds(conv_wmv[0])] * 4
    out_shape += [jax.ShapeDtypeStruct((1, C_BR), F32)] * (4 * n)
    args = [a for grp in groups for a in grp] + [g_conv_w, *conv_wmv, red, *ws, *ms, *vs]
    res = pl.pallas_call(
        body, name="adamw", out_shape=tuple(out_shape),
        in_specs=[vm] * len(args), out_specs=(vm,) * len(out_shape),
        compiler_params=_params(),
    )(*args)
    big_out = [res[3 * t:3 * t + 3] for t in range(ng)]
    o = 3 * ng
    conv_out = res[o:o + 4]
    o += 4
    return big_out, conv_out, (res[o:o + n], res[o + n:o + 2 * n], res[o + 2 * n:o + 3 * n], res[o + 3 * n:o + 4 * n])


def kernel(x, norm_g, w_in, conv_w, conv_b, conv_ln_g, conv_ln_b, sgu_ln_g, sgu_ln_b, w_s, b_s, w_out, final_g, loss_target, m_norm_g, m_w_in, m_conv_w, m_conv_b, m_conv_ln_g, m_conv_ln_b, m_sgu_ln_g, m_sgu_ln_b, m_w_s, m_b_s, m_w_out, m_final_g, v_norm_g, v_w_in, v_conv_w, v_conv_b, v_conv_ln_g, v_conv_ln_b, v_sgu_ln_g, v_sgu_ln_b, v_w_s, v_b_s, v_w_out, v_final_g):
    s = x.shape[1]
    xs = x.reshape(s, D_MODEL)
    tgt = loss_target.reshape(s, D_MODEL)
    tm = min(256, s)

    cw_pad = jnp.pad(conv_w[0], ((0, CONV_ROWS - CONV_WIDTH), (0, 0)))
    px, py, pc = _place()
    blocks = [(px, py, pc), (px, py, 1 - pc)]
    blocks += [(*chip, core) for chip in ((1 - px, py), (px, 1 - py), (1 - px, 1 - py)) for core in (pc, 1 - pc)]
    order = jnp.stack([4 * bx + 2 * by + bc for bx, by, bc in blocks]).astype(jnp.int32)
    proj, ht, win_all, wout_all, cw_all = _proj_ag(xs, norm_g, w_in[0], w_out[0], cw_pad, order, min(1024, s))
    wout_full = wout_all.reshape(2 * C_BR, D_MODEL)
    cw_tiles = jnp.transpose(cw_all, (1, 0, 2))

    ws = w_s[0].astype(BF16)
    wst = jnp.transpose(w_s[0], (0, 2, 1)).astype(BF16)
    bsb = jnp.repeat(jnp.transpose(b_s[0]), HEAD_DIM, axis=1)
    fg = final_g.reshape(1, D_MODEL)

    big = min(512, s)
    y, cv = _conv_fwd(proj, cw_tiles, conv_b, conv_ln_g, conv_ln_b, big)
    y = _sgu_fwd(proj, y, sgu_ln_g, sgu_ln_b, ws, bsb, big)
    dx2, dy, dwout, loss_p, dfg_p = _out_loss(xs, y, wout_full, fg, tgt, min(512, s))
    dproj, dwc, dcb_p, dclg_p, dclb_p, g_w_out = _conv_bwd(
        proj, cv, dy, cw_tiles, conv_ln_g, conv_ln_b, dwout.reshape(N_DEV, 2 * C_BR // N_DEV, D_MODEL), big)
    dwc = dwc.reshape(CONV_ROWS, C_BR)
    dproj, dws, dba, dslg_p, dslb_p = _sgu_bwd(proj, dy, dproj, sgu_ln_g, sgu_ln_b, ws, wst, bsb, big)
    grad_x, dng_p = _dx(dproj, win_all, xs, norm_g, dx2, big)
    rs_blocks = [(*chip, core) for chip in ((1 - px, 1 - py), (1 - px, py), (px, 1 - py), (px, py))
                 for core in (1 - pc, pc)]
    rs_order = jnp.stack([4 * bx + 2 * by + bc for bx, by, bc in rs_blocks]).astype(jnp.int32)
    g_w_in, red, g_w_s, g_cw, loss8 = _dwin_comm(
        ht, dproj, rs_order, [dng_p, dcb_p, dclg_p, dclb_p, dslg_p, dslb_p, dfg_p, loss_p], dwc, dba,
        dws, min(1024, s))
    loss = loss8[0, 0]

    row = lambda a: a.reshape(1, C_BR)
    rows = (ROW_NORM_G, ROW_CONV_B, ROW_CLN_G, ROW_CLN_B, ROW_SLN_G, ROW_SLN_B, ROW_B_S, ROW_FINAL_G)
    big_out, (g_cw, d_cw, nm_cw, nv_cw), (g_r, d_r, m_r, v_r) = _adamw_all(
        [(g_w_in, w_in[0], m_w_in[0], v_w_in[0]), (g_w_out, w_out[0], m_w_out[0], v_w_out[0]),
         (g_w_s, w_s[0], m_w_s[0], v_w_s[0])],
        g_cw, (conv_w[0], m_conv_w[0], v_conv_w[0]), red, rows,
        [norm_g, conv_b, conv_ln_g, conv_ln_b, sgu_ln_g, sgu_ln_b, row(b_s), row(final_g)],
        [m_norm_g, m_conv_b, m_conv_ln_g, m_conv_ln_b, m_sgu_ln_g, m_sgu_ln_b, row(m_b_s), row(m_final_g)],
        [v_norm_g, v_conv_b, v_conv_ln_g, v_conv_ln_b, v_sgu_ln_g, v_sgu_ln_b, row(v_b_s), row(v_final_g)])
    (d_w_in, nm_w_in, nv_w_in), (d_w_out, nm_w_out, nv_w_out), (d_ws, nm_ws, nv_ws) = big_out

    def leaves(r, w_in_l, cw_l, ws_l, w_out_l):
        return (r[0], w_in_l[None], cw_l[None], r[1], r[2], r[3], r[4], r[5],
                ws_l[None], r[6].reshape(1, HEADS, CHUNK), w_out_l[None],
                r[7].reshape(D_MODEL))

    return (loss, grad_x.reshape(1, s, D_MODEL),
            *leaves(g_r, g_w_in, g_cw, g_w_s, g_w_out),
            *leaves(d_r, d_w_in, d_cw, d_ws, d_w_out),
            *leaves(m_r, nm_w_in, nm_cw, nm_ws, nm_w_out),
            *leaves(v_r, nv_w_in, nv_cw, nv_ws, nv_w_out))
```

```python
import functools

import jax
import jax.numpy as jnp
from jax import lax
from jax.experimental import pallas as pl
from jax.experimental.pallas import tpu as pltpu

F32 = jnp.float32
BF16 = jnp.bfloat16
ACT = jnp.bfloat16

D_MODEL = 1024
C_BR = 1024
D_IN = 6 * C_BR
N_DEV = 8
W_BLK = D_IN // N_DEV
HEADS = 8
HEAD_DIM = 128
CHUNK = 128
CONV_WIDTH = 31
CONV_PAD = CONV_WIDTH // 2
HALO = 16
CONV_ROWS = 32
EPS = 1e-6

ADAM_LR = 0.001
ADAM_B1 = 0.9
ADAM_B2 = 0.999
ADAM_EPS = 1e-08
ADAM_WD = 0.01
ADAM_STEP = 10

VMEM_LIMIT = 56 * 1024 * 1024
MESH = pl.DeviceIdType.MESH

ROW_NORM_G, ROW_CONV_B, ROW_CLN_G, ROW_CLN_B, ROW_SLN_G, ROW_SLN_B, ROW_FINAL_G, ROW_B_S, ROW_LOSS = range(9)
ROW_CONV_W = 16
SMALL_ROWS = ROW_CONV_W + CONV_ROWS


def _params(sem=None, **kw):
    return pltpu.CompilerParams(dimension_semantics=sem, vmem_limit_bytes=VMEM_LIMIT, **kw)


def _fold8(a):
    r, n = a.shape
    return a.reshape(r // 8, 8, n).sum(axis=0)


def _sigmoid(z):
    return 0.5 * jnp.tanh(0.5 * z) + 0.5


def _ln_norm(xf):
    mu = jnp.mean(xf, axis=-1, keepdims=True)
    xc = xf - mu
    var = jnp.mean(xc * xc, axis=-1, keepdims=True)
    rstd = lax.rsqrt(var + EPS)
    return xc * rstd, rstd


def _ln_bwd(dy, xhat, rstd, g):
    dxhat = dy * g
    m1 = jnp.mean(dxhat, axis=-1, keepdims=True)
    m2 = jnp.mean(dxhat * xhat, axis=-1, keepdims=True)
    return rstd * (dxhat - m1 - xhat * m2)


def _place():
    return lax.axis_index("x"), lax.axis_index("y"), lax.axis_index("c")


def _proj_ag(x, norm_g, w_in, w_out, conv_w, order, tm):
    s = x.shape[0]
    nt = s // tm

    def body(order_ref, x_ref, g_ref, win_ref, wout_ref, cw_ref,
             proj_ref, ht_ref, win_all, wout_all, cw_all,
             h_ref, win_buf, wout_buf, cw_buf, send_sems, recv_sems, save_sems):
        jj, i = pl.program_id(0), pl.program_id(1)
        x_, y_, c_ = _place()
        me, sibling = (x_, y_, c_), (x_, y_, 1 - c_)
        chips = [(1 - x_, y_), (x_, 1 - y_), (1 - x_, 1 - y_)]
        bufs = (win_buf, wout_buf, cw_buf)
        outs = (win_all, wout_all, cw_all)
        start = i == 0

        def index(px, py, pc):
            return 4 * px + 2 * py + pc

        def copy(a, k, block, to):
            return pltpu.make_async_remote_copy(
                src_ref=bufs[a].at[index(*block)], dst_ref=bufs[a].at[index(*block)],
                send_sem=send_sems.at[a, k], recv_sem=recv_sems.at[a, k],
                device_id=to, device_id_type=MESH)

        def save(a, slot, block):
            return pltpu.make_async_copy(bufs[a].at[index(*block)], outs[a].at[index(*block)], save_sems.at[a, slot])

        def first(a):
            return [copy(a, 0, me, sibling)] + [copy(a, 1 + j, me, (*chip, c_)) for j, chip in enumerate(chips)]

        def saves(a):
            blocks = [me, sibling] + [(*chip, core) for chip in chips for core in (c_, 1 - c_)]
            return [save(a, slot, block) for slot, block in enumerate(blocks)]

        @pl.when((jj == 0) & start)
        def _():
            win_buf[index(*me)] = win_ref[...].astype(BF16)
            wout_buf[index(*me)] = wout_ref[...].astype(BF16)
            cw_buf[index(*me)] = cw_ref[...]
            for a in range(3):
                for cp in first(a):
                    cp.start()
                saves(a)[0].start()

        @pl.when((jj == 1) & start)
        def _():
            copy(0, 0, sibling, me).wait_recv()
            saves(0)[1].start()

        for j, chip in enumerate(chips):
            @pl.when((jj == 2 + 2 * j) & start)
            def _(j=j, chip=chip):
                copy(0, 1 + j, (*chip, c_), me).wait_recv()
                copy(0, 4 + j, (*chip, c_), sibling).start()
                saves(0)[2 + 2 * j].start()

            @pl.when((jj == 3 + 2 * j) & start)
            def _(j=j, chip=chip):
                copy(0, 4 + j, (*chip, 1 - c_), me).wait_recv()
                saves(0)[3 + 2 * j].start()

        @pl.when(jj == 0)
        def _():
            xf = x_ref[...]
            r = lax.rsqrt(jnp.mean(xf * xf, axis=-1, keepdims=True) + EPS)
            hf = xf * r * g_ref[...]
            h_ref[i] = hf.astype(BF16)
            ht_ref[...] = hf.T.astype(BF16)

        proj_ref[...] = jnp.dot(h_ref[i], win_buf[order_ref[jj]], preferred_element_type=F32).astype(ACT)

        @pl.when((jj == N_DEV - 1) & (i == nt - 1))
        def _():
            passed = [copy(0, 4 + j, (*chip, c_), sibling) for j, chip in enumerate(chips)]
            for a in (1, 2):
                for j, chip in enumerate(chips):
                    copy(a, 1 + j, (*chip, c_), me).wait_recv()
                    fwd = copy(a, 4 + j, (*chip, c_), sibling)
                    fwd.start()
                    passed.append(fwd)
                    saves(a)[2 + 2 * j].start()
            for a in (1, 2):
                copy(a, 0, sibling, me).wait_recv()
                saves(a)[1].start()
                for j, chip in enumerate(chips):
                    copy(a, 4 + j, (*chip, 1 - c_), me).wait_recv()
                    saves(a)[3 + 2 * j].start()
            for cp in saves(0) + saves(1) + saves(2):
                cp.wait()
            for cp in first(0) + first(1) + first(2) + passed:
                cp.wait_send()

    vm = pl.BlockSpec(memory_space=pltpu.VMEM)
    hbm = pl.BlockSpec(memory_space=pl.ANY)
    once = lambda jj, i: jnp.where(jj == 0, i, nt - 1)
    stacked = [(N_DEV,) + w.shape for w in (w_in, w_out, conv_w)]
    return pl.pallas_call(
        body, name="proj_ag",
        grid_spec=pltpu.PrefetchScalarGridSpec(
            num_scalar_prefetch=1, grid=(N_DEV, nt),
            in_specs=[pl.BlockSpec((tm, D_MODEL), lambda jj, i, o: (once(jj, i), 0)),
                      pl.BlockSpec((1, D_MODEL), lambda jj, i, o: (0, 0)), vm, vm, vm],
            out_specs=(pl.BlockSpec((tm, W_BLK), lambda jj, i, o: (i, o[jj])),
                       pl.BlockSpec((D_MODEL, tm), lambda jj, i, o: (0, once(jj, i))), hbm, hbm, hbm),
            scratch_shapes=[pltpu.VMEM((nt, tm, D_MODEL), BF16),
                            pltpu.VMEM(stacked[0], BF16), pltpu.VMEM(stacked[1], BF16), pltpu.VMEM(stacked[2], F32),
                            pltpu.SemaphoreType.DMA((3, 7)), pltpu.SemaphoreType.DMA((3, 7)),
                            pltpu.SemaphoreType.DMA((3, N_DEV))]),
        out_shape=(jax.ShapeDtypeStruct((s, D_IN), ACT), jax.ShapeDtypeStruct((D_MODEL, s), BF16),
                   jax.ShapeDtypeStruct(stacked[0], BF16), jax.ShapeDtypeStruct(stacked[1], BF16),
                   jax.ShapeDtypeStruct(stacked[2], F32)),
        compiler_params=_params(("arbitrary", "arbitrary")),
    )(order, x, norm_g, w_in, w_out, conv_w)


def _halo_specs(tm, s, col):
    per = tm // HALO
    last = s // HALO - 1
    return [pl.BlockSpec((HALO, C_BR), lambda i: (jnp.maximum(i * per - 1, 0), col)),
            pl.BlockSpec((tm, C_BR), lambda i: (i, col)),
            pl.BlockSpec((HALO, C_BR), lambda i: (jnp.minimum((i + 1) * per, last), col))]


def _conv_fwd(proj, conv_w3, conv_b, ln_g, ln_b, tm):
    s = proj.shape[0]
    nt = s // tm

    def body(av_p, av_m, av_n, ag_p, ag_m, ag_n, gc_ref, w_ref, cb_ref, lg_ref, lb_ref,
             y_ref, c_ref, ext_ref, cv_ref):
        i = pl.program_id(0)

        def glu(a_ref, g_ref):
            return a_ref[...].astype(F32) * _sigmoid(g_ref[...].astype(F32))

        def tiles(a):
            return a.reshape(a.shape[0], 8, 128)

        ext_ref[0:HALO] = tiles(jnp.where(i > 0, glu(av_p, ag_p), 0.0))
        ext_ref[HALO:HALO + tm] = tiles(glu(av_m, ag_m))
        ext_ref[HALO + tm:] = tiles(jnp.where(i < nt - 1, glu(av_n, ag_n), 0.0))

        nb = 16

        def step(t, carry):
            s0 = t * nb
            accs = [None] * nb
            for k in range(CONV_WIDTH):
                w = w_ref[k]
                for j in range(nb):
                    term = w * ext_ref[s0 + HALO - CONV_PAD + j + k]
                    accs[j] = term if accs[j] is None else accs[j] + term
            for j in range(nb):
                cv_ref[s0 + j] = accs[j]
            return carry

        lax.fori_loop(0, tm // nb, step, 0)
        cv = cv_ref[...].reshape(tm, C_BR) + cb_ref[...]
        c_ref[...] = cv.astype(ACT)
        xhat, _ = _ln_norm(cv)
        ln = xhat * lg_ref[...] + lb_ref[...]
        gc = gc_ref[...].astype(F32)
        y_ref[...] = (ln * _sigmoid(ln) * (gc * _sigmoid(gc))).astype(ACT)

    vec = pl.BlockSpec((1, C_BR), lambda i: (0, 0))
    return pl.pallas_call(
        body, name="conv_fwd", grid=(nt,),
        out_shape=(jax.ShapeDtypeStruct((s, 2 * C_BR), ACT), jax.ShapeDtypeStruct((s, C_BR), ACT)),
        in_specs=_halo_specs(tm, s, 0) + _halo_specs(tm, s, 1)
        + [pl.BlockSpec((tm, C_BR), lambda i: (i, 2)),
           pl.BlockSpec((CONV_ROWS, 8, 128), lambda i: (0, 0, 0)), vec, vec, vec],
        out_specs=(pl.BlockSpec((tm, C_BR), lambda i: (i, 0)), pl.BlockSpec((tm, C_BR), lambda i: (i, 0))),
        scratch_shapes=[pltpu.VMEM((tm + 2 * HALO, 8, 128), F32),
                        pltpu.VMEM((tm, 8, 128), F32)],
        compiler_params=_params(("parallel",)),
    )(proj, proj, proj, proj, proj, proj, proj, conv_w3, conv_b, ln_g, ln_b)


def _sgu_fwd(proj, y, ln_g, ln_b, ws, bsb, tm):
    s = proj.shape[0]

    def body(u_ref, v_ref, gs_ref, y_in, lg_ref, lb_ref, ws_ref, bsb_ref, y_ref):
        del y_in
        xhat, _ = _ln_norm(v_ref[...].astype(F32))
        vn = (xhat * lg_ref[...] + lb_ref[...]).astype(BF16)
        for cidx in range(tm // CHUNK):
            rows = slice(cidx * CHUNK, (cidx + 1) * CHUNK)
            for h in range(HEADS):
                cols = slice(h * HEAD_DIM, (h + 1) * HEAD_DIM)
                mixed = jnp.dot(ws_ref[h], vn[rows, cols], preferred_element_type=F32) + bsb_ref[:, cols]
                gs = gs_ref[rows, cols].astype(F32)
                y_ref[rows, cols] = (u_ref[rows, cols].astype(F32) * mixed * (gs * _sigmoid(gs))).astype(ACT)

    vec = pl.BlockSpec((1, C_BR), lambda i: (0, 0))
    return pl.pallas_call(
        body, name="sgu_fwd", grid=(s // tm,),
        out_shape=jax.ShapeDtypeStruct((s, 2 * C_BR), ACT),
        in_specs=[pl.BlockSpec((tm, C_BR), lambda i: (i, 3)),
                  pl.BlockSpec((tm, C_BR), lambda i: (i, 4)),
                  pl.BlockSpec((tm, C_BR), lambda i: (i, 5)),
                  pl.BlockSpec(memory_space=pl.ANY),
                  vec, vec,
                  pl.BlockSpec((HEADS, CHUNK, CHUNK), lambda i: (0, 0, 0)),
                  pl.BlockSpec((CHUNK, C_BR), lambda i: (0, 0))],
        out_specs=pl.BlockSpec((tm, C_BR), lambda i: (i, 1)),
        input_output_aliases={3: 0},
        compiler_params=_params(("parallel",)),
    )(proj, proj, proj, y, ln_g, ln_b, ws, bsb)


def _out_loss(x, y, wout, final_g, target, tm):
    s = x.shape[0]
    nt = s // tm
    inv_d = 1.0 / D_MODEL

    def body(x_ref, y_ref, w_ref, g_ref, t_ref, dx2_ref, dy_ref, dw_ref, loss_ref, dfg_ref, acc_ref):
        i = pl.program_id(0)

        @pl.when(i == 0)
        def _():
            acc_ref[...] = jnp.zeros_like(acc_ref)
            loss_ref[...] = jnp.zeros_like(loss_ref)
            dfg_ref[...] = jnp.zeros_like(dfg_ref)

        yb = y_ref[...]
        x2 = x_ref[...] + jnp.dot(yb, w_ref[...], preferred_element_type=F32)
        r2 = lax.rsqrt(jnp.mean(x2 * x2, axis=-1, keepdims=True) + EPS)
        n = x2 * r2
        g = g_ref[...]
        e = n * g - t_ref[...]
        loss_ref[...] += _fold8(e * e)
        dout = e * inv_d
        dfg_ref[...] += _fold8(dout * n)
        dn = dout * g
        dx2 = r2 * (dn - n * jnp.mean(dn * n, axis=-1, keepdims=True))
        dx2_ref[...] = dx2
        dxb = dx2.astype(BF16)
        dy_ref[...] = lax.dot_general(dxb, w_ref[...], (((1,), (1,)), ((), ())),
                                      preferred_element_type=F32).astype(ACT)
        acc_ref[...] += lax.dot_general(yb, dxb, (((0,), (0,)), ((), ())), preferred_element_type=F32)

        @pl.when(i == nt - 1)
        def _():
            dw_ref[...] = acc_ref[...].astype(BF16)

    part = pl.BlockSpec((8, D_MODEL), lambda i: (0, 0))
    return pl.pallas_call(
        body, name="out_loss", grid=(nt,),
        out_shape=(jax.ShapeDtypeStruct((s, D_MODEL), F32), jax.ShapeDtypeStruct((s, 2 * C_BR), ACT),
                   jax.ShapeDtypeStruct((2 * C_BR, D_MODEL), BF16),
                   jax.ShapeDtypeStruct((8, D_MODEL), F32), jax.ShapeDtypeStruct((8, D_MODEL), F32)),
        in_specs=[pl.BlockSpec((tm, D_MODEL), lambda i: (i, 0)),
                  pl.BlockSpec((tm, 2 * C_BR), lambda i: (i, 0)),
                  pl.BlockSpec((2 * C_BR, D_MODEL), lambda i: (0, 0), pipeline_mode=pl.Buffered(1)),
                  pl.BlockSpec((1, D_MODEL), lambda i: (0, 0)),
                  pl.BlockSpec((tm, D_MODEL), lambda i: (i, 0))],
        out_specs=(pl.BlockSpec((tm, D_MODEL), lambda i: (i, 0)),
                   pl.BlockSpec((tm, 2 * C_BR), lambda i: (i, 0)),
                   pl.BlockSpec((2 * C_BR, D_MODEL), lambda i: (0, 0), pipeline_mode=pl.Buffered(1)), part, part),
        scratch_shapes=[pltpu.VMEM((2 * C_BR, D_MODEL), F32)],
        compiler_params=_params(("arbitrary",)),
    )(x, y, wout, final_g, target)


def _conv_bwd(proj, cv, dy, conv_w3, ln_g, ln_b, dwout, tm):
    s = proj.shape[0]
    nt = s // tm
    wo_rows = dwout.shape[1]

    def body(av_ref, ag_ref, gc_p, gc_m, gc_n, c_p, c_m, c_n, dy_p, dy_m, dy_n, w_ref, lg_ref, lb_ref,
             dwout_ref, dp_ref, dwc_ref, dcb_ref, dlg_ref, dlb_ref, gwo_ref,
             dce_ref, glu_ref, dgl_ref, land_ref, send_sems, recv_sems, loc_sem):
        i = pl.program_id(0)
        px, py, pc = _place()
        me = 4 * px + 2 * py + pc

        def exchanges():
            out = []
            for rel in range(1, N_DEV):
                qx, qy, qc = px ^ (rel >> 2), py ^ ((rel >> 1) & 1), pc ^ (rel & 1)
                out.append(pltpu.make_async_remote_copy(
                    src_ref=dwout_ref.at[4 * qx + 2 * qy + qc], dst_ref=land_ref.at[me],
                    send_sem=send_sems.at[rel - 1], recv_sem=recv_sems.at[rel - 1],
                    device_id=(qx, qy, qc), device_id_type=MESH))
            return out

        own = pltpu.make_async_copy(dwout_ref.at[me], land_ref.at[me], loc_sem)

        @pl.when(i == 0)
        def _():
            dwc_ref[...] = jnp.zeros_like(dwc_ref)
            dcb_ref[...] = jnp.zeros_like(dcb_ref)
            dlg_ref[...] = jnp.zeros_like(dlg_ref)
            dlb_ref[...] = jnp.zeros_like(dlb_ref)
            own.start()
            for cp in exchanges():
                cp.start()

        def ext(p, m, n):
            return jnp.concatenate([p[...], m[...], n[...]], axis=0).astype(F32)

        main = slice(HALO, HALO + tm)
        cf, gc, dyc = ext(c_p, c_m, c_n), ext(gc_p, gc_m, gc_n), ext(dy_p, dy_m, dy_n)
        xhat, rstd = _ln_norm(cf)
        lg = lg_ref[...]
        ln = xhat * lg + lb_ref[...]
        s_ln, s_gc = _sigmoid(ln), _sigmoid(gc)
        dln = dyc * (gc * s_gc) * (s_ln * (1.0 + ln * (1.0 - s_ln)))
        dp_ref[:, 2 * C_BR:] = (dyc[main] * (ln[main] * s_ln[main])
                                * (s_gc[main] * (1.0 + gc[main] * (1.0 - s_gc[main])))).astype(ACT)
        dlg_ref[...] += _fold8(dln[main] * xhat[main])
        dlb_ref[...] += _fold8(dln[main])
        dc = _ln_bwd(dln, xhat, rstd, lg)
        dcb_ref[...] += _fold8(dc[main])

        def tiles(a):
            return a.reshape(a.shape[0], 8, 128)

        dce_ref[0:HALO] = tiles(jnp.where(i > 0, dc[0:HALO], 0.0))
        dce_ref[HALO:HALO + tm] = tiles(dc[main])
        dce_ref[HALO + tm:] = tiles(jnp.where(i < nt - 1, dc[HALO + tm:], 0.0))
        av = av_ref[...].astype(F32)
        sa = _sigmoid(ag_ref[...].astype(F32))
        glu_ref[...] = tiles(av * sa)

        nb = 16

        def step(t, carry):
            s0 = t * nb
            accs = [None] * nb
            for k in range(CONV_WIDTH):
                w = w_ref[k]
                prods = []
                for j in range(nb):
                    v = dce_ref[s0 + HALO + CONV_PAD + j - k]
                    term = w * v
                    accs[j] = term if accs[j] is None else accs[j] + term
                    prods.append(glu_ref[s0 + j] * v)
                while len(prods) > 1:
                    prods = [p + q for p, q in zip(prods[::2], prods[1::2])]
                dwc_ref[k] += prods[0]
            for j in range(nb):
                dgl_ref[s0 + j] = accs[j]
            return carry

        lax.fori_loop(0, tm // nb, step, 0)
        dglu = dgl_ref[...].reshape(tm, C_BR)
        dp_ref[:, 0:C_BR] = (dglu * sa).astype(ACT)
        dp_ref[:, C_BR:2 * C_BR] = (dglu * av * sa * (1.0 - sa)).astype(ACT)

        @pl.when(i == nt - 1)
        def _():
            copies = exchanges()
            own.wait()
            for cp in copies:
                cp.wait_recv()

            def step(t, carry):
                sl = pl.ds(pl.multiple_of(t * 64, 64), 64)
                g = land_ref[0, sl, :].astype(F32)
                for d in range(1, N_DEV):
                    g = g + land_ref[d, sl, :].astype(F32)
                gwo_ref[sl, :] = g
                return carry

            lax.fori_loop(0, wo_rows // 64, step, 0)
            for cp in copies:
                cp.wait_send()

    vec = pl.BlockSpec((1, C_BR), lambda i: (0, 0))
    part = pl.BlockSpec((8, C_BR), lambda i: (0, 0))
    return pl.pallas_call(
        body, name="conv_bwd", grid=(nt,),
        out_shape=(jax.ShapeDtypeStruct((s, D_IN), ACT), jax.ShapeDtypeStruct((CONV_ROWS, 8, 128), F32),
                   jax.ShapeDtypeStruct((8, C_BR), F32), jax.ShapeDtypeStruct((8, C_BR), F32),
                   jax.ShapeDtypeStruct((8, C_BR), F32), jax.ShapeDtypeStruct(dwout.shape[1:], F32)),
        in_specs=[pl.BlockSpec((tm, C_BR), lambda i: (i, 0)), pl.BlockSpec((tm, C_BR), lambda i: (i, 1))]
        + _halo_specs(tm, s, 2) + _halo_specs(tm, s, 0) + _halo_specs(tm, s, 0)
        + [pl.BlockSpec((CONV_ROWS, 8, 128), lambda i: (0, 0, 0)), vec, vec, pl.BlockSpec(memory_space=pl.ANY)],
        out_specs=(pl.BlockSpec((tm, 3 * C_BR), lambda i: (i, 0)),
                   pl.BlockSpec((CONV_ROWS, 8, 128), lambda i: (0, 0, 0)), part, part, part,
                   pl.BlockSpec(memory_space=pltpu.VMEM)),
        scratch_shapes=[pltpu.VMEM((tm + 2 * HALO, 8, 128), F32),
                        pltpu.VMEM((tm, 8, 128), F32), pltpu.VMEM((tm, 8, 128), F32),
                        pltpu.VMEM(dwout.shape, BF16),
                        pltpu.SemaphoreType.DMA((N_DEV - 1,)), pltpu.SemaphoreType.DMA((N_DEV - 1,)),
                        pltpu.SemaphoreType.DMA],
        compiler_params=_params(("arbitrary",)),
    )(proj, proj, proj, proj, proj, cv, cv, cv, dy, dy, dy, conv_w3, ln_g, ln_b, dwout)


def _sgu_bwd(proj, dy, dproj, ln_g, ln_b, ws, wst, bsb, tm):
    s = proj.shape[0]

    def body(u_ref, v_ref, gs_ref, dy_ref, dp_in, lg_ref, lb_ref, ws_ref, wst_ref, bsb_ref,
             dp_ref, dws_ref, dba_ref, dlg_ref, dlb_ref, dvn_ref):
        del dp_in
        i = pl.program_id(0)

        @pl.when(i == 0)
        def _():
            dws_ref[...] = jnp.zeros_like(dws_ref)
            dba_ref[...] = jnp.zeros_like(dba_ref)
            dlg_ref[...] = jnp.zeros_like(dlg_ref)
            dlb_ref[...] = jnp.zeros_like(dlb_ref)

        xhat, rstd = _ln_norm(v_ref[...].astype(F32))
        lg = lg_ref[...]
        vn = (xhat * lg + lb_ref[...]).astype(BF16)
        for cidx in range(tm // CHUNK):
            rows = slice(cidx * CHUNK, (cidx + 1) * CHUNK)
            for h in range(HEADS):
                cols = slice(h * HEAD_DIM, (h + 1) * HEAD_DIM)
                ocols = slice(C_BR + h * HEAD_DIM, C_BR + (h + 1) * HEAD_DIM)
                gcols = slice(2 * C_BR + h * HEAD_DIM, 2 * C_BR + (h + 1) * HEAD_DIM)
                vb = vn[rows, cols]
                mixed = jnp.dot(ws_ref[h], vb, preferred_element_type=F32) + bsb_ref[:, cols]
                gs = gs_ref[rows, cols].astype(F32)
                sg = _sigmoid(gs)
                u = u_ref[rows, cols].astype(F32)
                dyb = dy_ref[rows, cols].astype(F32)
                t = dyb * (gs * sg)
                dp_ref[rows, cols] = (t * mixed).astype(ACT)
                dp_ref[rows, gcols] = (dyb * u * mixed * (sg * (1.0 + gs * (1.0 - sg)))).astype(ACT)
                dm = t * u
                dmb = dm.astype(BF16)
                dvn_ref[rows, cols] = jnp.dot(wst_ref[h], dmb, preferred_element_type=F32)
                dws_ref[h] += lax.dot_general(dmb, vb, (((1,), (1,)), ((), ())), preferred_element_type=F32)
                dba_ref[:, cols] += dm
        dvn = dvn_ref[...]
        dlg_ref[...] += _fold8(dvn * xhat)
        dlb_ref[...] += _fold8(dvn)
        dp_ref[:, C_BR:2 * C_BR] = _ln_bwd(dvn, xhat, rstd, lg).astype(ACT)

    vec = pl.BlockSpec((1, C_BR), lambda i: (0, 0))
    part = pl.BlockSpec((8, C_BR), lambda i: (0, 0))
    wsp = pl.BlockSpec((HEADS, CHUNK, CHUNK), lambda i: (0, 0, 0))
    return pl.pallas_call(
        body, name="sgu_bwd", grid=(s // tm,),
        out_shape=(jax.ShapeDtypeStruct((s, D_IN), ACT), jax.ShapeDtypeStruct((HEADS, CHUNK, CHUNK), F32),
                   jax.ShapeDtypeStruct((CHUNK, C_BR), F32), jax.ShapeDtypeStruct((8, C_BR), F32),
                   jax.ShapeDtypeStruct((8, C_BR), F32)),
        in_specs=[pl.BlockSpec((tm, C_BR), lambda i: (i, 3)),
                  pl.BlockSpec((tm, C_BR), lambda i: (i, 4)),
                  pl.BlockSpec((tm, C_BR), lambda i: (i, 5)),
                  pl.BlockSpec((tm, C_BR), lambda i: (i, 1)),
                  pl.BlockSpec(memory_space=pl.ANY),
                  vec, vec, wsp, wsp, pl.BlockSpec((CHUNK, C_BR), lambda i: (0, 0))],
        out_specs=(pl.BlockSpec((tm, 3 * C_BR), lambda i: (i, 1)), wsp,
                   pl.BlockSpec((CHUNK, C_BR), lambda i: (0, 0)), part, part),
        scratch_shapes=[pltpu.VMEM((tm, C_BR), F32)],
        input_output_aliases={4: 0},
        compiler_params=_params(("arbitrary",)),
    )(proj, proj, proj, dy, dproj, ln_g, ln_b, ws, wst, bsb)


def _dx(dproj, win_all, x, norm_g, dx2, tm):
    s = x.shape[0]

    def body(dp_ref, w_ref, x_ref, g_ref, dx2_ref, gx_ref, dng_ref):
        i = pl.program_id(0)

        @pl.when(i == 0)
        def _():
            dng_ref[...] = jnp.zeros_like(dng_ref)

        dh = None
        for j in range(N_DEV):
            term = lax.dot_general(dp_ref[:, j * W_BLK:(j + 1) * W_BLK], w_ref[j],
                                   (((1,), (1,)), ((), ())), preferred_element_type=F32)
            dh = term if dh is None else dh + term
        xf = x_ref[...]
        r = lax.rsqrt(jnp.mean(xf * xf, axis=-1, keepdims=True) + EPS)
        n = xf * r
        dng_ref[...] += _fold8(dh * n)
        dn = dh * g_ref[...]
        gx_ref[...] = dx2_ref[...] + r * (dn - n * jnp.mean(dn * n, axis=-1, keepdims=True))

    return pl.pallas_call(
        body, name="dx", grid=(s // tm,),
        out_shape=(jax.ShapeDtypeStruct((s, D_MODEL), F32), jax.ShapeDtypeStruct((8, D_MODEL), F32)),
        in_specs=[pl.BlockSpec((tm, D_IN), lambda i: (i, 0)),
                  pl.BlockSpec((N_DEV, D_MODEL, W_BLK), lambda i: (0, 0, 0), pipeline_mode=pl.Buffered(1)),
                  pl.BlockSpec((tm, D_MODEL), lambda i: (i, 0)),
                  pl.BlockSpec((1, D_MODEL), lambda i: (0, 0)),
                  pl.BlockSpec((tm, D_MODEL), lambda i: (i, 0))],
        out_specs=(pl.BlockSpec((tm, D_MODEL), lambda i: (i, 0)), pl.BlockSpec((8, D_MODEL), lambda i: (0, 0))),
        compiler_params=_params(("arbitrary",)),
    )(dproj, win_all, x, norm_g, dx2)


def _dwin_comm(ht, dproj, order, parts, dwc, dba, dws, tk):
    s = ht.shape[1]
    nk = s // tk
    n_part = len(parts)

    def body(*refs):
        order_ref, ht_ref, dp_ref = refs[:3]
        del order_ref
        part_refs = refs[3:3 + n_part]
        dwc_ref, dba_ref, dws_ref = refs[3 + n_part:6 + n_part]
        gw_ref, red_ref, wsr_ref, cws_ref, loss_ref = refs[6 + n_part:11 + n_part]
        (acc_ref, all1_ref, all2_ref, out_s, land_s, out_x, land_x,
         send_sems, recv_sems, send_s, recv_s, send_x, recv_x) = refs[11 + n_part:]
        p, k = pl.program_id(0), pl.program_id(1)
        x, y, c = _place()
        me = 4 * x + 2 * y + c
        sibling = (x, y, 1 - c)
        chips = [(1 - x, 1 - y), (1 - x, y), (x, 1 - y)]
        last = k == nk - 1
        landed = k == min(1, nk - 1)

        def exchanges():
            out = []
            for rel in range(1, N_DEV):
                peer = (x ^ (rel >> 2), y ^ ((rel >> 1) & 1), c ^ (rel & 1))
                for a, buf in enumerate((all1_ref, all2_ref)):
                    out.append(pltpu.make_async_remote_copy(
                        src_ref=buf.at[me], dst_ref=buf.at[me],
                        send_sem=send_sems.at[a, rel - 1], recv_sem=recv_sems.at[a, rel - 1],
                        device_id=peer, device_id_type=MESH))
            return out

        def to_sibling(slot):
            return pltpu.make_async_remote_copy(
                src_ref=out_s.at[slot], dst_ref=land_s.at[slot],
                send_sem=send_s.at[slot], recv_sem=recv_s.at[slot], device_id=sibling, device_id_type=MESH)

        def to_chip(slot):
            return pltpu.make_async_remote_copy(
                src_ref=out_x.at[slot], dst_ref=land_x.at[slot],
                send_sem=send_x.at[slot], recv_sem=recv_x.at[slot],
                device_id=(*chips[slot], c), device_id_type=MESH)

        @pl.when((p == 0) & (k == 0))
        def _():
            all1_ref[me] = jnp.zeros((SMALL_ROWS, C_BR), F32)
            for row, p_ref in zip((ROW_NORM_G, ROW_CONV_B, ROW_CLN_G, ROW_CLN_B, ROW_SLN_G, ROW_SLN_B,
                                   ROW_FINAL_G, ROW_LOSS), part_refs):
                all1_ref[me, row:row + 1, :] = jnp.sum(p_ref[...], axis=0, keepdims=True)
            ones = jnp.ones((8, HEAD_DIM), F32)
            brow = [lax.dot_general(ones, dba_ref[:, h * HEAD_DIM:(h + 1) * HEAD_DIM], (((1,), (1,)), ((), ())),
                                    precision=lax.Precision.HIGHEST, preferred_element_type=F32)[0:1]
                    for h in range(HEADS)]
            all1_ref[me, ROW_B_S:ROW_B_S + 1, :] = jnp.concatenate(brow, axis=1)
            all1_ref[me, ROW_CONV_W:ROW_CONV_W + CONV_ROWS, :] = dwc_ref[...]
            all2_ref[me] = dws_ref[...]
            for cp in exchanges():
                cp.start()

        @pl.when(k == 0)
        def _():
            acc_ref[...] = jnp.zeros_like(acc_ref)

        for core in range(2):
            acc_ref[core] += jnp.dot(ht_ref[...], dp_ref[:, core * W_BLK:(core + 1) * W_BLK],
                                     preferred_element_type=F32)

        for slot in range(4):
            @pl.when((p == slot) & last)
            def _(slot=slot):
                out_s[slot] = acc_ref[1 - c].astype(BF16)
                to_sibling(slot).start()
                if slot < 3:
                    out_x[slot] = acc_ref[c].astype(BF16)

        for slot in range(3):
            @pl.when((p == slot + 1) & landed)
            def _(slot=slot):
                to_sibling(slot).wait_recv()
                out_x[slot] = (out_x[slot].astype(F32) + land_s[slot].astype(F32)).astype(BF16)
                to_chip(slot).start()

        @pl.when((p == 3) & last)
        def _():
            to_sibling(3).wait_recv()
            total = acc_ref[c] + land_s[3].astype(F32)
            for slot in range(3):
                to_chip(slot).wait_recv()
                total = total + land_x[slot].astype(F32)
            gw_ref[...] = total

            copies = exchanges()
            for cp in copies:
                cp.wait_recv()
            tot = all1_ref[0]
            for d in range(1, N_DEV):
                tot = tot + all1_ref[d]
            red_ref[...] = tot
            loss_ref[...] = jnp.broadcast_to(
                jnp.sum(tot[ROW_LOSS:ROW_LOSS + 1, :], axis=1, keepdims=True) * (0.5 / D_MODEL), loss_ref.shape)
            shard = jnp.zeros(cws_ref.shape, F32)
            for d in range(N_DEV):
                shard = jnp.where(me == d, tot[ROW_CONV_W:ROW_CONV_W + CONV_ROWS, d * 128:(d + 1) * 128], shard)
            cws_ref[...] = shard
            tot2 = all2_ref[0]
            for d in range(1, N_DEV):
                tot2 = tot2 + all2_ref[d]
            wsr_ref[...] = tot2
            for cp in copies + [to_sibling(slot) for slot in range(4)] + [to_chip(slot) for slot in range(3)]:
                cp.wait_send()

    vm = pl.BlockSpec(memory_space=pltpu.VMEM)
    blk = (D_MODEL, W_BLK)
    return pl.pallas_call(
        body, name="dwin",
        grid_spec=pltpu.PrefetchScalarGridSpec(
            num_scalar_prefetch=1, grid=(N_DEV // 2, nk),
            in_specs=[pl.BlockSpec((D_MODEL, tk), lambda p, k, o: (0, k)),
                      pl.BlockSpec((tk, 2 * W_BLK), lambda p, k, o: (k, o[p]))] + [vm] * (n_part + 3),
            out_specs=(vm, vm, vm, vm, vm),
            scratch_shapes=[pltpu.VMEM((2,) + blk, F32),
                            pltpu.VMEM((N_DEV, SMALL_ROWS, C_BR), F32), pltpu.VMEM((N_DEV,) + dws.shape, F32),
                            pltpu.VMEM((4,) + blk, BF16), pltpu.VMEM((4,) + blk, BF16),
                            pltpu.VMEM((3,) + blk, BF16), pltpu.VMEM((3,) + blk, BF16),
                            pltpu.SemaphoreType.DMA((2, N_DEV - 1)), pltpu.SemaphoreType.DMA((2, N_DEV - 1)),
                            pltpu.SemaphoreType.DMA((4,)), pltpu.SemaphoreType.DMA((4,)),
                            pltpu.SemaphoreType.DMA((3,)), pltpu.SemaphoreType.DMA((3,))]),
        out_shape=(jax.ShapeDtypeStruct(blk, F32),
                   jax.ShapeDtypeStruct((SMALL_ROWS, C_BR), F32), jax.ShapeDtypeStruct(dws.shape, F32),
                   jax.ShapeDtypeStruct((CONV_ROWS, 128), F32), jax.ShapeDtypeStruct((8, 128), F32)),
        compiler_params=_params(("arbitrary", "arbitrary")),
    )(order, ht, dproj, *parts, dwc, dba, dws)


def _adamw_math(w, g, m, v):
    m = ADAM_B1 * m + (1.0 - ADAM_B1) * g
    v = ADAM_B2 * v + (1.0 - ADAM_B2) * (g * g)
    m_hat = m / (1.0 - ADAM_B1 ** ADAM_STEP)
    v_hat = v / (1.0 - ADAM_B2 ** ADAM_STEP)
    delta = -ADAM_LR * (m_hat / (jnp.sqrt(v_hat) + ADAM_EPS) + ADAM_WD * w)
    return delta, m, v


def _adamw_all(groups, g_conv_w, conv_wmv, red, rows, ws, ms, vs):
    ng, n = len(groups), len(rows)
    rchunk = 64

    def body(*refs):
        big = refs[:4 * ng]
        gcw_ref, cw_ref, cm_ref, cv_ref, red_ref = refs[4 * ng:4 * ng + 5]
        small = refs[4 * ng + 5:4 * ng + 5 + 3 * n]
        w_refs, m_refs, v_refs = small[:n], small[n:2 * n], small[2 * n:]
        outs = refs[4 * ng + 5 + 3 * n:]
        for t in range(ng):
            g_ref, w_ref, m_ref, v_ref = big[4 * t:4 * t + 4]
            d_ref, mo_ref, vo_ref = outs[3 * t:3 * t + 3]
            lead = w_ref.shape[0]
            chunks = lead // rchunk if len(w_ref.shape) == 2 else lead

            def step(c, carry, refs_=(g_ref, w_ref, m_ref, v_ref, d_ref, mo_ref, vo_ref), two_d=len(w_ref.shape) == 2):
                sl = pl.ds(pl.multiple_of(c * rchunk, rchunk), rchunk) if two_d else c
                g_, w_, m_, v_, d_, mo_, vo_ = refs_
                d_[sl], mo_[sl], vo_[sl] = _adamw_math(w_[sl], g_[sl], m_[sl], v_[sl])
                return carry

            lax.fori_loop(0, chunks, step, 0)
        g = gcw_ref[0:CONV_WIDTH, :]
        o = 3 * ng
        outs[o][...] = g
        outs[o + 1][...], outs[o + 2][...], outs[o + 3][...] = _adamw_math(cw_ref[...], g, cm_ref[...], cv_ref[...])
        o += 4
        for t, row in enumerate(rows):
            g = red_ref[row:row + 1, :]
            delta, m, v = _adamw_math(w_refs[t][...], g, m_refs[t][...], v_refs[t][...])
            outs[o + t][...] = g
            outs[o + n + t][...] = delta
            outs[o + 2 * n + t][...] = m
            outs[o + 3 * n + t][...] = v

    vm = pl.BlockSpec(memory_space=pltpu.VMEM)
    sds = lambda a: jax.ShapeDtypeStruct(a.shape, F32)
    out_shape = []
    for grp in groups:
        out_shape += [sds(grp[1])] * 3
    out_shape += [sds(conv_wmv[0])] * 4
    out_shape += [jax.ShapeDtypeStruct((1, C_BR), F32)] * (4 * n)
    args = [a for grp in groups for a in grp] + [g_conv_w, *conv_wmv, red, *ws, *ms, *vs]
    res = pl.pallas_call(
        body, name="adamw", out_shape=tuple(out_shape),
        in_specs=[vm] * len(args), out_specs=(vm,) * len(out_shape),
        compiler_params=_params(),
    )(*args)
    big_out = [res[3 * t:3 * t + 3] for t in range(ng)]
    o = 3 * ng
    conv_out = res[o:o + 4]
    o += 4
    return big_out, conv_out, (res[o:o + n], res[o + n:o + 2 * n], res[o + 2 * n:o + 3 * n], res[o + 3 * n:o + 4 * n])


def kernel(x, norm_g, w_in, conv_w, conv_b, conv_ln_g, conv_ln_b, sgu_ln_g, sgu_ln_b, w_s, b_s, w_out, final_g, loss_target, m_norm_g, m_w_in, m_conv_w, m_conv_b, m_conv_ln_g, m_conv_ln_b, m_sgu_ln_g, m_sgu_ln_b, m_w_s, m_b_s, m_w_out, m_final_g, v_norm_g, v_w_in, v_conv_w, v_conv_b, v_conv_ln_g, v_conv_ln_b, v_sgu_ln_g, v_sgu_ln_b, v_w_s, v_b_s, v_w_out, v_final_g):
    s = x.shape[1]
    xs = x.reshape(s, D_MODEL)
    tgt = loss_target.reshape(s, D_MODEL)
    tm = min(256, s)

    cw_pad = jnp.pad(conv_w[0], ((0, CONV_ROWS - CONV_WIDTH), (0, 0)))
    px, py, pc = _place()
    blocks = [(px, py, pc), (px, py, 1 - pc)]
    blocks += [(*chip, core) for chip in ((1 - px, py), (px, 1 - py), (1 - px, 1 - py)) for core in (pc, 1 - pc)]
    order = jnp.stack([4 * bx + 2 * by + bc for bx, by, bc in blocks]).astype(jnp.int32)
    proj, ht, win_all, wout_all, cw_all = _proj_ag(xs, norm_g, w_in[0], w_out[0], cw_pad, order, min(1024, s))
    wout_full = wout_all.reshape(2 * C_BR, D_MODEL)
    cw_tiles = jnp.transpose(cw_all, (1, 0, 2))

    ws = w_s[0].astype(BF16)
    wst = jnp.transpose(w_s[0], (0, 2, 1)).astype(BF16)
    bsb = jnp.repeat(jnp.transpose(b_s[0]), HEAD_DIM, axis=1)
    fg = final_g.reshape(1, D_MODEL)

    big = min(512, s)
    y, cv = _conv_fwd(proj, cw_tiles, conv_b, conv_ln_g, conv_ln_b, big)
    y = _sgu_fwd(proj, y, sgu_ln_g, sgu_ln_b, ws, bsb, big)
    dx2, dy, dwout, loss_p, dfg_p = _out_loss(xs, y, wout_full, fg, tgt, min(512, s))
    dproj, dwc, dcb_p, dclg_p, dclb_p, g_w_out = _conv_bwd(
        proj, cv, dy, cw_tiles, conv_ln_g, conv_ln_b, dwout.reshape(N_DEV, 2 * C_BR // N_DEV, D_MODEL), big)
    dwc = dwc.reshape(CONV_ROWS, C_BR)
    dproj, dws, dba, dslg_p, dslb_p = _sgu_bwd(proj, dy, dproj, sgu_ln_g, sgu_ln_b, ws, wst, bsb, big)
    grad_x, dng_p = _dx(dproj, win_all, xs, norm_g, dx2, big)
    rs_order = jnp.stack([2 * cx + cy for cx, cy in ((1 - px, 1 - py), (1 - px, py), (px, 1 - py), (px, py))])
    rs_order = rs_order.astype(jnp.int32)
    g_w_in, red, g_w_s, g_cw, loss8 = _dwin_comm(
        ht, dproj, rs_order, [dng_p, dcb_p, dclg_p, dclb_p, dslg_p, dslb_p, dfg_p, loss_p], dwc, dba,
        dws, min(512, s))
    loss = loss8[0, 0]

    row = lambda a: a.reshape(1, C_BR)
    rows = (ROW_NORM_G, ROW_CONV_B, ROW_CLN_G, ROW_CLN_B, ROW_SLN_G, ROW_SLN_B, ROW_B_S, ROW_FINAL_G)
    big_out, (g_cw, d_cw, nm_cw, nv_cw), (g_r, d_r, m_r, v_r) = _adamw_all(
        [(g_w_in, w_in[0], m_w_in[0], v_w_in[0]), (g_w_out, w_out[0], m_w_out[0], v_w_out[0]),
         (g_w_s, w_s[0], m_w_s[0], v_w_s[0])],
        g_cw, (conv_w[0], m_conv_w[0], v_conv_w[0]), red, rows,
        [norm_g, conv_b, conv_ln_g, conv_ln_b, sgu_ln_g, sgu_ln_b, row(b_s), row(final_g)],
        [m_norm_g, m_conv_b, m_conv_ln_g, m_conv_ln_b, m_sgu_ln_g, m_sgu_ln_b, row(m_b_s), row(m_final_g)],
        [v_norm_g, v_conv_b, v_conv_ln_g, v_conv_ln_b, v_sgu_ln_g, v_sgu_ln_b, row(v_b_s), row(v_final_g)])
    (d_w_in, nm_w_in, nv_w_in), (d_w_out, nm_w_out, nv_w_out), (d_ws, nm_ws, nv_ws) = big_out

    def leaves(r, w_in_l, cw_l, ws_l, w_out_l):
        return (r[0], w_in_l[None], cw_l[None], r[1], r[2], r[3], r[4], r[5],
                ws_l[None], r[6].reshape(1, HEADS, CHUNK), w_out_l[None],
                r[7].reshape(D_MODEL))

    return (loss, grad_x.reshape(1, s, D_MODEL),
            *leaves(g_r, g_w_in, g_cw, g_w_s, g_w_out),
            *leaves(d_r, d_w_in, d_cw, d_ws, d_w_out),
            *leaves(m_r, nm_w_in, nm_cw, nm_ws, nm_w_out),
            *leaves(v_r, nv_w_in, nv_cw, nv_ws, nv_w_out))
```

```python
import functools

import jax
import jax.numpy as jnp
from jax import lax
from jax.experimental import pallas as pl
from jax.experimental.pallas import tpu as pltpu

F32 = jnp.float32
BF16 = jnp.bfloat16
ACT = jnp.bfloat16

D_MODEL = 1024
C_BR = 1024
D_IN = 6 * C_BR
N_DEV = 8
W_BLK = D_IN // N_DEV
HEADS = 8
HEAD_DIM = 128
CHUNK = 128
CONV_WIDTH = 31
CONV_PAD = CONV_WIDTH // 2
HALO = 16
CONV_ROWS = 32
EPS = 1e-6

ADAM_LR = 0.001
ADAM_B1 = 0.9
ADAM_B2 = 0.999
ADAM_EPS = 1e-08
ADAM_WD = 0.01
ADAM_STEP = 10

VMEM_LIMIT = 56 * 1024 * 1024
MESH = pl.DeviceIdType.MESH

ROW_NORM_G, ROW_CONV_B, ROW_CLN_G, ROW_CLN_B, ROW_SLN_G, ROW_SLN_B, ROW_FINAL_G, ROW_B_S, ROW_LOSS = range(9)
ROW_CONV_W = 16
SMALL_ROWS = ROW_CONV_W + CONV_ROWS


def _params(sem=None, **kw):
    return pltpu.CompilerParams(dimension_semantics=sem, vmem_limit_bytes=VMEM_LIMIT, **kw)


def _fold8(a):
    r, n = a.shape
    return a.reshape(r // 8, 8, n).sum(axis=0)


def _sigmoid(z):
    return 0.5 * jnp.tanh(0.5 * z) + 0.5


def _ln_norm(xf):
    mu = jnp.mean(xf, axis=-1, keepdims=True)
    xc = xf - mu
    var = jnp.mean(xc * xc, axis=-1, keepdims=True)
    rstd = lax.rsqrt(var + EPS)
    return xc * rstd, rstd


def _ln_bwd(dy, xhat, rstd, g):
    dxhat = dy * g
    m1 = jnp.mean(dxhat, axis=-1, keepdims=True)
    m2 = jnp.mean(dxhat * xhat, axis=-1, keepdims=True)
    return rstd * (dxhat - m1 - xhat * m2)


def _place():
    return lax.axis_index("x"), lax.axis_index("y"), lax.axis_index("c")


def _proj_ag(x, norm_g, w_in, w_out, conv_w, order, tm):
    s = x.shape[0]
    nt = s // tm

    def body(order_ref, x_ref, g_ref, win_ref, wout_ref, cw_ref,
             proj_ref, ht_ref, win_all, wout_all, cw_all,
             h_ref, win_buf, wout_buf, cw_buf, send_sems, recv_sems, save_sems):
        jj, i = pl.program_id(0), pl.program_id(1)
        x_, y_, c_ = _place()
        me, sibling = (x_, y_, c_), (x_, y_, 1 - c_)
        chips = [(1 - x_, y_), (x_, 1 - y_), (1 - x_, 1 - y_)]
        bufs = (win_buf, wout_buf, cw_buf)
        outs = (win_all, wout_all, cw_all)
        start = i == 0

        def index(px, py, pc):
            return 4 * px + 2 * py + pc

        def copy(a, k, block, to):
            return pltpu.make_async_remote_copy(
                src_ref=bufs[a].at[index(*block)], dst_ref=bufs[a].at[index(*block)],
                send_sem=send_sems.at[a, k], recv_sem=recv_sems.at[a, k],
                device_id=to, device_id_type=MESH)

        def save(a, slot, block):
            return pltpu.make_async_copy(bufs[a].at[index(*block)], outs[a].at[index(*block)], save_sems.at[a, slot])

        def first(a):
            return [copy(a, 0, me, sibling)] + [copy(a, 1 + j, me, (*chip, c_)) for j, chip in enumerate(chips)]

        def saves(a):
            blocks = [me, sibling] + [(*chip, core) for chip in chips for core in (c_, 1 - c_)]
            return [save(a, slot, block) for slot, block in enumerate(blocks)]

        @pl.when((jj == 0) & start)
        def _():
            win_buf[index(*me)] = win_ref[...].astype(BF16)
            wout_buf[index(*me)] = wout_ref[...].astype(BF16)
            cw_buf[index(*me)] = cw_ref[...]
            for a in range(3):
                for cp in first(a):
                    cp.start()
                saves(a)[0].start()

        @pl.when((jj == 1) & start)
        def _():
            copy(0, 0, sibling, me).wait_recv()
            saves(0)[1].start()

        for j, chip in enumerate(chips):
            @pl.when((jj == 2 + 2 * j) & start)
            def _(j=j, chip=chip):
                copy(0, 1 + j, (*chip, c_), me).wait_recv()
                copy(0, 4 + j, (*chip, c_), sibling).start()
                saves(0)[2 + 2 * j].start()

            @pl.when((jj == 3 + 2 * j) & start)
            def _(j=j, chip=chip):
                copy(0, 4 + j, (*chip, 1 - c_), me).wait_recv()
                saves(0)[3 + 2 * j].start()

        @pl.when(jj == 0)
        def _():
            xf = x_ref[...]
            r = lax.rsqrt(jnp.mean(xf * xf, axis=-1, keepdims=True) + EPS)
            hf = xf * r * g_ref[...]
            h_ref[i] = hf.astype(BF16)
            ht_ref[...] = hf.T.astype(BF16)

        proj_ref[...] = jnp.dot(h_ref[i], win_buf[order_ref[jj]], preferred_element_type=F32).astype(ACT)

        @pl.when((jj == N_DEV - 1) & start)
        def _():
            for a in (1, 2):
                for j, chip in enumerate(chips):
                    copy(a, 1 + j, (*chip, c_), me).wait_recv()
                    copy(a, 4 + j, (*chip, c_), sibling).start()
                    saves(a)[2 + 2 * j].start()

        @pl.when((jj == N_DEV - 1) & (i == nt - 1))
        def _():
            passed = [copy(a, 4 + j, (*chip, c_), sibling) for a in range(3) for j, chip in enumerate(chips)]
            for a in (1, 2):
                copy(a, 0, sibling, me).wait_recv()
                saves(a)[1].start()
                for j, chip in enumerate(chips):
                    copy(a, 4 + j, (*chip, 1 - c_), me).wait_recv()
                    saves(a)[3 + 2 * j].start()
            for cp in saves(0) + saves(1) + saves(2):
                cp.wait()
            for cp in first(0) + first(1) + first(2) + passed:
                cp.wait_send()

    vm = pl.BlockSpec(memory_space=pltpu.VMEM)
    hbm = pl.BlockSpec(memory_space=pl.ANY)
    once = lambda jj, i: jnp.where(jj == 0, i, nt - 1)
    stacked = [(N_DEV,) + w.shape for w in (w_in, w_out, conv_w)]
    return pl.pallas_call(
        body, name="proj_ag",
        grid_spec=pltpu.PrefetchScalarGridSpec(
            num_scalar_prefetch=1, grid=(N_DEV, nt),
            in_specs=[pl.BlockSpec((tm, D_MODEL), lambda jj, i, o: (once(jj, i), 0)),
                      pl.BlockSpec((1, D_MODEL), lambda jj, i, o: (0, 0)), vm, vm, vm],
            out_specs=(pl.BlockSpec((tm, W_BLK), lambda jj, i, o: (i, o[jj])),
                       pl.BlockSpec((D_MODEL, tm), lambda jj, i, o: (0, once(jj, i))), hbm, hbm, hbm),
            scratch_shapes=[pltpu.VMEM((nt, tm, D_MODEL), BF16),
                            pltpu.VMEM(stacked[0], BF16), pltpu.VMEM(stacked[1], BF16), pltpu.VMEM(stacked[2], F32),
                            pltpu.SemaphoreType.DMA((3, 7)), pltpu.SemaphoreType.DMA((3, 7)),
                            pltpu.SemaphoreType.DMA((3, N_DEV))]),
        out_shape=(jax.ShapeDtypeStruct((s, D_IN), ACT), jax.ShapeDtypeStruct((D_MODEL, s), BF16),
                   jax.ShapeDtypeStruct(stacked[0], BF16), jax.ShapeDtypeStruct(stacked[1], BF16),
                   jax.ShapeDtypeStruct(stacked[2], F32)),
        compiler_params=_params(("arbitrary", "arbitrary")),
    )(order, x, norm_g, w_in, w_out, conv_w)


def _halo_specs(tm, s, col):
    per = tm // HALO
    last = s // HALO - 1
    return [pl.BlockSpec((HALO, C_BR), lambda i: (jnp.maximum(i * per - 1, 0), col)),
            pl.BlockSpec((tm, C_BR), lambda i: (i, col)),
            pl.BlockSpec((HALO, C_BR), lambda i: (jnp.minimum((i + 1) * per, last), col))]


def _conv_fwd(proj, conv_w3, conv_b, ln_g, ln_b, tm):
    s = proj.shape[0]
    nt = s // tm

    def body(av_p, av_m, av_n, ag_p, ag_m, ag_n, gc_ref, w_ref, cb_ref, lg_ref, lb_ref,
             y_ref, c_ref, ext_ref, cv_ref):
        i = pl.program_id(0)

        def glu(a_ref, g_ref):
            return a_ref[...].astype(F32) * _sigmoid(g_ref[...].astype(F32))

        def tiles(a):
            return a.reshape(a.shape[0], 8, 128)

        ext_ref[0:HALO] = tiles(jnp.where(i > 0, glu(av_p, ag_p), 0.0))
        ext_ref[HALO:HALO + tm] = tiles(glu(av_m, ag_m))
        ext_ref[HALO + tm:] = tiles(jnp.where(i < nt - 1, glu(av_n, ag_n), 0.0))

        nb = 32

        def step(t, carry):
            s0 = t * nb
            accs = [None] * nb
            for k in range(CONV_WIDTH):
                w = w_ref[k]
                for j in range(nb):
                    term = w * ext_ref[s0 + HALO - CONV_PAD + j + k]
                    accs[j] = term if accs[j] is None else accs[j] + term
            for j in range(nb):
                cv_ref[s0 + j] = accs[j]
            return carry

        lax.fori_loop(0, tm // nb, step, 0)
        cv = cv_ref[...].reshape(tm, C_BR) + cb_ref[...]
        c_ref[...] = cv.astype(ACT)
        xhat, _ = _ln_norm(cv)
        ln = xhat * lg_ref[...] + lb_ref[...]
        gc = gc_ref[...].astype(F32)
        y_ref[...] = (ln * _sigmoid(ln) * (gc * _sigmoid(gc))).astype(ACT)

    vec = pl.BlockSpec((1, C_BR), lambda i: (0, 0))
    return pl.pallas_call(
        body, name="conv_fwd", grid=(nt,),
        out_shape=(jax.ShapeDtypeStruct((s, 2 * C_BR), ACT), jax.ShapeDtypeStruct((s, C_BR), ACT)),
        in_specs=_halo_specs(tm, s, 0) + _halo_specs(tm, s, 1)
        + [pl.BlockSpec((tm, C_BR), lambda i: (i, 2)),
           pl.BlockSpec((CONV_ROWS, 8, 128), lambda i: (0, 0, 0)), vec, vec, vec],
        out_specs=(pl.BlockSpec((tm, C_BR), lambda i: (i, 0)), pl.BlockSpec((tm, C_BR), lambda i: (i, 0))),
        scratch_shapes=[pltpu.VMEM((tm + 2 * HALO, 8, 128), F32),
                        pltpu.VMEM((tm, 8, 128), F32)],
        compiler_params=_params(("parallel",)),
    )(proj, proj, proj, proj, proj, proj, proj, conv_w3, conv_b, ln_g, ln_b)


def _sgu_fwd(proj, y, ln_g, ln_b, ws, bsb, tm):
    s = proj.shape[0]

    def body(u_ref, v_ref, gs_ref, y_in, lg_ref, lb_ref, ws_ref, bsb_ref, y_ref):
        del y_in
        xhat, _ = _ln_norm(v_ref[...].astype(F32))
        vn = (xhat * lg_ref[...] + lb_ref[...]).astype(BF16)
        for cidx in range(tm // CHUNK):
            rows = slice(cidx * CHUNK, (cidx + 1) * CHUNK)
            for h in range(HEADS):
                cols = slice(h * HEAD_DIM, (h + 1) * HEAD_DIM)
                mixed = jnp.dot(ws_ref[h], vn[rows, cols], preferred_element_type=F32) + bsb_ref[:, cols]
                gs = gs_ref[rows, cols].astype(F32)
                y_ref[rows, cols] = (u_ref[rows, cols].astype(F32) * mixed * (gs * _sigmoid(gs))).astype(ACT)

    vec = pl.BlockSpec((1, C_BR), lambda i: (0, 0))
    return pl.pallas_call(
        body, name="sgu_fwd", grid=(s // tm,),
        out_shape=jax.ShapeDtypeStruct((s, 2 * C_BR), ACT),
        in_specs=[pl.BlockSpec((tm, C_BR), lambda i: (i, 3)),
                  pl.BlockSpec((tm, C_BR), lambda i: (i, 4)),
                  pl.BlockSpec((tm, C_BR), lambda i: (i, 5)),
                  pl.BlockSpec(memory_space=pl.ANY),
                  vec, vec,
                  pl.BlockSpec((HEADS, CHUNK, CHUNK), lambda i: (0, 0, 0)),
                  pl.BlockSpec((CHUNK, C_BR), lambda i: (0, 0))],
        out_specs=pl.BlockSpec((tm, C_BR), lambda i: (i, 1)),
        input_output_aliases={3: 0},
        compiler_params=_params(("parallel",)),
    )(proj, proj, proj, y, ln_g, ln_b, ws, bsb)


def _out_loss(x, y, wout, final_g, target, tm):
    s = x.shape[0]
    nt = s // tm
    inv_d = 1.0 / D_MODEL

    def body(x_ref, y_ref, w_ref, g_ref, t_ref, dx2_ref, dy_ref, dw_ref, loss_ref, dfg_ref, acc_ref):
        i = pl.program_id(0)

        @pl.when(i == 0)
        def _():
            acc_ref[...] = jnp.zeros_like(acc_ref)
            loss_ref[...] = jnp.zeros_like(loss_ref)
            dfg_ref[...] = jnp.zeros_like(dfg_ref)

        yb = y_ref[...]
        x2 = x_ref[...] + jnp.dot(yb, w_ref[...], preferred_element_type=F32)
        r2 = lax.rsqrt(jnp.mean(x2 * x2, axis=-1, keepdims=True) + EPS)
        n = x2 * r2
        g = g_ref[...]
        e = n * g - t_ref[...]
        loss_ref[...] += _fold8(e * e)
        dout = e * inv_d
        dfg_ref[...] += _fold8(dout * n)
        dn = dout * g
        dx2 = r2 * (dn - n * jnp.mean(dn * n, axis=-1, keepdims=True))
        dx2_ref[...] = dx2
        dxb = dx2.astype(BF16)
        dy_ref[...] = lax.dot_general(dxb, w_ref[...], (((1,), (1,)), ((), ())),
                                      preferred_element_type=F32).astype(ACT)
        acc_ref[...] += lax.dot_general(yb, dxb, (((0,), (0,)), ((), ())), preferred_element_type=F32)

        @pl.when(i == nt - 1)
        def _():
            dw_ref[...] = acc_ref[...].astype(BF16)

    part = pl.BlockSpec((8, D_MODEL), lambda i: (0, 0))
    return pl.pallas_call(
        body, name="out_loss", grid=(nt,),
        out_shape=(jax.ShapeDtypeStruct((s, D_MODEL), F32), jax.ShapeDtypeStruct((s, 2 * C_BR), ACT),
                   jax.ShapeDtypeStruct((2 * C_BR, D_MODEL), BF16),
                   jax.ShapeDtypeStruct((8, D_MODEL), F32), jax.ShapeDtypeStruct((8, D_MODEL), F32)),
        in_specs=[pl.BlockSpec((tm, D_MODEL), lambda i: (i, 0)),
                  pl.BlockSpec((tm, 2 * C_BR), lambda i: (i, 0)),
                  pl.BlockSpec((2 * C_BR, D_MODEL), lambda i: (0, 0), pipeline_mode=pl.Buffered(1)),
                  pl.BlockSpec((1, D_MODEL), lambda i: (0, 0)),
                  pl.BlockSpec((tm, D_MODEL), lambda i: (i, 0))],
        out_specs=(pl.BlockSpec((tm, D_MODEL), lambda i: (i, 0)),
                   pl.BlockSpec((tm, 2 * C_BR), lambda i: (i, 0)),
                   pl.BlockSpec((2 * C_BR, D_MODEL), lambda i: (0, 0), pipeline_mode=pl.Buffered(1)), part, part),
        scratch_shapes=[pltpu.VMEM((2 * C_BR, D_MODEL), F32)],
        compiler_params=_params(("arbitrary",)),
    )(x, y, wout, final_g, target)


def _conv_bwd(proj, cv, dy, conv_w3, ln_g, ln_b, dwout, tm):
    s = proj.shape[0]
    nt = s // tm
    wo_rows = dwout.shape[1]

    def body(av_ref, ag_ref, gc_p, gc_m, gc_n, c_p, c_m, c_n, dy_p, dy_m, dy_n, w_ref, lg_ref, lb_ref,
             dwout_ref, dp_ref, dwc_ref, dcb_ref, dlg_ref, dlb_ref, gwo_ref,
             dce_ref, glu_ref, dgl_ref, land_ref, send_sems, recv_sems, loc_sem):
        i = pl.program_id(0)
        px, py, pc = _place()
        me = 4 * px + 2 * py + pc

        def exchanges():
            out = []
            for rel in range(1, N_DEV):
                qx, qy, qc = px ^ (rel >> 2), py ^ ((rel >> 1) & 1), pc ^ (rel & 1)
                out.append(pltpu.make_async_remote_copy(
                    src_ref=dwout_ref.at[4 * qx + 2 * qy + qc], dst_ref=land_ref.at[me],
                    send_sem=send_sems.at[rel - 1], recv_sem=recv_sems.at[rel - 1],
                    device_id=(qx, qy, qc), device_id_type=MESH))
            return out

        own = pltpu.make_async_copy(dwout_ref.at[me], land_ref.at[me], loc_sem)

        @pl.when(i == 0)
        def _():
            dwc_ref[...] = jnp.zeros_like(dwc_ref)
            dcb_ref[...] = jnp.zeros_like(dcb_ref)
            dlg_ref[...] = jnp.zeros_like(dlg_ref)
            dlb_ref[...] = jnp.zeros_like(dlb_ref)
            own.start()
            for cp in exchanges():
                cp.start()

        def ext(p, m, n):
            return jnp.concatenate([p[...], m[...], n[...]], axis=0).astype(F32)

        main = slice(HALO, HALO + tm)
        cf, gc, dyc = ext(c_p, c_m, c_n), ext(gc_p, gc_m, gc_n), ext(dy_p, dy_m, dy_n)
        xhat, rstd = _ln_norm(cf)
        lg = lg_ref[...]
        ln = xhat * lg + lb_ref[...]
        s_ln, s_gc = _sigmoid(ln), _sigmoid(gc)
        dln = dyc * (gc * s_gc) * (s_ln * (1.0 + ln * (1.0 - s_ln)))
        dp_ref[:, 2 * C_BR:] = (dyc[main] * (ln[main] * s_ln[main])
                                * (s_gc[main] * (1.0 + gc[main] * (1.0 - s_gc[main])))).astype(ACT)
        dlg_ref[...] += _fold8(dln[main] * xhat[main])
        dlb_ref[...] += _fold8(dln[main])
        dc = _ln_bwd(dln, xhat, rstd, lg)
        dcb_ref[...] += _fold8(dc[main])

        def tiles(a):
            return a.reshape(a.shape[0], 8, 128)

        dce_ref[0:HALO] = tiles(jnp.where(i > 0, dc[0:HALO], 0.0))
        dce_ref[HALO:HALO + tm] = tiles(dc[main])
        dce_ref[HALO + tm:] = tiles(jnp.where(i < nt - 1, dc[HALO + tm:], 0.0))
        av = av_ref[...].astype(F32)
        sa = _sigmoid(ag_ref[...].astype(F32))
        glu_ref[...] = tiles(av * sa)

        nb = 16

        def step(t, carry):
            s0 = t * nb
            accs = [None] * nb
            for k in range(CONV_WIDTH):
                w = w_ref[k]
                prods = []
                for j in range(nb):
                    v = dce_ref[s0 + HALO + CONV_PAD + j - k]
                    term = w * v
                    accs[j] = term if accs[j] is None else accs[j] + term
                    prods.append(glu_ref[s0 + j] * v)
                while len(prods) > 1:
                    prods = [p + q for p, q in zip(prods[::2], prods[1::2])]
                dwc_ref[k] += prods[0]
            for j in range(nb):
                dgl_ref[s0 + j] = accs[j]
            return carry

        lax.fori_loop(0, tm // nb, step, 0)
        dglu = dgl_ref[...].reshape(tm, C_BR)
        dp_ref[:, 0:C_BR] = (dglu * sa).astype(ACT)
        dp_ref[:, C_BR:2 * C_BR] = (dglu * av * sa * (1.0 - sa)).astype(ACT)

        @pl.when(i == nt - 1)
        def _():
            copies = exchanges()
            own.wait()
            for cp in copies:
                cp.wait_recv()

            def step(t, carry):
                sl = pl.ds(pl.multiple_of(t * 64, 64), 64)
                g = land_ref[0, sl, :].astype(F32)
                for d in range(1, N_DEV):
                    g = g + land_ref[d, sl, :].astype(F32)
                gwo_ref[sl, :] = g
                return carry

            lax.fori_loop(0, wo_rows // 64, step, 0)
            for cp in copies:
                cp.wait_send()

    vec = pl.BlockSpec((1, C_BR), lambda i: (0, 0))
    part = pl.BlockSpec((8, C_BR), lambda i: (0, 0))
    return pl.pallas_call(
        body, name="conv_bwd", grid=(nt,),
        out_shape=(jax.ShapeDtypeStruct((s, D_IN), ACT), jax.ShapeDtypeStruct((CONV_ROWS, 8, 128), F32),
                   jax.ShapeDtypeStruct((8, C_BR), F32), jax.ShapeDtypeStruct((8, C_BR), F32),
                   jax.ShapeDtypeStruct((8, C_BR), F32), jax.ShapeDtypeStruct(dwout.shape[1:], F32)),
        in_specs=[pl.BlockSpec((tm, C_BR), lambda i: (i, 0)), pl.BlockSpec((tm, C_BR), lambda i: (i, 1))]
        + _halo_specs(tm, s, 2) + _halo_specs(tm, s, 0) + _halo_specs(tm, s, 0)
        + [pl.BlockSpec((CONV_ROWS, 8, 128), lambda i: (0, 0, 0)), vec, vec, pl.BlockSpec(memory_space=pl.ANY)],
        out_specs=(pl.BlockSpec((tm, 3 * C_BR), lambda i: (i, 0)),
                   pl.BlockSpec((CONV_ROWS, 8, 128), lambda i: (0, 0, 0)), part, part, part,
                   pl.BlockSpec(memory_space=pltpu.VMEM)),
        scratch_shapes=[pltpu.VMEM((tm + 2 * HALO, 8, 128), F32),
                        pltpu.VMEM((tm, 8, 128), F32), pltpu.VMEM((tm, 8, 128), F32),
                        pltpu.VMEM(dwout.shape, BF16),
                        pltpu.SemaphoreType.DMA((N_DEV - 1,)), pltpu.SemaphoreType.DMA((N_DEV - 1,)),
                        pltpu.SemaphoreType.DMA],
        compiler_params=_params(("arbitrary",)),
    )(proj, proj, proj, proj, proj, cv, cv, cv, dy, dy, dy, conv_w3, ln_g, ln_b, dwout)


def _sgu_bwd(proj, dy, dproj, ln_g, ln_b, ws, wst, bsb, tm):
    s = proj.shape[0]

    def body(u_ref, v_ref, gs_ref, dy_ref, dp_in, lg_ref, lb_ref, ws_ref, wst_ref, bsb_ref,
             dp_ref, dws_ref, dba_ref, dlg_ref, dlb_ref, dvn_ref):
        del dp_in
        i = pl.program_id(0)

        @pl.when(i == 0)
        def _():
            dws_ref[...] = jnp.zeros_like(dws_ref)
            dba_ref[...] = jnp.zeros_like(dba_ref)
            dlg_ref[...] = jnp.zeros_like(dlg_ref)
            dlb_ref[...] = jnp.zeros_like(dlb_ref)

        xhat, rstd = _ln_norm(v_ref[...].astype(F32))
        lg = lg_ref[...]
        vn = (xhat * lg + lb_ref[...]).astype(BF16)
        for cidx in range(tm // CHUNK):
            rows = slice(cidx * CHUNK, (cidx + 1) * CHUNK)
            for h in range(HEADS):
                cols = slice(h * HEAD_DIM, (h + 1) * HEAD_DIM)
                ocols = slice(C_BR + h * HEAD_DIM, C_BR + (h + 1) * HEAD_DIM)
                gcols = slice(2 * C_BR + h * HEAD_DIM, 2 * C_BR + (h + 1) * HEAD_DIM)
                vb = vn[rows, cols]
                mixed = jnp.dot(ws_ref[h], vb, preferred_element_type=F32) + bsb_ref[:, cols]
                gs = gs_ref[rows, cols].astype(F32)
                sg = _sigmoid(gs)
                u = u_ref[rows, cols].astype(F32)
                dyb = dy_ref[rows, cols].astype(F32)
                t = dyb * (gs * sg)
                dp_ref[rows, cols] = (t * mixed).astype(ACT)
                dp_ref[rows, gcols] = (dyb * u * mixed * (sg * (1.0 + gs * (1.0 - sg)))).astype(ACT)
                dm = t * u
                dmb = dm.astype(BF16)
                dvn_ref[rows, cols] = jnp.dot(wst_ref[h], dmb, preferred_element_type=F32)
                dws_ref[h] += lax.dot_general(dmb, vb, (((1,), (1,)), ((), ())), preferred_element_type=F32)
                dba_ref[:, cols] += dm
        dvn = dvn_ref[...]
        dlg_ref[...] += _fold8(dvn * xhat)
        dlb_ref[...] += _fold8(dvn)
        dp_ref[:, C_BR:2 * C_BR] = _ln_bwd(dvn, xhat, rstd, lg).astype(ACT)

    vec = pl.BlockSpec((1, C_BR), lambda i: (0, 0))
    part = pl.BlockSpec((8, C_BR), lambda i: (0, 0))
    wsp = pl.BlockSpec((HEADS, CHUNK, CHUNK), lambda i: (0, 0, 0))
    return pl.pallas_call(
        body, name="sgu_bwd", grid=(s // tm,),
        out_shape=(jax.ShapeDtypeStruct((s, D_IN), ACT), jax.ShapeDtypeStruct((HEADS, CHUNK, CHUNK), F32),
                   jax.ShapeDtypeStruct((CHUNK, C_BR), F32), jax.ShapeDtypeStruct((8, C_BR), F32),
                   jax.ShapeDtypeStruct((8, C_BR), F32)),
        in_specs=[pl.BlockSpec((tm, C_BR), lambda i: (i, 3)),
                  pl.BlockSpec((tm, C_BR), lambda i: (i, 4)),
                  pl.BlockSpec((tm, C_BR), lambda i: (i, 5)),
                  pl.BlockSpec((tm, C_BR), lambda i: (i, 1)),
                  pl.BlockSpec(memory_space=pl.ANY),
                  vec, vec, wsp, wsp, pl.BlockSpec((CHUNK, C_BR), lambda i: (0, 0))],
        out_specs=(pl.BlockSpec((tm, 3 * C_BR), lambda i: (i, 1)), wsp,
                   pl.BlockSpec((CHUNK, C_BR), lambda i: (0, 0)), part, part),
        scratch_shapes=[pltpu.VMEM((tm, C_BR), F32)],
        input_output_aliases={4: 0},
        compiler_params=_params(("arbitrary",)),
    )(proj, proj, proj, dy, dproj, ln_g, ln_b, ws, wst, bsb)


def _dx(dproj, win_all, x, norm_g, dx2, tm):
    s = x.shape[0]

    def body(dp_ref, w_ref, x_ref, g_ref, dx2_ref, gx_ref, dng_ref):
        i = pl.program_id(0)

        @pl.when(i == 0)
        def _():
            dng_ref[...] = jnp.zeros_like(dng_ref)

        dh = None
        for j in range(N_DEV):
            term = lax.dot_general(dp_ref[:, j * W_BLK:(j + 1) * W_BLK], w_ref[j],
                                   (((1,), (1,)), ((), ())), preferred_element_type=F32)
            dh = term if dh is None else dh + term
        xf = x_ref[...]
        r = lax.rsqrt(jnp.mean(xf * xf, axis=-1, keepdims=True) + EPS)
        n = xf * r
        dng_ref[...] += _fold8(dh * n)
        dn = dh * g_ref[...]
        gx_ref[...] = dx2_ref[...] + r * (dn - n * jnp.mean(dn * n, axis=-1, keepdims=True))

    return pl.pallas_call(
        body, name="dx", grid=(s // tm,),
        out_shape=(jax.ShapeDtypeStruct((s, D_MODEL), F32), jax.ShapeDtypeStruct((8, D_MODEL), F32)),
        in_specs=[pl.BlockSpec((tm, D_IN), lambda i: (i, 0)),
                  pl.BlockSpec((N_DEV, D_MODEL, W_BLK), lambda i: (0, 0, 0), pipeline_mode=pl.Buffered(1)),
                  pl.BlockSpec((tm, D_MODEL), lambda i: (i, 0)),
                  pl.BlockSpec((1, D_MODEL), lambda i: (0, 0)),
                  pl.BlockSpec((tm, D_MODEL), lambda i: (i, 0))],
        out_specs=(pl.BlockSpec((tm, D_MODEL), lambda i: (i, 0)), pl.BlockSpec((8, D_MODEL), lambda i: (0, 0))),
        compiler_params=_params(("arbitrary",)),
    )(dproj, win_all, x, norm_g, dx2)


def _dwin_comm(ht, dproj, order, parts, dwc, dba, dws, tk):
    s = ht.shape[1]
    nk = s // tk
    n_part = len(parts)

    def body(*refs):
        order_ref, ht_ref, dp_ref = refs[:3]
        del order_ref
        part_refs = refs[3:3 + n_part]
        dwc_ref, dba_ref, dws_ref = refs[3 + n_part:6 + n_part]
        gw_ref, red_ref, wsr_ref, cws_ref, loss_ref = refs[6 + n_part:11 + n_part]
        (acc_ref, all1_ref, all2_ref, out_s, land_s, out_x, land_x,
         send_sems, recv_sems, send_s, recv_s, send_x, recv_x) = refs[11 + n_part:]
        jj, k = pl.program_id(0), pl.program_id(1)
        x, y, c = _place()
        me = 4 * x + 2 * y + c
        sibling = (x, y, 1 - c)
        chips = [(1 - x, 1 - y), (1 - x, y), (x, 1 - y)]
        last = k == nk - 1

        def exchanges():
            out = []
            for rel in range(1, N_DEV):
                peer = (x ^ (rel >> 2), y ^ ((rel >> 1) & 1), c ^ (rel & 1))
                for a, buf in enumerate((all1_ref, all2_ref)):
                    out.append(pltpu.make_async_remote_copy(
                        src_ref=buf.at[me], dst_ref=buf.at[me],
                        send_sem=send_sems.at[a, rel - 1], recv_sem=recv_sems.at[a, rel - 1],
                        device_id=peer, device_id_type=MESH))
            return out

        def to_sibling(slot):
            return pltpu.make_async_remote_copy(
                src_ref=out_s.at[slot], dst_ref=land_s.at[slot],
                send_sem=send_s.at[slot], recv_sem=recv_s.at[slot], device_id=sibling, device_id_type=MESH)

        def to_chip(slot):
            return pltpu.make_async_remote_copy(
                src_ref=out_x.at[slot], dst_ref=land_x.at[slot],
                send_sem=send_x.at[slot], recv_sem=recv_x.at[slot],
                device_id=(*chips[slot], c), device_id_type=MESH)

        @pl.when((jj == 0) & (k == 0))
        def _():
            all1_ref[me] = jnp.zeros((SMALL_ROWS, C_BR), F32)
            for row, p_ref in zip((ROW_NORM_G, ROW_CONV_B, ROW_CLN_G, ROW_CLN_B, ROW_SLN_G, ROW_SLN_B,
                                   ROW_FINAL_G, ROW_LOSS), part_refs):
                all1_ref[me, row:row + 1, :] = jnp.sum(p_ref[...], axis=0, keepdims=True)
            ones = jnp.ones((8, HEAD_DIM), F32)
            brow = [lax.dot_general(ones, dba_ref[:, h * HEAD_DIM:(h + 1) * HEAD_DIM], (((1,), (1,)), ((), ())),
                                    precision=lax.Precision.HIGHEST, preferred_element_type=F32)[0:1]
                    for h in range(HEADS)]
            all1_ref[me, ROW_B_S:ROW_B_S + 1, :] = jnp.concatenate(brow, axis=1)
            all1_ref[me, ROW_CONV_W:ROW_CONV_W + CONV_ROWS, :] = dwc_ref[...]
            all2_ref[me] = dws_ref[...]
            for cp in exchanges():
                cp.start()

        @pl.when(k == 0)
        def _():
            acc_ref[...] = jnp.zeros_like(acc_ref)

        acc_ref[...] += jnp.dot(ht_ref[...], dp_ref[...], preferred_element_type=F32)

        for slot in range(4):
            @pl.when((jj == 2 * slot) & last)
            def _(slot=slot):
                out_s[slot] = acc_ref[...].astype(BF16)
                to_sibling(slot).start()

        for slot in range(3):
            @pl.when((jj == 2 * slot + 1) & last)
            def _(slot=slot):
                to_sibling(slot).wait_recv()
                out_x[slot] = (acc_ref[...] + land_s[slot].astype(F32)).astype(BF16)
                to_chip(slot).start()

        @pl.when((jj == N_DEV - 1) & last)
        def _():
            to_sibling(3).wait_recv()
            total = acc_ref[...] + land_s[3].astype(F32)
            for slot in range(3):
                to_chip(slot).wait_recv()
                total = total + land_x[slot].astype(F32)
            gw_ref[...] = total

            copies = exchanges()
            for cp in copies:
                cp.wait_recv()
            tot = all1_ref[0]
            for d in range(1, N_DEV):
                tot = tot + all1_ref[d]
            red_ref[...] = tot
            loss_ref[...] = jnp.broadcast_to(
                jnp.sum(tot[ROW_LOSS:ROW_LOSS + 1, :], axis=1, keepdims=True) * (0.5 / D_MODEL), loss_ref.shape)
            shard = jnp.zeros(cws_ref.shape, F32)
            for d in range(N_DEV):
                shard = jnp.where(me == d, tot[ROW_CONV_W:ROW_CONV_W + CONV_ROWS, d * 128:(d + 1) * 128], shard)
            cws_ref[...] = shard
            tot2 = all2_ref[0]
            for d in range(1, N_DEV):
                tot2 = tot2 + all2_ref[d]
            wsr_ref[...] = tot2
            for cp in copies + [to_sibling(slot) for slot in range(4)] + [to_chip(slot) for slot in range(3)]:
                cp.wait_send()

    vm = pl.BlockSpec(memory_space=pltpu.VMEM)
    blk = (D_MODEL, W_BLK)
    return pl.pallas_call(
        body, name="dwin",
        grid_spec=pltpu.PrefetchScalarGridSpec(
            num_scalar_prefetch=1, grid=(N_DEV, nk),
            in_specs=[pl.BlockSpec((D_MODEL, tk), lambda jj, k, o: (0, k)),
                      pl.BlockSpec((tk, W_BLK), lambda jj, k, o: (k, o[jj]))] + [vm] * (n_part + 3),
            out_specs=(vm, vm, vm, vm, vm),
            scratch_shapes=[pltpu.VMEM(blk, F32),
                            pltpu.VMEM((N_DEV, SMALL_ROWS, C_BR), F32), pltpu.VMEM((N_DEV,) + dws.shape, F32),
                            pltpu.VMEM((4,) + blk, BF16), pltpu.VMEM((4,) + blk, BF16),
                            pltpu.VMEM((3,) + blk, BF16), pltpu.VMEM((3,) + blk, BF16),
                            pltpu.SemaphoreType.DMA((2, N_DEV - 1)), pltpu.SemaphoreType.DMA((2, N_DEV - 1)),
                            pltpu.SemaphoreType.DMA((4,)), pltpu.SemaphoreType.DMA((4,)),
                            pltpu.SemaphoreType.DMA((3,)), pltpu.SemaphoreType.DMA((3,))]),
        out_shape=(jax.ShapeDtypeStruct(blk, F32),
                   jax.ShapeDtypeStruct((SMALL_ROWS, C_BR), F32), jax.ShapeDtypeStruct(dws.shape, F32),
                   jax.ShapeDtypeStruct((CONV_ROWS, 128), F32), jax.ShapeDtypeStruct((8, 128), F32)),
        compiler_params=_params(("arbitrary", "arbitrary")),
    )(order, ht, dproj, *parts, dwc, dba, dws)


def _adamw_math(w, g, m, v):
    m = ADAM_B1 * m + (1.0 - ADAM_B1) * g
    v = ADAM_B2 * v + (1.0 - ADAM_B2) * (g * g)
    m_hat = m / (1.0 - ADAM_B1 ** ADAM_STEP)
    v_hat = v / (1.0 - ADAM_B2 ** ADAM_STEP)
    delta = -ADAM_LR * (m_hat / (jnp.sqrt(v_hat) + ADAM_EPS) + ADAM_WD * w)
    return delta, m, v


def _adamw_all(groups, g_conv_w, conv_wmv, red, rows, ws, ms, vs):
    ng, n = len(groups), len(rows)
    rchunk = 64

    def body(*refs):
        big = refs[:4 * ng]
        gcw_ref, cw_ref, cm_ref, cv_ref, red_ref = refs[4 * ng:4 * ng + 5]
        small = refs[4 * ng + 5:4 * ng + 5 + 3 * n]
        w_refs, m_refs, v_refs = small[:n], small[n:2 * n], small[2 * n:]
        outs = refs[4 * ng + 5 + 3 * n:]
        for t in range(ng):
            g_ref, w_ref, m_ref, v_ref = big[4 * t:4 * t + 4]
            d_ref, mo_ref, vo_ref = outs[3 * t:3 * t + 3]
            lead = w_ref.shape[0]
            chunks = lead // rchunk if len(w_ref.shape) == 2 else lead

            def step(c, carry, refs_=(g_ref, w_ref, m_ref, v_ref, d_ref, mo_ref, vo_ref), two_d=len(w_ref.shape) == 2):
                sl = pl.ds(pl.multiple_of(c * rchunk, rchunk), rchunk) if two_d else c
                g_, w_, m_, v_, d_, mo_, vo_ = refs_
                d_[sl], mo_[sl], vo_[sl] = _adamw_math(w_[sl], g_[sl], m_[sl], v_[sl])
                return carry

            lax.fori_loop(0, chunks, step, 0)
        g = gcw_ref[0:CONV_WIDTH, :]
        o = 3 * ng
        outs[o][...] = g
        outs[o + 1][...], outs[o + 2][...], outs[o + 3][...] = _adamw_math(cw_ref[...], g, cm_ref[...], cv_ref[...])
        o += 4
        for t, row in enumerate(rows):
            g = red_ref[row:row + 1, :]
            delta, m, v = _adamw_math(w_refs[t][...], g, m_refs[t][...], v_refs[t][...])
            outs[o + t][...] = g
            outs[o + n + t][...] = delta
            outs[o + 2 * n + t][...] = m
            outs[o + 3 * n + t][...] = v

    vm = pl.BlockSpec(memory_space=pltpu.VMEM)
    sds = lambda a: jax.ShapeDtypeStruct(a.shape, F32)
    out_shape = []
    for grp in groups:
        out_shape += [sds(grp[1])] * 3
    out_shape += [sds(conv_wmv[0])] * 4
    out_shape += [jax.ShapeDtypeStruct((1, C_BR), F32)] * (4 * n)
    args = [a for grp in groups for a in grp] + [g_conv_w, *conv_wmv, red, *ws, *ms, *vs]
    res = pl.pallas_call(
        body, name="adamw", out_shape=tuple(out_shape),
        in_specs=[vm] * len(args), out_specs=(vm,) * len(out_shape),
        compiler_params=_params(),
    )(*args)
    big_out = [res[3 * t:3 * t + 3] for t in range(ng)]
    o = 3 * ng
    conv_out = res[o:o + 4]
    o += 4
    return big_out, conv_out, (res[o:o + n], res[o + n:o + 2 * n], res[o + 2 * n:o + 3 * n], res[o + 3 * n:o + 4 * n])


def kernel(x, norm_g, w_in, conv_w, conv_b, conv_ln_g, conv_ln_b, sgu_ln_g, sgu_ln_b, w_s, b_s, w_out, final_g, loss_target, m_norm_g, m_w_in, m_conv_w, m_conv_b, m_conv_ln_g, m_conv_ln_b, m_sgu_ln_g, m_sgu_ln_b, m_w_s, m_b_s, m_w_out, m_final_g, v_norm_g, v_w_in, v_conv_w, v_conv_b, v_conv_ln_g, v_conv_ln_b, v_sgu_ln_g, v_sgu_ln_b, v_w_s, v_b_s, v_w_out, v_final_g):
    s = x.shape[1]
    xs = x.reshape(s, D_MODEL)
    tgt = loss_target.reshape(s, D_MODEL)
    tm = min(256, s)

    cw_pad = jnp.pad(conv_w[0], ((0, CONV_ROWS - CONV_WIDTH), (0, 0)))
    px, py, pc = _place()
    blocks = [(px, py, pc), (px, py, 1 - pc)]
    blocks += [(*chip, core) for chip in ((1 - px, py), (px, 1 - py), (1 - px, 1 - py)) for core in (pc, 1 - pc)]
    order = jnp.stack([4 * bx + 2 * by + bc for bx, by, bc in blocks]).astype(jnp.int32)
    proj, ht, win_all, wout_all, cw_all = _proj_ag(xs, norm_g, w_in[0], w_out[0], cw_pad, order, min(1024, s))
    wout_full = wout_all.reshape(2 * C_BR, D_MODEL)
    cw_tiles = jnp.transpose(cw_all, (1, 0, 2))

    ws = w_s[0].astype(BF16)
    wst = jnp.transpose(w_s[0], (0, 2, 1)).astype(BF16)
    bsb = jnp.repeat(jnp.transpose(b_s[0]), HEAD_DIM, axis=1)
    fg = final_g.reshape(1, D_MODEL)

    big = min(512, s)
    y, cv = _conv_fwd(proj, cw_tiles, conv_b, conv_ln_g, conv_ln_b, big)
    y = _sgu_fwd(proj, y, sgu_ln_g, sgu_ln_b, ws, bsb, big)
    dx2, dy, dwout, loss_p, dfg_p = _out_loss(xs, y, wout_full, fg, tgt, min(512, s))
    dproj, dwc, dcb_p, dclg_p, dclb_p, g_w_out = _conv_bwd(
        proj, cv, dy, cw_tiles, conv_ln_g, conv_ln_b, dwout.reshape(N_DEV, 2 * C_BR // N_DEV, D_MODEL), big)
    dwc = dwc.reshape(CONV_ROWS, C_BR)
    dproj, dws, dba, dslg_p, dslb_p = _sgu_bwd(proj, dy, dproj, sgu_ln_g, sgu_ln_b, ws, wst, bsb, big)
    grad_x, dng_p = _dx(dproj, win_all, xs, norm_g, dx2, big)
    rs_blocks = [(*chip, core) for chip in ((1 - px, 1 - py), (1 - px, py), (px, 1 - py), (px, py))
                 for core in (1 - pc, pc)]
    rs_order = jnp.stack([4 * bx + 2 * by + bc for bx, by, bc in rs_blocks]).astype(jnp.int32)
    g_w_in, red, g_w_s, g_cw, loss8 = _dwin_comm(
        ht, dproj, rs_order, [dng_p, dcb_p, dclg_p, dclb_p, dslg_p, dslb_p, dfg_p, loss_p], dwc, dba,
        dws, min(1024, s))
    loss = loss8[0, 0]

    row = lambda a: a.reshape(1, C_BR)
    rows = (ROW_NORM_G, ROW_CONV_B, ROW_CLN_G, ROW_CLN_B, ROW_SLN_G, ROW_SLN_B, ROW_B_S, ROW_FINAL_G)
    big_out, (g_cw, d_cw, nm_cw, nv_cw), (g_r, d_r, m_r, v_r) = _adamw_all(
        [(g_w_in, w_in[0], m_w_in[0], v_w_in[0]), (g_w_out, w_out[0], m_w_out[0], v_w_out[0]),
         (g_w_s, w_s[0], m_w_s[0], v_w_s[0])],
        g_cw, (conv_w[0], m_conv_w[0], v_conv_w[0]), red, rows,
        [norm_g, conv_b, conv_ln_g, conv_ln_b, sgu_ln_g, sgu_ln_b, row(b_s), row(final_g)],
        [m_norm_g, m_conv_b, m_conv_ln_g, m_conv_ln_b, m_sgu_ln_g, m_sgu_ln_b, row(m_b_s), row(m_final_g)],
        [v_norm_g, v_conv_b, v_conv_ln_g, v_conv_ln_b, v_sgu_ln_g, v_sgu_ln_b, row(v_b_s), row(v_final_g)])
    (d_w_in, nm_w_in, nv_w_in), (d_w_out, nm_w_out, nv_w_out), (d_ws, nm_ws, nv_ws) = big_out

    def leaves(r, w_in_l, cw_l, ws_l, w_out_l):
        return (r[0], w_in_l[None], cw_l[None], r[1], r[2], r[3], r[4], r[5],
                ws_l[None], r[6].reshape(1, HEADS, CHUNK), w_out_l[None],
                r[7].reshape(D_MODEL))

    return (loss, grad_x.reshape(1, s, D_MODEL),
            *leaves(g_r, g_w_in, g_cw, g_w_s, g_w_out),
            *leaves(d_r, d_w_in, d_cw, d_ws, d_w_out),
            *leaves(m_r, nm_w_in, nm_cw, nm_ws, nm_w_out),
            *leaves(v_r, nv_w_in, nv_cw, nv_ws, nv_w_out))
```

```python
import functools

import jax
import jax.numpy as jnp
from jax import lax
from jax.experimental import pallas as pl
from jax.experimental.pallas import tpu as pltpu

F32 = jnp.float32
BF16 = jnp.bfloat16
ACT = jnp.bfloat16

D_MODEL = 1024
C_BR = 1024
D_IN = 6 * C_BR
N_DEV = 8
W_BLK = D_IN // N_DEV
HEADS = 8
HEAD_DIM = 128
CHUNK = 128
CONV_WIDTH = 31
CONV_PAD = CONV_WIDTH // 2
HALO = 16
CONV_ROWS = 32
EPS = 1e-6

ADAM_LR = 0.001
ADAM_B1 = 0.9
ADAM_B2 = 0.999
ADAM_EPS = 1e-08
ADAM_WD = 0.01
ADAM_STEP = 10

VMEM_LIMIT = 56 * 1024 * 1024
MESH = pl.DeviceIdType.MESH

ROW_NORM_G, ROW_CONV_B, ROW_CLN_G, ROW_CLN_B, ROW_SLN_G, ROW_SLN_B, ROW_FINAL_G, ROW_B_S, ROW_LOSS = range(9)
ROW_CONV_W = 16
SMALL_ROWS = ROW_CONV_W + CONV_ROWS


def _params(sem=None, **kw):
    return pltpu.CompilerParams(dimension_semantics=sem, vmem_limit_bytes=VMEM_LIMIT, **kw)


def _fold8(a):
    r, n = a.shape
    return a.reshape(r // 8, 8, n).sum(axis=0)


def _sigmoid(z):
    return 0.5 * jnp.tanh(0.5 * z) + 0.5


def _ln_norm(xf):
    mu = jnp.mean(xf, axis=-1, keepdims=True)
    xc = xf - mu
    var = jnp.mean(xc * xc, axis=-1, keepdims=True)
    rstd = lax.rsqrt(var + EPS)
    return xc * rstd, rstd


def _ln_bwd(dy, xhat, rstd, g):
    dxhat = dy * g
    m1 = jnp.mean(dxhat, axis=-1, keepdims=True)
    m2 = jnp.mean(dxhat * xhat, axis=-1, keepdims=True)
    return rstd * (dxhat - m1 - xhat * m2)


def _place():
    return lax.axis_index("x"), lax.axis_index("y"), lax.axis_index("c")


def _proj_ag(x, norm_g, w_in, w_out, conv_w, order, tm):
    s = x.shape[0]
    nt = s // tm

    def body(order_ref, x_ref, g_ref, win_ref, wout_ref, cw_ref,
             proj_ref, ht_ref, win_all, wout_all, cw_all,
             h_ref, win_buf, wout_buf, cw_buf, send_sems, recv_sems, save_sems):
        jj, i = pl.program_id(0), pl.program_id(1)
        x_, y_, c_ = _place()
        me, sibling = (x_, y_, c_), (x_, y_, 1 - c_)
        chips = [(1 - x_, y_), (x_, 1 - y_), (1 - x_, 1 - y_)]
        bufs = (win_buf, wout_buf, cw_buf)
        outs = (win_all, wout_all, cw_all)
        start = i == 0

        def index(px, py, pc):
            return 4 * px + 2 * py + pc

        def copy(a, k, block, to):
            return pltpu.make_async_remote_copy(
                src_ref=bufs[a].at[index(*block)], dst_ref=bufs[a].at[index(*block)],
                send_sem=send_sems.at[a, k], recv_sem=recv_sems.at[a, k],
                device_id=to, device_id_type=MESH)

        def save(a, slot, block):
            return pltpu.make_async_copy(bufs[a].at[index(*block)], outs[a].at[index(*block)], save_sems.at[a, slot])

        def first(a):
            return [copy(a, 0, me, sibling)] + [copy(a, 1 + j, me, (*chip, c_)) for j, chip in enumerate(chips)]

        def saves(a):
            blocks = [me, sibling] + [(*chip, core) for chip in chips for core in (c_, 1 - c_)]
            return [save(a, slot, block) for slot, block in enumerate(blocks)]

        @pl.when((jj == 0) & start)
        def _():
            win_buf[index(*me)] = win_ref[...].astype(BF16)
            wout_buf[index(*me)] = wout_ref[...].astype(BF16)
            cw_buf[index(*me)] = cw_ref[...]
            for a in range(3):
                for cp in first(a):
                    cp.start()
                saves(a)[0].start()

        @pl.when((jj == 1) & start)
        def _():
            copy(0, 0, sibling, me).wait_recv()
            saves(0)[1].start()

        for j, chip in enumerate(chips):
            @pl.when((jj == 2 + 2 * j) & start)
            def _(j=j, chip=chip):
                copy(0, 1 + j, (*chip, c_), me).wait_recv()
                copy(0, 4 + j, (*chip, c_), sibling).start()
                saves(0)[2 + 2 * j].start()

            @pl.when((jj == 3 + 2 * j) & start)
            def _(j=j, chip=chip):
                copy(0, 4 + j, (*chip, 1 - c_), me).wait_recv()
                saves(0)[3 + 2 * j].start()

        @pl.when(jj == 0)
        def _():
            xf = x_ref[...]
            r = lax.rsqrt(jnp.mean(xf * xf, axis=-1, keepdims=True) + EPS)
            hf = xf * r * g_ref[...]
            h_ref[i] = hf.astype(BF16)
            ht_ref[...] = hf.T.astype(BF16)

        proj_ref[...] = jnp.dot(h_ref[i], win_buf[order_ref[jj]], preferred_element_type=F32).astype(ACT)

        @pl.when((jj == N_DEV - 1) & start)
        def _():
            for a in (1, 2):
                for j, chip in enumerate(chips):
                    copy(a, 1 + j, (*chip, c_), me).wait_recv()
                    copy(a, 4 + j, (*chip, c_), sibling).start()
                    saves(a)[2 + 2 * j].start()

        @pl.when((jj == N_DEV - 1) & (i == nt - 1))
        def _():
            passed = [copy(a, 4 + j, (*chip, c_), sibling) for a in range(3) for j, chip in enumerate(chips)]
            for a in (1, 2):
                copy(a, 0, sibling, me).wait_recv()
                saves(a)[1].start()
                for j, chip in enumerate(chips):
                    copy(a, 4 + j, (*chip, 1 - c_), me).wait_recv()
                    saves(a)[3 + 2 * j].start()
            for cp in saves(0) + saves(1) + saves(2):
                cp.wait()
            for cp in first(0) + first(1) + first(2) + passed:
                cp.wait_send()

    vm = pl.BlockSpec(memory_space=pltpu.VMEM)
    hbm = pl.BlockSpec(memory_space=pl.ANY)
    once = lambda jj, i: jnp.where(jj == 0, i, nt - 1)
    stacked = [(N_DEV,) + w.shape for w in (w_in, w_out, conv_w)]
    return pl.pallas_call(
        body, name="proj_ag",
        grid_spec=pltpu.PrefetchScalarGridSpec(
            num_scalar_prefetch=1, grid=(N_DEV, nt),
            in_specs=[pl.BlockSpec((tm, D_MODEL), lambda jj, i, o: (once(jj, i), 0)),
                      pl.BlockSpec((1, D_MODEL), lambda jj, i, o: (0, 0)), vm, vm, vm],
            out_specs=(pl.BlockSpec((tm, W_BLK), lambda jj, i, o: (i, o[jj])),
                       pl.BlockSpec((D_MODEL, tm), lambda jj, i, o: (0, once(jj, i))), hbm, hbm, hbm),
            scratch_shapes=[pltpu.VMEM((nt, tm, D_MODEL), BF16),
                            pltpu.VMEM(stacked[0], BF16), pltpu.VMEM(stacked[1], BF16), pltpu.VMEM(stacked[2], F32),
                            pltpu.SemaphoreType.DMA((3, 7)), pltpu.SemaphoreType.DMA((3, 7)),
                            pltpu.SemaphoreType.DMA((3, N_DEV))]),
        out_shape=(jax.ShapeDtypeStruct((s, D_IN), ACT), jax.ShapeDtypeStruct((D_MODEL, s), BF16),
                   jax.ShapeDtypeStruct(stacked[0], BF16), jax.ShapeDtypeStruct(stacked[1], BF16),
                   jax.ShapeDtypeStruct(stacked[2], F32)),
        compiler_params=_params(("arbitrary", "arbitrary")),
    )(order, x, norm_g, w_in, w_out, conv_w)


def _halo_specs(tm, s, col):
    per = tm // HALO
    last = s // HALO - 1
    return [pl.BlockSpec((HALO, C_BR), lambda i: (jnp.maximum(i * per - 1, 0), col)),
            pl.BlockSpec((tm, C_BR), lambda i: (i, col)),
            pl.BlockSpec((HALO, C_BR), lambda i: (jnp.minimum((i + 1) * per, last), col))]


def _conv_fwd(proj, conv_w3, conv_b, ln_g, ln_b, tm):
    s = proj.shape[0]
    nt = s // tm

    def body(av_p, av_m, av_n, ag_p, ag_m, ag_n, gc_ref, w_ref, cb_ref, lg_ref, lb_ref,
             y_ref, c_ref, ext_ref, cv_ref):
        i = pl.program_id(0)

        def glu(a_ref, g_ref):
            return a_ref[...].astype(F32) * _sigmoid(g_ref[...].astype(F32))

        def tiles(a):
            return a.reshape(a.shape[0], 8, 128)

        ext_ref[0:HALO] = tiles(jnp.where(i > 0, glu(av_p, ag_p), 0.0))
        ext_ref[HALO:HALO + tm] = tiles(glu(av_m, ag_m))
        ext_ref[HALO + tm:] = tiles(jnp.where(i < nt - 1, glu(av_n, ag_n), 0.0))

        nb = 32

        def step(t, carry):
            s0 = t * nb
            accs = [None] * nb
            for k in range(CONV_WIDTH):
                w = w_ref[k]
                for j in range(nb):
                    term = w * ext_ref[s0 + HALO - CONV_PAD + j + k]
                    accs[j] = term if accs[j] is None else accs[j] + term
            for j in range(nb):
                cv_ref[s0 + j] = accs[j]
            return carry

        lax.fori_loop(0, tm // nb, step, 0)
        cv = cv_ref[...].reshape(tm, C_BR) + cb_ref[...]
        c_ref[...] = cv.astype(ACT)
        xhat, _ = _ln_norm(cv)
        ln = xhat * lg_ref[...] + lb_ref[...]
        gc = gc_ref[...].astype(F32)
        y_ref[...] = (ln * _sigmoid(ln) * (gc * _sigmoid(gc))).astype(ACT)

    vec = pl.BlockSpec((1, C_BR), lambda i: (0, 0))
    return pl.pallas_call(
        body, name="conv_fwd", grid=(nt,),
        out_shape=(jax.ShapeDtypeStruct((s, 2 * C_BR), ACT), jax.ShapeDtypeStruct((s, C_BR), ACT)),
        in_specs=_halo_specs(tm, s, 0) + _halo_specs(tm, s, 1)
        + [pl.BlockSpec((tm, C_BR), lambda i: (i, 2)),
           pl.BlockSpec((CONV_ROWS, 8, 128), lambda i: (0, 0, 0)), vec, vec, vec],
        out_specs=(pl.BlockSpec((tm, C_BR), lambda i: (i, 0)), pl.BlockSpec((tm, C_BR), lambda i: (i, 0))),
        scratch_shapes=[pltpu.VMEM((tm + 2 * HALO, 8, 128), F32),
                        pltpu.VMEM((tm, 8, 128), F32)],
        compiler_params=_params(("parallel",)),
    )(proj, proj, proj, proj, proj, proj, proj, conv_w3, conv_b, ln_g, ln_b)


def _sgu_fwd(proj, y, ln_g, ln_b, ws, bsb, tm):
    s = proj.shape[0]

    def body(u_ref, v_ref, gs_ref, y_in, lg_ref, lb_ref, ws_ref, bsb_ref, y_ref):
        del y_in
        xhat, _ = _ln_norm(v_ref[...].astype(F32))
        vn = (xhat * lg_ref[...] + lb_ref[...]).astype(BF16)
        for cidx in range(tm // CHUNK):
            rows = slice(cidx * CHUNK, (cidx + 1) * CHUNK)
            for h in range(HEADS):
                cols = slice(h * HEAD_DIM, (h + 1) * HEAD_DIM)
                mixed = jnp.dot(ws_ref[h], vn[rows, cols], preferred_element_type=F32) + bsb_ref[:, cols]
                gs = gs_ref[rows, cols].astype(F32)
                y_ref[rows, cols] = (u_ref[rows, cols].astype(F32) * mixed * (gs * _sigmoid(gs))).astype(ACT)

    vec = pl.BlockSpec((1, C_BR), lambda i: (0, 0))
    return pl.pallas_call(
        body, name="sgu_fwd", grid=(s // tm,),
        out_shape=jax.ShapeDtypeStruct((s, 2 * C_BR), ACT),
        in_specs=[pl.BlockSpec((tm, C_BR), lambda i: (i, 3)),
                  pl.BlockSpec((tm, C_BR), lambda i: (i, 4)),
                  pl.BlockSpec((tm, C_BR), lambda i: (i, 5)),
                  pl.BlockSpec(memory_space=pl.ANY),
                  vec, vec,
                  pl.BlockSpec((HEADS, CHUNK, CHUNK), lambda i: (0, 0, 0)),
                  pl.BlockSpec((CHUNK, C_BR), lambda i: (0, 0))],
        out_specs=pl.BlockSpec((tm, C_BR), lambda i: (i, 1)),
        input_output_aliases={3: 0},
        compiler_params=_params(("parallel",)),
    )(proj, proj, proj, y, ln_g, ln_b, ws, bsb)


def _out_loss(x, y, wout, final_g, target, tm):
    s = x.shape[0]
    nt = s // tm
    inv_d = 1.0 / D_MODEL

    def body(x_ref, y_ref, w_ref, g_ref, t_ref, dx2_ref, dy_ref, dw_ref, loss_ref, dfg_ref, acc_ref):
        i = pl.program_id(0)

        @pl.when(i == 0)
        def _():
            acc_ref[...] = jnp.zeros_like(acc_ref)
            loss_ref[...] = jnp.zeros_like(loss_ref)
            dfg_ref[...] = jnp.zeros_like(dfg_ref)

        yb = y_ref[...]
        x2 = x_ref[...] + jnp.dot(yb, w_ref[...], preferred_element_type=F32)
        r2 = lax.rsqrt(jnp.mean(x2 * x2, axis=-1, keepdims=True) + EPS)
        n = x2 * r2
        g = g_ref[...]
        e = n * g - t_ref[...]
        loss_ref[...] += _fold8(e * e)
        dout = e * inv_d
        dfg_ref[...] += _fold8(dout * n)
        dn = dout * g
        dx2 = r2 * (dn - n * jnp.mean(dn * n, axis=-1, keepdims=True))
        dxb = dx2.astype(BF16)
        dx2_ref[...] = dxb
        dy_ref[...] = lax.dot_general(dxb, w_ref[...], (((1,), (1,)), ((), ())),
                                      preferred_element_type=F32).astype(ACT)
        acc_ref[...] += lax.dot_general(yb, dxb, (((0,), (0,)), ((), ())), preferred_element_type=F32)

        @pl.when(i == nt - 1)
        def _():
            dw_ref[...] = acc_ref[...].astype(BF16)

    part = pl.BlockSpec((8, D_MODEL), lambda i: (0, 0))
    return pl.pallas_call(
        body, name="out_loss", grid=(nt,),
        out_shape=(jax.ShapeDtypeStruct((s, D_MODEL), ACT), jax.ShapeDtypeStruct((s, 2 * C_BR), ACT),
                   jax.ShapeDtypeStruct((2 * C_BR, D_MODEL), BF16),
                   jax.ShapeDtypeStruct((8, D_MODEL), F32), jax.ShapeDtypeStruct((8, D_MODEL), F32)),
        in_specs=[pl.BlockSpec((tm, D_MODEL), lambda i: (i, 0)),
                  pl.BlockSpec((tm, 2 * C_BR), lambda i: (i, 0)),
                  pl.BlockSpec((2 * C_BR, D_MODEL), lambda i: (0, 0), pipeline_mode=pl.Buffered(1)),
                  pl.BlockSpec((1, D_MODEL), lambda i: (0, 0)),
                  pl.BlockSpec((tm, D_MODEL), lambda i: (i, 0))],
        out_specs=(pl.BlockSpec((tm, D_MODEL), lambda i: (i, 0)),
                   pl.BlockSpec((tm, 2 * C_BR), lambda i: (i, 0)),
                   pl.BlockSpec((2 * C_BR, D_MODEL), lambda i: (0, 0), pipeline_mode=pl.Buffered(1)), part, part),
        scratch_shapes=[pltpu.VMEM((2 * C_BR, D_MODEL), F32)],
        compiler_params=_params(("arbitrary",)),
    )(x, y, wout, final_g, target)


def _conv_bwd(proj, cv, dy, conv_w3, ln_g, ln_b, dwout, tm):
    s = proj.shape[0]
    nt = s // tm
    wo_rows = dwout.shape[1]

    def body(av_ref, ag_ref, gc_p, gc_m, gc_n, c_p, c_m, c_n, dy_p, dy_m, dy_n, w_ref, lg_ref, lb_ref,
             dwout_ref, dp_ref, dwc_ref, dcb_ref, dlg_ref, dlb_ref, gwo_ref,
             dce_ref, glu_ref, dgl_ref, land_ref, send_sems, recv_sems, loc_sem):
        i = pl.program_id(0)
        px, py, pc = _place()
        me = 4 * px + 2 * py + pc

        def exchanges():
            out = []
            for rel in range(1, N_DEV):
                qx, qy, qc = px ^ (rel >> 2), py ^ ((rel >> 1) & 1), pc ^ (rel & 1)
                out.append(pltpu.make_async_remote_copy(
                    src_ref=dwout_ref.at[4 * qx + 2 * qy + qc], dst_ref=land_ref.at[me],
                    send_sem=send_sems.at[rel - 1], recv_sem=recv_sems.at[rel - 1],
                    device_id=(qx, qy, qc), device_id_type=MESH))
            return out

        own = pltpu.make_async_copy(dwout_ref.at[me], land_ref.at[me], loc_sem)

        @pl.when(i == 0)
        def _():
            dwc_ref[...] = jnp.zeros_like(dwc_ref)
            dcb_ref[...] = jnp.zeros_like(dcb_ref)
            dlg_ref[...] = jnp.zeros_like(dlg_ref)
            dlb_ref[...] = jnp.zeros_like(dlb_ref)
            own.start()
            for cp in exchanges():
                cp.start()

        def ext(p, m, n):
            return jnp.concatenate([p[...], m[...], n[...]], axis=0).astype(F32)

        main = slice(HALO, HALO + tm)
        cf, gc, dyc = ext(c_p, c_m, c_n), ext(gc_p, gc_m, gc_n), ext(dy_p, dy_m, dy_n)
        xhat, rstd = _ln_norm(cf)
        lg = lg_ref[...]
        ln = xhat * lg + lb_ref[...]
        s_ln, s_gc = _sigmoid(ln), _sigmoid(gc)
        dln = dyc * (gc * s_gc) * (s_ln * (1.0 + ln * (1.0 - s_ln)))
        dp_ref[:, 2 * C_BR:] = (dyc[main] * (ln[main] * s_ln[main])
                                * (s_gc[main] * (1.0 + gc[main] * (1.0 - s_gc[main])))).astype(ACT)
        dlg_ref[...] += _fold8(dln[main] * xhat[main])
        dlb_ref[...] += _fold8(dln[main])
        dc = _ln_bwd(dln, xhat, rstd, lg)
        dcb_ref[...] += _fold8(dc[main])

        def tiles(a):
            return a.reshape(a.shape[0], 8, 128)

        dce_ref[0:HALO] = tiles(jnp.where(i > 0, dc[0:HALO], 0.0))
        dce_ref[HALO:HALO + tm] = tiles(dc[main])
        dce_ref[HALO + tm:] = tiles(jnp.where(i < nt - 1, dc[HALO + tm:], 0.0))
        av = av_ref[...].astype(F32)
        sa = _sigmoid(ag_ref[...].astype(F32))
        glu_ref[...] = tiles(av * sa)

        nb = 16

        def step(t, carry):
            s0 = t * nb
            accs = [None] * nb
            for k in range(CONV_WIDTH):
                w = w_ref[k]
                prods = []
                for j in range(nb):
                    v = dce_ref[s0 + HALO + CONV_PAD + j - k]
                    term = w * v
                    accs[j] = term if accs[j] is None else accs[j] + term
                    prods.append(glu_ref[s0 + j] * v)
                while len(prods) > 1:
                    prods = [p + q for p, q in zip(prods[::2], prods[1::2])]
                dwc_ref[k] += prods[0]
            for j in range(nb):
                dgl_ref[s0 + j] = accs[j]
            return carry

        lax.fori_loop(0, tm // nb, step, 0)
        dglu = dgl_ref[...].reshape(tm, C_BR)
        dp_ref[:, 0:C_BR] = (dglu * sa).astype(ACT)
        dp_ref[:, C_BR:2 * C_BR] = (dglu * av * sa * (1.0 - sa)).astype(ACT)

        @pl.when(i == nt - 1)
        def _():
            copies = exchanges()
            own.wait()
            for cp in copies:
                cp.wait_recv()

            def step(t, carry):
                sl = pl.ds(pl.multiple_of(t * 64, 64), 64)
                g = land_ref[0, sl, :].astype(F32)
                for d in range(1, N_DEV):
                    g = g + land_ref[d, sl, :].astype(F32)
                gwo_ref[sl, :] = g
                return carry

            lax.fori_loop(0, wo_rows // 64, step, 0)
            for cp in copies:
                cp.wait_send()

    vec = pl.BlockSpec((1, C_BR), lambda i: (0, 0))
    part = pl.BlockSpec((8, C_BR), lambda i: (0, 0))
    return pl.pallas_call(
        body, name="conv_bwd", grid=(nt,),
        out_shape=(jax.ShapeDtypeStruct((s, D_IN), ACT), jax.ShapeDtypeStruct((CONV_ROWS, 8, 128), F32),
                   jax.ShapeDtypeStruct((8, C_BR), F32), jax.ShapeDtypeStruct((8, C_BR), F32),
                   jax.ShapeDtypeStruct((8, C_BR), F32), jax.ShapeDtypeStruct(dwout.shape[1:], F32)),
        in_specs=[pl.BlockSpec((tm, C_BR), lambda i: (i, 0)), pl.BlockSpec((tm, C_BR), lambda i: (i, 1))]
        + _halo_specs(tm, s, 2) + _halo_specs(tm, s, 0) + _halo_specs(tm, s, 0)
        + [pl.BlockSpec((CONV_ROWS, 8, 128), lambda i: (0, 0, 0)), vec, vec, pl.BlockSpec(memory_space=pl.ANY)],
        out_specs=(pl.BlockSpec((tm, 3 * C_BR), lambda i: (i, 0)),
                   pl.BlockSpec((CONV_ROWS, 8, 128), lambda i: (0, 0, 0)), part, part, part,
                   pl.BlockSpec(memory_space=pltpu.VMEM)),
        scratch_shapes=[pltpu.VMEM((tm + 2 * HALO, 8, 128), F32),
                        pltpu.VMEM((tm, 8, 128), F32), pltpu.VMEM((tm, 8, 128), F32),
                        pltpu.VMEM(dwout.shape, BF16),
                        pltpu.SemaphoreType.DMA((N_DEV - 1,)), pltpu.SemaphoreType.DMA((N_DEV - 1,)),
                        pltpu.SemaphoreType.DMA],
        compiler_params=_params(("arbitrary",)),
    )(proj, proj, proj, proj, proj, cv, cv, cv, dy, dy, dy, conv_w3, ln_g, ln_b, dwout)


def _sgu_bwd(proj, dy, dproj, ln_g, ln_b, ws, wst, bsb, tm):
    s = proj.shape[0]

    def body(u_ref, v_ref, gs_ref, dy_ref, dp_in, lg_ref, lb_ref, ws_ref, wst_ref, bsb_ref,
             dp_ref, dws_ref, dba_ref, dlg_ref, dlb_ref, dvn_ref):
        del dp_in
        i = pl.program_id(0)

        @pl.when(i == 0)
        def _():
            dws_ref[...] = jnp.zeros_like(dws_ref)
            dba_ref[...] = jnp.zeros_like(dba_ref)
            dlg_ref[...] = jnp.zeros_like(dlg_ref)
            dlb_ref[...] = jnp.zeros_like(dlb_ref)

        xhat, rstd = _ln_norm(v_ref[...].astype(F32))
        lg = lg_ref[...]
        vn = (xhat * lg + lb_ref[...]).astype(BF16)
        for cidx in range(tm // CHUNK):
            rows = slice(cidx * CHUNK, (cidx + 1) * CHUNK)
            for h in range(HEADS):
                cols = slice(h * HEAD_DIM, (h + 1) * HEAD_DIM)
                ocols = slice(C_BR + h * HEAD_DIM, C_BR + (h + 1) * HEAD_DIM)
                gcols = slice(2 * C_BR + h * HEAD_DIM, 2 * C_BR + (h + 1) * HEAD_DIM)
                vb = vn[rows, cols]
                mixed = jnp.dot(ws_ref[h], vb, preferred_element_type=F32) + bsb_ref[:, cols]
                gs = gs_ref[rows, cols].astype(F32)
                sg = _sigmoid(gs)
                u = u_ref[rows, cols].astype(F32)
                dyb = dy_ref[rows, cols].astype(F32)
                t = dyb * (gs * sg)
                dp_ref[rows, cols] = (t * mixed).astype(ACT)
                dp_ref[rows, gcols] = (dyb * u * mixed * (sg * (1.0 + gs * (1.0 - sg)))).astype(ACT)
                dm = t * u
                dmb = dm.astype(BF16)
                dvn_ref[rows, cols] = jnp.dot(wst_ref[h], dmb, preferred_element_type=F32)
                dws_ref[h] += lax.dot_general(dmb, vb, (((1,), (1,)), ((), ())), preferred_element_type=F32)
                dba_ref[:, cols] += dm
        dvn = dvn_ref[...]
        dlg_ref[...] += _fold8(dvn * xhat)
        dlb_ref[...] += _fold8(dvn)
        dp_ref[:, C_BR:2 * C_BR] = _ln_bwd(dvn, xhat, rstd, lg).astype(ACT)

    vec = pl.BlockSpec((1, C_BR), lambda i: (0, 0))
    part = pl.BlockSpec((8, C_BR), lambda i: (0, 0))
    wsp = pl.BlockSpec((HEADS, CHUNK, CHUNK), lambda i: (0, 0, 0))
    return pl.pallas_call(
        body, name="sgu_bwd", grid=(s // tm,),
        out_shape=(jax.ShapeDtypeStruct((s, D_IN), ACT), jax.ShapeDtypeStruct((HEADS, CHUNK, CHUNK), F32),
                   jax.ShapeDtypeStruct((CHUNK, C_BR), F32), jax.ShapeDtypeStruct((8, C_BR), F32),
                   jax.ShapeDtypeStruct((8, C_BR), F32)),
        in_specs=[pl.BlockSpec((tm, C_BR), lambda i: (i, 3)),
                  pl.BlockSpec((tm, C_BR), lambda i: (i, 4)),
                  pl.BlockSpec((tm, C_BR), lambda i: (i, 5)),
                  pl.BlockSpec((tm, C_BR), lambda i: (i, 1)),
                  pl.BlockSpec(memory_space=pl.ANY),
                  vec, vec, wsp, wsp, pl.BlockSpec((CHUNK, C_BR), lambda i: (0, 0))],
        out_specs=(pl.BlockSpec((tm, 3 * C_BR), lambda i: (i, 1)), wsp,
                   pl.BlockSpec((CHUNK, C_BR), lambda i: (0, 0)), part, part),
        scratch_shapes=[pltpu.VMEM((tm, C_BR), F32)],
        input_output_aliases={4: 0},
        compiler_params=_params(("arbitrary",)),
    )(proj, proj, proj, dy, dproj, ln_g, ln_b, ws, wst, bsb)


def _dx(dproj, win_all, x, norm_g, dx2, tm):
    s = x.shape[0]

    def body(dp_ref, w_ref, x_ref, g_ref, dx2_ref, gx_ref, dng_ref):
        i = pl.program_id(0)

        @pl.when(i == 0)
        def _():
            dng_ref[...] = jnp.zeros_like(dng_ref)

        dh = None
        for j in range(N_DEV):
            term = lax.dot_general(dp_ref[:, j * W_BLK:(j + 1) * W_BLK], w_ref[j],
                                   (((1,), (1,)), ((), ())), preferred_element_type=F32)
            dh = term if dh is None else dh + term
        xf = x_ref[...]
        r = lax.rsqrt(jnp.mean(xf * xf, axis=-1, keepdims=True) + EPS)
        n = xf * r
        dng_ref[...] += _fold8(dh * n)
        dn = dh * g_ref[...]
        gx_ref[...] = dx2_ref[...].astype(F32) + r * (dn - n * jnp.mean(dn * n, axis=-1, keepdims=True))

    return pl.pallas_call(
        body, name="dx", grid=(s // tm,),
        out_shape=(jax.ShapeDtypeStruct((s, D_MODEL), F32), jax.ShapeDtypeStruct((8, D_MODEL), F32)),
        in_specs=[pl.BlockSpec((tm, D_IN), lambda i: (i, 0)),
                  pl.BlockSpec((N_DEV, D_MODEL, W_BLK), lambda i: (0, 0, 0), pipeline_mode=pl.Buffered(1)),
                  pl.BlockSpec((tm, D_MODEL), lambda i: (i, 0)),
                  pl.BlockSpec((1, D_MODEL), lambda i: (0, 0)),
                  pl.BlockSpec((tm, D_MODEL), lambda i: (i, 0))],
        out_specs=(pl.BlockSpec((tm, D_MODEL), lambda i: (i, 0)), pl.BlockSpec((8, D_MODEL), lambda i: (0, 0))),
        compiler_params=_params(("arbitrary",)),
    )(dproj, win_all, x, norm_g, dx2)


def _dwin_comm(ht, dproj, order, parts, dwc, dba, dws, tk):
    s = ht.shape[1]
    nk = s // tk
    n_part = len(parts)

    def body(*refs):
        order_ref, ht_ref, dp_ref = refs[:3]
        del order_ref
        part_refs = refs[3:3 + n_part]
        dwc_ref, dba_ref, dws_ref = refs[3 + n_part:6 + n_part]
        gw_ref, red_ref, wsr_ref, cws_ref, loss_ref = refs[6 + n_part:11 + n_part]
        (acc_ref, all1_ref, all2_ref, out_s, land_s, out_x, land_x,
         send_sems, recv_sems, send_s, recv_s, send_x, recv_x) = refs[11 + n_part:]
        jj, k = pl.program_id(0), pl.program_id(1)
        x, y, c = _place()
        me = 4 * x + 2 * y + c
        sibling = (x, y, 1 - c)
        chips = [(1 - x, 1 - y), (1 - x, y), (x, 1 - y)]
        last = k == nk - 1

        def exchanges():
            out = []
            for rel in range(1, N_DEV):
                peer = (x ^ (rel >> 2), y ^ ((rel >> 1) & 1), c ^ (rel & 1))
                for a, buf in enumerate((all1_ref, all2_ref)):
                    out.append(pltpu.make_async_remote_copy(
                        src_ref=buf.at[me], dst_ref=buf.at[me],
                        send_sem=send_sems.at[a, rel - 1], recv_sem=recv_sems.at[a, rel - 1],
                        device_id=peer, device_id_type=MESH))
            return out

        def to_sibling(slot):
            return pltpu.make_async_remote_copy(
                src_ref=out_s.at[slot], dst_ref=land_s.at[slot],
                send_sem=send_s.at[slot], recv_sem=recv_s.at[slot], device_id=sibling, device_id_type=MESH)

        def to_chip(slot):
            return pltpu.make_async_remote_copy(
                src_ref=out_x.at[slot], dst_ref=land_x.at[slot],
                send_sem=send_x.at[slot], recv_sem=recv_x.at[slot],
                device_id=(*chips[slot], c), device_id_type=MESH)

        @pl.when((jj == 0) & (k == 0))
        def _():
            all1_ref[me] = jnp.zeros((SMALL_ROWS, C_BR), F32)
            for row, p_ref in zip((ROW_NORM_G, ROW_CONV_B, ROW_CLN_G, ROW_CLN_B, ROW_SLN_G, ROW_SLN_B,
                                   ROW_FINAL_G, ROW_LOSS), part_refs):
                all1_ref[me, row:row + 1, :] = jnp.sum(p_ref[...], axis=0, keepdims=True)
            ones = jnp.ones((8, HEAD_DIM), F32)
            brow = [lax.dot_general(ones, dba_ref[:, h * HEAD_DIM:(h + 1) * HEAD_DIM], (((1,), (1,)), ((), ())),
                                    precision=lax.Precision.HIGHEST, preferred_element_type=F32)[0:1]
                    for h in range(HEADS)]
            all1_ref[me, ROW_B_S:ROW_B_S + 1, :] = jnp.concatenate(brow, axis=1)
            all1_ref[me, ROW_CONV_W:ROW_CONV_W + CONV_ROWS, :] = dwc_ref[...]
            all2_ref[me] = dws_ref[...]
            for cp in exchanges():
                cp.start()

        @pl.when(k == 0)
        def _():
            acc_ref[...] = jnp.zeros_like(acc_ref)

        acc_ref[...] += jnp.dot(ht_ref[...], dp_ref[...], preferred_element_type=F32)

        for slot in range(4):
            @pl.when((jj == 2 * slot) & last)
            def _(slot=slot):
                out_s[slot] = acc_ref[...].astype(BF16)
                to_sibling(slot).start()

        for slot in range(3):
            @pl.when((jj == 2 * slot + 1) & last)
            def _(slot=slot):
                to_sibling(slot).wait_recv()
                out_x[slot] = (acc_ref[...] + land_s[slot].astype(F32)).astype(BF16)
                to_chip(slot).start()

        @pl.when((jj == N_DEV - 1) & last)
        def _():
            to_sibling(3).wait_recv()
            total = acc_ref[...] + land_s[3].astype(F32)
            for slot in range(3):
                to_chip(slot).wait_recv()
                total = total + land_x[slot].astype(F32)
            gw_ref[...] = total

            copies = exchanges()
            for cp in copies:
                cp.wait_recv()
            tot = all1_ref[0]
            for d in range(1, N_DEV):
                tot = tot + all1_ref[d]
            red_ref[...] = tot
            loss_ref[...] = jnp.broadcast_to(
                jnp.sum(tot[ROW_LOSS:ROW_LOSS + 1, :], axis=1, keepdims=True) * (0.5 / D_MODEL), loss_ref.shape)
            shard = jnp.zeros(cws_ref.shape, F32)
            for d in range(N_DEV):
                shard = jnp.where(me == d, tot[ROW_CONV_W:ROW_CONV_W + CONV_ROWS, d * 128:(d + 1) * 128], shard)
            cws_ref[...] = shard
            tot2 = all2_ref[0]
            for d in range(1, N_DEV):
                tot2 = tot2 + all2_ref[d]
            wsr_ref[...] = tot2
            for cp in copies + [to_sibling(slot) for slot in range(4)] + [to_chip(slot) for slot in range(3)]:
                cp.wait_send()

    vm = pl.BlockSpec(memory_space=pltpu.VMEM)
    blk = (D_MODEL, W_BLK)
    return pl.pallas_call(
        body, name="dwin",
        grid_spec=pltpu.PrefetchScalarGridSpec(
            num_scalar_prefetch=1, grid=(N_DEV, nk),
            in_specs=[pl.BlockSpec((D_MODEL, tk), lambda jj, k, o: (0, k)),
                      pl.BlockSpec((tk, W_BLK), lambda jj, k, o: (k, o[jj]))] + [vm] * (n_part + 3),
            out_specs=(vm, vm, vm, vm, vm),
            scratch_shapes=[pltpu.VMEM(blk, F32),
                            pltpu.VMEM((N_DEV, SMALL_ROWS, C_BR), F32), pltpu.VMEM((N_DEV,) + dws.shape, F32),
                            pltpu.VMEM((4,) + blk, BF16), pltpu.VMEM((4,) + blk, BF16),
                            pltpu.VMEM((3,) + blk, BF16), pltpu.VMEM((3,) + blk, BF16),
                            pltpu.SemaphoreType.DMA((2, N_DEV - 1)), pltpu.SemaphoreType.DMA((2, N_DEV - 1)),
                            pltpu.SemaphoreType.DMA((4,)), pltpu.SemaphoreType.DMA((4,)),
                            pltpu.SemaphoreType.DMA((3,)), pltpu.SemaphoreType.DMA((3,))]),
        out_shape=(jax.ShapeDtypeStruct(blk, F32),
                   jax.ShapeDtypeStruct((SMALL_ROWS, C_BR), F32), jax.ShapeDtypeStruct(dws.shape, F32),
                   jax.ShapeDtypeStruct((CONV_ROWS, 128), F32), jax.ShapeDtypeStruct((8, 128), F32)),
        compiler_params=_params(("arbitrary", "arbitrary")),
    )(order, ht, dproj, *parts, dwc, dba, dws)


def _adamw_math(w, g, m, v):
    m = ADAM_B1 * m + (1.0 - ADAM_B1) * g
    v = ADAM_B2 * v + (1.0 - ADAM_B2) * (g * g)
    m_hat = m / (1.0 - ADAM_B1 ** ADAM_STEP)
    v_hat = v / (1.0 - ADAM_B2 ** ADAM_STEP)
    delta = -ADAM_LR * (m_hat / (jnp.sqrt(v_hat) + ADAM_EPS) + ADAM_WD * w)
    return delta, m, v


def _adamw_all(groups, g_conv_w, conv_wmv, red, rows, ws, ms, vs):
    ng, n = len(groups), len(rows)
    rchunk = 64

    def body(*refs):
        big = refs[:4 * ng]
        gcw_ref, cw_ref, cm_ref, cv_ref, red_ref = refs[4 * ng:4 * ng + 5]
        small = refs[4 * ng + 5:4 * ng + 5 + 3 * n]
        w_refs, m_refs, v_refs = small[:n], small[n:2 * n], small[2 * n:]
        outs = refs[4 * ng + 5 + 3 * n:]
        for t in range(ng):
            g_ref, w_ref, m_ref, v_ref = big[4 * t:4 * t + 4]
            d_ref, mo_ref, vo_ref = outs[3 * t:3 * t + 3]
            lead = w_ref.shape[0]
            chunks = lead // rchunk if len(w_ref.shape) == 2 else lead

            def step(c, carry, refs_=(g_ref, w_ref, m_ref, v_ref, d_ref, mo_ref, vo_ref), two_d=len(w_ref.shape) == 2):
                sl = pl.ds(pl.multiple_of(c * rchunk, rchunk), rchunk) if two_d else c
                g_, w_, m_, v_, d_, mo_, vo_ = refs_
                d_[sl], mo_[sl], vo_[sl] = _adamw_math(w_[sl], g_[sl], m_[sl], v_[sl])
                return carry

            lax.fori_loop(0, chunks, step, 0)
        g = gcw_ref[0:CONV_WIDTH, :]
        o = 3 * ng
        outs[o][...] = g
        outs[o + 1][...], outs[o + 2][...], outs[o + 3][...] = _adamw_math(cw_ref[...], g, cm_ref[...], cv_ref[...])
        o += 4
        for t, row in enumerate(rows):
            g = red_ref[row:row + 1, :]
            delta, m, v = _adamw_math(w_refs[t][...], g, m_refs[t][...], v_refs[t][...])
            outs[o + t][...] = g
            outs[o + n + t][...] = delta
            outs[o + 2 * n + t][...] = m
            outs[o + 3 * n + t][...] = v

    vm = pl.BlockSpec(memory_space=pltpu.VMEM)
    sds = lambda a: jax.ShapeDtypeStruct(a.shape, F32)
    out_shape = []
    for grp in groups:
        out_shape += [sds(grp[1])] * 3
    out_shape += [sds(conv_wmv[0])] * 4
    out_shape += [jax.ShapeDtypeStruct((1, C_BR), F32)] * (4 * n)
    args = [a for grp in groups for a in grp] + [g_conv_w, *conv_wmv, red, *ws, *ms, *vs]
    res = pl.pallas_call(
        body, name="adamw", out_shape=tuple(out_shape),
        in_specs=[vm] * len(args), out_specs=(vm,) * len(out_shape),
        compiler_params=_params(),
    )(*args)
    big_out = [res[3 * t:3 * t + 3] for t in range(ng)]
    o = 3 * ng
    conv_out = res[o:o + 4]
    o += 4
    return big_out, conv_out, (res[o:o + n], res[o + n:o + 2 * n], res[o + 2 * n:o + 3 * n], res[o + 3 * n:o + 4 * n])


def kernel(x, norm_g, w_in, conv_w, conv_b, conv_ln_g, conv_ln_b, sgu_ln_g, sgu_ln_b, w_s, b_s, w_out, final_g, loss_target, m_norm_g, m_w_in, m_conv_w, m_conv_b, m_conv_ln_g, m_conv_ln_b, m_sgu_ln_g, m_sgu_ln_b, m_w_s, m_b_s, m_w_out, m_final_g, v_norm_g, v_w_in, v_conv_w, v_conv_b, v_conv_ln_g, v_conv_ln_b, v_sgu_ln_g, v_sgu_ln_b, v_w_s, v_b_s, v_w_out, v_final_g):
    s = x.shape[1]
    xs = x.reshape(s, D_MODEL)
    tgt = loss_target.reshape(s, D_MODEL)
    tm = min(256, s)

    cw_pad = jnp.pad(conv_w[0], ((0, CONV_ROWS - CONV_WIDTH), (0, 0)))
    px, py, pc = _place()
    blocks = [(px, py, pc), (px, py, 1 - pc)]
    blocks += [(*chip, core) for chip in ((1 - px, py), (px, 1 - py), (1 - px, 1 - py)) for core in (pc, 1 - pc)]
    order = jnp.stack([4 * bx + 2 * by + bc for bx, by, bc in blocks]).astype(jnp.int32)
    proj, ht, win_all, wout_all, cw_all = _proj_ag(xs, norm_g, w_in[0], w_out[0], cw_pad, order, min(1024, s))
    wout_full = wout_all.reshape(2 * C_BR, D_MODEL)
    cw_tiles = jnp.transpose(cw_all, (1, 0, 2))

    ws = w_s[0].astype(BF16)
    wst = jnp.transpose(w_s[0], (0, 2, 1)).astype(BF16)
    bsb = jnp.repeat(jnp.transpose(b_s[0]), HEAD_DIM, axis=1)
    fg = final_g.reshape(1, D_MODEL)

    big = min(512, s)
    y, cv = _conv_fwd(proj, cw_tiles, conv_b, conv_ln_g, conv_ln_b, big)
    y = _sgu_fwd(proj, y, sgu_ln_g, sgu_ln_b, ws, bsb, big)
    dx2, dy, dwout, loss_p, dfg_p = _out_loss(xs, y, wout_full, fg, tgt, min(512, s))
    dproj, dwc, dcb_p, dclg_p, dclb_p, g_w_out = _conv_bwd(
        proj, cv, dy, cw_tiles, conv_ln_g, conv_ln_b, dwout.reshape(N_DEV, 2 * C_BR // N_DEV, D_MODEL), big)
    dwc = dwc.reshape(CONV_ROWS, C_BR)
    dproj, dws, dba, dslg_p, dslb_p = _sgu_bwd(proj, dy, dproj, sgu_ln_g, sgu_ln_b, ws, wst, bsb, big)
    grad_x, dng_p = _dx(dproj, win_all, xs, norm_g, dx2, big)
    rs_blocks = [(*chip, core) for chip in ((1 - px, 1 - py), (1 - px, py), (px, 1 - py), (px, py))
                 for core in (1 - pc, pc)]
    rs_order = jnp.stack([4 * bx + 2 * by + bc for bx, by, bc in rs_blocks]).astype(jnp.int32)
    g_w_in, red, g_w_s, g_cw, loss8 = _dwin_comm(
        ht, dproj, rs_order, [dng_p, dcb_p, dclg_p, dclb_p, dslg_p, dslb_p, dfg_p, loss_p], dwc, dba,
        dws, min(1024, s))
    loss = loss8[0, 0]

    row = lambda a: a.reshape(1, C_BR)
    rows = (ROW_NORM_G, ROW_CONV_B, ROW_CLN_G, ROW_CLN_B, ROW_SLN_G, ROW_SLN_B, ROW_B_S, ROW_FINAL_G)
    big_out, (g_cw, d_cw, nm_cw, nv_cw), (g_r, d_r, m_r, v_r) = _adamw_all(
        [(g_w_in, w_in[0], m_w_in[0], v_w_in[0]), (g_w_out, w_out[0], m_w_out[0], v_w_out[0]),
         (g_w_s, w_s[0], m_w_s[0], v_w_s[0])],
        g_cw, (conv_w[0], m_conv_w[0], v_conv_w[0]), red, rows,
        [norm_g, conv_b, conv_ln_g, conv_ln_b, sgu_ln_g, sgu_ln_b, row(b_s), row(final_g)],
        [m_norm_g, m_conv_b, m_conv_ln_g, m_conv_ln_b, m_sgu_ln_g, m_sgu_ln_b, row(m_b_s), row(m_final_g)],
        [v_norm_g, v_conv_b, v_conv_ln_g, v_conv_ln_b, v_sgu_ln_g, v_sgu_ln_b, row(v_b_s), row(v_final_g)])
    (d_w_in, nm_w_in, nv_w_in), (d_w_out, nm_w_out, nv_w_out), (d_ws, nm_ws, nv_ws) = big_out

    def leaves(r, w_in_l, cw_l, ws_l, w_out_l):
        return (r[0], w_in_l[None], cw_l[None], r[1], r[2], r[3], r[4], r[5],
                ws_l[None], r[6].reshape(1, HEADS, CHUNK), w_out_l[None],
                r[7].reshape(D_MODEL))

    return (loss, grad_x.reshape(1, s, D_MODEL),
            *leaves(g_r, g_w_in, g_cw, g_w_s, g_w_out),
            *leaves(d_r, d_w_in, d_cw, d_ws, d_w_out),
            *leaves(m_r, nm_w_in, nm_cw, nm_ws, nm_w_out),
            *leaves(v_r, nv_w_in, nv_cw, nv_ws, nv_w_out))
```

```python
import functools

import jax
import jax.numpy as jnp
from jax import lax
from jax.experimental import pallas as pl
from jax.experimental.pallas import tpu as pltpu

F32 = jnp.float32
BF16 = jnp.bfloat16
ACT = jnp.bfloat16

D_MODEL = 1024
C_BR = 1024
D_IN = 6 * C_BR
N_DEV = 8
W_BLK = D_IN // N_DEV
HEADS = 8
HEAD_DIM = 128
CHUNK = 128
CONV_WIDTH = 31
CONV_PAD = CONV_WIDTH // 2
HALO = 16
CONV_ROWS = 32
EPS = 1e-6

ADAM_LR = 0.001
ADAM_B1 = 0.9
ADAM_B2 = 0.999
ADAM_EPS = 1e-08
ADAM_WD = 0.01
ADAM_STEP = 10

VMEM_LIMIT = 56 * 1024 * 1024
MESH = pl.DeviceIdType.MESH

ROW_NORM_G, ROW_CONV_B, ROW_CLN_G, ROW_CLN_B, ROW_SLN_G, ROW_SLN_B, ROW_FINAL_G, ROW_B_S, ROW_LOSS = range(9)
ROW_CONV_W = 16
SMALL_ROWS = ROW_CONV_W + CONV_ROWS


def _params(sem=None, **kw):
    return pltpu.CompilerParams(dimension_semantics=sem, vmem_limit_bytes=VMEM_LIMIT, **kw)


def _fold8(a):
    r, n = a.shape
    return a.reshape(r // 8, 8, n).sum(axis=0)


def _sigmoid(z):
    return 0.5 * jnp.tanh(0.5 * z) + 0.5


def _ln_norm(xf):
    mu = jnp.mean(xf, axis=-1, keepdims=True)
    xc = xf - mu
    var = jnp.mean(xc * xc, axis=-1, keepdims=True)
    rstd = lax.rsqrt(var + EPS)
    return xc * rstd, rstd


def _ln_bwd(dy, xhat, rstd, g):
    dxhat = dy * g
    m1 = jnp.mean(dxhat, axis=-1, keepdims=True)
    m2 = jnp.mean(dxhat * xhat, axis=-1, keepdims=True)
    return rstd * (dxhat - m1 - xhat * m2)


def _place():
    return lax.axis_index("x"), lax.axis_index("y"), lax.axis_index("c")


def _proj_ag(x, norm_g, w_in, w_out, conv_w, order, tm):
    s = x.shape[0]
    nt = s // tm

    def body(order_ref, x_ref, g_ref, win_ref, wout_ref, cw_ref,
             proj_ref, ht_ref, win_all, wout_all, cw_all,
             h_ref, win_buf, wout_buf, cw_buf, send_sems, recv_sems, save_sems):
        jj, i = pl.program_id(0), pl.program_id(1)
        x_, y_, c_ = _place()
        me, sibling = (x_, y_, c_), (x_, y_, 1 - c_)
        chips = [(1 - x_, y_), (x_, 1 - y_), (1 - x_, 1 - y_)]
        bufs = (win_buf, wout_buf, cw_buf)
        outs = (win_all, wout_all, cw_all)
        start = i == 0

        def index(px, py, pc):
            return 4 * px + 2 * py + pc

        def copy(a, k, block, to):
            return pltpu.make_async_remote_copy(
                src_ref=bufs[a].at[index(*block)], dst_ref=bufs[a].at[index(*block)],
                send_sem=send_sems.at[a, k], recv_sem=recv_sems.at[a, k],
                device_id=to, device_id_type=MESH)

        def save(a, slot, block):
            return pltpu.make_async_copy(bufs[a].at[index(*block)], outs[a].at[index(*block)], save_sems.at[a, slot])

        def first(a):
            return [copy(a, 0, me, sibling)] + [copy(a, 1 + j, me, (*chip, c_)) for j, chip in enumerate(chips)]

        def saves(a):
            blocks = [me, sibling] + [(*chip, core) for chip in chips for core in (c_, 1 - c_)]
            return [save(a, slot, block) for slot, block in enumerate(blocks)]

        @pl.when((jj == 0) & start)
        def _():
            win_buf[index(*me)] = win_ref[...].astype(BF16)
            wout_buf[index(*me)] = wout_ref[...].astype(BF16)
            cw_buf[index(*me)] = cw_ref[...]
            for a in range(3):
                for cp in first(a):
                    cp.start()
                saves(a)[0].start()

        @pl.when((jj == 1) & start)
        def _():
            copy(0, 0, sibling, me).wait_recv()
            saves(0)[1].start()

        for j, chip in enumerate(chips):
            @pl.when((jj == 2 + 2 * j) & start)
            def _(j=j, chip=chip):
                copy(0, 1 + j, (*chip, c_), me).wait_recv()
                copy(0, 4 + j, (*chip, c_), sibling).start()
                saves(0)[2 + 2 * j].start()

            @pl.when((jj == 3 + 2 * j) & start)
            def _(j=j, chip=chip):
                copy(0, 4 + j, (*chip, 1 - c_), me).wait_recv()
                saves(0)[3 + 2 * j].start()

        @pl.when(jj == 0)
        def _():
            xf = x_ref[...]
            r = lax.rsqrt(jnp.mean(xf * xf, axis=-1, keepdims=True) + EPS)
            hf = xf * r * g_ref[...]
            h_ref[i] = hf.astype(BF16)
            ht_ref[...] = hf.T.astype(BF16)

        proj_ref[...] = jnp.dot(h_ref[i], win_buf[order_ref[jj]], preferred_element_type=F32).astype(ACT)

        @pl.when((jj == N_DEV - 1) & start)
        def _():
            for a in (1, 2):
                for j, chip in enumerate(chips):
                    copy(a, 1 + j, (*chip, c_), me).wait_recv()
                    copy(a, 4 + j, (*chip, c_), sibling).start()
                    saves(a)[2 + 2 * j].start()

        @pl.when((jj == N_DEV - 1) & (i == nt - 1))
        def _():
            passed = [copy(a, 4 + j, (*chip, c_), sibling) for a in range(3) for j, chip in enumerate(chips)]
            for a in (1, 2):
                copy(a, 0, sibling, me).wait_recv()
                saves(a)[1].start()
                for j, chip in enumerate(chips):
                    copy(a, 4 + j, (*chip, 1 - c_), me).wait_recv()
                    saves(a)[3 + 2 * j].start()
            for cp in saves(0) + saves(1) + saves(2):
                cp.wait()
            for cp in first(0) + first(1) + first(2) + passed:
                cp.wait_send()

    vm = pl.BlockSpec(memory_space=pltpu.VMEM)
    hbm = pl.BlockSpec(memory_space=pl.ANY)
    once = lambda jj, i: jnp.where(jj == 0, i, nt - 1)
    stacked = [(N_DEV,) + w.shape for w in (w_in, w_out, conv_w)]
    return pl.pallas_call(
        body, name="proj_ag",
        grid_spec=pltpu.PrefetchScalarGridSpec(
            num_scalar_prefetch=1, grid=(N_DEV, nt),
            in_specs=[pl.BlockSpec((tm, D_MODEL), lambda jj, i, o: (once(jj, i), 0)),
                      pl.BlockSpec((1, D_MODEL), lambda jj, i, o: (0, 0)), vm, vm, vm],
            out_specs=(pl.BlockSpec((tm, W_BLK), lambda jj, i, o: (i, o[jj])),
                       pl.BlockSpec((D_MODEL, tm), lambda jj, i, o: (0, once(jj, i))), hbm, hbm, hbm),
            scratch_shapes=[pltpu.VMEM((nt, tm, D_MODEL), BF16),
                            pltpu.VMEM(stacked[0], BF16), pltpu.VMEM(stacked[1], BF16), pltpu.VMEM(stacked[2], F32),
                            pltpu.SemaphoreType.DMA((3, 7)), pltpu.SemaphoreType.DMA((3, 7)),
                            pltpu.SemaphoreType.DMA((3, N_DEV))]),
        out_shape=(jax.ShapeDtypeStruct((s, D_IN), ACT), jax.ShapeDtypeStruct((D_MODEL, s), BF16),
                   jax.ShapeDtypeStruct(stacked[0], BF16), jax.ShapeDtypeStruct(stacked[1], BF16),
                   jax.ShapeDtypeStruct(stacked[2], F32)),
        compiler_params=_params(("arbitrary", "arbitrary")),
    )(order, x, norm_g, w_in, w_out, conv_w)


def _halo_specs(tm, s, col):
    per = tm // HALO
    last = s // HALO - 1
    return [pl.BlockSpec((HALO, C_BR), lambda i: (jnp.maximum(i * per - 1, 0), col)),
            pl.BlockSpec((tm, C_BR), lambda i: (i, col)),
            pl.BlockSpec((HALO, C_BR), lambda i: (jnp.minimum((i + 1) * per, last), col))]


def _conv_fwd(proj, conv_w3, conv_b, ln_g, ln_b, tm):
    s = proj.shape[0]
    nt = s // tm

    def body(av_p, av_m, av_n, ag_p, ag_m, ag_n, gc_ref, w_ref, cb_ref, lg_ref, lb_ref,
             y_ref, c_ref, ext_ref, cv_ref):
        i = pl.program_id(0)

        def glu(a_ref, g_ref):
            return a_ref[...].astype(F32) * _sigmoid(g_ref[...].astype(F32))

        def tiles(a):
            return a.reshape(a.shape[0], 8, 128)

        ext_ref[0:HALO] = tiles(jnp.where(i > 0, glu(av_p, ag_p), 0.0))
        ext_ref[HALO:HALO + tm] = tiles(glu(av_m, ag_m))
        ext_ref[HALO + tm:] = tiles(jnp.where(i < nt - 1, glu(av_n, ag_n), 0.0))

        nb = 32

        def step(t, carry):
            s0 = t * nb
            accs = [None] * nb
            for k in range(CONV_WIDTH):
                w = w_ref[k]
                for j in range(nb):
                    term = w * ext_ref[s0 + HALO - CONV_PAD + j + k]
                    accs[j] = term if accs[j] is None else accs[j] + term
            for j in range(nb):
                cv_ref[s0 + j] = accs[j]
            return carry

        lax.fori_loop(0, tm // nb, step, 0)
        cv = cv_ref[...].reshape(tm, C_BR) + cb_ref[...]
        c_ref[...] = cv.astype(ACT)
        xhat, _ = _ln_norm(cv)
        ln = xhat * lg_ref[...] + lb_ref[...]
        gc = gc_ref[...].astype(F32)
        y_ref[...] = (ln * _sigmoid(ln) * (gc * _sigmoid(gc))).astype(ACT)

    vec = pl.BlockSpec((1, C_BR), lambda i: (0, 0))
    return pl.pallas_call(
        body, name="conv_fwd", grid=(nt,),
        out_shape=(jax.ShapeDtypeStruct((s, 2 * C_BR), ACT), jax.ShapeDtypeStruct((s, C_BR), ACT)),
        in_specs=_halo_specs(tm, s, 0) + _halo_specs(tm, s, 1)
        + [pl.BlockSpec((tm, C_BR), lambda i: (i, 2)),
           pl.BlockSpec((CONV_ROWS, 8, 128), lambda i: (0, 0, 0)), vec, vec, vec],
        out_specs=(pl.BlockSpec((tm, C_BR), lambda i: (i, 0)), pl.BlockSpec((tm, C_BR), lambda i: (i, 0))),
        scratch_shapes=[pltpu.VMEM((tm + 2 * HALO, 8, 128), F32),
                        pltpu.VMEM((tm, 8, 128), F32)],
        compiler_params=_params(("parallel",)),
    )(proj, proj, proj, proj, proj, proj, proj, conv_w3, conv_b, ln_g, ln_b)


def _sgu_fwd(proj, y, ln_g, ln_b, ws, bsb, tm):
    s = proj.shape[0]

    def body(u_ref, v_ref, gs_ref, y_in, lg_ref, lb_ref, ws_ref, bsb_ref, y_ref):
        del y_in
        xhat, _ = _ln_norm(v_ref[...].astype(F32))
        vn = (xhat * lg_ref[...] + lb_ref[...]).astype(BF16)
        for cidx in range(tm // CHUNK):
            rows = slice(cidx * CHUNK, (cidx + 1) * CHUNK)
            for h in range(HEADS):
                cols = slice(h * HEAD_DIM, (h + 1) * HEAD_DIM)
                mixed = jnp.dot(ws_ref[h], vn[rows, cols], preferred_element_type=F32) + bsb_ref[:, cols]
                gs = gs_ref[rows, cols].astype(F32)
                y_ref[rows, cols] = (u_ref[rows, cols].astype(F32) * mixed * (gs * _sigmoid(gs))).astype(ACT)

    vec = pl.BlockSpec((1, C_BR), lambda i: (0, 0))
    return pl.pallas_call(
        body, name="sgu_fwd", grid=(s // tm,),
        out_shape=jax.ShapeDtypeStruct((s, 2 * C_BR), ACT),
        in_specs=[pl.BlockSpec((tm, C_BR), lambda i: (i, 3)),
                  pl.BlockSpec((tm, C_BR), lambda i: (i, 4)),
                  pl.BlockSpec((tm, C_BR), lambda i: (i, 5)),
                  pl.BlockSpec(memory_space=pl.ANY),
                  vec, vec,
                  pl.BlockSpec((HEADS, CHUNK, CHUNK), lambda i: (0, 0, 0)),
                  pl.BlockSpec((CHUNK, C_BR), lambda i: (0, 0))],
        out_specs=pl.BlockSpec((tm, C_BR), lambda i: (i, 1)),
        input_output_aliases={3: 0},
        compiler_params=_params(("parallel",)),
    )(proj, proj, proj, y, ln_g, ln_b, ws, bsb)


def _out_loss(x, y, wout, final_g, target, tm):
    s = x.shape[0]
    nt = s // tm
    inv_d = 1.0 / D_MODEL

    def body(x_ref, y_ref, w_ref, g_ref, t_ref, dx2_ref, dy_ref, dw_ref, loss_ref, dfg_ref, acc_ref):
        i = pl.program_id(0)

        @pl.when(i == 0)
        def _():
            acc_ref[...] = jnp.zeros_like(acc_ref)
            loss_ref[...] = jnp.zeros_like(loss_ref)
            dfg_ref[...] = jnp.zeros_like(dfg_ref)

        yb = y_ref[...]
        x2 = x_ref[...] + jnp.dot(yb, w_ref[...], preferred_element_type=F32)
        r2 = lax.rsqrt(jnp.mean(x2 * x2, axis=-1, keepdims=True) + EPS)
        n = x2 * r2
        g = g_ref[...]
        e = n * g - t_ref[...]
        loss_ref[...] += _fold8(e * e)
        dout = e * inv_d
        dfg_ref[...] += _fold8(dout * n)
        dn = dout * g
        dx2 = r2 * (dn - n * jnp.mean(dn * n, axis=-1, keepdims=True))
        dx2_ref[...] = dx2
        dxb = dx2.astype(BF16)
        dy_ref[...] = lax.dot_general(dxb, w_ref[...], (((1,), (1,)), ((), ())),
                                      preferred_element_type=F32).astype(ACT)
        acc_ref[...] += lax.dot_general(yb, dxb, (((0,), (0,)), ((), ())), preferred_element_type=F32)

        @pl.when(i == nt - 1)
        def _():
            dw_ref[...] = acc_ref[...].astype(BF16)

    part = pl.BlockSpec((8, D_MODEL), lambda i: (0, 0))
    return pl.pallas_call(
        body, name="out_loss", grid=(nt,),
        out_shape=(jax.ShapeDtypeStruct((s, D_MODEL), F32), jax.ShapeDtypeStruct((s, 2 * C_BR), ACT),
                   jax.ShapeDtypeStruct((2 * C_BR, D_MODEL), BF16),
                   jax.ShapeDtypeStruct((8, D_MODEL), F32), jax.ShapeDtypeStruct((8, D_MODEL), F32)),
        in_specs=[pl.BlockSpec((tm, D_MODEL), lambda i: (i, 0)),
                  pl.BlockSpec((tm, 2 * C_BR), lambda i: (i, 0)),
                  pl.BlockSpec((2 * C_BR, D_MODEL), lambda i: (0, 0), pipeline_mode=pl.Buffered(1)),
                  pl.BlockSpec((1, D_MODEL), lambda i: (0, 0)),
                  pl.BlockSpec((tm, D_MODEL), lambda i: (i, 0))],
        out_specs=(pl.BlockSpec((tm, D_MODEL), lambda i: (i, 0)),
                   pl.BlockSpec((tm, 2 * C_BR), lambda i: (i, 0)),
                   pl.BlockSpec((2 * C_BR, D_MODEL), lambda i: (0, 0), pipeline_mode=pl.Buffered(1)), part, part),
        scratch_shapes=[pltpu.VMEM((2 * C_BR, D_MODEL), F32)],
        compiler_params=_params(("arbitrary",)),
    )(x, y, wout, final_g, target)


def _conv_bwd(proj, cv, dy, conv_w3, ln_g, ln_b, dwout, tm):
    s = proj.shape[0]
    nt = s // tm
    wo_rows = dwout.shape[1]

    def body(av_ref, ag_ref, gc_p, gc_m, gc_n, c_p, c_m, c_n, dy_p, dy_m, dy_n, w_ref, lg_ref, lb_ref,
             dwout_ref, dp_ref, dwc_ref, dcb_ref, dlg_ref, dlb_ref, gwo_ref,
             dce_ref, glu_ref, dgl_ref, land_ref, send_sems, recv_sems, loc_sem):
        i = pl.program_id(0)
        px, py, pc = _place()
        me = 4 * px + 2 * py + pc

        def exchanges():
            out = []
            for rel in range(1, N_DEV):
                qx, qy, qc = px ^ (rel >> 2), py ^ ((rel >> 1) & 1), pc ^ (rel & 1)
                out.append(pltpu.make_async_remote_copy(
                    src_ref=dwout_ref.at[4 * qx + 2 * qy + qc], dst_ref=land_ref.at[me],
                    send_sem=send_sems.at[rel - 1], recv_sem=recv_sems.at[rel - 1],
                    device_id=(qx, qy, qc), device_id_type=MESH))
            return out

        own = pltpu.make_async_copy(dwout_ref.at[me], land_ref.at[me], loc_sem)

        @pl.when(i == 0)
        def _():
            dwc_ref[...] = jnp.zeros_like(dwc_ref)
            dcb_ref[...] = jnp.zeros_like(dcb_ref)
            dlg_ref[...] = jnp.zeros_like(dlg_ref)
            dlb_ref[...] = jnp.zeros_like(dlb_ref)
            own.start()
            for cp in exchanges():
                cp.start()

        def ext(p, m, n):
            return jnp.concatenate([p[...], m[...], n[...]], axis=0).astype(F32)

        main = slice(HALO, HALO + tm)
        cf, gc, dyc = ext(c_p, c_m, c_n), ext(gc_p, gc_m, gc_n), ext(dy_p, dy_m, dy_n)
        xhat, rstd = _ln_norm(cf)
        lg = lg_ref[...]
        ln = xhat * lg + lb_ref[...]
        s_ln, s_gc = _sigmoid(ln), _sigmoid(gc)
        dln = dyc * (gc * s_gc) * (s_ln * (1.0 + ln * (1.0 - s_ln)))
        dp_ref[:, 2 * C_BR:] = (dyc[main] * (ln[main] * s_ln[main])
                                * (s_gc[main] * (1.0 + gc[main] * (1.0 - s_gc[main])))).astype(ACT)
        dlg_ref[...] += _fold8(dln[main] * xhat[main])
        dlb_ref[...] += _fold8(dln[main])
        dc = _ln_bwd(dln, xhat, rstd, lg)
        dcb_ref[...] += _fold8(dc[main])

        def tiles(a):
            return a.reshape(a.shape[0], 8, 128)

        dce_ref[0:HALO] = tiles(jnp.where(i > 0, dc[0:HALO], 0.0))
        dce_ref[HALO:HALO + tm] = tiles(dc[main])
        dce_ref[HALO + tm:] = tiles(jnp.where(i < nt - 1, dc[HALO + tm:], 0.0))
        av = av_ref[...].astype(F32)
        sa = _sigmoid(ag_ref[...].astype(F32))
        glu_ref[...] = tiles(av * sa)

        nb = 16

        def step(t, carry):
            s0 = t * nb
            accs = [None] * nb
            for k in range(CONV_WIDTH):
                w = w_ref[k]
                prods = []
                for j in range(nb):
                    v = dce_ref[s0 + HALO + CONV_PAD + j - k]
                    term = w * v
                    accs[j] = term if accs[j] is None else accs[j] + term
                    prods.append(glu_ref[s0 + j] * v)
                while len(prods) > 1:
                    prods = [p + q for p, q in zip(prods[::2], prods[1::2])]
                dwc_ref[k] += prods[0]
            for j in range(nb):
                dgl_ref[s0 + j] = accs[j]
            return carry

        lax.fori_loop(0, tm // nb, step, 0)
        dglu = dgl_ref[...].reshape(tm, C_BR)
        dp_ref[:, 0:C_BR] = (dglu * sa).astype(ACT)
        dp_ref[:, C_BR:2 * C_BR] = (dglu * av * sa * (1.0 - sa)).astype(ACT)

        @pl.when(i == nt - 1)
        def _():
            copies = exchanges()
            own.wait()
            for cp in copies:
                cp.wait_recv()

            def step(t, carry):
                sl = pl.ds(pl.multiple_of(t * 64, 64), 64)
                g = land_ref[0, sl, :].astype(F32)
                for d in range(1, N_DEV):
                    g = g + land_ref[d, sl, :].astype(F32)
                gwo_ref[sl, :] = g
                return carry

            lax.fori_loop(0, wo_rows // 64, step, 0)
            for cp in copies:
                cp.wait_send()

    vec = pl.BlockSpec((1, C_BR), lambda i: (0, 0))
    part = pl.BlockSpec((8, C_BR), lambda i: (0, 0))
    return pl.pallas_call(
        body, name="conv_bwd", grid=(nt,),
        out_shape=(jax.ShapeDtypeStruct((s, D_IN), ACT), jax.ShapeDtypeStruct((CONV_ROWS, 8, 128), F32),
                   jax.ShapeDtypeStruct((8, C_BR), F32), jax.ShapeDtypeStruct((8, C_BR), F32),
                   jax.ShapeDtypeStruct((8, C_BR), F32), jax.ShapeDtypeStruct(dwout.shape[1:], F32)),
        in_specs=[pl.BlockSpec((tm, C_BR), lambda i: (i, 0)), pl.BlockSpec((tm, C_BR), lambda i: (i, 1))]
        + _halo_specs(tm, s, 2) + _halo_specs(tm, s, 0) + _halo_specs(tm, s, 0)
        + [pl.BlockSpec((CONV_ROWS, 8, 128), lambda i: (0, 0, 0)), vec, vec, pl.BlockSpec(memory_space=pl.ANY)],
        out_specs=(pl.BlockSpec((tm, 3 * C_BR), lambda i: (i, 0)),
                   pl.BlockSpec((CONV_ROWS, 8, 128), lambda i: (0, 0, 0)), part, part, part,
                   pl.BlockSpec(memory_space=pltpu.VMEM)),
        scratch_shapes=[pltpu.VMEM((tm + 2 * HALO, 8, 128), F32),
                        pltpu.VMEM((tm, 8, 128), F32), pltpu.VMEM((tm, 8, 128), F32),
                        pltpu.VMEM(dwout.shape, BF16),
                        pltpu.SemaphoreType.DMA((N_DEV - 1,)), pltpu.SemaphoreType.DMA((N_DEV - 1,)),
                        pltpu.SemaphoreType.DMA],
        compiler_params=_params(("arbitrary",)),
    )(proj, proj, proj, proj, proj, cv, cv, cv, dy, dy, dy, conv_w3, ln_g, ln_b, dwout)


def _sgu_bwd(proj, dy, dproj, ln_g, ln_b, ws, wst, bsb, tm):
    s = proj.shape[0]

    def body(u_ref, v_ref, gs_ref, dy_ref, dp_in, lg_ref, lb_ref, ws_ref, wst_ref, bsb_ref,
             dp_ref, dws_ref, dba_ref, dlg_ref, dlb_ref, dvn_ref):
        del dp_in
        i = pl.program_id(0)

        @pl.when(i == 0)
        def _():
            dws_ref[...] = jnp.zeros_like(dws_ref)
            dba_ref[...] = jnp.zeros_like(dba_ref)
            dlg_ref[...] = jnp.zeros_like(dlg_ref)
            dlb_ref[...] = jnp.zeros_like(dlb_ref)

        xhat, rstd = _ln_norm(v_ref[...].astype(F32))
        lg = lg_ref[...]
        vn = (xhat * lg + lb_ref[...]).astype(BF16)
        for cidx in range(tm // CHUNK):
            rows = slice(cidx * CHUNK, (cidx + 1) * CHUNK)
            for h in range(HEADS):
                cols = slice(h * HEAD_DIM, (h + 1) * HEAD_DIM)
                ocols = slice(C_BR + h * HEAD_DIM, C_BR + (h + 1) * HEAD_DIM)
                gcols = slice(2 * C_BR + h * HEAD_DIM, 2 * C_BR + (h + 1) * HEAD_DIM)
                vb = vn[rows, cols]
                mixed = jnp.dot(ws_ref[h], vb, preferred_element_type=F32) + bsb_ref[:, cols]
                gs = gs_ref[rows, cols].astype(F32)
                sg = _sigmoid(gs)
                u = u_ref[rows, cols].astype(F32)
                dyb = dy_ref[rows, cols].astype(F32)
                t = dyb * (gs * sg)
                dp_ref[rows, cols] = (t * mixed).astype(ACT)
                dp_ref[rows, gcols] = (dyb * u * mixed * (sg * (1.0 + gs * (1.0 - sg)))).astype(ACT)
                dm = t * u
                dmb = dm.astype(BF16)
                dvn_ref[rows, cols] = jnp.dot(wst_ref[h], dmb, preferred_element_type=F32)
                dws_ref[h] += lax.dot_general(dmb, vb, (((1,), (1,)), ((), ())), preferred_element_type=F32)
                dba_ref[:, cols] += dm
        dvn = dvn_ref[...]
        dlg_ref[...] += _fold8(dvn * xhat)
        dlb_ref[...] += _fold8(dvn)
        dp_ref[:, C_BR:2 * C_BR] = _ln_bwd(dvn, xhat, rstd, lg).astype(ACT)

    vec = pl.BlockSpec((1, C_BR), lambda i: (0, 0))
    part = pl.BlockSpec((8, C_BR), lambda i: (0, 0))
    wsp = pl.BlockSpec((HEADS, CHUNK, CHUNK), lambda i: (0, 0, 0))
    return pl.pallas_call(
        body, name="sgu_bwd", grid=(s // tm,),
        out_shape=(jax.ShapeDtypeStruct((s, D_IN), ACT), jax.ShapeDtypeStruct((HEADS, CHUNK, CHUNK), F32),
                   jax.ShapeDtypeStruct((CHUNK, C_BR), F32), jax.ShapeDtypeStruct((8, C_BR), F32),
                   jax.ShapeDtypeStruct((8, C_BR), F32)),
        in_specs=[pl.BlockSpec((tm, C_BR), lambda i: (i, 3)),
                  pl.BlockSpec((tm, C_BR), lambda i: (i, 4)),
                  pl.BlockSpec((tm, C_BR), lambda i: (i, 5)),
                  pl.BlockSpec((tm, C_BR), lambda i: (i, 1)),
                  pl.BlockSpec(memory_space=pl.ANY),
                  vec, vec, wsp, wsp, pl.BlockSpec((CHUNK, C_BR), lambda i: (0, 0))],
        out_specs=(pl.BlockSpec((tm, 3 * C_BR), lambda i: (i, 1)), wsp,
                   pl.BlockSpec((CHUNK, C_BR), lambda i: (0, 0)), part, part),
        scratch_shapes=[pltpu.VMEM((tm, C_BR), F32)],
        input_output_aliases={4: 0},
        compiler_params=_params(("arbitrary",)),
    )(proj, proj, proj, dy, dproj, ln_g, ln_b, ws, wst, bsb)


def _dx(dproj, win_all, x, norm_g, dx2, tm):
    s = x.shape[0]
    nt = s // tm
    ring = 3

    def body(dp_hbm, w_ref, x_ref, g_ref, dx2_ref, gx_ref, dng_ref, dp_buf, dp_sems):
        i = pl.program_id(0)

        def fetch(t):
            if isinstance(t, int):
                slot, start = t % ring, t * tm
            else:
                slot, start = lax.rem(t, ring), pl.multiple_of(t * tm, tm)
            return pltpu.make_async_copy(dp_hbm.at[pl.ds(start, tm), :], dp_buf.at[slot], dp_sems.at[slot])

        @pl.when(i == 0)
        def _():
            dng_ref[...] = jnp.zeros_like(dng_ref)
            for t in range(min(ring - 1, nt)):
                fetch(t).start()

        @pl.when(i + ring - 1 < nt)
        def _():
            fetch(i + ring - 1).start()

        fetch(i).wait()
        dp_ref = dp_buf.at[lax.rem(i, ring)]
        dh = None
        for j in range(N_DEV):
            term = lax.dot_general(dp_ref[:, j * W_BLK:(j + 1) * W_BLK], w_ref[j],
                                   (((1,), (1,)), ((), ())), preferred_element_type=F32)
            dh = term if dh is None else dh + term
        xf = x_ref[...]
        r = lax.rsqrt(jnp.mean(xf * xf, axis=-1, keepdims=True) + EPS)
        n = xf * r
        dng_ref[...] += _fold8(dh * n)
        dn = dh * g_ref[...]
        gx_ref[...] = dx2_ref[...] + r * (dn - n * jnp.mean(dn * n, axis=-1, keepdims=True))

    return pl.pallas_call(
        body, name="dx", grid=(s // tm,),
        out_shape=(jax.ShapeDtypeStruct((s, D_MODEL), F32), jax.ShapeDtypeStruct((8, D_MODEL), F32)),
        in_specs=[pl.BlockSpec(memory_space=pl.ANY),
                  pl.BlockSpec((N_DEV, D_MODEL, W_BLK), lambda i: (0, 0, 0), pipeline_mode=pl.Buffered(1)),
                  pl.BlockSpec((tm, D_MODEL), lambda i: (i, 0)),
                  pl.BlockSpec((1, D_MODEL), lambda i: (0, 0)),
                  pl.BlockSpec((tm, D_MODEL), lambda i: (i, 0))],
        out_specs=(pl.BlockSpec((tm, D_MODEL), lambda i: (i, 0)), pl.BlockSpec((8, D_MODEL), lambda i: (0, 0))),
        scratch_shapes=[pltpu.VMEM((ring, tm, D_IN), ACT), pltpu.SemaphoreType.DMA((ring,))],
        compiler_params=_params(("arbitrary",)),
    )(dproj, win_all, x, norm_g, dx2)


def _dwin_comm(ht, dproj, order, parts, dwc, dba, dws, tk):
    s = ht.shape[1]
    nk = s // tk
    n_part = len(parts)

    def body(*refs):
        order_ref, ht_ref, dp_ref = refs[:3]
        del order_ref
        part_refs = refs[3:3 + n_part]
        dwc_ref, dba_ref, dws_ref = refs[3 + n_part:6 + n_part]
        gw_ref, red_ref, wsr_ref, cws_ref, loss_ref = refs[6 + n_part:11 + n_part]
        (acc_ref, all1_ref, all2_ref, out_s, land_s, out_x, land_x,
         send_sems, recv_sems, send_s, recv_s, send_x, recv_x) = refs[11 + n_part:]
        jj, k = pl.program_id(0), pl.program_id(1)
        x, y, c = _place()
        me = 4 * x + 2 * y + c
        sibling = (x, y, 1 - c)
        chips = [(1 - x, 1 - y), (1 - x, y), (x, 1 - y)]
        last = k == nk - 1

        def exchanges():
            out = []
            for rel in range(1, N_DEV):
                peer = (x ^ (rel >> 2), y ^ ((rel >> 1) & 1), c ^ (rel & 1))
                for a, buf in enumerate((all1_ref, all2_ref)):
                    out.append(pltpu.make_async_remote_copy(
                        src_ref=buf.at[me], dst_ref=buf.at[me],
                        send_sem=send_sems.at[a, rel - 1], recv_sem=recv_sems.at[a, rel - 1],
                        device_id=peer, device_id_type=MESH))
            return out

        def to_sibling(slot):
            return pltpu.make_async_remote_copy(
                src_ref=out_s.at[slot], dst_ref=land_s.at[slot],
                send_sem=send_s.at[slot], recv_sem=recv_s.at[slot], device_id=sibling, device_id_type=MESH)

        def to_chip(slot):
            return pltpu.make_async_remote_copy(
                src_ref=out_x.at[slot], dst_ref=land_x.at[slot],
                send_sem=send_x.at[slot], recv_sem=recv_x.at[slot],
                device_id=(*chips[slot], c), device_id_type=MESH)

        @pl.when((jj == 0) & (k == 0))
        def _():
            all1_ref[me] = jnp.zeros((SMALL_ROWS, C_BR), F32)
            for row, p_ref in zip((ROW_NORM_G, ROW_CONV_B, ROW_CLN_G, ROW_CLN_B, ROW_SLN_G, ROW_SLN_B,
                                   ROW_FINAL_G, ROW_LOSS), part_refs):
                all1_ref[me, row:row + 1, :] = jnp.sum(p_ref[...], axis=0, keepdims=True)
            ones = jnp.ones((8, HEAD_DIM), F32)
            brow = [lax.dot_general(ones, dba_ref[:, h * HEAD_DIM:(h + 1) * HEAD_DIM], (((1,), (1,)), ((), ())),
                                    precision=lax.Precision.HIGHEST, preferred_element_type=F32)[0:1]
                    for h in range(HEADS)]
            all1_ref[me, ROW_B_S:ROW_B_S + 1, :] = jnp.concatenate(brow, axis=1)
            all1_ref[me, ROW_CONV_W:ROW_CONV_W + CONV_ROWS, :] = dwc_ref[...]
            all2_ref[me] = dws_ref[...]
            for cp in exchanges():
                cp.start()

        @pl.when(k == 0)
        def _():
            acc_ref[...] = jnp.zeros_like(acc_ref)

        acc_ref[...] += jnp.dot(ht_ref[...], dp_ref[...], preferred_element_type=F32)

        for slot in range(4):
            @pl.when((jj == 2 * slot) & last)
            def _(slot=slot):
                out_s[slot] = acc_ref[...].astype(BF16)
                to_sibling(slot).start()

        for slot in range(3):
            @pl.when((jj == 2 * slot + 1) & last)
            def _(slot=slot):
                to_sibling(slot).wait_recv()
                out_x[slot] = (acc_ref[...] + land_s[slot].astype(F32)).astype(BF16)
                to_chip(slot).start()

        @pl.when((jj == N_DEV - 1) & last)
        def _():
            to_sibling(3).wait_recv()
            total = acc_ref[...] + land_s[3].astype(F32)
            for slot in range(3):
                to_chip(slot).wait_recv()
                total = total + land_x[slot].astype(F32)
            gw_ref[...] = total

            copies = exchanges()
            for cp in copies:
                cp.wait_recv()
            tot = all1_ref[0]
            for d in range(1, N_DEV):
                tot = tot + all1_ref[d]
            red_ref[...] = tot
            loss_ref[...] = jnp.broadcast_to(
                jnp.sum(tot[ROW_LOSS:ROW_LOSS + 1, :], axis=1, keepdims=True) * (0.5 / D_MODEL), loss_ref.shape)
            shard = jnp.zeros(cws_ref.shape, F32)
            for d in range(N_DEV):
                shard = jnp.where(me == d, tot[ROW_CONV_W:ROW_CONV_W + CONV_ROWS, d * 128:(d + 1) * 128], shard)
            cws_ref[...] = shard
            tot2 = all2_ref[0]
            for d in range(1, N_DEV):
                tot2 = tot2 + all2_ref[d]
            wsr_ref[...] = tot2
            for cp in copies + [to_sibling(slot) for slot in range(4)] + [to_chip(slot) for slot in range(3)]:
                cp.wait_send()

    vm = pl.BlockSpec(memory_space=pltpu.VMEM)
    blk = (D_MODEL, W_BLK)
    return pl.pallas_call(
        body, name="dwin",
        grid_spec=pltpu.PrefetchScalarGridSpec(
            num_scalar_prefetch=1, grid=(N_DEV, nk),
            in_specs=[pl.BlockSpec((D_MODEL, tk), lambda jj, k, o: (0, k)),
                      pl.BlockSpec((tk, W_BLK), lambda jj, k, o: (k, o[jj]))] + [vm] * (n_part + 3),
            out_specs=(vm, vm, vm, vm, vm),
            scratch_shapes=[pltpu.VMEM(blk, F32),
                            pltpu.VMEM((N_DEV, SMALL_ROWS, C_BR), F32), pltpu.VMEM((N_DEV,) + dws.shape, F32),
                            pltpu.VMEM((4,) + blk, BF16), pltpu.VMEM((4,) + blk, BF16),
                            pltpu.VMEM((3,) + blk, BF16), pltpu.VMEM((3,) + blk, BF16),
                            pltpu.SemaphoreType.DMA((2, N_DEV - 1)), pltpu.SemaphoreType.DMA((2, N_DEV - 1)),
                            pltpu.SemaphoreType.DMA((4,)), pltpu.SemaphoreType.DMA((4,)),
                            pltpu.SemaphoreType.DMA((3,)), pltpu.SemaphoreType.DMA((3,))]),
        out_shape=(jax.ShapeDtypeStruct(blk, F32),
                   jax.ShapeDtypeStruct((SMALL_ROWS, C_BR), F32), jax.ShapeDtypeStruct(dws.shape, F32),
                   jax.ShapeDtypeStruct((CONV_ROWS, 128), F32), jax.ShapeDtypeStruct((8, 128), F32)),
        compiler_params=_params(("arbitrary", "arbitrary")),
    )(order, ht, dproj, *parts, dwc, dba, dws)


def _adamw_math(w, g, m, v):
    m = ADAM_B1 * m + (1.0 - ADAM_B1) * g
    v = ADAM_B2 * v + (1.0 - ADAM_B2) * (g * g)
    m_hat = m / (1.0 - ADAM_B1 ** ADAM_STEP)
    v_hat = v / (1.0 - ADAM_B2 ** ADAM_STEP)
    delta = -ADAM_LR * (m_hat / (jnp.sqrt(v_hat) + ADAM_EPS) + ADAM_WD * w)
    return delta, m, v


def _adamw_all(groups, g_conv_w, conv_wmv, red, rows, ws, ms, vs):
    ng, n = len(groups), len(rows)
    rchunk = 64

    def body(*refs):
        big = refs[:4 * ng]
        gcw_ref, cw_ref, cm_ref, cv_ref, red_ref = refs[4 * ng:4 * ng + 5]
        small = refs[4 * ng + 5:4 * ng + 5 + 3 * n]
        w_refs, m_refs, v_refs = small[:n], small[n:2 * n], small[2 * n:]
        outs = refs[4 * ng + 5 + 3 * n:]
        for t in range(ng):
            g_ref, w_ref, m_ref, v_ref = big[4 * t:4 * t + 4]
            d_ref, mo_ref, vo_ref = outs[3 * t:3 * t + 3]
            lead = w_ref.shape[0]
            chunks = lead // rchunk if len(w_ref.shape) == 2 else lead

            def step(c, carry, refs_=(g_ref, w_ref, m_ref, v_ref, d_ref, mo_ref, vo_ref), two_d=len(w_ref.shape) == 2):
                sl = pl.ds(pl.multiple_of(c * rchunk, rchunk), rchunk) if two_d else c
                g_, w_, m_, v_, d_, mo_, vo_ = refs_
                d_[sl], mo_[sl], vo_[sl] = _adamw_math(w_[sl], g_[sl], m_[sl], v_[sl])
                return carry

            lax.fori_loop(0, chunks, step, 0)
        g = gcw_ref[0:CONV_WIDTH, :]
        o = 3 * ng
        outs[o][...] = g
        outs[o + 1][...], outs[o + 2][...], outs[o + 3][...] = _adamw_math(cw_ref[...], g, cm_ref[...], cv_ref[...])
        o += 4
        for t, row in enumerate(rows):
            g = red_ref[row:row + 1, :]
            delta, m, v = _adamw_math(w_refs[t][...], g, m_refs[t][...], v_refs[t][...])
            outs[o + t][...] = g
            outs[o + n + t][...] = delta
            outs[o + 2 * n + t][...] = m
            outs[o + 3 * n + t][...] = v

    vm = pl.BlockSpec(memory_space=pltpu.VMEM)
    sds = lambda a: jax.ShapeDtypeStruct(a.shape, F32)
    out_shape = []
    for grp in groups:
        out_shape += [sds(grp[1])] * 3
    out_shape += [sds(conv_wmv[0])] * 4
    out_shape += [jax.ShapeDtypeStruct((1, C_BR), F32)] * (4 * n)
    args = [a for grp in groups for a in grp] + [g_conv_w, *conv_wmv, red, *ws, *ms, *vs]
    res = pl.pallas_call(
        body, name="adamw", out_shape=tuple(out_shape),
        in_specs=[vm] * len(args), out_specs=(vm,) * len(out_shape),
        compiler_params=_params(),
    )(*args)
    big_out = [res[3 * t:3 * t + 3] for t in range(ng)]
    o = 3 * ng
    conv_out = res[o:o + 4]
    o += 4
    return big_out, conv_out, (res[o:o + n], res[o + n:o + 2 * n], res[o + 2 * n:o + 3 * n], res[o + 3 * n:o + 4 * n])


def kernel(x, norm_g, w_in, conv_w, conv_b, conv_ln_g, conv_ln_b, sgu_ln_g, sgu_ln_b, w_s, b_s, w_out, final_g, loss_target, m_norm_g, m_w_in, m_conv_w, m_conv_b, m_conv_ln_g, m_conv_ln_b, m_sgu_ln_g, m_sgu_ln_b, m_w_s, m_b_s, m_w_out, m_final_g, v_norm_g, v_w_in, v_conv_w, v_conv_b, v_conv_ln_g, v_conv_ln_b, v_sgu_ln_g, v_sgu_ln_b, v_w_s, v_b_s, v_w_out, v_final_g):
    s = x.shape[1]
    xs = x.reshape(s, D_MODEL)
    tgt = loss_target.reshape(s, D_MODEL)
    tm = min(256, s)

    cw_pad = jnp.pad(conv_w[0], ((0, CONV_ROWS - CONV_WIDTH), (0, 0)))
    px, py, pc = _place()
    blocks = [(px, py, pc), (px, py, 1 - pc)]
    blocks += [(*chip, core) for chip in ((1 - px, py), (px, 1 - py), (1 - px, 1 - py)) for core in (pc, 1 - pc)]
    order = jnp.stack([4 * bx + 2 * by + bc for bx, by, bc in blocks]).astype(jnp.int32)
    proj, ht, win_all, wout_all, cw_all = _proj_ag(xs, norm_g, w_in[0], w_out[0], cw_pad, order, min(1024, s))
    wout_full = wout_all.reshape(2 * C_BR, D_MODEL)
    cw_tiles = jnp.transpose(cw_all, (1, 0, 2))

    ws = w_s[0].astype(BF16)
    wst = jnp.transpose(w_s[0], (0, 2, 1)).astype(BF16)
    bsb = jnp.repeat(jnp.transpose(b_s[0]), HEAD_DIM, axis=1)
    fg = final_g.reshape(1, D_MODEL)

    big = min(512, s)
    y, cv = _conv_fwd(proj, cw_tiles, conv_b, conv_ln_g, conv_ln_b, big)
    y = _sgu_fwd(proj, y, sgu_ln_g, sgu_ln_b, ws, bsb, big)
    dx2, dy, dwout, loss_p, dfg_p = _out_loss(xs, y, wout_full, fg, tgt, min(512, s))
    dproj, dwc, dcb_p, dclg_p, dclb_p, g_w_out = _conv_bwd(
        proj, cv, dy, cw_tiles, conv_ln_g, conv_ln_b, dwout.reshape(N_DEV, 2 * C_BR // N_DEV, D_MODEL), big)
    dwc = dwc.reshape(CONV_ROWS, C_BR)
    dproj, dws, dba, dslg_p, dslb_p = _sgu_bwd(proj, dy, dproj, sgu_ln_g, sgu_ln_b, ws, wst, bsb, big)
    grad_x, dng_p = _dx(dproj, win_all, xs, norm_g, dx2, big)
    rs_blocks = [(*chip, core) for chip in ((1 - px, 1 - py), (1 - px, py), (px, 1 - py), (px, py))
                 for core in (1 - pc, pc)]
    rs_order = jnp.stack([4 * bx + 2 * by + bc for bx, by, bc in rs_blocks]).astype(jnp.int32)
    g_w_in, red, g_w_s, g_cw, loss8 = _dwin_comm(
        ht, dproj, rs_order, [dng_p, dcb_p, dclg_p, dclb_p, dslg_p, dslb_p, dfg_p, loss_p], dwc, dba,
        dws, min(1024, s))
    loss = loss8[0, 0]

    row = lambda a: a.reshape(1, C_BR)
    rows = (ROW_NORM_G, ROW_CONV_B, ROW_CLN_G, ROW_CLN_B, ROW_SLN_G, ROW_SLN_B, ROW_B_S, ROW_FINAL_G)
    big_out, (g_cw, d_cw, nm_cw, nv_cw), (g_r, d_r, m_r, v_r) = _adamw_all(
        [(g_w_in, w_in[0], m_w_in[0], v_w_in[0]), (g_w_out, w_out[0], m_w_out[0], v_w_out[0]),
         (g_w_s, w_s[0], m_w_s[0], v_w_s[0])],
        g_cw, (conv_w[0], m_conv_w[0], v_conv_w[0]), red, rows,
        [norm_g, conv_b, conv_ln_g, conv_ln_b, sgu_ln_g, sgu_ln_b, row(b_s), row(final_g)],
        [m_norm_g, m_conv_b, m_conv_ln_g, m_conv_ln_b, m_sgu_ln_g, m_sgu_ln_b, row(m_b_s), row(m_final_g)],
        [v_norm_g, v_conv_b, v_conv_ln_g, v_conv_ln_b, v_sgu_ln_g, v_sgu_ln_b, row(v_b_s), row(v_final_g)])
    (d_w_in, nm_w_in, nv_w_in), (d_w_out, nm_w_out, nv_w_out), (d_ws, nm_ws, nv_ws) = big_out

    def leaves(r, w_in_l, cw_l, ws_l, w_out_l):
        return (r[0], w_in_l[None], cw_l[None], r[1], r[2], r[3], r[4], r[5],
                ws_l[None], r[6].reshape(1, HEADS, CHUNK), w_out_l[None],
                r[7].reshape(D_MODEL))

    return (loss, grad_x.reshape(1, s, D_MODEL),
            *leaves(g_r, g_w_in, g_cw, g_w_s, g_w_out),
            *leaves(d_r, d_w_in, d_cw, d_ws, d_w_out),
            *leaves(m_r, nm_w_in, nm_cw, nm_ws, nm_w_out),
            *leaves(v_r, nv_w_in, nv_cw, nv_ws, nv_w_out))
```

```python
import functools

import jax
import jax.numpy as jnp
from jax import lax
from jax.experimental import pallas as pl
from jax.experimental.pallas import tpu as pltpu

F32 = jnp.float32
BF16 = jnp.bfloat16
ACT = jnp.bfloat16

D_MODEL = 1024
C_BR = 1024
D_IN = 6 * C_BR
N_DEV = 8
W_BLK = D_IN // N_DEV
HEADS = 8
HEAD_DIM = 128
CHUNK = 128
CONV_WIDTH = 31
CONV_PAD = CONV_WIDTH // 2
HALO = 16
CONV_ROWS = 32
EPS = 1e-6

ADAM_LR = 0.001
ADAM_B1 = 0.9
ADAM_B2 = 0.999
ADAM_EPS = 1e-08
ADAM_WD = 0.01
ADAM_STEP = 10

VMEM_LIMIT = 56 * 1024 * 1024
MESH = pl.DeviceIdType.MESH

ROW_NORM_G, ROW_CONV_B, ROW_CLN_G, ROW_CLN_B, ROW_SLN_G, ROW_SLN_B, ROW_FINAL_G, ROW_B_S, ROW_LOSS = range(9)
ROW_CONV_W = 16
SMALL_ROWS = ROW_CONV_W + CONV_ROWS


def _params(sem=None, **kw):
    return pltpu.CompilerParams(dimension_semantics=sem, vmem_limit_bytes=VMEM_LIMIT, **kw)


def _fold8(a):
    r, n = a.shape
    return a.reshape(r // 8, 8, n).sum(axis=0)


def _sigmoid(z):
    return 0.5 * jnp.tanh(0.5 * z) + 0.5


def _ln_norm(xf):
    mu = jnp.mean(xf, axis=-1, keepdims=True)
    xc = xf - mu
    var = jnp.mean(xc * xc, axis=-1, keepdims=True)
    rstd = lax.rsqrt(var + EPS)
    return xc * rstd, rstd


def _ln_bwd(dy, xhat, rstd, g):
    dxhat = dy * g
    m1 = jnp.mean(dxhat, axis=-1, keepdims=True)
    m2 = jnp.mean(dxhat * xhat, axis=-1, keepdims=True)
    return rstd * (dxhat - m1 - xhat * m2)


def _place():
    return lax.axis_index("x"), lax.axis_index("y"), lax.axis_index("c")


def _proj_ag(x, norm_g, w_in, w_out, conv_w, order, tm):
    s = x.shape[0]
    nt = s // tm

    def body(order_ref, x_ref, g_ref, win_ref, wout_ref, cw_ref,
             proj_ref, ht_ref, win_all, wout_all, cw_all,
             h_ref, win_buf, wout_buf, cw_buf, send_sems, recv_sems, save_sems):
        jj, i = pl.program_id(0), pl.program_id(1)
        x_, y_, c_ = _place()
        me, sibling = (x_, y_, c_), (x_, y_, 1 - c_)
        chips = [(1 - x_, y_), (x_, 1 - y_), (1 - x_, 1 - y_)]
        bufs = (win_buf, wout_buf, cw_buf)
        outs = (win_all, wout_all, cw_all)
        start = i == 0

        def index(px, py, pc):
            return 4 * px + 2 * py + pc

        def copy(a, k, block, to):
            return pltpu.make_async_remote_copy(
                src_ref=bufs[a].at[index(*block)], dst_ref=bufs[a].at[index(*block)],
                send_sem=send_sems.at[a, k], recv_sem=recv_sems.at[a, k],
                device_id=to, device_id_type=MESH)

        def save(a, slot, block):
            return pltpu.make_async_copy(bufs[a].at[index(*block)], outs[a].at[index(*block)], save_sems.at[a, slot])

        def first(a):
            return [copy(a, 0, me, sibling)] + [copy(a, 1 + j, me, (*chip, c_)) for j, chip in enumerate(chips)]

        def saves(a):
            blocks = [me, sibling] + [(*chip, core) for chip in chips for core in (c_, 1 - c_)]
            return [save(a, slot, block) for slot, block in enumerate(blocks)]

        @pl.when((jj == 0) & start)
        def _():
            win_buf[index(*me)] = win_ref[...].astype(BF16)
            wout_buf[index(*me)] = wout_ref[...].astype(BF16)
            cw_buf[index(*me)] = cw_ref[...]
            for a in range(3):
                for cp in first(a):
                    cp.start()
                saves(a)[0].start()

        @pl.when((jj == 1) & start)
        def _():
            copy(0, 0, sibling, me).wait_recv()
            saves(0)[1].start()

        for j, chip in enumerate(chips):
            @pl.when((jj == 2 + 2 * j) & start)
            def _(j=j, chip=chip):
                copy(0, 1 + j, (*chip, c_), me).wait_recv()
                copy(0, 4 + j, (*chip, c_), sibling).start()
                saves(0)[2 + 2 * j].start()

            @pl.when((jj == 3 + 2 * j) & start)
            def _(j=j, chip=chip):
                copy(0, 4 + j, (*chip, 1 - c_), me).wait_recv()
                saves(0)[3 + 2 * j].start()

        @pl.when(jj == 0)
        def _():
            xf = x_ref[...]
            r = lax.rsqrt(jnp.mean(xf * xf, axis=-1, keepdims=True) + EPS)
            hf = xf * r * g_ref[...]
            h_ref[i] = hf.astype(BF16)
            ht_ref[...] = hf.T.astype(BF16)

        proj_ref[...] = jnp.dot(h_ref[i], win_buf[order_ref[jj]], preferred_element_type=F32).astype(ACT)

        @pl.when((jj == N_DEV - 1) & start)
        def _():
            for a in (1, 2):
                for j, chip in enumerate(chips):
                    copy(a, 1 + j, (*chip, c_), me).wait_recv()
                    copy(a, 4 + j, (*chip, c_), sibling).start()
                    saves(a)[2 + 2 * j].start()

        @pl.when((jj == N_DEV - 1) & (i == nt - 1))
        def _():
            passed = [copy(a, 4 + j, (*chip, c_), sibling) for a in range(3) for j, chip in enumerate(chips)]
            for a in (1, 2):
                copy(a, 0, sibling, me).wait_recv()
                saves(a)[1].start()
                for j, chip in enumerate(chips):
                    copy(a, 4 + j, (*chip, 1 - c_), me).wait_recv()
                    saves(a)[3 + 2 * j].start()
            for cp in saves(0) + saves(1) + saves(2):
                cp.wait()
            for cp in first(0) + first(1) + first(2) + passed:
                cp.wait_send()

    vm = pl.BlockSpec(memory_space=pltpu.VMEM)
    hbm = pl.BlockSpec(memory_space=pl.ANY)
    once = lambda jj, i: jnp.where(jj == 0, i, nt - 1)
    stacked = [(N_DEV,) + w.shape for w in (w_in, w_out, conv_w)]
    return pl.pallas_call(
        body, name="proj_ag",
        grid_spec=pltpu.PrefetchScalarGridSpec(
            num_scalar_prefetch=1, grid=(N_DEV, nt),
            in_specs=[pl.BlockSpec((tm, D_MODEL), lambda jj, i, o: (once(jj, i), 0)),
                      pl.BlockSpec((1, D_MODEL), lambda jj, i, o: (0, 0)), vm, vm, vm],
            out_specs=(pl.BlockSpec((tm, W_BLK), lambda jj, i, o: (i, o[jj])),
                       pl.BlockSpec((D_MODEL, tm), lambda jj, i, o: (0, once(jj, i))), hbm, hbm, hbm),
            scratch_shapes=[pltpu.VMEM((nt, tm, D_MODEL), BF16),
                            pltpu.VMEM(stacked[0], BF16), pltpu.VMEM(stacked[1], BF16), pltpu.VMEM(stacked[2], F32),
                            pltpu.SemaphoreType.DMA((3, 7)), pltpu.SemaphoreType.DMA((3, 7)),
                            pltpu.SemaphoreType.DMA((3, N_DEV))]),
        out_shape=(jax.ShapeDtypeStruct((s, D_IN), ACT), jax.ShapeDtypeStruct((D_MODEL, s), BF16),
                   jax.ShapeDtypeStruct(stacked[0], BF16), jax.ShapeDtypeStruct(stacked[1], BF16),
                   jax.ShapeDtypeStruct(stacked[2], F32)),
        compiler_params=_params(("arbitrary", "arbitrary")),
    )(order, x, norm_g, w_in, w_out, conv_w)


def _halo_specs(tm, s, col):
    per = tm // HALO
    last = s // HALO - 1
    return [pl.BlockSpec((HALO, C_BR), lambda i: (jnp.maximum(i * per - 1, 0), col)),
            pl.BlockSpec((tm, C_BR), lambda i: (i, col)),
            pl.BlockSpec((HALO, C_BR), lambda i: (jnp.minimum((i + 1) * per, last), col))]


def _conv_fwd(proj, conv_w3, conv_b, ln_g, ln_b, tm):
    s = proj.shape[0]
    nt = s // tm

    def body(av_p, av_m, av_n, ag_p, ag_m, ag_n, gc_ref, w_ref, cb_ref, lg_ref, lb_ref,
             y_ref, c_ref, ext_ref, cv_ref):
        i = pl.program_id(0)

        def glu(a_ref, g_ref):
            return a_ref[...].astype(F32) * _sigmoid(g_ref[...].astype(F32))

        def tiles(a):
            return a.reshape(a.shape[0], 8, 128)

        ext_ref[0:HALO] = tiles(jnp.where(i > 0, glu(av_p, ag_p), 0.0))
        ext_ref[HALO:HALO + tm] = tiles(glu(av_m, ag_m))
        ext_ref[HALO + tm:] = tiles(jnp.where(i < nt - 1, glu(av_n, ag_n), 0.0))

        nb = 32

        def step(t, carry):
            s0 = t * nb
            accs = [None] * nb
            for k in range(CONV_WIDTH):
                w = w_ref[k]
                for j in range(nb):
                    term = w * ext_ref[s0 + HALO - CONV_PAD + j + k]
                    accs[j] = term if accs[j] is None else accs[j] + term
            for j in range(nb):
                cv_ref[s0 + j] = accs[j]
            return carry

        lax.fori_loop(0, tm // nb, step, 0)
        cv = cv_ref[...].reshape(tm, C_BR) + cb_ref[...]
        c_ref[...] = cv.astype(ACT)
        xhat, _ = _ln_norm(cv)
        ln = xhat * lg_ref[...] + lb_ref[...]
        gc = gc_ref[...].astype(F32)
        y_ref[...] = (ln * _sigmoid(ln) * (gc * _sigmoid(gc))).astype(ACT)

    vec = pl.BlockSpec((1, C_BR), lambda i: (0, 0))
    return pl.pallas_call(
        body, name="conv_fwd", grid=(nt,),
        out_shape=(jax.ShapeDtypeStruct((s, 2 * C_BR), ACT), jax.ShapeDtypeStruct((s, C_BR), ACT)),
        in_specs=_halo_specs(tm, s, 0) + _halo_specs(tm, s, 1)
        + [pl.BlockSpec((tm, C_BR), lambda i: (i, 2)),
           pl.BlockSpec((CONV_ROWS, 8, 128), lambda i: (0, 0, 0)), vec, vec, vec],
        out_specs=(pl.BlockSpec((tm, C_BR), lambda i: (i, 0)), pl.BlockSpec((tm, C_BR), lambda i: (i, 0))),
        scratch_shapes=[pltpu.VMEM((tm + 2 * HALO, 8, 128), F32),
                        pltpu.VMEM((tm, 8, 128), F32)],
        compiler_params=_params(("parallel",)),
    )(proj, proj, proj, proj, proj, proj, proj, conv_w3, conv_b, ln_g, ln_b)


def _sgu_fwd(proj, y, ln_g, ln_b, ws, bsb, tm):
    s = proj.shape[0]

    def body(u_ref, v_ref, gs_ref, y_in, lg_ref, lb_ref, ws_ref, bsb_ref, y_ref):
        del y_in
        xhat, _ = _ln_norm(v_ref[...].astype(F32))
        vn = (xhat * lg_ref[...] + lb_ref[...]).astype(BF16)
        for cidx in range(tm // CHUNK):
            rows = slice(cidx * CHUNK, (cidx + 1) * CHUNK)
            for h in range(HEADS):
                cols = slice(h * HEAD_DIM, (h + 1) * HEAD_DIM)
                mixed = jnp.dot(ws_ref[h], vn[rows, cols], preferred_element_type=F32) + bsb_ref[:, cols]
                gs = gs_ref[rows, cols].astype(F32)
                y_ref[rows, cols] = (u_ref[rows, cols].astype(F32) * mixed * (gs * _sigmoid(gs))).astype(ACT)

    vec = pl.BlockSpec((1, C_BR), lambda i: (0, 0))
    return pl.pallas_call(
        body, name="sgu_fwd", grid=(s // tm,),
        out_shape=jax.ShapeDtypeStruct((s, 2 * C_BR), ACT),
        in_specs=[pl.BlockSpec((tm, C_BR), lambda i: (i, 3)),
                  pl.BlockSpec((tm, C_BR), lambda i: (i, 4)),
                  pl.BlockSpec((tm, C_BR), lambda i: (i, 5)),
                  pl.BlockSpec(memory_space=pl.ANY),
                  vec, vec,
                  pl.BlockSpec((HEADS, CHUNK, CHUNK), lambda i: (0, 0, 0)),
                  pl.BlockSpec((CHUNK, C_BR), lambda i: (0, 0))],
        out_specs=pl.BlockSpec((tm, C_BR), lambda i: (i, 1)),
        input_output_aliases={3: 0},
        compiler_params=_params(("parallel",)),
    )(proj, proj, proj, y, ln_g, ln_b, ws, bsb)


def _out_loss(x, y, wout, final_g, target, tm):
    s = x.shape[0]
    nt = s // tm
    inv_d = 1.0 / D_MODEL

    def body(x_ref, y_ref, w_ref, g_ref, t_ref, dx2_ref, dy_ref, dw_ref, loss_ref, dfg_ref, acc_ref):
        i = pl.program_id(0)

        @pl.when(i == 0)
        def _():
            acc_ref[...] = jnp.zeros_like(acc_ref)
            loss_ref[...] = jnp.zeros_like(loss_ref)
            dfg_ref[...] = jnp.zeros_like(dfg_ref)

        g = g_ref[...]
        halves = (slice(0, tm // 2), slice(tm // 2, tm))

        def mm1(rows):
            return x_ref[rows, :] + jnp.dot(y_ref[rows, :], w_ref[...], preferred_element_type=F32)

        def vec(rows, x2):
            r2 = lax.rsqrt(jnp.mean(x2 * x2, axis=-1, keepdims=True) + EPS)
            n = x2 * r2
            e = n * g - t_ref[rows, :]
            loss_ref[...] += _fold8(e * e)
            dout = e * inv_d
            dfg_ref[...] += _fold8(dout * n)
            dn = dout * g
            dx2 = r2 * (dn - n * jnp.mean(dn * n, axis=-1, keepdims=True))
            dx2_ref[rows, :] = dx2
            return dx2.astype(BF16)

        def mm23(rows, dxb):
            dy_ref[rows, :] = lax.dot_general(dxb, w_ref[...], (((1,), (1,)), ((), ())),
                                              preferred_element_type=F32).astype(ACT)
            acc_ref[...] += lax.dot_general(y_ref[rows, :], dxb, (((0,), (0,)), ((), ())),
                                            preferred_element_type=F32)

        xa = mm1(halves[0])
        xb = mm1(halves[1])
        da = vec(halves[0], xa)
        mm23(halves[0], da)
        db = vec(halves[1], xb)
        mm23(halves[1], db)

        @pl.when(i == nt - 1)
        def _():
            dw_ref[...] = acc_ref[...].astype(BF16)

    part = pl.BlockSpec((8, D_MODEL), lambda i: (0, 0))
    return pl.pallas_call(
        body, name="out_loss", grid=(nt,),
        out_shape=(jax.ShapeDtypeStruct((s, D_MODEL), F32), jax.ShapeDtypeStruct((s, 2 * C_BR), ACT),
                   jax.ShapeDtypeStruct((2 * C_BR, D_MODEL), BF16),
                   jax.ShapeDtypeStruct((8, D_MODEL), F32), jax.ShapeDtypeStruct((8, D_MODEL), F32)),
        in_specs=[pl.BlockSpec((tm, D_MODEL), lambda i: (i, 0)),
                  pl.BlockSpec((tm, 2 * C_BR), lambda i: (i, 0)),
                  pl.BlockSpec((2 * C_BR, D_MODEL), lambda i: (0, 0), pipeline_mode=pl.Buffered(1)),
                  pl.BlockSpec((1, D_MODEL), lambda i: (0, 0)),
                  pl.BlockSpec((tm, D_MODEL), lambda i: (i, 0))],
        out_specs=(pl.BlockSpec((tm, D_MODEL), lambda i: (i, 0)),
                   pl.BlockSpec((tm, 2 * C_BR), lambda i: (i, 0)),
                   pl.BlockSpec((2 * C_BR, D_MODEL), lambda i: (0, 0), pipeline_mode=pl.Buffered(1)), part, part),
        scratch_shapes=[pltpu.VMEM((2 * C_BR, D_MODEL), F32)],
        compiler_params=_params(("arbitrary",)),
    )(x, y, wout, final_g, target)


def _conv_bwd(proj, cv, dy, conv_w3, ln_g, ln_b, dwout, tm):
    s = proj.shape[0]
    nt = s // tm
    wo_rows = dwout.shape[1]

    def body(av_ref, ag_ref, gc_p, gc_m, gc_n, c_p, c_m, c_n, dy_p, dy_m, dy_n, w_ref, lg_ref, lb_ref,
             dwout_ref, dp_ref, dwc_ref, dcb_ref, dlg_ref, dlb_ref, gwo_ref,
             dce_ref, glu_ref, dgl_ref, land_ref, send_sems, recv_sems, loc_sem):
        i = pl.program_id(0)
        px, py, pc = _place()
        me = 4 * px + 2 * py + pc

        def exchanges():
            out = []
            for rel in range(1, N_DEV):
                qx, qy, qc = px ^ (rel >> 2), py ^ ((rel >> 1) & 1), pc ^ (rel & 1)
                out.append(pltpu.make_async_remote_copy(
                    src_ref=dwout_ref.at[4 * qx + 2 * qy + qc], dst_ref=land_ref.at[me],
                    send_sem=send_sems.at[rel - 1], recv_sem=recv_sems.at[rel - 1],
                    device_id=(qx, qy, qc), device_id_type=MESH))
            return out

        own = pltpu.make_async_copy(dwout_ref.at[me], land_ref.at[me], loc_sem)

        @pl.when(i == 0)
        def _():
            dwc_ref[...] = jnp.zeros_like(dwc_ref)
            dcb_ref[...] = jnp.zeros_like(dcb_ref)
            dlg_ref[...] = jnp.zeros_like(dlg_ref)
            dlb_ref[...] = jnp.zeros_like(dlb_ref)
            own.start()
            for cp in exchanges():
                cp.start()

        def ext(p, m, n):
            return jnp.concatenate([p[...], m[...], n[...]], axis=0).astype(F32)

        main = slice(HALO, HALO + tm)
        cf, gc, dyc = ext(c_p, c_m, c_n), ext(gc_p, gc_m, gc_n), ext(dy_p, dy_m, dy_n)
        xhat, rstd = _ln_norm(cf)
        lg = lg_ref[...]
        ln = xhat * lg + lb_ref[...]
        s_ln, s_gc = _sigmoid(ln), _sigmoid(gc)
        dln = dyc * (gc * s_gc) * (s_ln * (1.0 + ln * (1.0 - s_ln)))
        dp_ref[:, 2 * C_BR:] = (dyc[main] * (ln[main] * s_ln[main])
                                * (s_gc[main] * (1.0 + gc[main] * (1.0 - s_gc[main])))).astype(ACT)
        dlg_ref[...] += _fold8(dln[main] * xhat[main])
        dlb_ref[...] += _fold8(dln[main])
        dc = _ln_bwd(dln, xhat, rstd, lg)
        dcb_ref[...] += _fold8(dc[main])

        def tiles(a):
            return a.reshape(a.shape[0], 8, 128)

        dce_ref[0:HALO] = tiles(jnp.where(i > 0, dc[0:HALO], 0.0))
        dce_ref[HALO:HALO + tm] = tiles(dc[main])
        dce_ref[HALO + tm:] = tiles(jnp.where(i < nt - 1, dc[HALO + tm:], 0.0))
        av = av_ref[...].astype(F32)
        sa = _sigmoid(ag_ref[...].astype(F32))
        glu_ref[...] = tiles(av * sa)

        nb = 16

        def step(t, carry):
            s0 = t * nb
            accs = [None] * nb
            for k in range(CONV_WIDTH):
                w = w_ref[k]
                prods = []
                for j in range(nb):
                    v = dce_ref[s0 + HALO + CONV_PAD + j - k]
                    term = w * v
                    accs[j] = term if accs[j] is None else accs[j] + term
                    prods.append(glu_ref[s0 + j] * v)
                while len(prods) > 1:
                    prods = [p + q for p, q in zip(prods[::2], prods[1::2])]
                dwc_ref[k] += prods[0]
            for j in range(nb):
                dgl_ref[s0 + j] = accs[j]
            return carry

        lax.fori_loop(0, tm // nb, step, 0)
        dglu = dgl_ref[...].reshape(tm, C_BR)
        dp_ref[:, 0:C_BR] = (dglu * sa).astype(ACT)
        dp_ref[:, C_BR:2 * C_BR] = (dglu * av * sa * (1.0 - sa)).astype(ACT)

        @pl.when(i == nt - 1)
        def _():
            copies = exchanges()
            own.wait()
            for cp in copies:
                cp.wait_recv()

            def step(t, carry):
                sl = pl.ds(pl.multiple_of(t * 64, 64), 64)
                g = land_ref[0, sl, :].astype(F32)
                for d in range(1, N_DEV):
                    g = g + land_ref[d, sl, :].astype(F32)
                gwo_ref[sl, :] = g
                return carry

            lax.fori_loop(0, wo_rows // 64, step, 0)
            for cp in copies:
                cp.wait_send()

    vec = pl.BlockSpec((1, C_BR), lambda i: (0, 0))
    part = pl.BlockSpec((8, C_BR), lambda i: (0, 0))
    return pl.pallas_call(
        body, name="conv_bwd", grid=(nt,),
        out_shape=(jax.ShapeDtypeStruct((s, D_IN), ACT), jax.ShapeDtypeStruct((CONV_ROWS, 8, 128), F32),
                   jax.ShapeDtypeStruct((8, C_BR), F32), jax.ShapeDtypeStruct((8, C_BR), F32),
                   jax.ShapeDtypeStruct((8, C_BR), F32), jax.ShapeDtypeStruct(dwout.shape[1:], F32)),
        in_specs=[pl.BlockSpec((tm, C_BR), lambda i: (i, 0)), pl.BlockSpec((tm, C_BR), lambda i: (i, 1))]
        + _halo_specs(tm, s, 2) + _halo_specs(tm, s, 0) + _halo_specs(tm, s, 0)
        + [pl.BlockSpec((CONV_ROWS, 8, 128), lambda i: (0, 0, 0)), vec, vec, pl.BlockSpec(memory_space=pl.ANY)],
        out_specs=(pl.BlockSpec((tm, 3 * C_BR), lambda i: (i, 0)),
                   pl.BlockSpec((CONV_ROWS, 8, 128), lambda i: (0, 0, 0)), part, part, part,
                   pl.BlockSpec(memory_space=pltpu.VMEM)),
        scratch_shapes=[pltpu.VMEM((tm + 2 * HALO, 8, 128), F32),
                        pltpu.VMEM((tm, 8, 128), F32), pltpu.VMEM((tm, 8, 128), F32),
                        pltpu.VMEM(dwout.shape, BF16),
                        pltpu.SemaphoreType.DMA((N_DEV - 1,)), pltpu.SemaphoreType.DMA((N_DEV - 1,)),
                        pltpu.SemaphoreType.DMA],
        compiler_params=_params(("arbitrary",)),
    )(proj, proj, proj, proj, proj, cv, cv, cv, dy, dy, dy, conv_w3, ln_g, ln_b, dwout)


def _sgu_bwd(proj, dy, dproj, ln_g, ln_b, ws, wst, bsb, tm):
    s = proj.shape[0]

    def body(u_ref, v_ref, gs_ref, dy_ref, dp_in, lg_ref, lb_ref, ws_ref, wst_ref, bsb_ref,
             dp_ref, dws_ref, dba_ref, dlg_ref, dlb_ref, dvn_ref):
        del dp_in
        i = pl.program_id(0)

        @pl.when(i == 0)
        def _():
            dws_ref[...] = jnp.zeros_like(dws_ref)
            dba_ref[...] = jnp.zeros_like(dba_ref)
            dlg_ref[...] = jnp.zeros_like(dlg_ref)
            dlb_ref[...] = jnp.zeros_like(dlb_ref)

        xhat, rstd = _ln_norm(v_ref[...].astype(F32))
        lg = lg_ref[...]
        vn = (xhat * lg + lb_ref[...]).astype(BF16)
        for cidx in range(tm // CHUNK):
            rows = slice(cidx * CHUNK, (cidx + 1) * CHUNK)
            for h in range(HEADS):
                cols = slice(h * HEAD_DIM, (h + 1) * HEAD_DIM)
                ocols = slice(C_BR + h * HEAD_DIM, C_BR + (h + 1) * HEAD_DIM)
                gcols = slice(2 * C_BR + h * HEAD_DIM, 2 * C_BR + (h + 1) * HEAD_DIM)
                vb = vn[rows, cols]
                mixed = jnp.dot(ws_ref[h], vb, preferred_element_type=F32) + bsb_ref[:, cols]
                gs = gs_ref[rows, cols].astype(F32)
                sg = _sigmoid(gs)
                u = u_ref[rows, cols].astype(F32)
                dyb = dy_ref[rows, cols].astype(F32)
                t = dyb * (gs * sg)
                dp_ref[rows, cols] = (t * mixed).astype(ACT)
                dp_ref[rows, gcols] = (dyb * u * mixed * (sg * (1.0 + gs * (1.0 - sg)))).astype(ACT)
                dm = t * u
                dmb = dm.astype(BF16)
                dvn_ref[rows, cols] = jnp.dot(wst_ref[h], dmb, preferred_element_type=F32)
                dws_ref[h] += lax.dot_general(dmb, vb, (((1,), (1,)), ((), ())), preferred_element_type=F32)
                dba_ref[:, cols] += dm
        dvn = dvn_ref[...]
        dlg_ref[...] += _fold8(dvn * xhat)
        dlb_ref[...] += _fold8(dvn)
        dp_ref[:, C_BR:2 * C_BR] = _ln_bwd(dvn, xhat, rstd, lg).astype(ACT)

    vec = pl.BlockSpec((1, C_BR), lambda i: (0, 0))
    part = pl.BlockSpec((8, C_BR), lambda i: (0, 0))
    wsp = pl.BlockSpec((HEADS, CHUNK, CHUNK), lambda i: (0, 0, 0))
    return pl.pallas_call(
        body, name="sgu_bwd", grid=(s // tm,),
        out_shape=(jax.ShapeDtypeStruct((s, D_IN), ACT), jax.ShapeDtypeStruct((HEADS, CHUNK, CHUNK), F32),
                   jax.ShapeDtypeStruct((CHUNK, C_BR), F32), jax.ShapeDtypeStruct((8, C_BR), F32),
                   jax.ShapeDtypeStruct((8, C_BR), F32)),
        in_specs=[pl.BlockSpec((tm, C_BR), lambda i: (i, 3)),
                  pl.BlockSpec((tm, C_BR), lambda i: (i, 4)),
                  pl.BlockSpec((tm, C_BR), lambda i: (i, 5)),
                  pl.BlockSpec((tm, C_BR), lambda i: (i, 1)),
                  pl.BlockSpec(memory_space=pl.ANY),
                  vec, vec, wsp, wsp, pl.BlockSpec((CHUNK, C_BR), lambda i: (0, 0))],
        out_specs=(pl.BlockSpec((tm, 3 * C_BR), lambda i: (i, 1)), wsp,
                   pl.BlockSpec((CHUNK, C_BR), lambda i: (0, 0)), part, part),
        scratch_shapes=[pltpu.VMEM((tm, C_BR), F32)],
        input_output_aliases={4: 0},
        compiler_params=_params(("arbitrary",)),
    )(proj, proj, proj, dy, dproj, ln_g, ln_b, ws, wst, bsb)


def _dx(dproj, win_all, x, norm_g, dx2, tm):
    s = x.shape[0]

    def body(dp_ref, w_ref, x_ref, g_ref, dx2_ref, gx_ref, dng_ref):
        i = pl.program_id(0)

        @pl.when(i == 0)
        def _():
            dng_ref[...] = jnp.zeros_like(dng_ref)

        dh = None
        for j in range(N_DEV):
            term = lax.dot_general(dp_ref[:, j * W_BLK:(j + 1) * W_BLK], w_ref[j],
                                   (((1,), (1,)), ((), ())), preferred_element_type=F32)
            dh = term if dh is None else dh + term
        xf = x_ref[...]
        r = lax.rsqrt(jnp.mean(xf * xf, axis=-1, keepdims=True) + EPS)
        n = xf * r
        dng_ref[...] += _fold8(dh * n)
        dn = dh * g_ref[...]
        gx_ref[...] = dx2_ref[...] + r * (dn - n * jnp.mean(dn * n, axis=-1, keepdims=True))

    return pl.pallas_call(
        body, name="dx", grid=(s // tm,),
        out_shape=(jax.ShapeDtypeStruct((s, D_MODEL), F32), jax.ShapeDtypeStruct((8, D_MODEL), F32)),
        in_specs=[pl.BlockSpec((tm, D_IN), lambda i: (i, 0)),
                  pl.BlockSpec((N_DEV, D_MODEL, W_BLK), lambda i: (0, 0, 0), pipeline_mode=pl.Buffered(1)),
                  pl.BlockSpec((tm, D_MODEL), lambda i: (i, 0)),
                  pl.BlockSpec((1, D_MODEL), lambda i: (0, 0)),
                  pl.BlockSpec((tm, D_MODEL), lambda i: (i, 0))],
        out_specs=(pl.BlockSpec((tm, D_MODEL), lambda i: (i, 0)), pl.BlockSpec((8, D_MODEL), lambda i: (0, 0))),
        compiler_params=_params(("arbitrary",)),
    )(dproj, win_all, x, norm_g, dx2)


def _dwin_comm(ht, dproj, order, parts, dwc, dba, dws, tk):
    s = ht.shape[1]
    nk = s // tk
    n_part = len(parts)

    def body(*refs):
        order_ref, ht_ref, dp_ref = refs[:3]
        del order_ref
        part_refs = refs[3:3 + n_part]
        dwc_ref, dba_ref, dws_ref = refs[3 + n_part:6 + n_part]
        gw_ref, red_ref, wsr_ref, cws_ref, loss_ref = refs[6 + n_part:11 + n_part]
        (acc_ref, all1_ref, all2_ref, out_s, land_s, out_x, land_x,
         send_sems, recv_sems, send_s, recv_s, send_x, recv_x) = refs[11 + n_part:]
        jj, k = pl.program_id(0), pl.program_id(1)
        x, y, c = _place()
        me = 4 * x + 2 * y + c
        sibling = (x, y, 1 - c)
        chips = [(1 - x, 1 - y), (1 - x, y), (x, 1 - y)]
        last = k == nk - 1

        def exchanges():
            out = []
            for rel in range(1, N_DEV):
                peer = (x ^ (rel >> 2), y ^ ((rel >> 1) & 1), c ^ (rel & 1))
                for a, buf in enumerate((all1_ref, all2_ref)):
                    out.append(pltpu.make_async_remote_copy(
                        src_ref=buf.at[me], dst_ref=buf.at[me],
                        send_sem=send_sems.at[a, rel - 1], recv_sem=recv_sems.at[a, rel - 1],
                        device_id=peer, device_id_type=MESH))
            return out

        def to_sibling(slot):
            return pltpu.make_async_remote_copy(
                src_ref=out_s.at[slot], dst_ref=land_s.at[slot],
                send_sem=send_s.at[slot], recv_sem=recv_s.at[slot], device_id=sibling, device_id_type=MESH)

        def to_chip(slot):
            return pltpu.make_async_remote_copy(
                src_ref=out_x.at[slot], dst_ref=land_x.at[slot],
                send_sem=send_x.at[slot], recv_sem=recv_x.at[slot],
                device_id=(*chips[slot], c), device_id_type=MESH)

        @pl.when((jj == 0) & (k == 0))
        def _():
            all1_ref[me] = jnp.zeros((SMALL_ROWS, C_BR), F32)
            for row, p_ref in zip((ROW_NORM_G, ROW_CONV_B, ROW_CLN_G, ROW_CLN_B, ROW_SLN_G, ROW_SLN_B,
                                   ROW_FINAL_G, ROW_LOSS), part_refs):
                all1_ref[me, row:row + 1, :] = jnp.sum(p_ref[...], axis=0, keepdims=True)
            ones = jnp.ones((8, HEAD_DIM), F32)
            brow = [lax.dot_general(ones, dba_ref[:, h * HEAD_DIM:(h + 1) * HEAD_DIM], (((1,), (1,)), ((), ())),
                                    precision=lax.Precision.HIGHEST, preferred_element_type=F32)[0:1]
                    for h in range(HEADS)]
            all1_ref[me, ROW_B_S:ROW_B_S + 1, :] = jnp.concatenate(brow, axis=1)
            all1_ref[me, ROW_CONV_W:ROW_CONV_W + CONV_ROWS, :] = dwc_ref[...]
            all2_ref[me] = dws_ref[...]
            for cp in exchanges():
                cp.start()

        @pl.when(k == 0)
        def _():
            acc_ref[...] = jnp.zeros_like(acc_ref)

        acc_ref[...] += jnp.dot(ht_ref[...], dp_ref[...], preferred_element_type=F32)

        for slot in range(4):
            @pl.when((jj == 2 * slot) & last)
            def _(slot=slot):
                out_s[slot] = acc_ref[...].astype(BF16)
                to_sibling(slot).start()

        for slot in range(3):
            @pl.when((jj == 2 * slot + 1) & last)
            def _(slot=slot):
                to_sibling(slot).wait_recv()
                out_x[slot] = (acc_ref[...] + land_s[slot].astype(F32)).astype(BF16)
                to_chip(slot).start()

        @pl.when((jj == N_DEV - 1) & last)
        def _():
            to_sibling(3).wait_recv()
            total = acc_ref[...] + land_s[3].astype(F32)
            for slot in range(3):
                to_chip(slot).wait_recv()
                total = total + land_x[slot].astype(F32)
            gw_ref[...] = total

            copies = exchanges()
            for cp in copies:
                cp.wait_recv()
            tot = all1_ref[0]
            for d in range(1, N_DEV):
                tot = tot + all1_ref[d]
            red_ref[...] = tot
            loss_ref[...] = jnp.broadcast_to(
                jnp.sum(tot[ROW_LOSS:ROW_LOSS + 1, :], axis=1, keepdims=True) * (0.5 / D_MODEL), loss_ref.shape)
            shard = jnp.zeros(cws_ref.shape, F32)
            for d in range(N_DEV):
                shard = jnp.where(me == d, tot[ROW_CONV_W:ROW_CONV_W + CONV_ROWS, d * 128:(d + 1) * 128], shard)
            cws_ref[...] = shard
            tot2 = all2_ref[0]
            for d in range(1, N_DEV):
                tot2 = tot2 + all2_ref[d]
            wsr_ref[...] = tot2
            for cp in copies + [to_sibling(slot) for slot in range(4)] + [to_chip(slot) for slot in range(3)]:
                cp.wait_send()

    vm = pl.BlockSpec(memory_space=pltpu.VMEM)
    blk = (D_MODEL, W_BLK)
    return pl.pallas_call(
        body, name="dwin",
        grid_spec=pltpu.PrefetchScalarGridSpec(
            num_scalar_prefetch=1, grid=(N_DEV, nk),
            in_specs=[pl.BlockSpec((D_MODEL, tk), lambda jj, k, o: (0, k)),
                      pl.BlockSpec((tk, W_BLK), lambda jj, k, o: (k, o[jj]))] + [vm] * (n_part + 3),
            out_specs=(vm, vm, vm, vm, vm),
            scratch_shapes=[pltpu.VMEM(blk, F32),
                            pltpu.VMEM((N_DEV, SMALL_ROWS, C_BR), F32), pltpu.VMEM((N_DEV,) + dws.shape, F32),
                            pltpu.VMEM((4,) + blk, BF16), pltpu.VMEM((4,) + blk, BF16),
                            pltpu.VMEM((3,) + blk, BF16), pltpu.VMEM((3,) + blk, BF16),
                            pltpu.SemaphoreType.DMA((2, N_DEV - 1)), pltpu.SemaphoreType.DMA((2, N_DEV - 1)),
                            pltpu.SemaphoreType.DMA((4,)), pltpu.SemaphoreType.DMA((4,)),
                            pltpu.SemaphoreType.DMA((3,)), pltpu.SemaphoreType.DMA((3,))]),
        out_shape=(jax.ShapeDtypeStruct(blk, F32),
                   jax.ShapeDtypeStruct((SMALL_ROWS, C_BR), F32), jax.ShapeDtypeStruct(dws.shape, F32),
                   jax.ShapeDtypeStruct((CONV_ROWS, 128), F32), jax.ShapeDtypeStruct((8, 128), F32)),
        compiler_params=_params(("arbitrary", "arbitrary")),
    )(order, ht, dproj, *parts, dwc, dba, dws)


def _adamw_math(w, g, m, v):
    m = ADAM_B1 * m + (1.0 - ADAM_B1) * g
    v = ADAM_B2 * v + (1.0 - ADAM_B2) * (g * g)
    m_hat = m / (1.0 - ADAM_B1 ** ADAM_STEP)
    v_hat = v / (1.0 - ADAM_B2 ** ADAM_STEP)
    delta = -ADAM_LR * (m_hat / (jnp.sqrt(v_hat) + ADAM_EPS) + ADAM_WD * w)
    return delta, m, v


def _adamw_all(groups, g_conv_w, conv_wmv, red, rows, ws, ms, vs):
    ng, n = len(groups), len(rows)
    rchunk = 64

    def body(*refs):
        big = refs[:4 * ng]
        gcw_ref, cw_ref, cm_ref, cv_ref, red_ref = refs[4 * ng:4 * ng + 5]
        small = refs[4 * ng + 5:4 * ng + 5 + 3 * n]
        w_refs, m_refs, v_refs = small[:n], small[n:2 * n], small[2 * n:]
        outs = refs[4 * ng + 5 + 3 * n:]
        for t in range(ng):
            g_ref, w_ref, m_ref, v_ref = big[4 * t:4 * t + 4]
            d_ref, mo_ref, vo_ref = outs[3 * t:3 * t + 3]
            lead = w_ref.shape[0]
            chunks = lead // rchunk if len(w_ref.shape) == 2 else lead

            def step(c, carry, refs_=(g_ref, w_ref, m_ref, v_ref, d_ref, mo_ref, vo_ref), two_d=len(w_ref.shape) == 2):
                sl = pl.ds(pl.multiple_of(c * rchunk, rchunk), rchunk) if two_d else c
                g_, w_, m_, v_, d_, mo_, vo_ = refs_
                d_[sl], mo_[sl], vo_[sl] = _adamw_math(w_[sl], g_[sl], m_[sl], v_[sl])
                return carry

            lax.fori_loop(0, chunks, step, 0)
        g = gcw_ref[0:CONV_WIDTH, :]
        o = 3 * ng
        outs[o][...] = g
        outs[o + 1][...], outs[o + 2][...], outs[o + 3][...] = _adamw_math(cw_ref[...], g, cm_ref[...], cv_ref[...])
        o += 4
        for t, row in enumerate(rows):
            g = red_ref[row:row + 1, :]
            delta, m, v = _adamw_math(w_refs[t][...], g, m_refs[t][...], v_refs[t][...])
            outs[o + t][...] = g
            outs[o + n + t][...] = delta
            outs[o + 2 * n + t][...] = m
            outs[o + 3 * n + t][...] = v

    vm = pl.BlockSpec(memory_space=pltpu.VMEM)
    sds = lambda a: jax.ShapeDtypeStruct(a.shape, F32)
    out_shape = []
    for grp in groups:
        out_shape += [sds(grp[1])] * 3
    out_shape += [sds(conv_wmv[0])] * 4
    out_shape += [jax.ShapeDtypeStruct((1, C_BR), F32)] * (4 * n)
    args = [a for grp in groups for a in grp] + [g_conv_w, *conv_wmv, red, *ws, *ms, *vs]
    res = pl.pallas_call(
        body, name="adamw", out_shape=tuple(out_shape),
        in_specs=[vm] * len(args), out_specs=(vm,) * len(out_shape),
        compiler_params=_params(),
    )(*args)
    big_out = [res[3 * t:3 * t + 3] for t in range(ng)]
    o = 3 * ng
    conv_out = res[o:o + 4]
    o += 4
    return big_out, conv_out, (res[o:o + n], res[o + n:o + 2 * n], res[o + 2 * n:o + 3 * n], res[o + 3 * n:o + 4 * n])


def kernel(x, norm_g, w_in, conv_w, conv_b, conv_ln_g, conv_ln_b, sgu_ln_g, sgu_ln_b, w_s, b_s, w_out, final_g, loss_target, m_norm_g, m_w_in, m_conv_w, m_conv_b, m_conv_ln_g, m_conv_ln_b, m_sgu_ln_g, m_sgu_ln_b, m_w_s, m_b_s, m_w_out, m_final_g, v_norm_g, v_w_in, v_conv_w, v_conv_b, v_conv_ln_g, v_conv_ln_b, v_sgu_ln_g, v_sgu_ln_b, v_w_s, v_b_s, v_w_out, v_final_g):
    s = x.shape[1]
    xs = x.reshape(s, D_MODEL)
    tgt = loss_target.reshape(s, D_MODEL)
    tm = min(256, s)

    cw_pad = jnp.pad(conv_w[0], ((0, CONV_ROWS - CONV_WIDTH), (0, 0)))
    px, py, pc = _place()
    blocks = [(px, py, pc), (px, py, 1 - pc)]
    blocks += [(*chip, core) for chip in ((1 - px, py), (px, 1 - py), (1 - px, 1 - py)) for core in (pc, 1 - pc)]
    order = jnp.stack([4 * bx + 2 * by + bc for bx, by, bc in blocks]).astype(jnp.int32)
    proj, ht, win_all, wout_all, cw_all = _proj_ag(xs, norm_g, w_in[0], w_out[0], cw_pad, order, min(1024, s))
    wout_full = wout_all.reshape(2 * C_BR, D_MODEL)
    cw_tiles = jnp.transpose(cw_all, (1, 0, 2))

    ws = w_s[0].astype(BF16)
    wst = jnp.transpose(w_s[0], (0, 2, 1)).astype(BF16)
    bsb = jnp.repeat(jnp.transpose(b_s[0]), HEAD_DIM, axis=1)
    fg = final_g.reshape(1, D_MODEL)

    big = min(512, s)
    y, cv = _conv_fwd(proj, cw_tiles, conv_b, conv_ln_g, conv_ln_b, big)
    y = _sgu_fwd(proj, y, sgu_ln_g, sgu_ln_b, ws, bsb, big)
    dx2, dy, dwout, loss_p, dfg_p = _out_loss(xs, y, wout_full, fg, tgt, min(512, s))
    dproj, dwc, dcb_p, dclg_p, dclb_p, g_w_out = _conv_bwd(
        proj, cv, dy, cw_tiles, conv_ln_g, conv_ln_b, dwout.reshape(N_DEV, 2 * C_BR // N_DEV, D_MODEL), big)
    dwc = dwc.reshape(CONV_ROWS, C_BR)
    dproj, dws, dba, dslg_p, dslb_p = _sgu_bwd(proj, dy, dproj, sgu_ln_g, sgu_ln_b, ws, wst, bsb, big)
    grad_x, dng_p = _dx(dproj, win_all, xs, norm_g, dx2, big)
    rs_blocks = [(*chip, core) for chip in ((1 - px, 1 - py), (1 - px, py), (px, 1 - py), (px, py))
                 for core in (1 - pc, pc)]
    rs_order = jnp.stack([4 * bx + 2 * by + bc for bx, by, bc in rs_blocks]).astype(jnp.int32)
    g_w_in, red, g_w_s, g_cw, loss8 = _dwin_comm(
        ht, dproj, rs_order, [dng_p, dcb_p, dclg_p, dclb_p, dslg_p, dslb_p, dfg_p, loss_p], dwc, dba,
        dws, min(1024, s))
    loss = loss8[0, 0]

    row = lambda a: a.reshape(1, C_BR)
    rows = (ROW_NORM_G, ROW_CONV_B, ROW_CLN_G, ROW_CLN_B, ROW_SLN_G, ROW_SLN_B, ROW_B_S, ROW_FINAL_G)
    big_out, (g_cw, d_cw, nm_cw, nv_cw), (g_r, d_r, m_r, v_r) = _adamw_all(
        [(g_w_in, w_in[0], m_w_in[0], v_w_in[0]), (g_w_out, w_out[0], m_w_out[0], v_w_out[0]),
         (g_w_s, w_s[0], m_w_s[0], v_w_s[0])],
        g_cw, (conv_w[0], m_conv_w[0], v_conv_w[0]), red, rows,
        [norm_g, conv_b, conv_ln_g, conv_ln_b, sgu_ln_g, sgu_ln_b, row(b_s), row(final_g)],
        [m_norm_g, m_conv_b, m_conv_ln_g, m_conv_ln_b, m_sgu_ln_g, m_sgu_ln_b, row(m_b_s), row(m_final_g)],
        [v_norm_g, v_conv_b, v_conv_ln_g, v_conv_ln_b, v_sgu_ln_g, v_sgu_ln_b, row(v_b_s), row(v_final_g)])
    (d_w_in, nm_w_in, nv_w_in), (d_w_out, nm_w_out, nv_w_out), (d_ws, nm_ws, nv_ws) = big_out

    def leaves(r, w_in_l, cw_l, ws_l, w_out_l):
        return (r[0], w_in_l[None], cw_l[None], r[1], r[2], r[3], r[4], r[5],
                ws_l[None], r[6].reshape(1, HEADS, CHUNK), w_out_l[None],
                r[7].reshape(D_MODEL))

    return (loss, grad_x.reshape(1, s, D_MODEL),
            *leaves(g_r, g_w_in, g_cw, g_w_s, g_w_out),
            *leaves(d_r, d_w_in, d_cw, d_ws, d_w_out),
            *leaves(m_r, nm_w_in, nm_cw, nm_ws, nm_w_out),
            *leaves(v_r, nv_w_in, nv_cw, nv_ws, nv_w_out))
```

```python
import functools

import jax
import jax.numpy as jnp
from jax import lax
from jax.experimental import pallas as pl
from jax.experimental.pallas import tpu as pltpu

F32 = jnp.float32
BF16 = jnp.bfloat16
ACT = jnp.bfloat16

D_MODEL = 1024
C_BR = 1024
D_IN = 6 * C_BR
N_DEV = 8
W_BLK = D_IN // N_DEV
HEADS = 8
HEAD_DIM = 128
CHUNK = 128
CONV_WIDTH = 31
CONV_PAD = CONV_WIDTH // 2
HALO = 16
CONV_ROWS = 32
EPS = 1e-6

ADAM_LR = 0.001
ADAM_B1 = 0.9
ADAM_B2 = 0.999
ADAM_EPS = 1e-08
ADAM_WD = 0.01
ADAM_STEP = 10

VMEM_LIMIT = 56 * 1024 * 1024
MESH = pl.DeviceIdType.MESH

ROW_NORM_G, ROW_CONV_B, ROW_CLN_G, ROW_CLN_B, ROW_SLN_G, ROW_SLN_B, ROW_FINAL_G, ROW_B_S, ROW_LOSS = range(9)
ROW_CONV_W = 16
SMALL_ROWS = ROW_CONV_W + CONV_ROWS


def _params(sem=None, **kw):
    return pltpu.CompilerParams(dimension_semantics=sem, vmem_limit_bytes=VMEM_LIMIT, **kw)


def _fold8(a):
    r, n = a.shape
    return a.reshape(r // 8, 8, n).sum(axis=0)


def _sigmoid(z):
    return 0.5 * jnp.tanh(0.5 * z) + 0.5


def _ln_norm(xf):
    mu = jnp.mean(xf, axis=-1, keepdims=True)
    xc = xf - mu
    var = jnp.mean(xc * xc, axis=-1, keepdims=True)
    rstd = lax.rsqrt(var + EPS)
    return xc * rstd, rstd


def _ln_bwd(dy, xhat, rstd, g):
    dxhat = dy * g
    m1 = jnp.mean(dxhat, axis=-1, keepdims=True)
    m2 = jnp.mean(dxhat * xhat, axis=-1, keepdims=True)
    return rstd * (dxhat - m1 - xhat * m2)


def _place():
    return lax.axis_index("x"), lax.axis_index("y"), lax.axis_index("c")


def _proj_ag(x, norm_g, w_in, w_out, conv_w, order, tm):
    s = x.shape[0]
    nt = s // tm

    def body(order_ref, x_ref, g_ref, win_ref, wout_ref, cw_ref,
             proj_ref, ht_ref, win_all, wout_all, cw_all,
             h_ref, win_buf, wout_buf, cw_buf, send_sems, recv_sems, save_sems):
        jj, i = pl.program_id(0), pl.program_id(1)
        x_, y_, c_ = _place()
        me, sibling = (x_, y_, c_), (x_, y_, 1 - c_)
        chips = [(1 - x_, y_), (x_, 1 - y_), (1 - x_, 1 - y_)]
        bufs = (win_buf, wout_buf, cw_buf)
        outs = (win_all, wout_all, cw_all)
        start = i == 0

        def index(px, py, pc):
            return 4 * px + 2 * py + pc

        def copy(a, k, block, to):
            return pltpu.make_async_remote_copy(
                src_ref=bufs[a].at[index(*block)], dst_ref=bufs[a].at[index(*block)],
                send_sem=send_sems.at[a, k], recv_sem=recv_sems.at[a, k],
                device_id=to, device_id_type=MESH)

        def save(a, slot, block):
            return pltpu.make_async_copy(bufs[a].at[index(*block)], outs[a].at[index(*block)], save_sems.at[a, slot])

        def first(a):
            return [copy(a, 0, me, sibling)] + [copy(a, 1 + j, me, (*chip, c_)) for j, chip in enumerate(chips)]

        def saves(a):
            blocks = [me, sibling] + [(*chip, core) for chip in chips for core in (c_, 1 - c_)]
            return [save(a, slot, block) for slot, block in enumerate(blocks)]

        @pl.when((jj == 0) & start)
        def _():
            win_buf[index(*me)] = win_ref[...].astype(BF16)
            wout_buf[index(*me)] = wout_ref[...].astype(BF16)
            cw_buf[index(*me)] = cw_ref[...]
            for a in range(3):
                for cp in first(a):
                    cp.start()
                saves(a)[0].start()

        @pl.when((jj == 1) & start)
        def _():
            copy(0, 0, sibling, me).wait_recv()
            saves(0)[1].start()

        for j, chip in enumerate(chips):
            @pl.when((jj == 2 + 2 * j) & start)
            def _(j=j, chip=chip):
                copy(0, 1 + j, (*chip, c_), me).wait_recv()
                copy(0, 4 + j, (*chip, c_), sibling).start()
                saves(0)[2 + 2 * j].start()

            @pl.when((jj == 3 + 2 * j) & start)
            def _(j=j, chip=chip):
                copy(0, 4 + j, (*chip, 1 - c_), me).wait_recv()
                saves(0)[3 + 2 * j].start()

        @pl.when(jj == 0)
        def _():
            xf = x_ref[...]
            r = lax.rsqrt(jnp.mean(xf * xf, axis=-1, keepdims=True) + EPS)
            hf = xf * r * g_ref[...]
            h_ref[i] = hf.astype(BF16)
            ht_ref[...] = hf.T.astype(BF16)

        proj_ref[...] = jnp.dot(h_ref[i], win_buf[order_ref[jj]], preferred_element_type=F32).astype(ACT)

        @pl.when((jj == N_DEV - 1) & start)
        def _():
            for a in (1, 2):
                for j, chip in enumerate(chips):
                    copy(a, 1 + j, (*chip, c_), me).wait_recv()
                    copy(a, 4 + j, (*chip, c_), sibling).start()
                    saves(a)[2 + 2 * j].start()

        @pl.when((jj == N_DEV - 1) & (i == nt - 1))
        def _():
            passed = [copy(a, 4 + j, (*chip, c_), sibling) for a in range(3) for j, chip in enumerate(chips)]
            for a in (1, 2):
                copy(a, 0, sibling, me).wait_recv()
                saves(a)[1].start()
                for j, chip in enumerate(chips):
                    copy(a, 4 + j, (*chip, 1 - c_), me).wait_recv()
                    saves(a)[3 + 2 * j].start()
            for cp in saves(0) + saves(1) + saves(2):
                cp.wait()
            for cp in first(0) + first(1) + first(2) + passed:
                cp.wait_send()

    vm = pl.BlockSpec(memory_space=pltpu.VMEM)
    hbm = pl.BlockSpec(memory_space=pl.ANY)
    once = lambda jj, i: jnp.where(jj == 0, i, nt - 1)
    stacked = [(N_DEV,) + w.shape for w in (w_in, w_out, conv_w)]
    return pl.pallas_call(
        body, name="proj_ag",
        grid_spec=pltpu.PrefetchScalarGridSpec(
            num_scalar_prefetch=1, grid=(N_DEV, nt),
            in_specs=[pl.BlockSpec((tm, D_MODEL), lambda jj, i, o: (once(jj, i), 0)),
                      pl.BlockSpec((1, D_MODEL), lambda jj, i, o: (0, 0)), vm, vm, vm],
            out_specs=(pl.BlockSpec((tm, W_BLK), lambda jj, i, o: (i, o[jj])),
                       pl.BlockSpec((D_MODEL, tm), lambda jj, i, o: (0, once(jj, i))), hbm, hbm, hbm),
            scratch_shapes=[pltpu.VMEM((nt, tm, D_MODEL), BF16),
                            pltpu.VMEM(stacked[0], BF16), pltpu.VMEM(stacked[1], BF16), pltpu.VMEM(stacked[2], F32),
                            pltpu.SemaphoreType.DMA((3, 7)), pltpu.SemaphoreType.DMA((3, 7)),
                            pltpu.SemaphoreType.DMA((3, N_DEV))]),
        out_shape=(jax.ShapeDtypeStruct((s, D_IN), ACT), jax.ShapeDtypeStruct((D_MODEL, s), BF16),
                   jax.ShapeDtypeStruct(stacked[0], BF16), jax.ShapeDtypeStruct(stacked[1], BF16),
                   jax.ShapeDtypeStruct(stacked[2], F32)),
        compiler_params=_params(("arbitrary", "arbitrary")),
    )(order, x, norm_g, w_in, w_out, conv_w)


def _halo_specs(tm, s, col):
    per = tm // HALO
    last = s // HALO - 1
    return [pl.BlockSpec((HALO, C_BR), lambda i: (jnp.maximum(i * per - 1, 0), col)),
            pl.BlockSpec((tm, C_BR), lambda i: (i, col)),
            pl.BlockSpec((HALO, C_BR), lambda i: (jnp.minimum((i + 1) * per, last), col))]


def _conv_fwd(proj, conv_w3, conv_b, ln_g, ln_b, tm):
    s = proj.shape[0]
    nt = s // tm

    def body(av_p, av_m, av_n, ag_p, ag_m, ag_n, gc_ref, w_ref, cb_ref, lg_ref, lb_ref,
             y_ref, c_ref, ext_ref, cv_ref):
        i = pl.program_id(0)

        def glu(a_ref, g_ref):
            return a_ref[...].astype(F32) * _sigmoid(g_ref[...].astype(F32))

        def tiles(a):
            return a.reshape(a.shape[0], 8, 128)

        ext_ref[0:HALO] = tiles(jnp.where(i > 0, glu(av_p, ag_p), 0.0))
        ext_ref[HALO:HALO + tm] = tiles(glu(av_m, ag_m))
        ext_ref[HALO + tm:] = tiles(jnp.where(i < nt - 1, glu(av_n, ag_n), 0.0))

        nb = 32

        def step(t, carry):
            s0 = t * nb
            accs = [None] * nb
            for k in range(CONV_WIDTH):
                w = w_ref[k]
                for j in range(nb):
                    term = w * ext_ref[s0 + HALO - CONV_PAD + j + k]
                    accs[j] = term if accs[j] is None else accs[j] + term
            for j in range(nb):
                cv_ref[s0 + j] = accs[j]
            return carry

        lax.fori_loop(0, tm // nb, step, 0)
        cv = cv_ref[...].reshape(tm, C_BR) + cb_ref[...]
        c_ref[...] = cv.astype(ACT)
        xhat, _ = _ln_norm(cv)
        ln = xhat * lg_ref[...] + lb_ref[...]
        gc = gc_ref[...].astype(F32)
        y_ref[...] = (ln * _sigmoid(ln) * (gc * _sigmoid(gc))).astype(ACT)

    vec = pl.BlockSpec((1, C_BR), lambda i: (0, 0))
    return pl.pallas_call(
        body, name="conv_fwd", grid=(nt,),
        out_shape=(jax.ShapeDtypeStruct((s, 2 * C_BR), ACT), jax.ShapeDtypeStruct((s, C_BR), ACT)),
        in_specs=_halo_specs(tm, s, 0) + _halo_specs(tm, s, 1)
        + [pl.BlockSpec((tm, C_BR), lambda i: (i, 2)),
           pl.BlockSpec((CONV_ROWS, 8, 128), lambda i: (0, 0, 0)), vec, vec, vec],
        out_specs=(pl.BlockSpec((tm, C_BR), lambda i: (i, 0)), pl.BlockSpec((tm, C_BR), lambda i: (i, 0))),
        scratch_shapes=[pltpu.VMEM((tm + 2 * HALO, 8, 128), F32),
                        pltpu.VMEM((tm, 8, 128), F32)],
        compiler_params=_params(("parallel",)),
    )(proj, proj, proj, proj, proj, proj, proj, conv_w3, conv_b, ln_g, ln_b)


def _sgu_fwd(proj, y, ln_g, ln_b, ws, bsb, tm):
    s = proj.shape[0]

    def body(u_ref, v_ref, gs_ref, y_in, lg_ref, lb_ref, ws_ref, bsb_ref, y_ref):
        del y_in
        for cidx in range(tm // CHUNK):
            rows = slice(cidx * CHUNK, (cidx + 1) * CHUNK)
            xhat, _ = _ln_norm(v_ref[rows, :].astype(F32))
            vn = (xhat * lg_ref[...] + lb_ref[...]).astype(BF16)
            for h in range(HEADS):
                cols = slice(h * HEAD_DIM, (h + 1) * HEAD_DIM)
                mixed = jnp.dot(ws_ref[h], vn[:, cols], preferred_element_type=F32) + bsb_ref[:, cols]
                gs = gs_ref[rows, cols].astype(F32)
                y_ref[rows, cols] = (u_ref[rows, cols].astype(F32) * mixed * (gs * _sigmoid(gs))).astype(ACT)

    vec = pl.BlockSpec((1, C_BR), lambda i: (0, 0))
    return pl.pallas_call(
        body, name="sgu_fwd", grid=(s // tm,),
        out_shape=jax.ShapeDtypeStruct((s, 2 * C_BR), ACT),
        in_specs=[pl.BlockSpec((tm, C_BR), lambda i: (i, 3)),
                  pl.BlockSpec((tm, C_BR), lambda i: (i, 4)),
                  pl.BlockSpec((tm, C_BR), lambda i: (i, 5)),
                  pl.BlockSpec(memory_space=pl.ANY),
                  vec, vec,
                  pl.BlockSpec((HEADS, CHUNK, CHUNK), lambda i: (0, 0, 0)),
                  pl.BlockSpec((CHUNK, C_BR), lambda i: (0, 0))],
        out_specs=pl.BlockSpec((tm, C_BR), lambda i: (i, 1)),
        input_output_aliases={3: 0},
        compiler_params=_params(("parallel",)),
    )(proj, proj, proj, y, ln_g, ln_b, ws, bsb)


def _out_loss(x, y, wout, final_g, target, tm):
    s = x.shape[0]
    nt = s // tm
    inv_d = 1.0 / D_MODEL

    def body(x_ref, y_ref, w_ref, g_ref, t_ref, dx2_ref, dy_ref, dw_ref, loss_ref, dfg_ref, acc_ref):
        i = pl.program_id(0)

        @pl.when(i == 0)
        def _():
            acc_ref[...] = jnp.zeros_like(acc_ref)
            loss_ref[...] = jnp.zeros_like(loss_ref)
            dfg_ref[...] = jnp.zeros_like(dfg_ref)

        yb = y_ref[...]
        x2 = x_ref[...] + jnp.dot(yb, w_ref[...], preferred_element_type=F32)
        r2 = lax.rsqrt(jnp.mean(x2 * x2, axis=-1, keepdims=True) + EPS)
        n = x2 * r2
        g = g_ref[...]
        e = n * g - t_ref[...]
        loss_ref[...] += _fold8(e * e)
        dout = e * inv_d
        dfg_ref[...] += _fold8(dout * n)
        dn = dout * g
        dx2 = r2 * (dn - n * jnp.mean(dn * n, axis=-1, keepdims=True))
        dx2_ref[...] = dx2
        dxb = dx2.astype(BF16)
        dy_ref[...] = lax.dot_general(dxb, w_ref[...], (((1,), (1,)), ((), ())),
                                      preferred_element_type=F32).astype(ACT)
        acc_ref[...] += lax.dot_general(yb, dxb, (((0,), (0,)), ((), ())), preferred_element_type=F32)

        @pl.when(i == nt - 1)
        def _():
            dw_ref[...] = acc_ref[...].astype(BF16)

    part = pl.BlockSpec((8, D_MODEL), lambda i: (0, 0))
    return pl.pallas_call(
        body, name="out_loss", grid=(nt,),
        out_shape=(jax.ShapeDtypeStruct((s, D_MODEL), F32), jax.ShapeDtypeStruct((s, 2 * C_BR), ACT),
                   jax.ShapeDtypeStruct((2 * C_BR, D_MODEL), BF16),
                   jax.ShapeDtypeStruct((8, D_MODEL), F32), jax.ShapeDtypeStruct((8, D_MODEL), F32)),
        in_specs=[pl.BlockSpec((tm, D_MODEL), lambda i: (i, 0)),
                  pl.BlockSpec((tm, 2 * C_BR), lambda i: (i, 0)),
                  pl.BlockSpec((2 * C_BR, D_MODEL), lambda i: (0, 0), pipeline_mode=pl.Buffered(1)),
                  pl.BlockSpec((1, D_MODEL), lambda i: (0, 0)),
                  pl.BlockSpec((tm, D_MODEL), lambda i: (i, 0))],
        out_specs=(pl.BlockSpec((tm, D_MODEL), lambda i: (i, 0)),
                   pl.BlockSpec((tm, 2 * C_BR), lambda i: (i, 0)),
                   pl.BlockSpec((2 * C_BR, D_MODEL), lambda i: (0, 0), pipeline_mode=pl.Buffered(1)), part, part),
        scratch_shapes=[pltpu.VMEM((2 * C_BR, D_MODEL), F32)],
        compiler_params=_params(("arbitrary",)),
    )(x, y, wout, final_g, target)


def _conv_bwd(proj, cv, dy, conv_w3, ln_g, ln_b, dwout, tm):
    s = proj.shape[0]
    nt = s // tm
    wo_rows = dwout.shape[1]

    def body(av_ref, ag_ref, gc_p, gc_m, gc_n, c_p, c_m, c_n, dy_p, dy_m, dy_n, w_ref, lg_ref, lb_ref,
             dwout_ref, dp_ref, dwc_ref, dcb_ref, dlg_ref, dlb_ref, gwo_ref,
             dce_ref, glu_ref, dgl_ref, land_ref, send_sems, recv_sems, loc_sem):
        i = pl.program_id(0)
        px, py, pc = _place()
        me = 4 * px + 2 * py + pc

        def exchanges():
            out = []
            for rel in range(1, N_DEV):
                qx, qy, qc = px ^ (rel >> 2), py ^ ((rel >> 1) & 1), pc ^ (rel & 1)
                out.append(pltpu.make_async_remote_copy(
                    src_ref=dwout_ref.at[4 * qx + 2 * qy + qc], dst_ref=land_ref.at[me],
                    send_sem=send_sems.at[rel - 1], recv_sem=recv_sems.at[rel - 1],
                    device_id=(qx, qy, qc), device_id_type=MESH))
            return out

        own = pltpu.make_async_copy(dwout_ref.at[me], land_ref.at[me], loc_sem)

        @pl.when(i == 0)
        def _():
            dwc_ref[...] = jnp.zeros_like(dwc_ref)
            dcb_ref[...] = jnp.zeros_like(dcb_ref)
            dlg_ref[...] = jnp.zeros_like(dlg_ref)
            dlb_ref[...] = jnp.zeros_like(dlb_ref)
            own.start()
            for cp in exchanges():
                cp.start()

        def ext(p, m, n):
            return jnp.concatenate([p[...], m[...], n[...]], axis=0).astype(F32)

        main = slice(HALO, HALO + tm)
        cf, gc, dyc = ext(c_p, c_m, c_n), ext(gc_p, gc_m, gc_n), ext(dy_p, dy_m, dy_n)
        xhat, rstd = _ln_norm(cf)
        lg = lg_ref[...]
        ln = xhat * lg + lb_ref[...]
        s_ln, s_gc = _sigmoid(ln), _sigmoid(gc)
        dln = dyc * (gc * s_gc) * (s_ln * (1.0 + ln * (1.0 - s_ln)))
        dp_ref[:, 2 * C_BR:] = (dyc[main] * (ln[main] * s_ln[main])
                                * (s_gc[main] * (1.0 + gc[main] * (1.0 - s_gc[main])))).astype(ACT)
        dlg_ref[...] += _fold8(dln[main] * xhat[main])
        dlb_ref[...] += _fold8(dln[main])
        dc = _ln_bwd(dln, xhat, rstd, lg)
        dcb_ref[...] += _fold8(dc[main])

        def tiles(a):
            return a.reshape(a.shape[0], 8, 128)

        dce_ref[0:HALO] = tiles(jnp.where(i > 0, dc[0:HALO], 0.0))
        dce_ref[HALO:HALO + tm] = tiles(dc[main])
        dce_ref[HALO + tm:] = tiles(jnp.where(i < nt - 1, dc[HALO + tm:], 0.0))
        av = av_ref[...].astype(F32)
        sa = _sigmoid(ag_ref[...].astype(F32))
        glu_ref[...] = tiles(av * sa)

        nb = 16

        def step(t, carry):
            s0 = t * nb
            accs = [None] * nb
            for k in range(CONV_WIDTH):
                w = w_ref[k]
                prods = []
                for j in range(nb):
                    v = dce_ref[s0 + HALO + CONV_PAD + j - k]
                    term = w * v
                    accs[j] = term if accs[j] is None else accs[j] + term
                    prods.append(glu_ref[s0 + j] * v)
                while len(prods) > 1:
                    prods = [p + q for p, q in zip(prods[::2], prods[1::2])]
                dwc_ref[k] += prods[0]
            for j in range(nb):
                dgl_ref[s0 + j] = accs[j]
            return carry

        lax.fori_loop(0, tm // nb, step, 0)
        dglu = dgl_ref[...].reshape(tm, C_BR)
        dp_ref[:, 0:C_BR] = (dglu * sa).astype(ACT)
        dp_ref[:, C_BR:2 * C_BR] = (dglu * av * sa * (1.0 - sa)).astype(ACT)

        @pl.when(i == nt - 1)
        def _():
            copies = exchanges()
            own.wait()
            for cp in copies:
                cp.wait_recv()

            def step(t, carry):
                sl = pl.ds(pl.multiple_of(t * 64, 64), 64)
                g = land_ref[0, sl, :].astype(F32)
                for d in range(1, N_DEV):
                    g = g + land_ref[d, sl, :].astype(F32)
                gwo_ref[sl, :] = g
                return carry

            lax.fori_loop(0, wo_rows // 64, step, 0)
            for cp in copies:
                cp.wait_send()

    vec = pl.BlockSpec((1, C_BR), lambda i: (0, 0))
    part = pl.BlockSpec((8, C_BR), lambda i: (0, 0))
    return pl.pallas_call(
        body, name="conv_bwd", grid=(nt,),
        out_shape=(jax.ShapeDtypeStruct((s, D_IN), ACT), jax.ShapeDtypeStruct((CONV_ROWS, 8, 128), F32),
                   jax.ShapeDtypeStruct((8, C_BR), F32), jax.ShapeDtypeStruct((8, C_BR), F32),
                   jax.ShapeDtypeStruct((8, C_BR), F32), jax.ShapeDtypeStruct(dwout.shape[1:], F32)),
        in_specs=[pl.BlockSpec((tm, C_BR), lambda i: (i, 0)), pl.BlockSpec((tm, C_BR), lambda i: (i, 1))]
        + _halo_specs(tm, s, 2) + _halo_specs(tm, s, 0) + _halo_specs(tm, s, 0)
        + [pl.BlockSpec((CONV_ROWS, 8, 128), lambda i: (0, 0, 0)), vec, vec, pl.BlockSpec(memory_space=pl.ANY)],
        out_specs=(pl.BlockSpec((tm, 3 * C_BR), lambda i: (i, 0)),
                   pl.BlockSpec((CONV_ROWS, 8, 128), lambda i: (0, 0, 0)), part, part, part,
                   pl.BlockSpec(memory_space=pltpu.VMEM)),
        scratch_shapes=[pltpu.VMEM((tm + 2 * HALO, 8, 128), F32),
                        pltpu.VMEM((tm, 8, 128), F32), pltpu.VMEM((tm, 8, 128), F32),
                        pltpu.VMEM(dwout.shape, BF16),
                        pltpu.SemaphoreType.DMA((N_DEV - 1,)), pltpu.SemaphoreType.DMA((N_DEV - 1,)),
                        pltpu.SemaphoreType.DMA],
        compiler_params=_params(("arbitrary",)),
    )(proj, proj, proj, proj, proj, cv, cv, cv, dy, dy, dy, conv_w3, ln_g, ln_b, dwout)


def _sgu_bwd(proj, dy, dproj, ln_g, ln_b, ws, wst, bsb, tm):
    s = proj.shape[0]

    def body(u_ref, v_ref, gs_ref, dy_ref, dp_in, lg_ref, lb_ref, ws_ref, wst_ref, bsb_ref,
             dp_ref, dws_ref, dba_ref, dlg_ref, dlb_ref, dvn_ref):
        del dp_in
        i = pl.program_id(0)

        @pl.when(i == 0)
        def _():
            dws_ref[...] = jnp.zeros_like(dws_ref)
            dba_ref[...] = jnp.zeros_like(dba_ref)
            dlg_ref[...] = jnp.zeros_like(dlg_ref)
            dlb_ref[...] = jnp.zeros_like(dlb_ref)

        xhat, rstd = _ln_norm(v_ref[...].astype(F32))
        lg = lg_ref[...]
        vn = (xhat * lg + lb_ref[...]).astype(BF16)
        for cidx in range(tm // CHUNK):
            rows = slice(cidx * CHUNK, (cidx + 1) * CHUNK)
            for h in range(HEADS):
                cols = slice(h * HEAD_DIM, (h + 1) * HEAD_DIM)
                ocols = slice(C_BR + h * HEAD_DIM, C_BR + (h + 1) * HEAD_DIM)
                gcols = slice(2 * C_BR + h * HEAD_DIM, 2 * C_BR + (h + 1) * HEAD_DIM)
                vb = vn[rows, cols]
                mixed = jnp.dot(ws_ref[h], vb, preferred_element_type=F32) + bsb_ref[:, cols]
                gs = gs_ref[rows, cols].astype(F32)
                sg = _sigmoid(gs)
                u = u_ref[rows, cols].astype(F32)
                dyb = dy_ref[rows, cols].astype(F32)
                t = dyb * (gs * sg)
                dp_ref[rows, cols] = (t * mixed).astype(ACT)
                dp_ref[rows, gcols] = (dyb * u * mixed * (sg * (1.0 + gs * (1.0 - sg)))).astype(ACT)
                dm = t * u
                dmb = dm.astype(BF16)
                dvn_ref[rows, cols] = jnp.dot(wst_ref[h], dmb, preferred_element_type=F32)
                dws_ref[h] += lax.dot_general(dmb, vb, (((1,), (1,)), ((), ())), preferred_element_type=F32)
                dba_ref[:, cols] += dm
        dvn = dvn_ref[...]
        dlg_ref[...] += _fold8(dvn * xhat)
        dlb_ref[...] += _fold8(dvn)
        dp_ref[:, C_BR:2 * C_BR] = _ln_bwd(dvn, xhat, rstd, lg).astype(ACT)

    vec = pl.BlockSpec((1, C_BR), lambda i: (0, 0))
    part = pl.BlockSpec((8, C_BR), lambda i: (0, 0))
    wsp = pl.BlockSpec((HEADS, CHUNK, CHUNK), lambda i: (0, 0, 0))
    return pl.pallas_call(
        body, name="sgu_bwd", grid=(s // tm,),
        out_shape=(jax.ShapeDtypeStruct((s, D_IN), ACT), jax.ShapeDtypeStruct((HEADS, CHUNK, CHUNK), F32),
                   jax.ShapeDtypeStruct((CHUNK, C_BR), F32), jax.ShapeDtypeStruct((8, C_BR), F32),
                   jax.ShapeDtypeStruct((8, C_BR), F32)),
        in_specs=[pl.BlockSpec((tm, C_BR), lambda i: (i, 3)),
                  pl.BlockSpec((tm, C_BR), lambda i: (i, 4)),
                  pl.BlockSpec((tm, C_BR), lambda i: (i, 5)),
                  pl.BlockSpec((tm, C_BR), lambda i: (i, 1)),
                  pl.BlockSpec(memory_space=pl.ANY),
                  vec, vec, wsp, wsp, pl.BlockSpec((CHUNK, C_BR), lambda i: (0, 0))],
        out_specs=(pl.BlockSpec((tm, 3 * C_BR), lambda i: (i, 1)), wsp,
                   pl.BlockSpec((CHUNK, C_BR), lambda i: (0, 0)), part, part),
        scratch_shapes=[pltpu.VMEM((tm, C_BR), F32)],
        input_output_aliases={4: 0},
        compiler_params=_params(("arbitrary",)),
    )(proj, proj, proj, dy, dproj, ln_g, ln_b, ws, wst, bsb)


def _dx(dproj, win_all, x, norm_g, dx2, tm):
    s = x.shape[0]

    def body(dp_ref, w_ref, x_ref, g_ref, dx2_ref, gx_ref, dng_ref):
        i = pl.program_id(0)

        @pl.when(i == 0)
        def _():
            dng_ref[...] = jnp.zeros_like(dng_ref)

        dh = None
        for j in range(N_DEV):
            term = lax.dot_general(dp_ref[:, j * W_BLK:(j + 1) * W_BLK], w_ref[j],
                                   (((1,), (1,)), ((), ())), preferred_element_type=F32)
            dh = term if dh is None else dh + term
        xf = x_ref[...]
        r = lax.rsqrt(jnp.mean(xf * xf, axis=-1, keepdims=True) + EPS)
        n = xf * r
        dng_ref[...] += _fold8(dh * n)
        dn = dh * g_ref[...]
        gx_ref[...] = dx2_ref[...] + r * (dn - n * jnp.mean(dn * n, axis=-1, keepdims=True))

    return pl.pallas_call(
        body, name="dx", grid=(s // tm,),
        out_shape=(jax.ShapeDtypeStruct((s, D_MODEL), F32), jax.ShapeDtypeStruct((8, D_MODEL), F32)),
        in_specs=[pl.BlockSpec((tm, D_IN), lambda i: (i, 0)),
                  pl.BlockSpec((N_DEV, D_MODEL, W_BLK), lambda i: (0, 0, 0), pipeline_mode=pl.Buffered(1)),
                  pl.BlockSpec((tm, D_MODEL), lambda i: (i, 0)),
                  pl.BlockSpec((1, D_MODEL), lambda i: (0, 0)),
                  pl.BlockSpec((tm, D_MODEL), lambda i: (i, 0))],
        out_specs=(pl.BlockSpec((tm, D_MODEL), lambda i: (i, 0)), pl.BlockSpec((8, D_MODEL), lambda i: (0, 0))),
        compiler_params=_params(("arbitrary",)),
    )(dproj, win_all, x, norm_g, dx2)


def _dwin_comm(ht, dproj, order, parts, dwc, dba, dws, tk):
    s = ht.shape[1]
    nk = s // tk
    n_part = len(parts)

    def body(*refs):
        order_ref, ht_ref, dp_ref = refs[:3]
        del order_ref
        part_refs = refs[3:3 + n_part]
        dwc_ref, dba_ref, dws_ref = refs[3 + n_part:6 + n_part]
        gw_ref, red_ref, wsr_ref, cws_ref, loss_ref = refs[6 + n_part:11 + n_part]
        (acc_ref, all1_ref, all2_ref, out_s, land_s, out_x, land_x,
         send_sems, recv_sems, send_s, recv_s, send_x, recv_x) = refs[11 + n_part:]
        jj, k = pl.program_id(0), pl.program_id(1)
        x, y, c = _place()
        me = 4 * x + 2 * y + c
        sibling = (x, y, 1 - c)
        chips = [(1 - x, 1 - y), (1 - x, y), (x, 1 - y)]
        last = k == nk - 1

        def exchanges():
            out = []
            for rel in range(1, N_DEV):
                peer = (x ^ (rel >> 2), y ^ ((rel >> 1) & 1), c ^ (rel & 1))
                for a, buf in enumerate((all1_ref, all2_ref)):
                    out.append(pltpu.make_async_remote_copy(
                        src_ref=buf.at[me], dst_ref=buf.at[me],
                        send_sem=send_sems.at[a, rel - 1], recv_sem=recv_sems.at[a, rel - 1],
                        device_id=peer, device_id_type=MESH))
            return out

        def to_sibling(slot):
            return pltpu.make_async_remote_copy(
                src_ref=out_s.at[slot], dst_ref=land_s.at[slot],
                send_sem=send_s.at[slot], recv_sem=recv_s.at[slot], device_id=sibling, device_id_type=MESH)

        def to_chip(slot):
            return pltpu.make_async_remote_copy(
                src_ref=out_x.at[slot], dst_ref=land_x.at[slot],
                send_sem=send_x.at[slot], recv_sem=recv_x.at[slot],
                device_id=(*chips[slot], c), device_id_type=MESH)

        @pl.when((jj == 0) & (k == 0))
        def _():
            all1_ref[me] = jnp.zeros((SMALL_ROWS, C_BR), F32)
            for row, p_ref in zip((ROW_NORM_G, ROW_CONV_B, ROW_CLN_G, ROW_CLN_B, ROW_SLN_G, ROW_SLN_B,
                                   ROW_FINAL_G, ROW_LOSS), part_refs):
                all1_ref[me, row:row + 1, :] = jnp.sum(p_ref[...], axis=0, keepdims=True)
            ones = jnp.ones((8, HEAD_DIM), F32)
            brow = [lax.dot_general(ones, dba_ref[:, h * HEAD_DIM:(h + 1) * HEAD_DIM], (((1,), (1,)), ((), ())),
                                    precision=lax.Precision.HIGHEST, preferred_element_type=F32)[0:1]
                    for h in range(HEADS)]
            all1_ref[me, ROW_B_S:ROW_B_S + 1, :] = jnp.concatenate(brow, axis=1)
            all1_ref[me, ROW_CONV_W:ROW_CONV_W + CONV_ROWS, :] = dwc_ref[...]
            all2_ref[me] = dws_ref[...]
            for cp in exchanges():
                cp.start()

        @pl.when(k == 0)
        def _():
            acc_ref[...] = jnp.zeros_like(acc_ref)

        acc_ref[...] += jnp.dot(ht_ref[...], dp_ref[...], preferred_element_type=F32)

        for slot in range(4):
            @pl.when((jj == 2 * slot) & last)
            def _(slot=slot):
                out_s[slot] = acc_ref[...].astype(BF16)
                to_sibling(slot).start()

        for slot in range(3):
            @pl.when((jj == 2 * slot + 1) & last)
            def _(slot=slot):
                to_sibling(slot).wait_recv()
                out_x[slot] = (acc_ref[...] + land_s[slot].astype(F32)).astype(BF16)
                to_chip(slot).start()

        @pl.when((jj == N_DEV - 1) & last)
        def _():
            to_sibling(3).wait_recv()
            total = acc_ref[...] + land_s[3].astype(F32)
            for slot in range(3):
                to_chip(slot).wait_recv()
                total = total + land_x[slot].astype(F32)
            gw_ref[...] = total

            copies = exchanges()
            for cp in copies:
                cp.wait_recv()
            tot = all1_ref[0]
            for d in range(1, N_DEV):
                tot = tot + all1_ref[d]
            red_ref[...] = tot
            loss_ref[...] = jnp.broadcast_to(
                jnp.sum(tot[ROW_LOSS:ROW_LOSS + 1, :], axis=1, keepdims=True) * (0.5 / D_MODEL), loss_ref.shape)
            shard = jnp.zeros(cws_ref.shape, F32)
            for d in range(N_DEV):
                shard = jnp.where(me == d, tot[ROW_CONV_W:ROW_CONV_W + CONV_ROWS, d * 128:(d + 1) * 128], shard)
            cws_ref[...] = shard
            tot2 = all2_ref[0]
            for d in range(1, N_DEV):
                tot2 = tot2 + all2_ref[d]
            wsr_ref[...] = tot2
            for cp in copies + [to_sibling(slot) for slot in range(4)] + [to_chip(slot) for slot in range(3)]:
                cp.wait_send()

    vm = pl.BlockSpec(memory_space=pltpu.VMEM)
    blk = (D_MODEL, W_BLK)
    return pl.pallas_call(
        body, name="dwin",
        grid_spec=pltpu.PrefetchScalarGridSpec(
            num_scalar_prefetch=1, grid=(N_DEV, nk),
            in_specs=[pl.BlockSpec((D_MODEL, tk), lambda jj, k, o: (0, k)),
                      pl.BlockSpec((tk, W_BLK), lambda jj, k, o: (k, o[jj]))] + [vm] * (n_part + 3),
            out_specs=(vm, vm, vm, vm, vm),
            scratch_shapes=[pltpu.VMEM(blk, F32),
                            pltpu.VMEM((N_DEV, SMALL_ROWS, C_BR), F32), pltpu.VMEM((N_DEV,) + dws.shape, F32),
                            pltpu.VMEM((4,) + blk, BF16), pltpu.VMEM((4,) + blk, BF16),
                            pltpu.VMEM((3,) + blk, BF16), pltpu.VMEM((3,) + blk, BF16),
                            pltpu.SemaphoreType.DMA((2, N_DEV - 1)), pltpu.SemaphoreType.DMA((2, N_DEV - 1)),
                            pltpu.SemaphoreType.DMA((4,)), pltpu.SemaphoreType.DMA((4,)),
                            pltpu.SemaphoreType.DMA((3,)), pltpu.SemaphoreType.DMA((3,))]),
        out_shape=(jax.ShapeDtypeStruct(blk, F32),
                   jax.ShapeDtypeStruct((SMALL_ROWS, C_BR), F32), jax.ShapeDtypeStruct(dws.shape, F32),
                   jax.ShapeDtypeStruct((CONV_ROWS, 128), F32), jax.ShapeDtypeStruct((8, 128), F32)),
        compiler_params=_params(("arbitrary", "arbitrary")),
    )(order, ht, dproj, *parts, dwc, dba, dws)


def _adamw_math(w, g, m, v):
    m = ADAM_B1 * m + (1.0 - ADAM_B1) * g
    v = ADAM_B2 * v + (1.0 - ADAM_B2) * (g * g)
    m_hat = m / (1.0 - ADAM_B1 ** ADAM_STEP)
    v_hat = v / (1.0 - ADAM_B2 ** ADAM_STEP)
    delta = -ADAM_LR * (m_hat / (jnp.sqrt(v_hat) + ADAM_EPS) + ADAM_WD * w)
    return delta, m, v


def _adamw_all(groups, g_conv_w, conv_wmv, red, rows, ws, ms, vs):
    ng, n = len(groups), len(rows)
    rchunk = 64

    def body(*refs):
        big = refs[:4 * ng]
        gcw_ref, cw_ref, cm_ref, cv_ref, red_ref = refs[4 * ng:4 * ng + 5]
        small = refs[4 * ng + 5:4 * ng + 5 + 3 * n]
        w_refs, m_refs, v_refs = small[:n], small[n:2 * n], small[2 * n:]
        outs = refs[4 * ng + 5 + 3 * n:]
        for t in range(ng):
            g_ref, w_ref, m_ref, v_ref = big[4 * t:4 * t + 4]
            d_ref, mo_ref, vo_ref = outs[3 * t:3 * t + 3]
            lead = w_ref.shape[0]
            chunks = lead // rchunk if len(w_ref.shape) == 2 else lead

            def step(c, carry, refs_=(g_ref, w_ref, m_ref, v_ref, d_ref, mo_ref, vo_ref), two_d=len(w_ref.shape) == 2):
                sl = pl.ds(pl.multiple_of(c * rchunk, rchunk), rchunk) if two_d else c
                g_, w_, m_, v_, d_, mo_, vo_ = refs_
                d_[sl], mo_[sl], vo_[sl] = _adamw_math(w_[sl], g_[sl], m_[sl], v_[sl])
                return carry

            lax.fori_loop(0, chunks, step, 0)
        g = gcw_ref[0:CONV_WIDTH, :]
        o = 3 * ng
        outs[o][...] = g
        outs[o + 1][...], outs[o + 2][...], outs[o + 3][...] = _adamw_math(cw_ref[...], g, cm_ref[...], cv_ref[...])
        o += 4
        for t, row in enumerate(rows):
            g = red_ref[row:row + 1, :]
            delta, m, v = _adamw_math(w_refs[t][...], g, m_refs[t][...], v_refs[t][...])
            outs[o + t][...] = g
            outs[o + n + t][...] = delta
            outs[o + 2 * n + t][...] = m
            outs[o + 3 * n + t][...] = v

    vm = pl.BlockSpec(memory_space=pltpu.VMEM)
    sds = lambda a: jax.ShapeDtypeStruct(a.shape, F32)
    out_shape = []
    for grp in groups:
        out_shape += [sds(grp[1])] * 3
    out_shape += [sds(conv_wmv[0])] * 4
    out_shape += [jax.ShapeDtypeStruct((1, C_BR), F32)] * (4 * n)
    args = [a for grp in groups for a in grp] + [g_conv_w, *conv_wmv, red, *ws, *ms, *vs]
    res = pl.pallas_call(
        body, name="adamw", out_shape=tuple(out_shape),
        in_specs=[vm] * len(args), out_specs=(vm,) * len(out_shape),
        compiler_params=_params(),
    )(*args)
    big_out = [res[3 * t:3 * t + 3] for t in range(ng)]
    o = 3 * ng
    conv_out = res[o:o + 4]
    o += 4
    return big_out, conv_out, (res[o:o + n], res[o + n:o + 2 * n], res[o + 2 * n:o + 3 * n], res[o + 3 * n:o + 4 * n])


def kernel(x, norm_g, w_in, conv_w, conv_b, conv_ln_g, conv_ln_b, sgu_ln_g, sgu_ln_b, w_s, b_s, w_out, final_g, loss_target, m_norm_g, m_w_in, m_conv_w, m_conv_b, m_conv_ln_g, m_conv_ln_b, m_sgu_ln_g, m_sgu_ln_b, m_w_s, m_b_s, m_w_out, m_final_g, v_norm_g, v_w_in, v_conv_w, v_conv_b, v_conv_ln_g, v_conv_ln_b, v_sgu_ln_g, v_sgu_ln_b, v_w_s, v_b_s, v_w_out, v_final_g):
    s = x.shape[1]
    xs = x.reshape(s, D_MODEL)
    tgt = loss_target.reshape(s, D_MODEL)
    tm = min(256, s)

    cw_pad = jnp.pad(conv_w[0], ((0, CONV_ROWS - CONV_WIDTH), (0, 0)))
    px, py, pc = _place()
    blocks = [(px, py, pc), (px, py, 1 - pc)]
    blocks += [(*chip, core) for chip in ((1 - px, py), (px, 1 - py), (1 - px, 1 - py)) for core in (pc, 1 - pc)]
    order = jnp.stack([4 * bx + 2 * by + bc for bx, by, bc in blocks]).astype(jnp.int32)
    proj, ht, win_all, wout_all, cw_all = _proj_ag(xs, norm_g, w_in[0], w_out[0], cw_pad, order, min(1024, s))
    wout_full = wout_all.reshape(2 * C_BR, D_MODEL)
    cw_tiles = jnp.transpose(cw_all, (1, 0, 2))

    ws = w_s[0].astype(BF16)
    wst = jnp.transpose(w_s[0], (0, 2, 1)).astype(BF16)
    bsb = jnp.repeat(jnp.transpose(b_s[0]), HEAD_DIM, axis=1)
    fg = final_g.reshape(1, D_MODEL)

    big = min(512, s)
    y, cv = _conv_fwd(proj, cw_tiles, conv_b, conv_ln_g, conv_ln_b, big)
    y = _sgu_fwd(proj, y, sgu_ln_g, sgu_ln_b, ws, bsb, big)
    dx2, dy, dwout, loss_p, dfg_p = _out_loss(xs, y, wout_full, fg, tgt, min(512, s))
    dproj, dwc, dcb_p, dclg_p, dclb_p, g_w_out = _conv_bwd(
        proj, cv, dy, cw_tiles, conv_ln_g, conv_ln_b, dwout.reshape(N_DEV, 2 * C_BR // N_DEV, D_MODEL), big)
    dwc = dwc.reshape(CONV_ROWS, C_BR)
    dproj, dws, dba, dslg_p, dslb_p = _sgu_bwd(proj, dy, dproj, sgu_ln_g, sgu_ln_b, ws, wst, bsb, big)
    grad_x, dng_p = _dx(dproj, win_all, xs, norm_g, dx2, big)
    rs_blocks = [(*chip, core) for chip in ((1 - px, 1 - py), (1 - px, py), (px, 1 - py), (px, py))
                 for core in (1 - pc, pc)]
    rs_order = jnp.stack([4 * bx + 2 * by + bc for bx, by, bc in rs_blocks]).astype(jnp.int32)
    g_w_in, red, g_w_s, g_cw, loss8 = _dwin_comm(
        ht, dproj, rs_order, [dng_p, dcb_p, dclg_p, dclb_p, dslg_p, dslb_p, dfg_p, loss_p], dwc, dba,
        dws, min(1024, s))
    loss = loss8[0, 0]

    row = lambda a: a.reshape(1, C_BR)
    rows = (ROW_NORM_G, ROW_CONV_B, ROW_CLN_G, ROW_CLN_B, ROW_SLN_G, ROW_SLN_B, ROW_B_S, ROW_FINAL_G)
    big_out, (g_cw, d_cw, nm_cw, nv_cw), (g_r, d_r, m_r, v_r) = _adamw_all(
        [(g_w_in, w_in[0], m_w_in[0], v_w_in[0]), (g_w_out, w_out[0], m_w_out[0], v_w_out[0]),
         (g_w_s, w_s[0], m_w_s[0], v_w_s[0])],
        g_cw, (conv_w[0], m_conv_w[0], v_conv_w[0]), red, rows,
        [norm_g, conv_b, conv_ln_g, conv_ln_b, sgu_ln_g, sgu_ln_b, row(b_s), row(final_g)],
        [m_norm_g, m_conv_b, m_conv_ln_g, m_conv_ln_b, m_sgu_ln_g, m_sgu_ln_b, row(m_b_s), row(m_final_g)],
        [v_norm_g, v_conv_b, v_conv_ln_g, v_conv_ln_b, v_sgu_ln_g, v_sgu_ln_b, row(v_b_s), row(v_final_g)])
    (d_w_in, nm_w_in, nv_w_in), (d_w_out, nm_w_out, nv_w_out), (d_ws, nm_ws, nv_ws) = big_out

    def leaves(r, w_in_l, cw_l, ws_l, w_out_l):
        return (r[0], w_in_l[None], cw_l[None], r[1], r[2], r[3], r[4], r[5],
                ws_l[None], r[6].reshape(1, HEADS, CHUNK), w_out_l[None],
                r[7].reshape(D_MODEL))

    return (loss, grad_x.reshape(1, s, D_MODEL),
            *leaves(g_r, g_w_in, g_cw, g_w_s, g_w_out),
            *leaves(d_r, d_w_in, d_cw, d_ws, d_w_out),
            *leaves(m_r, nm_w_in, nm_cw, nm_ws, nm_w_out),
            *leaves(v_r, nv_w_in, nv_cw, nv_ws, nv_w_out))
```
